```python
import math
import jax, jax.numpy as jnp
from jax import lax
import numpy as np

D_MODEL = 2048
BATCH = 4
SEQ = 4096
DEPTH = 1

ATT_HEADS = 16
ATT_KV_HEADS = 2
HEAD_DIM = 128
IDX_HEADS = 8
IDX_DIM = 64
TOPK_MAX = 256
Q_BLOCK = 128
ROPE_THETA = 10000.0
GDN_HEADS = 16
GDN_DK = 128
GDN_DV = 128
CONV_WIDTH = 4
CHUNK = 64
N_EXPERTS = 64
TOP_K = 8
N_GROUPS = 8
TOPK_GROUPS = 4
EXPERT_DIM = 512
SHARED_DIM = 512
ROUTED_SCALE = 2.5
EXPERT_BLOCK = 128
DN_ALPHA = (2 * DEPTH) ** 0.25
DN_BETA = (8 * DEPTH) ** -0.25
LN_EPS = 1e-5
RMS_EPS = 1e-6

ATT_Q = ATT_HEADS * HEAD_DIM
ATT_KV = ATT_KV_HEADS * HEAD_DIM
IDX_Q = IDX_HEADS * IDX_DIM
GDN_QK = GDN_HEADS * GDN_DK
GDN_V = GDN_HEADS * GDN_DV
SPLITS = (ATT_Q, ATT_KV, ATT_KV, IDX_Q, IDX_DIM, IDX_HEADS, GDN_QK, GDN_QK, GDN_V, GDN_V, GDN_HEADS, GDN_HEADS, D_MODEL, D_MODEL)
SPLIT_OFFSETS = tuple(int(o) for o in np.cumsum(SPLITS)[:-1])
IN_DIM = int(sum(SPLITS))

kernel_name = "hybrid_dsa_gdn_moe_deepnorm"


def layer_norm(x, g, b):
    xf = x.astype(jnp.float32)
    mu = xf.mean(-1, keepdims=True)
    var = jnp.square(xf - mu).mean(-1, keepdims=True)
    return ((xf - mu) * lax.rsqrt(var + LN_EPS) * g + b).astype(x.dtype)


def rope(x, pos):
    d = x.shape[-1]
    inv = ROPE_THETA ** (-jnp.arange(0, d, 2, dtype=jnp.float32) / d)
    ang = pos.astype(jnp.float32)[..., None, None] * inv
    cos, sin = jnp.cos(ang), jnp.sin(ang)
    x1, x2 = jnp.split(x.astype(jnp.float32), 2, axis=-1)
    return jnp.concatenate([x1 * cos - x2 * sin, x2 * cos + x1 * sin], -1).astype(x.dtype)


def dsa_attention(q, k, v, q_idx, k_idx, w_idx):
    B, S, H, D = q.shape
    G = k.shape[2]
    R = H // G
    topk = min(TOPK_MAX, S // 4)
    nb = S // Q_BLOCK

    def to_blocks(a):
        return a.reshape(B, nb, Q_BLOCK, *a.shape[2:]).swapaxes(0, 1)

    t_blocks = jnp.arange(S, dtype=jnp.int32).reshape(nb, Q_BLOCK)
    key_pos = jnp.arange(S, dtype=jnp.int32)

    def block(args):
        qb, qib, wb, tb = args
        rel = jax.nn.relu(jnp.einsum('bqhd,bsd->bqhs', qib, k_idx).astype(jnp.float32) * IDX_DIM ** -0.5)
        scores = jnp.einsum('bqh,bqhs->bqs', wb.astype(jnp.float32), rel)
        causal = key_pos[None, None, :] <= tb[None, :, None]
        scores = jnp.where(causal, scores, -jnp.inf)
        _, idx = lax.top_k(scores, topk)
        valid = idx <= tb[None, :, None]
        k_sel = jax.vmap(lambda kk, ii: kk[ii])(k, idx)
        v_sel = jax.vmap(lambda vv, ii: vv[ii])(v, idx)
        qg = qb.reshape(B, Q_BLOCK, G, R, D)
        logits = jnp.einsum('bqgrd,bqkgd->bqgrk', qg, k_sel).astype(jnp.float32) * D ** -0.5
        logits = jnp.where(valid[:, :, None, None, :], logits, -jnp.inf)
        p = jax.nn.softmax(logits, axis=-1).astype(v.dtype)
        o = jnp.einsum('bqgrk,bqkgd->bqgrd', p, v_sel)
        return o.reshape(B, Q_BLOCK, H * D)

    out = lax.map(block, (to_blocks(q), to_blocks(q_idx), to_blocks(w_idx), t_blocks))
    return out.swapaxes(0, 1).reshape(B, S, H * D)


def causal_conv(x, w):
    C = x.shape[-1]
    out = lax.conv_general_dilated(x, w[:, None, :].astype(x.dtype), window_strides=(1,),
                                   padding=[(CONV_WIDTH - 1, 0)],
                                   dimension_numbers=('NWC', 'WIO', 'NWC'),
                                   feature_group_count=C)
    return jax.nn.silu(out)


def gated_delta_rule(q, k, v, g, beta):
    f32 = jnp.float32
    B, S, H, DK = q.shape
    DV = v.shape[-1]
    nc = S // CHUNK

    def l2n(a):
        a = a.astype(f32)
        return a * lax.rsqrt(jnp.sum(a * a, -1, keepdims=True) + RMS_EPS)

    def chunks(a):
        return a.reshape(B, nc, CHUNK, H, *a.shape[3:]).swapaxes(2, 3)

    qc = chunks(l2n(q) * DK ** -0.5)
    kc = chunks(l2n(k))
    vc = chunks(v.astype(f32))
    bc = chunks(beta.astype(f32))
    gc = jnp.cumsum(chunks(g.astype(f32)), axis=-1)
    tri = jnp.tril(jnp.ones((CHUNK, CHUNK), dtype=bool))
    strict = jnp.tril(jnp.ones((CHUNK, CHUNK), dtype=bool), -1)
    decay = jnp.exp(jnp.where(tri, gc[..., :, None] - gc[..., None, :], -jnp.inf))
    kb = kc * bc[..., None]
    L = jnp.where(strict, jnp.einsum('bnhcd,bnhsd->bnhcs', kb, kc) * decay, 0.0)
    T = L + jnp.eye(CHUNK, dtype=f32)
    u = lax.linalg.triangular_solve(T, vc * bc[..., None], left_side=True, lower=True, unit_diagonal=True)
    w = lax.linalg.triangular_solve(T, kb * jnp.exp(gc)[..., None], left_side=True, lower=True, unit_diagonal=True)
    a_intra = jnp.einsum('bnhcd,bnhsd->bnhcs', qc, kc) * decay
    g_last = gc[..., -1]
    q_dec = qc * jnp.exp(gc)[..., None]
    k_dec = kc * jnp.exp(g_last[..., None] - gc)[..., None]

    def step(state, xs):
        u_i, w_i, qd_i, kd_i, a_i, gl_i = xs
        v_new = u_i - jnp.einsum('bhcd,bhde->bhce', w_i, state)
        o = jnp.einsum('bhcd,bhde->bhce', qd_i, state) + jnp.einsum('bhcs,bhse->bhce', a_i, v_new)
        state = state * jnp.exp(gl_i)[..., None, None] + jnp.einsum('bhcd,bhce->bhde', kd_i, v_new)
        return state, o

    xs = tuple(a.swapaxes(0, 1) for a in (u, w, q_dec, k_dec, a_intra, g_last))
    _, o = lax.scan(step, jnp.zeros((B, H, DK, DV), f32), xs)
    return o.transpose(1, 0, 3, 2, 4).reshape(B, S, H, DV)


def gated_deltanet(q, k, v, z, b, a, conv_w, A_log, dt_bias, norm_w):
    B, S, _ = q.shape
    qkv = causal_conv(jnp.concatenate([q, k, v], axis=-1), conv_w)
    q, k, v = jnp.split(qkv, [GDN_QK, 2 * GDN_QK], axis=-1)
    beta = jax.nn.sigmoid(b.astype(jnp.float32))
    g = -jnp.exp(A_log.astype(jnp.float32)) * jax.nn.softplus(a.astype(jnp.float32) + dt_bias.astype(jnp.float32))
    o = gated_delta_rule(q.reshape(B, S, GDN_HEADS, GDN_DK), k.reshape(B, S, GDN_HEADS, GDN_DK),
                         v.reshape(B, S, GDN_HEADS, GDN_DV), g, beta)
    zf = z.reshape(B, S, GDN_HEADS, GDN_DV).astype(jnp.float32)
    o = o * lax.rsqrt(jnp.mean(o * o, -1, keepdims=True) + RMS_EPS) * norm_w * jax.nn.silu(zf)
    return o.reshape(B, S, GDN_V).astype(q.dtype)


def hybrid_mixer(x, positions, w_in, conv_w, A_log, dt_bias, gdn_norm_w, idx_ln_g, idx_ln_b,
                 w_o_attn, w_o_gdn, w_out):
    B, S, _ = x.shape
    proj = jnp.einsum('bsd,de->bse', x, w_in)
    (aq, ak, av, iq, ik, iw, gq, gk, gv, gz, gb, ga, gate_a, gate_b) = jnp.split(proj, SPLIT_OFFSETS, axis=-1)
    aq = rope(aq.reshape(B, S, ATT_HEADS, HEAD_DIM), positions)
    ak = rope(ak.reshape(B, S, ATT_KV_HEADS, HEAD_DIM), positions)
    av = av.reshape(B, S, ATT_KV_HEADS, HEAD_DIM)
    iq = rope(iq.reshape(B, S, IDX_HEADS, IDX_DIM), positions)
    ik = rope(layer_norm(ik, idx_ln_g, idx_ln_b)[:, :, None, :], positions)[:, :, 0]
    iw = iw * IDX_HEADS ** -0.5
    att = dsa_attention(aq, ak, av, iq, ik, iw)
    gdn = gated_deltanet(gq, gk, gv, gz, gb, ga, conv_w, A_log, dt_bias, gdn_norm_w)
    merged = (jax.nn.sigmoid(gate_a) * jnp.einsum('bse,ed->bsd', att, w_o_attn)
              + jax.nn.sigmoid(gate_b) * jnp.einsum('bse,ed->bsd', gdn, w_o_gdn))
    return jnp.einsum('bsd,de->bse', merged, w_out)


def swiglu(x, w1, w3, w2):
    return (jax.nn.silu(x @ w1) * (x @ w3)) @ w2


def routed_experts(xt, eidx, gates, w1, w3, w2):
    N, D = xt.shape
    A = N * TOP_K
    e_flat = eidx.reshape(A)
    tok_flat = jnp.repeat(jnp.arange(N, dtype=jnp.int32), TOP_K)
    g_flat = gates.reshape(A)
    order = jnp.argsort(e_flat)
    e_s, tok_s, g_s = e_flat[order], tok_flat[order], g_flat[order]
    counts = jnp.bincount(e_flat, length=N_EXPERTS)
    padded = (counts + EXPERT_BLOCK - 1) // EXPERT_BLOCK * EXPERT_BLOCK
    start = jnp.cumsum(counts) - counts
    pend = jnp.cumsum(padded)
    pstart = pend - padded
    rows = pstart[e_s] + (jnp.arange(A, dtype=jnp.int32) - start[e_s])
    P = -(-A // EXPERT_BLOCK) * EXPERT_BLOCK + N_EXPERTS * EXPERT_BLOCK
    nb = P // EXPERT_BLOCK
    row_tok = jnp.zeros((P,), jnp.int32).at[rows].set(tok_s)
    row_gate = jnp.zeros((P,), xt.dtype).at[rows].set(g_s.astype(xt.dtype))
    blk_e = jnp.minimum(jnp.searchsorted(pend, jnp.arange(nb, dtype=jnp.int32) * EXPERT_BLOCK, side='right'),
                        N_EXPERTS - 1)

    def step(acc, xs):
        e, tok, gt = xs
        xb = xt[tok]
        y = swiglu(xb, w1[e], w3[e], w2[e]) * gt[:, None]
        return acc.at[tok].add(y), None

    acc, _ = lax.scan(step, jnp.zeros_like(xt),
                      (blk_e, row_tok.reshape(nb, EXPERT_BLOCK), row_gate.reshape(nb, EXPERT_BLOCK)))
    return acc


def moe(x, w_router, router_bias, w1, w3, w2, ws1, ws3, ws2):
    B, S, D = x.shape
    N = B * S
    xt = x.reshape(N, D)
    scores = jax.nn.sigmoid(jnp.einsum('nd,de->ne', xt, w_router).astype(jnp.float32))
    choice = scores + router_bias.astype(jnp.float32)
    grp_score = lax.top_k(choice.reshape(N, N_GROUPS, N_EXPERTS // N_GROUPS), 2)[0].sum(-1)
    _, top_groups = lax.top_k(grp_score, TOPK_GROUPS)
    gmask = jnp.any(top_groups[..., None] == jnp.arange(N_GROUPS), axis=1)
    masked = jnp.where(jnp.repeat(gmask, N_EXPERTS // N_GROUPS, axis=1), choice, -jnp.inf)
    _, eidx = lax.top_k(masked, TOP_K)
    wsel = jnp.take_along_axis(scores, eidx, axis=1)
    gates = wsel / jnp.sum(wsel, -1, keepdims=True) * ROUTED_SCALE
    routed = routed_experts(xt, eidx, gates, w1, w3, w2)
    shared = swiglu(xt, ws1, ws3, ws2)
    return (routed + shared).reshape(B, S, D)


def setup_inputs(seed: int = 0) -> dict:
    key = jax.random.key(seed)
    ks = jax.random.split(key, 24)
    f32 = jnp.float32
    nrm = lambda k, shape, s: jax.random.normal(k, shape, f32) * s
    x = jax.random.normal(ks[0], (BATCH, SEQ, D_MODEL), f32)
    offset = jax.random.randint(ks[1], (BATCH, 1), 0, 1024, dtype=jnp.int32)
    positions = offset + jnp.arange(SEQ, dtype=jnp.int32)[None, :]
    w_in = nrm(ks[2], (DEPTH, D_MODEL, IN_DIM), D_MODEL ** -0.5)
    w_in = w_in.at[..., SPLIT_OFFSETS[1]:SPLIT_OFFSETS[2]].multiply(DN_BETA)
    w_in = w_in.at[..., SPLIT_OFFSETS[7]:SPLIT_OFFSETS[8]].multiply(DN_BETA)
    conv_w = nrm(ks[3], (DEPTH, CONV_WIDTH, 2 * GDN_QK + GDN_V), CONV_WIDTH ** -0.5)
    A_log = jnp.log(jax.random.uniform(ks[4], (DEPTH, GDN_HEADS), f32, 1.0, 16.0))
    dt = jnp.exp(jax.random.uniform(ks[5], (DEPTH, GDN_HEADS), f32, math.log(1e-3), math.log(1e-1)))
    dt_bias = dt + jnp.log(-jnp.expm1(-dt))
    gdn_norm_w = 1.0 + nrm(ks[6], (DEPTH, GDN_DV), 0.02)
    idx_ln_g = 1.0 + nrm(ks[7], (DEPTH, IDX_DIM), 0.02)
    idx_ln_b = nrm(ks[8], (DEPTH, IDX_DIM), 0.02)
    w_o_attn = nrm(ks[9], (DEPTH, ATT_Q, D_MODEL), ATT_Q ** -0.5 * DN_BETA)
    w_o_gdn = nrm(ks[10], (DEPTH, GDN_V, D_MODEL), GDN_V ** -0.5 * DN_BETA)
    w_out = nrm(ks[11], (DEPTH, D_MODEL, D_MODEL), D_MODEL ** -0.5 * DN_BETA)
    ln1_g = 1.0 + nrm(ks[12], (DEPTH, D_MODEL), 0.02)
    ln1_b = nrm(ks[13], (DEPTH, D_MODEL), 0.02)
    w_router = nrm(ks[14], (DEPTH, D_MODEL, N_EXPERTS), D_MODEL ** -0.5)
    router_bias = nrm(ks[15], (DEPTH, N_EXPERTS), 0.01)
    w1 = nrm(ks[16], (DEPTH, N_EXPERTS, D_MODEL, EXPERT_DIM), D_MODEL ** -0.5)
    w3 = nrm(ks[17], (DEPTH, N_EXPERTS, D_MODEL, EXPERT_DIM), D_MODEL ** -0.5)
    w2 = nrm(ks[18], (DEPTH, N_EXPERTS, EXPERT_DIM, D_MODEL), EXPERT_DIM ** -0.5 * DN_BETA)
    ws1 = nrm(ks[19], (DEPTH, D_MODEL, SHARED_DIM), D_MODEL ** -0.5)
    ws3 = nrm(ks[20], (DEPTH, D_MODEL, SHARED_DIM), D_MODEL ** -0.5)
    ws2 = nrm(ks[21], (DEPTH, SHARED_DIM, D_MODEL), SHARED_DIM ** -0.5 * DN_BETA)
    ln2_g = 1.0 + nrm(ks[22], (DEPTH, D_MODEL), 0.02)
    ln2_b = nrm(ks[23], (DEPTH, D_MODEL), 0.02)
    return {"x": x, "positions": positions, "w_in": w_in, "conv_w": conv_w, "A_log": A_log,
            "dt_bias": dt_bias, "gdn_norm_w": gdn_norm_w, "idx_ln_g": idx_ln_g, "idx_ln_b": idx_ln_b,
            "w_o_attn": w_o_attn, "w_o_gdn": w_o_gdn, "w_out": w_out, "ln1_g": ln1_g, "ln1_b": ln1_b,
            "w_router": w_router, "router_bias": router_bias, "w1": w1, "w3": w3, "w2": w2,
            "ws1": ws1, "ws3": ws3, "ws2": ws2, "ln2_g": ln2_g, "ln2_b": ln2_b}


def reference(x, positions, w_in, conv_w, A_log, dt_bias, gdn_norm_w, idx_ln_g, idx_ln_b,
              w_o_attn, w_o_gdn, w_out, ln1_g, ln1_b, w_router, router_bias, w1, w3, w2,
              ws1, ws3, ws2, ln2_g, ln2_b):
    for l in range(DEPTH):
        m = hybrid_mixer(x, positions, w_in[l], conv_w[l], A_log[l], dt_bias[l], gdn_norm_w[l],
                         idx_ln_g[l], idx_ln_b[l], w_o_attn[l], w_o_gdn[l], w_out[l])
        x = layer_norm(DN_ALPHA * x + m, ln1_g[l], ln1_b[l])
        f = moe(x, w_router[l], router_bias[l], w1[l], w3[l], w2[l], ws1[l], ws3[l], ws2[l])
        x = layer_norm(DN_ALPHA * x + f, ln2_g[l], ln2_b[l])
    return x
```

```python
import functools
import math

import jax
import jax.numpy as jnp
import numpy as np
from jax import lax
from jax.experimental import pallas as pl
from jax.experimental.pallas import tpu as pltpu

F32 = jnp.float32
BF16 = jnp.bfloat16
I32 = jnp.int32
U32 = jnp.uint32
HI = lax.Precision.HIGHEST

D_MODEL = 2048
ATT_HEADS = 16
ATT_KV_HEADS = 2
HEAD_DIM = 128
IDX_HEADS = 8
IDX_DIM = 64
TOPK_MAX = 256
Q_BLOCK = 128
ROPE_THETA = 10000.0
GDN_HEADS = 16
GDN_DK = 128
GDN_DV = 128
CONV_WIDTH = 4
CHUNK = 64
N_EXPERTS = 64
TOP_K = 8
N_GROUPS = 8
TOPK_GROUPS = 4
EXPERT_DIM = 512
SHARED_DIM = 512
ROUTED_SCALE = 2.5
LN_EPS = 1e-5
RMS_EPS = 1e-6

ATT_Q = ATT_HEADS * HEAD_DIM
ATT_KV = ATT_KV_HEADS * HEAD_DIM
IDX_Q = IDX_HEADS * IDX_DIM
GDN_QK = GDN_HEADS * GDN_DK
GDN_V = GDN_HEADS * GDN_DV
SPLITS = (ATT_Q, ATT_KV, ATT_KV, IDX_Q, IDX_DIM, IDX_HEADS, GDN_QK, GDN_QK, GDN_V, GDN_V,
          GDN_HEADS, GDN_HEADS, D_MODEL, D_MODEL)
SPLIT_OFFSETS = tuple(int(o) for o in np.cumsum(SPLITS)[:-1])

LANES = 128
VMEM_LIMIT_BYTES = 56 * 1024 * 1024

OFF_AQ = 0
OFF_GQ = OFF_AQ + ATT_Q
OFF_GK = OFF_GQ + GDN_QK
OFF_GV = OFF_GK + GDN_QK
OFF_GZ = OFF_GV + GDN_V
OFF_GTA = OFF_GZ + GDN_V
OFF_GTB = OFF_GTA + D_MODEL
OFF_AK = OFF_GTB + D_MODEL
OFF_AV = OFF_AK + ATT_KV
OFF_IQ = OFF_AV + ATT_KV
OFF_SM = OFF_IQ + IDX_Q
PROJ_W = 16384
SM_IK = 0
SM_IW = SM_IK + IDX_DIM
SM_BETA = SM_IW + IDX_HEADS
SM_DT = SM_BETA + GDN_HEADS

KEY_CHUNK = 512
EXPERT_ROWS = 256
INT_MIN = -2147483648
MASK_BIAS = -1e30
M_INIT = -1e29


def _cparams(n_axes):
    return pltpu.CompilerParams(dimension_semantics=("arbitrary",) * n_axes,
                                vmem_limit_bytes=VMEM_LIMIT_BYTES)


def _nt(a, b, precision=None):
    return lax.dot_general(a, b, (((1,), (1,)), ((), ())), precision=precision,
                           preferred_element_type=F32)


def _tn(a, b, precision=None):
    return lax.dot_general(a, b, (((0,), (0,)), ((), ())), precision=precision,
                           preferred_element_type=F32)


def _sigmoid(x):
    return jax.nn.sigmoid(x)


def _layer_norm(x, g, b):
    mu = jnp.mean(x, axis=-1, keepdims=True)
    xc = x - mu
    var = jnp.mean(xc * xc, axis=-1, keepdims=True)
    return xc * lax.rsqrt(var + LN_EPS) * g + b


def _trig_kernel(pos_ref, inv_a_ref, inv_i_ref, sgn_a_ref, sgn_i_ref, cos_a, sin_a, cos_i, sin_i):
    p = pos_ref[...]
    ang_a = p * inv_a_ref[...]
    cos_a[...] = jnp.cos(ang_a)
    sin_a[...] = jnp.sin(ang_a) * sgn_a_ref[...]
    ang_i = p * inv_i_ref[...]
    cos_i[...] = jnp.cos(ang_i)
    sin_i[...] = jnp.sin(ang_i) * sgn_i_ref[...]


def _trig_tables(positions):
    n = positions.size
    tm = min(512, n)
    pos = positions.reshape(n, 1).astype(F32)
    inv_a = ROPE_THETA ** (-jnp.arange(0, HEAD_DIM, 2, dtype=F32) / HEAD_DIM)
    inv_i = ROPE_THETA ** (-jnp.arange(0, IDX_DIM, 2, dtype=F32) / IDX_DIM)
    half_a, half_i = HEAD_DIM // 2, IDX_DIM // 2
    inv_a_row = jnp.tile(inv_a, 2).reshape(1, LANES)
    inv_i_row = jnp.tile(inv_i, 4).reshape(1, LANES)
    lane = np.arange(LANES)
    sgn_a = jnp.asarray(np.where(lane < half_a, -1.0, 1.0), F32).reshape(1, LANES)
    sgn_i = jnp.asarray(np.where(lane % IDX_DIM < half_i, -1.0, 1.0), F32).reshape(1, LANES)
    row = pl.BlockSpec((1, LANES), lambda i: (0, 0))
    tab = pl.BlockSpec((tm, LANES), lambda i: (i, 0))
    return pl.pallas_call(
        _trig_kernel, grid=(n // tm,),
        in_specs=[pl.BlockSpec((tm, 1), lambda i: (i, 0)), row, row, row, row],
        out_specs=[tab, tab, tab, tab],
        out_shape=[jax.ShapeDtypeStruct((n, LANES), F32)] * 4,
        compiler_params=_cparams(1), name="trig_tables",
    )(pos, inv_a_row, inv_i_row, sgn_a, sgn_i)


def _pack_w_in(w_in):
    (aq, ak, av, iq, ik, iw, gq, gk, gv, gz, gb, ga, gta, gtb) = jnp.split(w_in, SPLIT_OFFSETS, axis=-1)
    used = OFF_SM + IDX_DIM + IDX_HEADS + 2 * GDN_HEADS
    pad = jnp.zeros((w_in.shape[0], PROJ_W - used), w_in.dtype)
    return jnp.concatenate([aq, gq, gk, gv, gz, gta, gtb, ak, av, iq, ik, iw, gb, ga, pad],
                           axis=-1).astype(BF16)


def _proj_kernel(x_ref, w_ref, o_ref, xb_ref):
    @pl.when(pl.program_id(1) == 0)
    def _():
        xb_ref[...] = x_ref[...].astype(BF16)

    o_ref[...] = jnp.dot(xb_ref[...], w_ref[...], preferred_element_type=F32)


def _project(x2d, w_packed):
    n, d = x2d.shape
    tm, tn = min(512, n), 1024
    return pl.pallas_call(
        _proj_kernel, grid=(n // tm, PROJ_W // tn),
        in_specs=[pl.BlockSpec((tm, d), lambda i, j: (i, 0)),
                  pl.BlockSpec((d, tn), lambda i, j: (0, j))],
        out_specs=pl.BlockSpec((tm, tn), lambda i, j: (i, j)),
        out_shape=jax.ShapeDtypeStruct((n, PROJ_W), F32),
        scratch_shapes=[pltpu.VMEM((tm, d), BF16)],
        compiler_params=_cparams(2), name="in_proj",
    )(x2d, w_packed)


def _rope_head(x, cos, sin_signed):
    return x * cos + pltpu.roll(x, HEAD_DIM // 2, 1) * sin_signed


def _rope_idx(x, cos, sin_signed, first_half):
    half = IDX_DIM // 2
    partner = jnp.where(first_half, pltpu.roll(x, LANES - half, 1), pltpu.roll(x, half, 1))
    return x * cos + partner * sin_signed


def _att_prep_kernel(aq_ref, ak_ref, av_ref, iq_ref, sm_ref, cos_a_ref, sin_a_ref, cos_i_ref,
                     sin_i_ref, lng_ref, lnb_ref,
                     q_out, k_out, v_out, iq_out, iklo_out, ikhi_out, iw_out):
    cos_a, sin_a = cos_a_ref[...], sin_a_ref[...]
    cos_i, sin_i = cos_i_ref[...], sin_i_ref[...]
    rows = cos_a.shape[0]
    lane = lax.broadcasted_iota(I32, (rows, LANES), 1)
    first_half = (lane % IDX_DIM) < (IDX_DIM // 2)

    for h in range(ATT_HEADS):
        q_out[0, h] = _rope_head(aq_ref[:, h * HEAD_DIM:(h + 1) * HEAD_DIM], cos_a, sin_a).astype(BF16)
    ones = jnp.ones((rows, HEAD_DIM), BF16)
    for g in range(ATT_KV_HEADS):
        sl = slice(g * HEAD_DIM, (g + 1) * HEAD_DIM)
        k_out[:, sl] = _rope_head(ak_ref[:, sl], cos_a, sin_a).astype(BF16)
        v_out[:, 2 * g * HEAD_DIM:(2 * g + 1) * HEAD_DIM] = av_ref[:, sl].astype(BF16)
        v_out[:, (2 * g + 1) * HEAD_DIM:(2 * g + 2) * HEAD_DIM] = ones
    for j in range(IDX_HEADS // 2):
        iq_out[0, j] = _rope_idx(iq_ref[:, j * LANES:(j + 1) * LANES], cos_i, sin_i, first_half).astype(BF16)

    sm = sm_ref[...]
    in_ik = lane < IDX_DIM
    mu = jnp.sum(jnp.where(in_ik, sm, 0.0), axis=1, keepdims=True) / IDX_DIM
    xc = sm - mu
    var = jnp.sum(jnp.where(in_ik, xc * xc, 0.0), axis=1, keepdims=True) / IDX_DIM
    ik = xc * lax.rsqrt(var + LN_EPS) * lng_ref[...] + lnb_ref[...]
    ik = jnp.where(in_ik, _rope_idx(ik, cos_i, sin_i, first_half), 0.0)
    iklo_out[...] = ik.astype(BF16)
    ikhi_out[...] = pltpu.roll(ik, IDX_DIM, 1).astype(BF16)
    iw_out[...] = (sm * IDX_HEADS ** -0.5) * IDX_DIM ** -0.5


def _att_prep(proj, tabs, idx_ln_g, idx_ln_b):
    n = proj.shape[0]
    tm = Q_BLOCK
    cos_a, sin_a, cos_i, sin_i = tabs
    pad = jnp.zeros((LANES - IDX_DIM,), F32)
    lng = jnp.concatenate([idx_ln_g.astype(F32), pad]).reshape(1, LANES)
    lnb = jnp.concatenate([idx_ln_b.astype(F32), pad]).reshape(1, LANES)
    tab = pl.BlockSpec((tm, LANES), lambda i: (i, 0))
    row = pl.BlockSpec((1, LANES), lambda i: (0, 0))

    def col(width, off):
        return pl.BlockSpec((tm, width), lambda i: (i, off // width))

    return pl.pallas_call(
        _att_prep_kernel, grid=(n // tm,),
        in_specs=[col(ATT_Q, OFF_AQ), col(ATT_KV, OFF_AK), col(ATT_KV, OFF_AV), col(IDX_Q, OFF_IQ),
                  col(LANES, OFF_SM), tab, tab, tab, tab, row, row],
        out_specs=[pl.BlockSpec((1, ATT_HEADS, tm, HEAD_DIM), lambda i: (i, 0, 0, 0)),
                   pl.BlockSpec((tm, ATT_KV), lambda i: (i, 0)),
                   pl.BlockSpec((tm, 2 * ATT_KV), lambda i: (i, 0)),
                   pl.BlockSpec((1, IDX_HEADS // 2, tm, LANES), lambda i: (i, 0, 0, 0)),
                   tab, tab, tab],
        out_shape=[jax.ShapeDtypeStruct((n // tm, ATT_HEADS, tm, HEAD_DIM), BF16),
                   jax.ShapeDtypeStruct((n, ATT_KV), BF16),
                   jax.ShapeDtypeStruct((n, 2 * ATT_KV), BF16),
                   jax.ShapeDtypeStruct((n // tm, IDX_HEADS // 2, tm, LANES), BF16),
                   jax.ShapeDtypeStruct((n, LANES), BF16),
                   jax.ShapeDtypeStruct((n, LANES), BF16),
                   jax.ShapeDtypeStruct((n, LANES), F32)],
        compiler_params=_cparams(1), name="att_prep",
    )(proj, proj, proj, proj, proj, cos_a, sin_a, cos_i, sin_i, lng, lnb)


def _dsa_kernel(q_ref, k_ref, v_ref, iq_ref, iklo_ref, ikhi_ref, iw_ref, o_ref,
                keys_sc, acc_sc, m_sc, wrep_sc, sel_sc, *, topk, seq):
    qb, kc = Q_BLOCK, KEY_CHUNK
    tiles = kc // LANES
    rep = ATT_HEADS // ATT_KV_HEADS
    i = pl.program_id(1)
    nch = (i * qb + qb + kc - 1) // kc
    row = lax.broadcasted_iota(I32, (qb, LANES), 0)
    lane = lax.broadcasted_iota(I32, (qb, LANES), 1)
    t_abs = i * qb + row

    iw = iw_ref[...]
    for h in range(IDX_HEADS):
        wrep_sc[h] = jnp.broadcast_to(iw[:, SM_IW + h:SM_IW + h + 1], (qb, LANES))
    qi = iq_ref[0].reshape((IDX_HEADS // 2) * qb, LANES)

    def score_chunk(c, carry):
        off = pl.multiple_of(c * kc, kc)
        s_lo = _nt(qi, iklo_ref[pl.ds(off, kc), :])
        s_hi = _nt(qi, ikhi_ref[pl.ds(off, kc), :])
        for ct in range(tiles):
            cs = slice(ct * LANES, (ct + 1) * LANES)
            acc = jnp.zeros((qb, LANES), F32)
            for j in range(IDX_HEADS // 2):
                rs = slice(j * qb, (j + 1) * qb)
                acc = acc + wrep_sc[2 * j] * jnp.maximum(s_lo[rs, cs], 0.0)
                acc = acc + wrep_sc[2 * j + 1] * jnp.maximum(s_hi[rs, cs], 0.0)
            bits = lax.bitcast_convert_type(acc, I32)
            key = bits ^ ((bits >> 31) & jnp.int32(0x7FFFFFFF))
            key = jnp.where(bits == jnp.int32(INT_MIN), 0, key)
            causal = (off + ct * LANES + lane) <= t_abs
            keys_sc[c, :, cs] = jnp.where(causal, key, jnp.int32(INT_MIN))
        return carry

    lax.fori_loop(0, nch, score_chunk, 0)

    kth = jnp.float32(topk)

    def count(pred):
        def body(c, cnt):
            kk = keys_sc[c]
            off = c * kc
            for ct in range(tiles):
                cs = slice(ct * LANES, (ct + 1) * LANES)
                cnt = cnt + jnp.where(pred(kk[:, cs], off + ct * LANES + lane), 1.0, 0.0)
            return cnt

        cnt = lax.fori_loop(0, nch, body, jnp.zeros((qb, LANES), F32))
        return jnp.sum(cnt, axis=1, keepdims=True)

    zero = jnp.zeros((qb, LANES), I32)
    base = jnp.where(count(lambda k, s: k >= zero) >= kth, zero, jnp.int32(INT_MIN))

    def bit_step(it, base):
        cand = base | (jnp.int32(1) << (30 - it))
        return jnp.where(count(lambda k, s: k >= cand) >= kth, cand, base)

    thr = lax.fori_loop(0, 31, bit_step, base)
    n_ge = count(lambda k, s: k >= thr)
    n_gt = count(lambda k, s: k > thr)
    need = kth - n_gt
    sel_sc[0] = thr
    sel_sc[1] = jnp.full((qb, LANES), seq, I32)

    @pl.when(jnp.max(n_ge) > kth)
    def _():
        nbits = max(1, (seq - 1).bit_length())

        def idx_step(it, posv):
            cand = posv + (jnp.int32(1) << (nbits - 1 - it))
            below = count(lambda k, s: (k == thr) & (s < cand))
            return jnp.where(below < need, cand, posv)

        sel_sc[1] = lax.fori_loop(0, nbits, idx_step, zero)

    thr = sel_sc[0]
    last_tie = sel_sc[1]

    m_sc[...] = jnp.full(m_sc.shape, M_INIT, F32)
    acc_sc[...] = jnp.zeros(acc_sc.shape, F32)
    scale = HEAD_DIM ** -0.5

    def attend_chunk(c, carry):
        off = pl.multiple_of(c * kc, kc)
        kk = keys_sc[c]
        bias = []
        for ct in range(tiles):
            cs = slice(ct * LANES, (ct + 1) * LANES)
            sidx = off + ct * LANES + lane
            keep = (kk[:, cs] > thr) | ((kk[:, cs] == thr) & (sidx <= last_tie))
            keep = keep & (sidx <= t_abs)
            bias.append(jnp.where(keep, 0.0, MASK_BIAS))
        for g in range(ATT_KV_HEADS):
            kg = k_ref[pl.ds(off, kc), g * HEAD_DIM:(g + 1) * HEAD_DIM]
            vg = v_ref[pl.ds(off, kc), 2 * g * HEAD_DIM:(2 * g + 2) * HEAD_DIM]
            qg = q_ref[0, g * rep:(g + 1) * rep].reshape(rep * qb, HEAD_DIM)
            s = _nt(qg, kg) * scale
            s_t = [(s[:, ct * LANES:(ct + 1) * LANES].reshape(rep, qb, LANES) + bias[ct][None]
                    ).reshape(rep * qb, LANES) for ct in range(tiles)]
            mx = s_t[0]
            for ct in range(1, tiles):
                mx = jnp.maximum(mx, s_t[ct])
            m_old = m_sc[g]
            m_new = jnp.maximum(m_old, jnp.max(mx, axis=1, keepdims=True))
            p = jnp.concatenate([jnp.exp(st - m_new) for st in s_t], axis=1).astype(BF16)
            alpha = jnp.exp(m_old - m_new)
            pv = jnp.dot(p, vg, preferred_element_type=F32)
            acc_sc[g] = acc_sc[g] * jnp.concatenate([alpha, alpha], axis=1) + pv
            m_sc[g] = m_new
        return carry

    lax.fori_loop(0, nch, attend_chunk, 0)

    for g in range(ATT_KV_HEADS):
        for r in range(rep):
            a = acc_sc[g, r * qb:(r + 1) * qb, :]
            h = g * rep + r
            o_ref[:, h * HEAD_DIM:(h + 1) * HEAD_DIM] = (a[:, :HEAD_DIM] / a[:, HEAD_DIM:]).astype(BF16)


def _dsa_attention(q_r, k_r, v_ext, iq_r, ik_lo, ik_hi, iw, batch, seq):
    n = batch * seq
    nqb = seq // Q_BLOCK
    topk = min(TOPK_MAX, seq // 4)
    rep = ATT_HEADS // ATT_KV_HEADS
    kern = functools.partial(_dsa_kernel, topk=topk, seq=seq)

    def per_batch(width):
        return pl.BlockSpec((None, seq, width), lambda b, i: (b, 0, 0))

    return pl.pallas_call(
        kern, grid=(batch, nqb),
        in_specs=[pl.BlockSpec((1, ATT_HEADS, Q_BLOCK, HEAD_DIM), lambda b, i: (b * nqb + i, 0, 0, 0)),
                  per_batch(ATT_KV), per_batch(2 * ATT_KV),
                  pl.BlockSpec((1, IDX_HEADS // 2, Q_BLOCK, LANES), lambda b, i: (b * nqb + i, 0, 0, 0)),
                  per_batch(LANES), per_batch(LANES),
                  pl.BlockSpec((Q_BLOCK, LANES), lambda b, i: (b * nqb + i, 0))],
        out_specs=pl.BlockSpec((Q_BLOCK, ATT_Q), lambda b, i: (b * nqb + i, 0)),
        out_shape=jax.ShapeDtypeStruct((n, ATT_Q), BF16),
        scratch_shapes=[pltpu.VMEM((seq // KEY_CHUNK, Q_BLOCK, KEY_CHUNK), I32),
                        pltpu.VMEM((ATT_KV_HEADS, rep * Q_BLOCK, 2 * HEAD_DIM), F32),
                        pltpu.VMEM((ATT_KV_HEADS, rep * Q_BLOCK, LANES), F32),
                        pltpu.VMEM((IDX_HEADS, Q_BLOCK, LANES), F32),
                        pltpu.VMEM((2, Q_BLOCK, LANES), I32)],
        compiler_params=_cparams(2), name="dsa_attention",
    )(q_r, k_r.reshape(batch, seq, ATT_KV), v_ext.reshape(batch, seq, 2 * ATT_KV), iq_r,
      ik_lo.reshape(batch, seq, LANES), ik_hi.reshape(batch, seq, LANES), iw)


GDN_HEADS_PER_STEP = 4
CONV_HALO = 8


def _gdn_kernel(q_ref, k_ref, v_ref, qp_ref, kp_ref, vp_ref, z_ref, sm_ref, gat_ref,
                cwq_ref, cwk_ref, cwv_ref, alog_row_ref, dtb_row_ref, alog_col_ref, dtb_col_ref, nw_ref,
                o_ref, xx_sc, gct_sc, st_sc):
    hb = GDN_HEADS_PER_STEP
    c = CHUNK
    hg = pl.program_id(1)
    first = pl.program_id(2) == 0

    @pl.when(first)
    def _():
        st_sc[...] = jnp.zeros(st_sc.shape, F32)

    def conv_silu(slot, cur_ref, prev_ref, cw_ref):
        prev = prev_ref[...]
        at_start = jnp.full(prev.shape, pl.program_id(2), I32) == 0
        xx_sc[slot, 0:CONV_HALO, :] = jnp.where(at_start, 0.0, prev)
        xx_sc[slot, CONV_HALO:CONV_HALO + c, :] = cur_ref[...]
        cw = cw_ref[...]
        start = CONV_HALO - (CONV_WIDTH - 1)
        acc = xx_sc[slot, pl.ds(start, c), :] * cw[0:1, :]
        for j in range(1, CONV_WIDTH):
            acc = acc + xx_sc[slot, pl.ds(start + j, c), :] * cw[j:j + 1, :]
        return acc * _sigmoid(acc)

    qa = conv_silu(0, q_ref, qp_ref, cwq_ref)
    ka = conv_silu(1, k_ref, kp_ref, cwk_ref)
    va = conv_silu(2, v_ref, vp_ref, cwv_ref)

    rowi = lax.broadcasted_iota(I32, (c, c), 0)
    coli = lax.broadcasted_iota(I32, (c, c), 1)
    lower = rowi >= coli
    strict = rowi > coli
    eye = jnp.where(rowi == coli, 1.0, 0.0)

    sm = sm_ref[...]
    beta_slab = _sigmoid(sm)
    g_slab = -jnp.exp(alog_row_ref[...]) * jax.nn.softplus(sm + dtb_row_ref[...])
    gc_slab = jnp.dot(jnp.where(lower, 1.0, 0.0), g_slab, precision=HI, preferred_element_type=F32)
    g_t = -jnp.exp(alog_col_ref[:, :c]) * jax.nn.softplus(gat_ref[...] + dtb_col_ref[:, :c])
    gct_sc[...] = jnp.dot(g_t, jnp.where(rowi <= coli, 1.0, 0.0), precision=HI, preferred_element_type=F32)
    lane = lax.broadcasted_iota(I32, (c, LANES), 1)

    for r in range(hb):
        hglob = hg * hb + r
        hs = slice(r * GDN_DK, (r + 1) * GDN_DK)
        gcc = jnp.sum(jnp.where(lane == SM_DT + hglob, gc_slab, 0.0), axis=1, keepdims=True)
        bcol = jnp.sum(jnp.where(lane == SM_BETA + hglob, beta_slab, 0.0), axis=1, keepdims=True)
        gcr = gct_sc[pl.ds(hglob, 1), :]
        glast = gcr[:, c - 1:c]

        q, k, v = qa[:, hs], ka[:, hs], va[:, hs]
        qn = q * lax.rsqrt(jnp.sum(q * q, axis=1, keepdims=True) + RMS_EPS) * GDN_DK ** -0.5
        kn = k * lax.rsqrt(jnp.sum(k * k, axis=1, keepdims=True) + RMS_EPS)
        eg = jnp.exp(gcc)
        decay = jnp.where(lower, jnp.exp(gcc - gcr), 0.0)
        kb = kn * bcol
        kq = _nt(jnp.concatenate([kb, qn], axis=0), kn, precision=HI)
        a_intra = kq[c:] * decay
        x = jnp.where(strict, -(kq[:c] * decay), 0.0)
        t_inv = eye + x
        y = x
        for _ in range(int(math.log2(c)) - 1):
            y = jnp.dot(y, y, precision=HI, preferred_element_type=F32)
            t_inv = t_inv + jnp.dot(t_inv, y, precision=HI, preferred_element_type=F32)
        uw = jnp.dot(t_inv, jnp.concatenate([v * bcol, kb * eg], axis=1), precision=HI,
                     preferred_element_type=F32)
        u, w = uw[:, :GDN_DV], uw[:, GDN_DV:]

        state = st_sc[r]
        wq = jnp.concatenate([w, qn * eg], axis=0).astype(BF16)
        ws_qs = jnp.dot(wq, state.astype(BF16), preferred_element_type=F32)
        v_new = u - ws_qs[:c]
        v_new_b = v_new.astype(BF16)
        o = ws_qs[c:] + jnp.dot(a_intra.astype(BF16), v_new_b, preferred_element_type=F32)
        k_dec = (kn * jnp.exp(glast - gcc)).astype(BF16)
        st_sc[r] = state * jnp.exp(glast) + _tn(k_dec, v_new_b)

        z = z_ref[:, hs]
        o = o * lax.rsqrt(jnp.mean(o * o, axis=1, keepdims=True) + RMS_EPS) * nw_ref[...] * (z * _sigmoid(z))
        o_ref[:, hs] = o.astype(BF16)


def _gated_deltanet(proj, conv_w, a_log, dt_bias, norm_w, batch, seq):
    n = batch * seq
    nc = seq // CHUNK
    hb = GDN_HEADS_PER_STEP
    w = hb * GDN_DK
    ga_t = proj[:, OFF_SM + SM_DT:OFF_SM + SM_DT + GDN_HEADS].reshape(batch * nc, CHUNK, GDN_HEADS)
    ga_t = ga_t.transpose(0, 2, 1)
    lane_pad = lambda vec, off: jnp.zeros((1, LANES), F32).at[0, off:off + GDN_HEADS].set(vec.astype(F32))
    alog_row, dtb_row = lane_pad(a_log, SM_DT), lane_pad(dt_bias, SM_DT)
    alog_col = jnp.broadcast_to(a_log.astype(F32)[:, None], (GDN_HEADS, LANES))
    dtb_col = jnp.broadcast_to(dt_bias.astype(F32)[:, None], (GDN_HEADS, LANES))
    nw = norm_w.astype(F32).reshape(1, GDN_DV)
    cw = conv_w.astype(F32)

    def cur(off):
        return pl.BlockSpec((CHUNK, w), lambda b, h, t: (b * nc + t, off // w + h))

    def prev(off):
        per = CHUNK // CONV_HALO
        return pl.BlockSpec((CONV_HALO, w), lambda b, h, t: (jnp.maximum((b * nc + t) * per - 1, 0), off // w + h))

    def cwspec(off):
        return pl.BlockSpec((CONV_WIDTH, w), lambda b, h, t: (0, off // w + h))

    row = pl.BlockSpec((1, LANES), lambda b, h, t: (0, 0))
    colv = pl.BlockSpec((GDN_HEADS, LANES), lambda b, h, t: (0, 0))
    return pl.pallas_call(
        _gdn_kernel, grid=(batch, GDN_HEADS // hb, nc),
        in_specs=[cur(OFF_GQ), cur(OFF_GK), cur(OFF_GV), prev(OFF_GQ), prev(OFF_GK), prev(OFF_GV),
                  cur(OFF_GZ),
                  pl.BlockSpec((CHUNK, LANES), lambda b, h, t: (b * nc + t, OFF_SM // LANES)),
                  pl.BlockSpec((None, GDN_HEADS, CHUNK), lambda b, h, t: (b * nc + t, 0, 0)),
                  cwspec(0), cwspec(GDN_QK), cwspec(2 * GDN_QK), row, row, colv, colv, row],
        out_specs=pl.BlockSpec((CHUNK, w), lambda b, h, t: (b * nc + t, h)),
        out_shape=jax.ShapeDtypeStruct((n, GDN_V), BF16),
        scratch_shapes=[pltpu.VMEM((3, CONV_HALO + CHUNK, w), F32),
                        pltpu.VMEM((GDN_HEADS, CHUNK), F32),
                        pltpu.VMEM((hb, GDN_DK, GDN_DV), F32)],
        compiler_params=_cparams(3), name="gated_deltanet",
    )(proj, proj, proj, proj, proj, proj, proj, proj, ga_t, cw, cw, cw,
      alog_row, dtb_row, alog_col, dtb_col, nw)


def _merge_kernel(att_ref, gdn_ref, wa_ref, wg_ref, ga_ref, gb_ref, o_ref):
    a = jnp.dot(att_ref[...], wa_ref[...], preferred_element_type=F32)
    g = jnp.dot(gdn_ref[...], wg_ref[...], preferred_element_type=F32)
    o_ref[...] = (_sigmoid(ga_ref[...]) * a + _sigmoid(gb_ref[...]) * g).astype(BF16)


def _merge(att, gdn, w_o_attn, w_o_gdn, proj):
    n = att.shape[0]
    tm, tn = min(512, n), 512
    return pl.pallas_call(
        _merge_kernel, grid=(n // tm, D_MODEL // tn),
        in_specs=[pl.BlockSpec((tm, ATT_Q), lambda i, j: (i, 0)),
                  pl.BlockSpec((tm, GDN_V), lambda i, j: (i, 0)),
                  pl.BlockSpec((ATT_Q, tn), lambda i, j: (0, j)),
                  pl.BlockSpec((GDN_V, tn), lambda i, j: (0, j)),
                  pl.BlockSpec((tm, tn), lambda i, j: (i, OFF_GTA // tn + j)),
                  pl.BlockSpec((tm, tn), lambda i, j: (i, OFF_GTB // tn + j))],
        out_specs=pl.BlockSpec((tm, tn), lambda i, j: (i, j)),
        out_shape=jax.ShapeDtypeStruct((n, D_MODEL), BF16),
        compiler_params=_cparams(2), name="mixer_merge",
    )(att, gdn, w_o_attn.astype(BF16), w_o_gdn.astype(BF16), proj, proj)


def _pack_halves(x):
    w = x.shape[1] // 2
    bits = lax.bitcast_convert_type(x.astype(BF16).astype(F32), U32)
    return (bits[:, :w] >> 16) | (bits[:, w:] & jnp.uint32(0xFFFF0000))


def _unpack_halves(u):
    lo = lax.bitcast_convert_type(u << 16, F32)
    hi = lax.bitcast_convert_type(u & jnp.uint32(0xFFFF0000), F32)
    return lo, hi


def _outproj_ln_kernel(m_ref, w_ref, x_ref, g_ref, b_ref, x1_ref, x1p_ref, *, alpha):
    y = jnp.dot(m_ref[...], w_ref[...], preferred_element_type=F32)
    x1 = _layer_norm(alpha * x_ref[...] + y, g_ref[...], b_ref[...])
    x1_ref[...] = x1
    x1p_ref[...] = _pack_halves(x1)


def _outproj_ln(merged, w_out, x2d, ln_g, ln_b, alpha):
    n, d = x2d.shape
    tm = min(256, n)
    row = pl.BlockSpec((1, d), lambda i: (0, 0))
    tile = pl.BlockSpec((tm, d), lambda i: (i, 0))
    return pl.pallas_call(
        functools.partial(_outproj_ln_kernel, alpha=alpha), grid=(n // tm,),
        in_specs=[tile, pl.BlockSpec((d, d), lambda i: (0, 0)), tile, row, row],
        out_specs=[tile, pl.BlockSpec((tm, d // 2), lambda i: (i, 0))],
        out_shape=[jax.ShapeDtypeStruct((n, d), F32), jax.ShapeDtypeStruct((n, d // 2), U32)],
        compiler_params=_cparams(1), name="outproj_ln",
    )(merged, w_out.astype(BF16), x2d, ln_g.astype(F32).reshape(1, d), ln_b.astype(F32).reshape(1, d))


def _first_max(x, iota, size):
    m = jnp.max(x, axis=0, keepdims=True)
    f = jnp.min(jnp.where(x == m, iota, float(size)), axis=0, keepdims=True)
    return m, f


def _router_kernel(x_ref, w_ref, b_ref, eidx_ref, gate_ref, rank_ref, cnt_ref, carry_sc):
    tm = x_ref.shape[0]
    per = N_EXPERTS // N_GROUPS
    neg = -jnp.inf

    @pl.when(pl.program_id(0) == 0)
    def _():
        carry_sc[...] = jnp.zeros(carry_sc.shape, F32)

    logits = _nt(w_ref[...], x_ref[...], precision=HI)
    s = _sigmoid(logits)
    choice = s + b_ref[:, 0:1]
    io_g = lax.broadcasted_iota(I32, (per, tm), 0).astype(F32)
    io_e = lax.broadcasted_iota(I32, (N_EXPERTS, tm), 0).astype(F32)

    gs = []
    for g in range(N_GROUPS):
        xg = choice[g * per:(g + 1) * per]
        m1, f1 = _first_max(xg, io_g, per)
        m2 = jnp.max(jnp.where(io_g == f1, neg, xg), axis=0, keepdims=True)
        gs.append(m1 + m2)
    gw = jnp.concatenate(gs, axis=0)
    gsel = jnp.zeros((N_GROUPS, tm), F32)
    for _ in range(TOPK_GROUPS):
        _, f = _first_max(gw, io_g, N_GROUPS)
        hit = io_g == f
        gsel = jnp.where(hit, 1.0, gsel)
        gw = jnp.where(hit, neg, gw)
    x = jnp.concatenate([jnp.where(gsel[g:g + 1] > 0.0, choice[g * per:(g + 1) * per], neg)
                         for g in range(N_GROUPS)], axis=0)

    mem = jnp.zeros((N_EXPERTS, tm), F32)
    picks, wsel = [], []
    for _ in range(TOP_K):
        _, f = _first_max(x, io_e, N_EXPERTS)
        hit = io_e == f
        picks.append(f)
        wsel.append(jnp.sum(jnp.where(hit, s, 0.0), axis=0, keepdims=True))
        mem = jnp.where(hit, 1.0, mem)
        x = jnp.where(hit, neg, x)
    wsel = jnp.concatenate(wsel, axis=0)
    gate_ref[...] = wsel / jnp.sum(wsel, axis=0, keepdims=True) * ROUTED_SCALE
    eidx_ref[...] = jnp.concatenate(picks, axis=0).astype(I32)

    tr = lax.broadcasted_iota(I32, (tm, tm), 0)
    tc = lax.broadcasted_iota(I32, (tm, tm), 1)
    before = jnp.where(tr < tc, 1.0, 0.0).astype(BF16)
    prefix = jnp.dot(mem.astype(BF16), before, preferred_element_type=F32) + carry_sc[:, 0:1]
    ranks = [jnp.sum(jnp.where(io_e == f, prefix, 0.0), axis=0, keepdims=True) for f in picks]
    rank_ref[...] = jnp.concatenate(ranks, axis=0).astype(I32)
    carry_sc[...] = carry_sc[...] + jnp.sum(mem, axis=1, keepdims=True)
    cnt_ref[...] = carry_sc[...]


def _router(x1, w_router, router_bias):
    n, d = x1.shape
    tm = min(512, n)
    out = pl.BlockSpec((TOP_K, tm), lambda i: (0, i))
    bias = jnp.broadcast_to(router_bias.astype(F32)[:, None], (N_EXPERTS, LANES))
    return pl.pallas_call(
        _router_kernel, grid=(n // tm,),
        in_specs=[pl.BlockSpec((tm, d), lambda i: (i, 0)),
                  pl.BlockSpec((N_EXPERTS, d), lambda i: (0, 0)),
                  pl.BlockSpec((N_EXPERTS, LANES), lambda i: (0, 0))],
        out_specs=[out, out, out, pl.BlockSpec((N_EXPERTS, LANES), lambda i: (0, 0))],
        out_shape=[jax.ShapeDtypeStruct((TOP_K, n), I32), jax.ShapeDtypeStruct((TOP_K, n), F32),
                   jax.ShapeDtypeStruct((TOP_K, n), I32), jax.ShapeDtypeStruct((N_EXPERTS, LANES), F32)],
        scratch_shapes=[pltpu.VMEM((N_EXPERTS, LANES), F32)],
        compiler_params=_cparams(1), name="router",
    )(x1, w_router.astype(F32).T, bias)


SCATTER_TOKENS = 128


def _scatter_kernel(pos_ref, x_ref, xs_in_ref, xs_ref, sem):
    del xs_in_ref
    tm = x_ref.shape[0]

    def row_copy(t, k):
        return pltpu.make_async_copy(x_ref.at[pl.ds(t, 1), :], xs_ref.at[pl.ds(pos_ref[k, t], 1), :], sem)

    def start(t, carry):
        for k in range(TOP_K):
            row_copy(t, k).start()
        return carry

    def wait(t, carry):
        for k in range(TOP_K):
            row_copy(t, k).wait()
        return carry

    lax.fori_loop(0, tm, start, 0)
    lax.fori_loop(0, tm, wait, 0)


def _scatter_rows(x1p, pos, n_rows):
    n, w = x1p.shape
    tm = min(SCATTER_TOKENS, n)
    zeros = jnp.zeros((n_rows, w), U32)
    return pl.pallas_call(
        _scatter_kernel, grid=(n // tm,),
        in_specs=[pl.BlockSpec((TOP_K, tm), lambda i: (0, i), memory_space=pltpu.SMEM),
                  pl.BlockSpec((tm, w), lambda i: (i, 0)),
                  pl.BlockSpec(memory_space=pl.ANY)],
        out_specs=pl.BlockSpec(memory_space=pl.ANY),
        out_shape=jax.ShapeDtypeStruct((n_rows, w), U32),
        scratch_shapes=[pltpu.SemaphoreType.DMA],
        input_output_aliases={2: 0},
        compiler_params=_cparams(1), name="moe_scatter",
    )(pos, x1p, zeros)


def _expert_kernel(blk_e_ref, blk_on_ref, x_ref, w1_ref, w3_ref, w2_ref, y_ref):
    del blk_e_ref

    @pl.when(blk_on_ref[pl.program_id(0)] > 0)
    def _():
        half = D_MODEL // 2
        lo, hi = _unpack_halves(x_ref[...])
        xa, xb = lo.astype(BF16), hi.astype(BF16)

        def up(w_ref):
            return (jnp.dot(xa, w_ref[0:half, :], preferred_element_type=F32)
                    + jnp.dot(xb, w_ref[half:, :], preferred_element_type=F32))

        h1, h3 = up(w1_ref), up(w3_ref)
        act = (h1 * _sigmoid(h1) * h3).astype(BF16)
        y_ref[...] = _pack_halves(jnp.dot(act, w2_ref[...], preferred_element_type=F32))


def _experts(xs, blk_e, blk_on, w1, w3, w2):
    n_rows, w = xs.shape
    bm = EXPERT_ROWS
    grid_spec = pltpu.PrefetchScalarGridSpec(
        num_scalar_prefetch=2, grid=(n_rows // bm,),
        in_specs=[pl.BlockSpec((bm, w), lambda i, e, on: (i, 0)),
                  pl.BlockSpec((None, D_MODEL, EXPERT_DIM), lambda i, e, on: (e[i], 0, 0)),
                  pl.BlockSpec((None, D_MODEL, EXPERT_DIM), lambda i, e, on: (e[i], 0, 0)),
                  pl.BlockSpec((None, EXPERT_DIM, D_MODEL), lambda i, e, on: (e[i], 0, 0))],
        out_specs=pl.BlockSpec((bm, w), lambda i, e, on: (i, 0)))
    return pl.pallas_call(
        _expert_kernel, grid_spec=grid_spec,
        out_shape=jax.ShapeDtypeStruct((n_rows, w), U32),
        compiler_params=_cparams(1), name="moe_experts",
    )(blk_e, blk_on, xs, w1.astype(BF16), w3.astype(BF16), w2.astype(BF16))


def _shared_up_kernel(x_ref, w_ref, h_ref):
    h = jnp.dot(x_ref[...].astype(BF16), w_ref[...], preferred_element_type=F32)
    h1, h3 = h[:, :SHARED_DIM], h[:, SHARED_DIM:]
    h_ref[...] = (h1 * _sigmoid(h1) * h3).astype(BF16)


def _shared_up(x1, ws1, ws3):
    n, d = x1.shape
    tm = min(512, n)
    w13 = jnp.concatenate([ws1, ws3], axis=1).astype(BF16)
    return pl.pallas_call(
        _shared_up_kernel, grid=(n // tm,),
        in_specs=[pl.BlockSpec((tm, d), lambda i: (i, 0)),
                  pl.BlockSpec((d, 2 * SHARED_DIM), lambda i: (0, 0))],
        out_specs=pl.BlockSpec((tm, SHARED_DIM), lambda i: (i, 0)),
        out_shape=jax.ShapeDtypeStruct((n, SHARED_DIM), BF16),
        compiler_params=_cparams(1), name="shared_up",
    )(x1, w13)


def _combine_kernel(pos_ref, x1_ref, h_ref, ws2_ref, gate_ref, g_ref, b_ref, y_hbm, o_ref, ybuf, sem,
                    *, alpha):
    tm = x1_ref.shape[0]

    def row_copy(t, k):
        return pltpu.make_async_copy(y_hbm.at[pl.ds(pos_ref[k, t], 1), :], ybuf.at[k, pl.ds(t, 1), :], sem)

    def start(t, carry):
        for k in range(TOP_K):
            row_copy(t, k).start()
        return carry

    def wait(t, carry):
        for k in range(TOP_K):
            row_copy(t, k).wait()
        return carry

    lax.fori_loop(0, tm, start, 0)
    base = alpha * x1_ref[...] + jnp.dot(h_ref[...], ws2_ref[...], preferred_element_type=F32)
    lax.fori_loop(0, tm, wait, 0)

    gates = gate_ref[...]
    acc_lo = jnp.zeros((tm, D_MODEL // 2), F32)
    acc_hi = jnp.zeros((tm, D_MODEL // 2), F32)
    for k in range(TOP_K):
        lo, hi = _unpack_halves(ybuf[k])
        gk = gates[:, k:k + 1]
        acc_lo = acc_lo + gk * lo
        acc_hi = acc_hi + gk * hi
    f = jnp.concatenate([acc_lo, acc_hi], axis=1)
    o_ref[...] = _layer_norm(base + f, g_ref[...], b_ref[...])


def _combine(x1, hsh, ws2, pos, gates_t, ys, ln_g, ln_b, alpha):
    n, d = x1.shape
    tm = min(SCATTER_TOKENS, n)
    row = pl.BlockSpec((1, d), lambda i: (0, 0))
    return pl.pallas_call(
        functools.partial(_combine_kernel, alpha=alpha), grid=(n // tm,),
        in_specs=[pl.BlockSpec((TOP_K, tm), lambda i: (0, i), memory_space=pltpu.SMEM),
                  pl.BlockSpec((tm, d), lambda i: (i, 0)),
                  pl.BlockSpec((tm, SHARED_DIM), lambda i: (i, 0)),
                  pl.BlockSpec((SHARED_DIM, d), lambda i: (0, 0)),
                  pl.BlockSpec((tm, TOP_K), lambda i: (i, 0)),
                  row, row,
                  pl.BlockSpec(memory_space=pl.ANY)],
        out_specs=pl.BlockSpec((tm, d), lambda i: (i, 0)),
        out_shape=jax.ShapeDtypeStruct((n, d), F32),
        scratch_shapes=[pltpu.VMEM((TOP_K, tm, d // 2), U32), pltpu.SemaphoreType.DMA],
        compiler_params=_cparams(1), name="moe_combine",
    )(pos, x1, hsh, ws2.astype(BF16), gates_t, ln_g.astype(F32).reshape(1, d),
      ln_b.astype(F32).reshape(1, d), ys)


def _moe(x1, x1p, w_router, router_bias, w1, w3, w2, ws1, ws3, ws2, ln_g, ln_b, alpha):
    n = x1.shape[0]
    bm = EXPERT_ROWS
    eidx, gates, rank, cnt = _router(x1, w_router, router_bias)
    counts = cnt[:, 0].astype(I32)
    padded = (counts + bm - 1) // bm * bm
    pend = jnp.cumsum(padded)
    pstart = pend - padded
    pos = pstart[eidx] + rank
    n_rows = -(-(n * TOP_K) // bm) * bm + N_EXPERTS * bm
    blk_start = jnp.arange(n_rows // bm, dtype=I32) * bm
    blk_on = (blk_start < pend[-1]).astype(I32)
    last = jnp.maximum(pend[-1] - bm, 0)
    blk_e = jnp.minimum(jnp.searchsorted(pend, jnp.minimum(blk_start, last), side="right"),
                        N_EXPERTS - 1).astype(I32)
    xs = _scatter_rows(x1p, pos, n_rows)
    ys = _experts(xs, blk_e, blk_on, w1, w3, w2)
    hsh = _shared_up(x1, ws1, ws3)
    return _combine(x1, hsh, ws2, pos, gates.T, ys, ln_g, ln_b, alpha)


def kernel(x, positions, w_in, conv_w, A_log, dt_bias, gdn_norm_w, idx_ln_g, idx_ln_b, w_o_attn, w_o_gdn, w_out, ln1_g, ln1_b, w_router, router_bias, w1, w3, w2, ws1, ws3, ws2, ln2_g, ln2_b):
    batch, seq, d = x.shape
    depth = w_in.shape[0]
    alpha = (2 * depth) ** 0.25
    n = batch * seq
    xf = x.reshape(n, d)
    tabs = _trig_tables(positions)
    for l in range(depth):
        proj = _project(xf, _pack_w_in(w_in[l]))
        q_r, k_r, v_ext, iq_r, ik_lo, ik_hi, iw = _att_prep(proj, tabs, idx_ln_g[l], idx_ln_b[l])
        att = _dsa_attention(q_r, k_r, v_ext, iq_r, ik_lo, ik_hi, iw, batch, seq)
        gdn = _gated_deltanet(proj, conv_w[l], A_log[l], dt_bias[l], gdn_norm_w[l], batch, seq)
        merged = _merge(att, gdn, w_o_attn[l], w_o_gdn[l], proj)
        x1, x1p = _outproj_ln(merged, w_out[l], xf, ln1_g[l], ln1_b[l], alpha)
        xf = _moe(x1, x1p, w_router[l], router_bias[l], w1[l], w3[l], w2[l], ws1[l], ws3[l], ws2[l],
                  ln2_g[l], ln2_b[l], alpha)
    return xf.reshape(batch, seq, d)
```

```python
import functools
import math

import jax
import jax.numpy as jnp
import numpy as np
from jax import lax
from jax.experimental import pallas as pl
from jax.experimental.pallas import tpu as pltpu

F32 = jnp.float32
BF16 = jnp.bfloat16
I32 = jnp.int32
U32 = jnp.uint32
HI = lax.Precision.HIGHEST

D_MODEL = 2048
ATT_HEADS = 16
ATT_KV_HEADS = 2
HEAD_DIM = 128
IDX_HEADS = 8
IDX_DIM = 64
TOPK_MAX = 256
Q_BLOCK = 128
ROPE_THETA = 10000.0
GDN_HEADS = 16
GDN_DK = 128
GDN_DV = 128
CONV_WIDTH = 4
CHUNK = 64
N_EXPERTS = 64
TOP_K = 8
N_GROUPS = 8
TOPK_GROUPS = 4
EXPERT_DIM = 512
SHARED_DIM = 512
ROUTED_SCALE = 2.5
LN_EPS = 1e-5
RMS_EPS = 1e-6

ATT_Q = ATT_HEADS * HEAD_DIM
ATT_KV = ATT_KV_HEADS * HEAD_DIM
IDX_Q = IDX_HEADS * IDX_DIM
GDN_QK = GDN_HEADS * GDN_DK
GDN_V = GDN_HEADS * GDN_DV
SPLITS = (ATT_Q, ATT_KV, ATT_KV, IDX_Q, IDX_DIM, IDX_HEADS, GDN_QK, GDN_QK, GDN_V, GDN_V,
          GDN_HEADS, GDN_HEADS, D_MODEL, D_MODEL)
SPLIT_OFFSETS = tuple(int(o) for o in np.cumsum(SPLITS)[:-1])

LANES = 128
VMEM_LIMIT_BYTES = 56 * 1024 * 1024

OFF_AQ = 0
OFF_GQ = OFF_AQ + ATT_Q
OFF_GK = OFF_GQ + GDN_QK
OFF_GV = OFF_GK + GDN_QK
OFF_GZ = OFF_GV + GDN_V
OFF_GTA = OFF_GZ + GDN_V
OFF_GTB = OFF_GTA + D_MODEL
OFF_AK = OFF_GTB + D_MODEL
OFF_AV = OFF_AK + ATT_KV
OFF_IQ = OFF_AV + ATT_KV
OFF_SM = OFF_IQ + IDX_Q
PROJ_W = 16384
SM_IK = 0
SM_IW = SM_IK + IDX_DIM
SM_BETA = SM_IW + IDX_HEADS
SM_DT = SM_BETA + GDN_HEADS

KEY_CHUNK = 512
EXPERT_ROWS = 256
INT_MIN = -2147483648
MASK_BIAS = -1e30
M_INIT = -1e29


def _cparams(n_axes):
    return pltpu.CompilerParams(dimension_semantics=("arbitrary",) * n_axes,
                                vmem_limit_bytes=VMEM_LIMIT_BYTES)


def _nt(a, b, precision=None):
    return lax.dot_general(a, b, (((1,), (1,)), ((), ())), precision=precision,
                           preferred_element_type=F32)


def _tn(a, b, precision=None):
    return lax.dot_general(a, b, (((0,), (0,)), ((), ())), precision=precision,
                           preferred_element_type=F32)


def _sigmoid(x):
    return jax.nn.sigmoid(x)


def _layer_norm(x, g, b):
    mu = jnp.mean(x, axis=-1, keepdims=True)
    xc = x - mu
    var = jnp.mean(xc * xc, axis=-1, keepdims=True)
    return xc * lax.rsqrt(var + LN_EPS) * g + b


def _trig_kernel(pos_ref, inv_a_ref, inv_i_ref, sgn_a_ref, sgn_i_ref, cos_a, sin_a, cos_i, sin_i):
    p = pos_ref[...]
    ang_a = p * inv_a_ref[...]
    cos_a[...] = jnp.cos(ang_a)
    sin_a[...] = jnp.sin(ang_a) * sgn_a_ref[...]
    ang_i = p * inv_i_ref[...]
    cos_i[...] = jnp.cos(ang_i)
    sin_i[...] = jnp.sin(ang_i) * sgn_i_ref[...]


def _trig_tables(positions):
    n = positions.size
    tm = min(512, n)
    pos = positions.reshape(n, 1).astype(F32)
    inv_a = ROPE_THETA ** (-jnp.arange(0, HEAD_DIM, 2, dtype=F32) / HEAD_DIM)
    inv_i = ROPE_THETA ** (-jnp.arange(0, IDX_DIM, 2, dtype=F32) / IDX_DIM)
    half_a, half_i = HEAD_DIM // 2, IDX_DIM // 2
    inv_a_row = jnp.tile(inv_a, 2).reshape(1, LANES)
    inv_i_row = jnp.tile(inv_i, 4).reshape(1, LANES)
    lane = np.arange(LANES)
    sgn_a = jnp.asarray(np.where(lane < half_a, -1.0, 1.0), F32).reshape(1, LANES)
    sgn_i = jnp.asarray(np.where(lane % IDX_DIM < half_i, -1.0, 1.0), F32).reshape(1, LANES)
    row = pl.BlockSpec((1, LANES), lambda i: (0, 0))
    tab = pl.BlockSpec((tm, LANES), lambda i: (i, 0))
    return pl.pallas_call(
        _trig_kernel, grid=(n // tm,),
        in_specs=[pl.BlockSpec((tm, 1), lambda i: (i, 0)), row, row, row, row],
        out_specs=[tab, tab, tab, tab],
        out_shape=[jax.ShapeDtypeStruct((n, LANES), F32)] * 4,
        compiler_params=_cparams(1), name="trig_tables",
    )(pos, inv_a_row, inv_i_row, sgn_a, sgn_i)


def _pack_w_in(w_in):
    (aq, ak, av, iq, ik, iw, gq, gk, gv, gz, gb, ga, gta, gtb) = jnp.split(w_in.astype(BF16), SPLIT_OFFSETS,
                                                                             axis=-1)
    used = OFF_SM + IDX_DIM + IDX_HEADS + 2 * GDN_HEADS
    pad = jnp.zeros((w_in.shape[0], PROJ_W - used), BF16)
    return jnp.concatenate([aq, gq, gk, gv, gz, gta, gtb, ak, av, iq, ik, iw, gb, ga, pad], axis=-1)


def _proj_kernel(x_ref, w_ref, o_ref, xb_ref):
    @pl.when(pl.program_id(1) == 0)
    def _():
        xb_ref[...] = x_ref[...].astype(BF16)

    o_ref[...] = jnp.dot(xb_ref[...], w_ref[...], preferred_element_type=F32)


def _project(x2d, w_packed):
    n, d = x2d.shape
    tm, tn = min(1024, n), 1024
    return pl.pallas_call(
        _proj_kernel, grid=(n // tm, PROJ_W // tn),
        in_specs=[pl.BlockSpec((tm, d), lambda i, j: (i, 0)),
                  pl.BlockSpec((d, tn), lambda i, j: (0, j))],
        out_specs=pl.BlockSpec((tm, tn), lambda i, j: (i, j)),
        out_shape=jax.ShapeDtypeStruct((n, PROJ_W), F32),
        scratch_shapes=[pltpu.VMEM((tm, d), BF16)],
        compiler_params=_cparams(2), name="in_proj",
    )(x2d, w_packed)


def _rope_head(x, cos, sin_signed):
    return x * cos + pltpu.roll(x, HEAD_DIM // 2, 1) * sin_signed


def _rope_idx(x, cos, sin_signed, first_half):
    half = IDX_DIM // 2
    partner = jnp.where(first_half, pltpu.roll(x, LANES - half, 1), pltpu.roll(x, half, 1))
    return x * cos + partner * sin_signed


def _att_prep_kernel(aq_ref, ak_ref, av_ref, iq_ref, sm_ref, cos_a_ref, sin_a_ref, cos_i_ref,
                     sin_i_ref, lng_ref, lnb_ref,
                     q_out, k_out, v_out, iq_out, iklo_out, ikhi_out, iw_out):
    cos_a, sin_a = cos_a_ref[...], sin_a_ref[...]
    cos_i, sin_i = cos_i_ref[...], sin_i_ref[...]
    rows = cos_a.shape[0]
    lane = lax.broadcasted_iota(I32, (rows, LANES), 1)
    first_half = (lane % IDX_DIM) < (IDX_DIM // 2)

    for h in range(ATT_HEADS):
        q_out[0, h] = _rope_head(aq_ref[:, h * HEAD_DIM:(h + 1) * HEAD_DIM], cos_a, sin_a).astype(BF16)
    ones = jnp.ones((rows, HEAD_DIM), BF16)
    for g in range(ATT_KV_HEADS):
        sl = slice(g * HEAD_DIM, (g + 1) * HEAD_DIM)
        k_out[:, sl] = _rope_head(ak_ref[:, sl], cos_a, sin_a).astype(BF16)
        v_out[:, 2 * g * HEAD_DIM:(2 * g + 1) * HEAD_DIM] = av_ref[:, sl].astype(BF16)
        v_out[:, (2 * g + 1) * HEAD_DIM:(2 * g + 2) * HEAD_DIM] = ones
    for j in range(IDX_HEADS // 2):
        iq_out[0, j] = _rope_idx(iq_ref[:, j * LANES:(j + 1) * LANES], cos_i, sin_i, first_half).astype(BF16)

    sm = sm_ref[...]
    in_ik = lane < IDX_DIM
    mu = jnp.sum(jnp.where(in_ik, sm, 0.0), axis=1, keepdims=True) / IDX_DIM
    xc = sm - mu
    var = jnp.sum(jnp.where(in_ik, xc * xc, 0.0), axis=1, keepdims=True) / IDX_DIM
    ik = xc * lax.rsqrt(var + LN_EPS) * lng_ref[...] + lnb_ref[...]
    ik = jnp.where(in_ik, _rope_idx(ik, cos_i, sin_i, first_half), 0.0)
    iklo_out[...] = ik.astype(BF16)
    ikhi_out[...] = pltpu.roll(ik, IDX_DIM, 1).astype(BF16)
    iw_out[...] = (sm * IDX_HEADS ** -0.5) * IDX_DIM ** -0.5


def _att_prep(proj, tabs, idx_ln_g, idx_ln_b):
    n = proj.shape[0]
    tm = Q_BLOCK
    cos_a, sin_a, cos_i, sin_i = tabs
    pad = jnp.zeros((LANES - IDX_DIM,), F32)
    lng = jnp.concatenate([idx_ln_g.astype(F32), pad]).reshape(1, LANES)
    lnb = jnp.concatenate([idx_ln_b.astype(F32), pad]).reshape(1, LANES)
    tab = pl.BlockSpec((tm, LANES), lambda i: (i, 0))
    row = pl.BlockSpec((1, LANES), lambda i: (0, 0))

    def col(width, off):
        return pl.BlockSpec((tm, width), lambda i: (i, off // width))

    return pl.pallas_call(
        _att_prep_kernel, grid=(n // tm,),
        in_specs=[col(ATT_Q, OFF_AQ), col(ATT_KV, OFF_AK), col(ATT_KV, OFF_AV), col(IDX_Q, OFF_IQ),
                  col(LANES, OFF_SM), tab, tab, tab, tab, row, row],
        out_specs=[pl.BlockSpec((1, ATT_HEADS, tm, HEAD_DIM), lambda i: (i, 0, 0, 0)),
                   pl.BlockSpec((tm, ATT_KV), lambda i: (i, 0)),
                   pl.BlockSpec((tm, 2 * ATT_KV), lambda i: (i, 0)),
                   pl.BlockSpec((1, IDX_HEADS // 2, tm, LANES), lambda i: (i, 0, 0, 0)),
                   tab, tab, tab],
        out_shape=[jax.ShapeDtypeStruct((n // tm, ATT_HEADS, tm, HEAD_DIM), BF16),
                   jax.ShapeDtypeStruct((n, ATT_KV), BF16),
                   jax.ShapeDtypeStruct((n, 2 * ATT_KV), BF16),
                   jax.ShapeDtypeStruct((n // tm, IDX_HEADS // 2, tm, LANES), BF16),
                   jax.ShapeDtypeStruct((n, LANES), BF16),
                   jax.ShapeDtypeStruct((n, LANES), BF16),
                   jax.ShapeDtypeStruct((n, LANES), F32)],
        compiler_params=_cparams(1), name="att_prep",
    )(proj, proj, proj, proj, proj, cos_a, sin_a, cos_i, sin_i, lng, lnb)


def _dsa_kernel(q_ref, k_ref, v_ref, iq_ref, iklo_ref, ikhi_ref, iw_ref, o_ref,
                keys_sc, acc_sc, m_sc, wrep_sc, sel_sc, *, topk, seq):
    qb, kc = Q_BLOCK, KEY_CHUNK
    tiles = kc // LANES
    rep = ATT_HEADS // ATT_KV_HEADS
    i = pl.program_id(1)
    nch = (i * qb + qb + kc - 1) // kc
    row = lax.broadcasted_iota(I32, (qb, LANES), 0)
    lane = lax.broadcasted_iota(I32, (qb, LANES), 1)
    t_abs = i * qb + row

    iw = iw_ref[...]
    for h in range(IDX_HEADS):
        wrep_sc[h] = jnp.broadcast_to(iw[:, SM_IW + h:SM_IW + h + 1], (qb, LANES))
    qi = iq_ref[0].reshape((IDX_HEADS // 2) * qb, LANES)

    def score_chunk(c, carry):
        off = pl.multiple_of(c * kc, kc)
        s_lo = _nt(qi, iklo_ref[pl.ds(off, kc), :])
        s_hi = _nt(qi, ikhi_ref[pl.ds(off, kc), :])
        for ct in range(tiles):
            cs = slice(ct * LANES, (ct + 1) * LANES)
            acc = jnp.zeros((qb, LANES), F32)
            for j in range(IDX_HEADS // 2):
                rs = slice(j * qb, (j + 1) * qb)
                acc = acc + wrep_sc[2 * j] * jnp.maximum(s_lo[rs, cs], 0.0)
                acc = acc + wrep_sc[2 * j + 1] * jnp.maximum(s_hi[rs, cs], 0.0)
            bits = lax.bitcast_convert_type(acc, I32)
            key = bits ^ ((bits >> 31) & jnp.int32(0x7FFFFFFF))
            key = jnp.where(bits == jnp.int32(INT_MIN), 0, key)
            causal = (off + ct * LANES + lane) <= t_abs
            keys_sc[c, :, cs] = jnp.where(causal, key, jnp.int32(INT_MIN))
        return carry

    lax.fori_loop(0, nch, score_chunk, 0)

    kth = jnp.float32(topk)

    def count(pred):
        def body(c, cnt):
            kk = keys_sc[c]
            off = c * kc
            for ct in range(tiles):
                cs = slice(ct * LANES, (ct + 1) * LANES)
                cnt = cnt + jnp.where(pred(kk[:, cs], off + ct * LANES + lane), 1.0, 0.0)
            return cnt

        cnt = lax.fori_loop(0, nch, body, jnp.zeros((qb, LANES), F32))
        return jnp.sum(cnt, axis=1, keepdims=True)

    zero = jnp.zeros((qb, LANES), I32)
    base = jnp.where(count(lambda k, s: k >= zero) >= kth, zero, jnp.int32(INT_MIN))

    def bit_step(it, base):
        cand = base | (jnp.int32(1) << (30 - it))
        return jnp.where(count(lambda k, s: k >= cand) >= kth, cand, base)

    thr = lax.fori_loop(0, 31, bit_step, base)
    n_ge = count(lambda k, s: k >= thr)
    n_gt = count(lambda k, s: k > thr)
    need = kth - n_gt
    sel_sc[0] = thr
    sel_sc[1] = jnp.full((qb, LANES), seq, I32)

    @pl.when(jnp.max(n_ge) > kth)
    def _():
        nbits = max(1, (seq - 1).bit_length())

        def idx_step(it, posv):
            cand = posv + (jnp.int32(1) << (nbits - 1 - it))
            below = count(lambda k, s: (k == thr) & (s < cand))
            return jnp.where(below < need, cand, posv)

        sel_sc[1] = lax.fori_loop(0, nbits, idx_step, zero)

    thr = sel_sc[0]
    last_tie = sel_sc[1]

    m_sc[...] = jnp.full(m_sc.shape, M_INIT, F32)
    acc_sc[...] = jnp.zeros(acc_sc.shape, F32)
    scale = HEAD_DIM ** -0.5

    def attend_chunk(c, carry):
        off = pl.multiple_of(c * kc, kc)
        kk = keys_sc[c]
        bias = []
        for ct in range(tiles):
            cs = slice(ct * LANES, (ct + 1) * LANES)
            sidx = off + ct * LANES + lane
            keep = (kk[:, cs] > thr) | ((kk[:, cs] == thr) & (sidx <= last_tie))
            keep = keep & (sidx <= t_abs)
            bias.append(jnp.where(keep, 0.0, MASK_BIAS))
        for g in range(ATT_KV_HEADS):
            kg = k_ref[pl.ds(off, kc), g * HEAD_DIM:(g + 1) * HEAD_DIM]
            vg = v_ref[pl.ds(off, kc), 2 * g * HEAD_DIM:(2 * g + 2) * HEAD_DIM]
            qg = q_ref[0, g * rep:(g + 1) * rep].reshape(rep * qb, HEAD_DIM)
            s = _nt(qg, kg) * scale
            s_t = [(s[:, ct * LANES:(ct + 1) * LANES].reshape(rep, qb, LANES) + bias[ct][None]
                    ).reshape(rep * qb, LANES) for ct in range(tiles)]
            mx = s_t[0]
            for ct in range(1, tiles):
                mx = jnp.maximum(mx, s_t[ct])
            m_old = m_sc[g]
            m_new = jnp.maximum(m_old, jnp.max(mx, axis=1, keepdims=True))
            p = jnp.concatenate([jnp.exp(st - m_new) for st in s_t], axis=1).astype(BF16)
            alpha = jnp.exp(m_old - m_new)
            pv = jnp.dot(p, vg, preferred_element_type=F32)
            acc_sc[g] = acc_sc[g] * jnp.concatenate([alpha, alpha], axis=1) + pv
            m_sc[g] = m_new
        return carry

    lax.fori_loop(0, nch, attend_chunk, 0)

    for g in range(ATT_KV_HEADS):
        for r in range(rep):
            a = acc_sc[g, r * qb:(r + 1) * qb, :]
            h = g * rep + r
            o_ref[:, h * HEAD_DIM:(h + 1) * HEAD_DIM] = (a[:, :HEAD_DIM] / a[:, HEAD_DIM:]).astype(BF16)


def _dsa_attention(q_r, k_r, v_ext, iq_r, ik_lo, ik_hi, iw, batch, seq):
    n = batch * seq
    nqb = seq // Q_BLOCK
    topk = min(TOPK_MAX, seq // 4)
    rep = ATT_HEADS // ATT_KV_HEADS
    kern = functools.partial(_dsa_kernel, topk=topk, seq=seq)

    def per_batch(width):
        return pl.BlockSpec((None, seq, width), lambda b, i: (b, 0, 0))

    return pl.pallas_call(
        kern, grid=(batch, nqb),
        in_specs=[pl.BlockSpec((1, ATT_HEADS, Q_BLOCK, HEAD_DIM), lambda b, i: (b * nqb + i, 0, 0, 0)),
                  per_batch(ATT_KV), per_batch(2 * ATT_KV),
                  pl.BlockSpec((1, IDX_HEADS // 2, Q_BLOCK, LANES), lambda b, i: (b * nqb + i, 0, 0, 0)),
                  per_batch(LANES), per_batch(LANES),
                  pl.BlockSpec((Q_BLOCK, LANES), lambda b, i: (b * nqb + i, 0))],
        out_specs=pl.BlockSpec((Q_BLOCK, ATT_Q), lambda b, i: (b * nqb + i, 0)),
        out_shape=jax.ShapeDtypeStruct((n, ATT_Q), BF16),
        scratch_shapes=[pltpu.VMEM((seq // KEY_CHUNK, Q_BLOCK, KEY_CHUNK), I32),
                        pltpu.VMEM((ATT_KV_HEADS, rep * Q_BLOCK, 2 * HEAD_DIM), F32),
                        pltpu.VMEM((ATT_KV_HEADS, rep * Q_BLOCK, LANES), F32),
                        pltpu.VMEM((IDX_HEADS, Q_BLOCK, LANES), F32),
                        pltpu.VMEM((2, Q_BLOCK, LANES), I32)],
        compiler_params=_cparams(2), name="dsa_attention",
    )(q_r, k_r.reshape(batch, seq, ATT_KV), v_ext.reshape(batch, seq, 2 * ATT_KV), iq_r,
      ik_lo.reshape(batch, seq, LANES), ik_hi.reshape(batch, seq, LANES), iw)


GDN_PREP_HEADS = 8
CONV_HALO = 8


def _split2(x):
    hi = x.astype(BF16)
    return hi, (x - hi.astype(F32)).astype(BF16)


def _split3(x):
    hi = x.astype(BF16)
    r = x - hi.astype(F32)
    mid = r.astype(BF16)
    return hi, mid, (r - mid.astype(F32)).astype(BF16)


def _lhs3(hi, lo):
    return jnp.concatenate([hi, lo, hi], axis=1)


def _rhs3(hi, lo):
    return jnp.concatenate([hi, hi, lo], axis=0)


def _gdn_prep_kernel(q_ref, k_ref, v_ref, qp_ref, kp_ref, vp_ref, sm_ref, gat_ref,
                     cwq_ref, cwk_ref, cwv_ref, alog_row_ref, dtb_row_ref, alog_col_ref, dtb_col_ref,
                     u_ref, w_ref, qd_ref, kd_ref, a_ref, e_ref, xx_sc, gct_sc, *, nc):
    hb = GDN_PREP_HEADS
    c = CHUNK
    hg = pl.program_id(1)
    chunk = pl.program_id(0) % nc
    heads = range(hb)

    def conv_silu(slot, cur_ref, prev_ref, cw_ref):
        prev = prev_ref[...]
        at_start = jnp.full(prev.shape, chunk, I32) == 0
        xx_sc[slot, 0:CONV_HALO, :] = jnp.where(at_start, 0.0, prev)
        xx_sc[slot, CONV_HALO:CONV_HALO + c, :] = cur_ref[...]
        cw = cw_ref[...]
        start = CONV_HALO - (CONV_WIDTH - 1)
        acc = xx_sc[slot, pl.ds(start, c), :] * cw[0:1, :]
        for j in range(1, CONV_WIDTH):
            acc = acc + xx_sc[slot, pl.ds(start + j, c), :] * cw[j:j + 1, :]
        return acc * _sigmoid(acc)

    qa = conv_silu(0, q_ref, qp_ref, cwq_ref)
    ka = conv_silu(1, k_ref, kp_ref, cwk_ref)
    va = conv_silu(2, v_ref, vp_ref, cwv_ref)

    rowi = lax.broadcasted_iota(I32, (c, c), 0)
    coli = lax.broadcasted_iota(I32, (c, c), 1)
    lower = rowi >= coli
    strict = rowi > coli
    eye = jnp.where(rowi == coli, 1.0, 0.0)

    sm = sm_ref[...]
    beta_slab = _sigmoid(sm)
    g_slab = -jnp.exp(alog_row_ref[...]) * jax.nn.softplus(sm + dtb_row_ref[...])
    ones_l = jnp.where(lower, 1.0, 0.0).astype(BF16)
    ones_u = jnp.where(rowi <= coli, 1.0, 0.0).astype(BF16)
    gc_slab = jnp.dot(jnp.concatenate([ones_l] * 3, axis=1), jnp.concatenate(_split3(g_slab), axis=0),
                      preferred_element_type=F32)
    g_t = -jnp.exp(alog_col_ref[:, :c]) * jax.nn.softplus(gat_ref[...] + dtb_col_ref[:, :c])
    gct_sc[...] = jnp.dot(jnp.concatenate(_split3(g_t), axis=1), jnp.concatenate([ones_u] * 3, axis=0),
                          preferred_element_type=F32)
    lane = lax.broadcasted_iota(I32, (c, LANES), 1)
    hsl = [slice(r * GDN_DK, (r + 1) * GDN_DK) for r in heads]

    gcc = [jnp.sum(jnp.where(lane == SM_DT + hg * hb + r, gc_slab, 0.0), axis=1, keepdims=True) for r in heads]
    bcol = [jnp.sum(jnp.where(lane == SM_BETA + hg * hb + r, beta_slab, 0.0), axis=1, keepdims=True)
            for r in heads]
    gcr = [gct_sc[pl.ds(hg * hb + r, 1), :] for r in heads]
    glast = [g[:, c - 1:c] for g in gcr]
    qn = [qa[:, s] * lax.rsqrt(jnp.sum(qa[:, s] * qa[:, s], axis=1, keepdims=True) + RMS_EPS) * GDN_DK ** -0.5
          for s in hsl]
    kn = [ka[:, s] * lax.rsqrt(jnp.sum(ka[:, s] * ka[:, s], axis=1, keepdims=True) + RMS_EPS) for s in hsl]
    eg = [jnp.exp(g) for g in gcc]
    decay = [jnp.where(lower, jnp.exp(gcc[r] - gcr[r]), 0.0) for r in heads]
    kb = [kn[r] * bcol[r] for r in heads]

    def nt3(a, b):
        (ah, al), (bh, bl) = _split2(a), _split2(b)
        return _nt(_lhs3(ah, al), jnp.concatenate([bh, bh, bl], axis=1))

    kq = [nt3(jnp.concatenate([kb[r], qn[r]], axis=0), kn[r]) for r in heads]
    y = [jnp.where(strict, -(kq[r][:c] * decay[r]), 0.0) for r in heads]
    t_inv = [eye + y[r] for r in heads]
    ys = [_split2(v) for v in y]
    y = [jnp.dot(_lhs3(*ys[r]), _rhs3(*ys[r]), preferred_element_type=F32) for r in heads]
    n_fac = int(math.log2(c))
    for j in range(1, n_fac):
        ys = [_split2(v) for v in y]
        ts = [_split2(v) for v in t_inv]
        if j < n_fac - 1:
            prod = [jnp.dot(jnp.concatenate([_lhs3(*ys[r]), _lhs3(*ts[r])], axis=0), _rhs3(*ys[r]),
                            preferred_element_type=F32) for r in heads]
            y = [p[:c] for p in prod]
            t_inv = [t_inv[r] + prod[r][c:] for r in heads]
        else:
            t_inv = [t_inv[r] + jnp.dot(_lhs3(*ts[r]), _rhs3(*ys[r]), preferred_element_type=F32)
                     for r in heads]
    uw = [jnp.dot(_lhs3(*_split2(t_inv[r])),
                  _rhs3(*_split2(jnp.concatenate([va[:, hsl[r]] * bcol[r], kb[r] * eg[r]], axis=1))),
                  preferred_element_type=F32) for r in heads]

    for r in heads:
        u_ref[:, hsl[r]] = uw[r][:, :GDN_DV]
        w_ref[:, hsl[r]] = uw[r][:, GDN_DV:].astype(BF16)
        qd_ref[:, hsl[r]] = (qn[r] * eg[r]).astype(BF16)
        kd_ref[:, hsl[r]] = (kn[r] * jnp.exp(glast[r] - gcc[r])).astype(BF16)
        a_ref[:, r * c:(r + 1) * c] = (kq[r][c:] * decay[r]).astype(BF16)
        e_ref[:, hsl[r]] = jnp.broadcast_to(jnp.exp(glast[r]), (CONV_HALO, GDN_DK))


def _gdn_scan_kernel(u_ref, w_ref, qd_ref, kd_ref, a_ref, e_ref, z_ref, nw_ref, o_ref, st_sc):
    c = CHUNK
    heads = range(GDN_HEADS)
    hsl = [slice(h * GDN_DK, (h + 1) * GDN_DK) for h in heads]

    @pl.when(pl.program_id(1) == 0)
    def _():
        st_sc[...] = jnp.zeros(st_sc.shape, F32)

    st = [st_sc[h] for h in heads]
    ws_qs = [jnp.dot(jnp.concatenate([w_ref[:, hsl[h]], qd_ref[:, hsl[h]]], axis=0), st[h].astype(BF16),
                     preferred_element_type=F32) for h in heads]
    v_new = [(u_ref[:, hsl[h]] - ws_qs[h][:c]).astype(BF16) for h in heads]
    o = [ws_qs[h][c:] + jnp.dot(a_ref[:, h * c:(h + 1) * c], v_new[h], preferred_element_type=F32)
         for h in heads]
    for h in heads:
        st_sc[h] = st[h] * e_ref[0:1, hsl[h]] + _tn(kd_ref[:, hsl[h]], v_new[h])
    for h in heads:
        z = z_ref[:, hsl[h]]
        on = o[h] * lax.rsqrt(jnp.mean(o[h] * o[h], axis=1, keepdims=True) + RMS_EPS)
        o_ref[:, hsl[h]] = (on * nw_ref[...] * (z * _sigmoid(z))).astype(BF16)


def _gated_deltanet(proj, conv_w, a_log, dt_bias, norm_w, batch, seq):
    n = batch * seq
    nc = seq // CHUNK
    hb = GDN_PREP_HEADS
    w = hb * GDN_DK
    ga_t = proj[:, OFF_SM + SM_DT:OFF_SM + SM_DT + GDN_HEADS].reshape(batch * nc, CHUNK, GDN_HEADS)
    ga_t = ga_t.transpose(0, 2, 1)
    lane_pad = lambda vec, off: jnp.zeros((1, LANES), F32).at[0, off:off + GDN_HEADS].set(vec.astype(F32))
    alog_row, dtb_row = lane_pad(a_log, SM_DT), lane_pad(dt_bias, SM_DT)
    alog_col = jnp.broadcast_to(a_log.astype(F32)[:, None], (GDN_HEADS, LANES))
    dtb_col = jnp.broadcast_to(dt_bias.astype(F32)[:, None], (GDN_HEADS, LANES))
    nw = norm_w.astype(F32).reshape(1, GDN_DV)
    cw = conv_w.astype(F32)

    def cur(off):
        return pl.BlockSpec((CHUNK, w), lambda i, h: (i, off // w + h))

    def prev(off):
        per = CHUNK // CONV_HALO
        return pl.BlockSpec((CONV_HALO, w), lambda i, h: (jnp.maximum(i * per - 1, 0), off // w + h))

    def cwspec(off):
        return pl.BlockSpec((CONV_WIDTH, w), lambda i, h: (0, off // w + h))

    row = pl.BlockSpec((1, LANES), lambda i, h: (0, 0))
    colv = pl.BlockSpec((GDN_HEADS, LANES), lambda i, h: (0, 0))
    head_tile = pl.BlockSpec((CHUNK, w), lambda i, h: (i, h))
    u, wm, qd, kd, a_in, egl = pl.pallas_call(
        functools.partial(_gdn_prep_kernel, nc=nc), grid=(batch * nc, GDN_HEADS // hb),
        in_specs=[cur(OFF_GQ), cur(OFF_GK), cur(OFF_GV), prev(OFF_GQ), prev(OFF_GK), prev(OFF_GV),
                  pl.BlockSpec((CHUNK, LANES), lambda i, h: (i, OFF_SM // LANES)),
                  pl.BlockSpec((None, GDN_HEADS, CHUNK), lambda i, h: (i, 0, 0)),
                  cwspec(0), cwspec(GDN_QK), cwspec(2 * GDN_QK), row, row, colv, colv],
        out_specs=[head_tile, head_tile, head_tile, head_tile,
                   pl.BlockSpec((CHUNK, hb * CHUNK), lambda i, h: (i, h)),
                   pl.BlockSpec((CONV_HALO, w), lambda i, h: (i, h))],
        out_shape=[jax.ShapeDtypeStruct((n, GDN_V), F32), jax.ShapeDtypeStruct((n, GDN_V), BF16),
                   jax.ShapeDtypeStruct((n, GDN_QK), BF16), jax.ShapeDtypeStruct((n, GDN_QK), BF16),
                   jax.ShapeDtypeStruct((n, GDN_HEADS * CHUNK), BF16),
                   jax.ShapeDtypeStruct((batch * nc * CONV_HALO, GDN_V), F32)],
        scratch_shapes=[pltpu.VMEM((3, CONV_HALO + CHUNK, w), F32),
                        pltpu.VMEM((GDN_HEADS, CHUNK), F32)],
        compiler_params=_cparams(2), name="gdn_prep",
    )(proj, proj, proj, proj, proj, proj, proj, ga_t, cw, cw, cw, alog_row, dtb_row, alog_col, dtb_col)

    def full(width):
        return pl.BlockSpec((CHUNK, width), lambda b, t: (b * nc + t, 0))

    return pl.pallas_call(
        _gdn_scan_kernel, grid=(batch, nc),
        in_specs=[full(GDN_V), full(GDN_V), full(GDN_QK), full(GDN_QK), full(GDN_HEADS * CHUNK),
                  pl.BlockSpec((CONV_HALO, GDN_V), lambda b, t: (b * nc + t, 0)),
                  pl.BlockSpec((CHUNK, GDN_V), lambda b, t: (b * nc + t, OFF_GZ // GDN_V)),
                  pl.BlockSpec((1, GDN_DV), lambda b, t: (0, 0))],
        out_specs=full(GDN_V),
        out_shape=jax.ShapeDtypeStruct((n, GDN_V), BF16),
        scratch_shapes=[pltpu.VMEM((GDN_HEADS, GDN_DK, GDN_DV), F32)],
        compiler_params=_cparams(2), name="gdn_scan",
    )(u, wm, qd, kd, a_in, egl, proj, nw)


def _merge_kernel(att_ref, gdn_ref, wa_ref, wg_ref, ga_ref, gb_ref, o_ref):
    a = jnp.dot(att_ref[...], wa_ref[...], preferred_element_type=F32)
    g = jnp.dot(gdn_ref[...], wg_ref[...], preferred_element_type=F32)
    o_ref[...] = (_sigmoid(ga_ref[...]) * a + _sigmoid(gb_ref[...]) * g).astype(BF16)


def _merge(att, gdn, w_o_attn, w_o_gdn, proj):
    n = att.shape[0]
    tm, tn = min(512, n), 512
    return pl.pallas_call(
        _merge_kernel, grid=(n // tm, D_MODEL // tn),
        in_specs=[pl.BlockSpec((tm, ATT_Q), lambda i, j: (i, 0)),
                  pl.BlockSpec((tm, GDN_V), lambda i, j: (i, 0)),
                  pl.BlockSpec((ATT_Q, tn), lambda i, j: (0, j)),
                  pl.BlockSpec((GDN_V, tn), lambda i, j: (0, j)),
                  pl.BlockSpec((tm, tn), lambda i, j: (i, OFF_GTA // tn + j)),
                  pl.BlockSpec((tm, tn), lambda i, j: (i, OFF_GTB // tn + j))],
        out_specs=pl.BlockSpec((tm, tn), lambda i, j: (i, j)),
        out_shape=jax.ShapeDtypeStruct((n, D_MODEL), BF16),
        compiler_params=_cparams(2), name="mixer_merge",
    )(att, gdn, w_o_attn.astype(BF16), w_o_gdn.astype(BF16), proj, proj)


def _pack_halves(x):
    w = x.shape[1] // 2
    bits = lax.bitcast_convert_type(x.astype(BF16).astype(F32), U32)
    return (bits[:, :w] >> 16) | (bits[:, w:] & jnp.uint32(0xFFFF0000))


def _unpack_halves(u):
    lo = lax.bitcast_convert_type(u << 16, F32)
    hi = lax.bitcast_convert_type(u & jnp.uint32(0xFFFF0000), F32)
    return lo, hi


def _outproj_ln_kernel(m_ref, w_ref, x_ref, g_ref, b_ref, x1_ref, x1p_ref, *, alpha):
    y = jnp.dot(m_ref[...], w_ref[...], preferred_element_type=F32)
    x1 = _layer_norm(alpha * x_ref[...] + y, g_ref[...], b_ref[...])
    x1_ref[...] = x1
    x1p_ref[...] = _pack_halves(x1)


def _outproj_ln(merged, w_out, x2d, ln_g, ln_b, alpha):
    n, d = x2d.shape
    tm = min(256, n)
    row = pl.BlockSpec((1, d), lambda i: (0, 0))
    tile = pl.BlockSpec((tm, d), lambda i: (i, 0))
    return pl.pallas_call(
        functools.partial(_outproj_ln_kernel, alpha=alpha), grid=(n // tm,),
        in_specs=[tile, pl.BlockSpec((d, d), lambda i: (0, 0)), tile, row, row],
        out_specs=[tile, pl.BlockSpec((tm, d // 2), lambda i: (i, 0))],
        out_shape=[jax.ShapeDtypeStruct((n, d), F32), jax.ShapeDtypeStruct((n, d // 2), U32)],
        compiler_params=_cparams(1), name="outproj_ln",
    )(merged, w_out.astype(BF16), x2d, ln_g.astype(F32).reshape(1, d), ln_b.astype(F32).reshape(1, d))


def _first_max(x, iota, size):
    m = jnp.max(x, axis=0, keepdims=True)
    f = jnp.min(jnp.where(x == m, iota, float(size)), axis=0, keepdims=True)
    return m, f


def _router_kernel(x_ref, w_ref, b_ref, eidx_ref, gate_ref, rank_ref, cnt_ref, carry_sc):
    tm = x_ref.shape[0]
    per = N_EXPERTS // N_GROUPS
    neg = -jnp.inf

    @pl.when(pl.program_id(0) == 0)
    def _():
        carry_sc[...] = jnp.zeros(carry_sc.shape, F32)

    logits = _nt(w_ref[...], x_ref[...], precision=HI)
    s = _sigmoid(logits)
    choice = s + b_ref[:, 0:1]
    io_g = lax.broadcasted_iota(I32, (per, tm), 0).astype(F32)
    io_e = lax.broadcasted_iota(I32, (N_EXPERTS, tm), 0).astype(F32)

    gs = []
    for g in range(N_GROUPS):
        xg = choice[g * per:(g + 1) * per]
        m1, f1 = _first_max(xg, io_g, per)
        m2 = jnp.max(jnp.where(io_g == f1, neg, xg), axis=0, keepdims=True)
        gs.append(m1 + m2)
    gw = jnp.concatenate(gs, axis=0)
    gsel = jnp.zeros((N_GROUPS, tm), F32)
    for _ in range(TOPK_GROUPS):
        _, f = _first_max(gw, io_g, N_GROUPS)
        hit = io_g == f
        gsel = jnp.where(hit, 1.0, gsel)
        gw = jnp.where(hit, neg, gw)
    x = jnp.concatenate([jnp.where(gsel[g:g + 1] > 0.0, choice[g * per:(g + 1) * per], neg)
                         for g in range(N_GROUPS)], axis=0)

    mem = jnp.zeros((N_EXPERTS, tm), F32)
    picks, wsel = [], []
    for _ in range(TOP_K):
        _, f = _first_max(x, io_e, N_EXPERTS)
        hit = io_e == f
        picks.append(f)
        wsel.append(jnp.sum(jnp.where(hit, s, 0.0), axis=0, keepdims=True))
        mem = jnp.where(hit, 1.0, mem)
        x = jnp.where(hit, neg, x)
    wsel = jnp.concatenate(wsel, axis=0)
    gate_ref[...] = wsel / jnp.sum(wsel, axis=0, keepdims=True) * ROUTED_SCALE
    eidx_ref[...] = jnp.concatenate(picks, axis=0).astype(I32)

    tr = lax.broadcasted_iota(I32, (tm, tm), 0)
    tc = lax.broadcasted_iota(I32, (tm, tm), 1)
    before = jnp.where(tr < tc, 1.0, 0.0).astype(BF16)
    prefix = jnp.dot(mem.astype(BF16), before, preferred_element_type=F32) + carry_sc[:, 0:1]
    ranks = [jnp.sum(jnp.where(io_e == f, prefix, 0.0), axis=0, keepdims=True) for f in picks]
    rank_ref[...] = jnp.concatenate(ranks, axis=0).astype(I32)
    carry_sc[...] = carry_sc[...] + jnp.sum(mem, axis=1, keepdims=True)
    cnt_ref[...] = carry_sc[...]


def _router(x1, w_router, router_bias):
    n, d = x1.shape
    tm = min(512, n)
    out = pl.BlockSpec((TOP_K, tm), lambda i: (0, i))
    bias = jnp.broadcast_to(router_bias.astype(F32)[:, None], (N_EXPERTS, LANES))
    return pl.pallas_call(
        _router_kernel, grid=(n // tm,),
        in_specs=[pl.BlockSpec((tm, d), lambda i: (i, 0)),
                  pl.BlockSpec((N_EXPERTS, d), lambda i: (0, 0)),
                  pl.BlockSpec((N_EXPERTS, LANES), lambda i: (0, 0))],
        out_specs=[out, out, out, pl.BlockSpec((N_EXPERTS, LANES), lambda i: (0, 0))],
        out_shape=[jax.ShapeDtypeStruct((TOP_K, n), I32), jax.ShapeDtypeStruct((TOP_K, n), F32),
                   jax.ShapeDtypeStruct((TOP_K, n), I32), jax.ShapeDtypeStruct((N_EXPERTS, LANES), F32)],
        scratch_shapes=[pltpu.VMEM((N_EXPERTS, LANES), F32)],
        compiler_params=_cparams(1), name="router",
    )(x1, w_router.astype(F32).T, bias)


SCATTER_TOKENS = 128


def _pos_kernel(pstart_ref, eidx_ref, rank_ref, pos_ref):
    e = eidx_ref[...]
    acc = rank_ref[...]
    for x in range(N_EXPERTS):
        acc = acc + jnp.where(e == x, pstart_ref[x], 0)
    pos_ref[...] = acc


def _positions(pstart, eidx, rank):
    n = eidx.shape[1]
    tm = min(2048, n)
    tile = pl.BlockSpec((TOP_K, tm), lambda i, ps: (0, i))
    return pl.pallas_call(
        _pos_kernel,
        grid_spec=pltpu.PrefetchScalarGridSpec(num_scalar_prefetch=1, grid=(n // tm,),
                                               in_specs=[tile, tile], out_specs=tile),
        out_shape=jax.ShapeDtypeStruct((TOP_K, n), I32),
        compiler_params=_cparams(1), name="moe_positions",
    )(pstart, eidx, rank)


def _scatter_kernel(pend_ref, padded_ref, pos_ref, x_ref, xs_ref, zero_sc, sem, zsem):
    tm = x_ref.shape[0]
    bm = EXPERT_ROWS

    @pl.when(pl.program_id(0) == 0)
    def _():
        zero_sc[...] = jnp.zeros(zero_sc.shape, U32)

        def tail_copy(e):
            tail = pl.multiple_of(pend_ref[e] - bm, bm)
            return pltpu.make_async_copy(zero_sc, xs_ref.at[pl.ds(tail, bm), :], zsem)

        def zstart(e, carry):
            @pl.when(padded_ref[e] > 0)
            def _():
                tail_copy(e).start()
            return carry

        def zwait(e, carry):
            @pl.when(padded_ref[e] > 0)
            def _():
                tail_copy(e).wait()
            return carry

        lax.fori_loop(0, N_EXPERTS, zstart, 0)
        lax.fori_loop(0, N_EXPERTS, zwait, 0)

    def row_copy(t, k):
        return pltpu.make_async_copy(x_ref.at[pl.ds(t, 1), :], xs_ref.at[pl.ds(pos_ref[k, t], 1), :], sem)

    def start(t, carry):
        for k in range(TOP_K):
            row_copy(t, k).start(priority=k % 2)
        return carry

    def wait(t, carry):
        for k in range(TOP_K):
            row_copy(t, k).wait()
        return carry

    lax.fori_loop(0, tm, start, 0)
    lax.fori_loop(0, tm, wait, 0)


def _scatter_rows(x1p, pos, pend, padded, n_rows):
    n, w = x1p.shape
    tm = min(SCATTER_TOKENS, n)
    grid_spec = pltpu.PrefetchScalarGridSpec(
        num_scalar_prefetch=2, grid=(n // tm,),
        in_specs=[pl.BlockSpec((TOP_K, tm), lambda i, pe, pd: (0, i), memory_space=pltpu.SMEM),
                  pl.BlockSpec((tm, w), lambda i, pe, pd: (i, 0))],
        out_specs=pl.BlockSpec(memory_space=pl.ANY),
        scratch_shapes=[pltpu.VMEM((EXPERT_ROWS, w), U32), pltpu.SemaphoreType.DMA, pltpu.SemaphoreType.DMA])
    return pl.pallas_call(
        _scatter_kernel, grid_spec=grid_spec,
        out_shape=jax.ShapeDtypeStruct((n_rows, w), U32),
        compiler_params=_cparams(1), name="moe_scatter",
    )(pend, padded, pos, x1p)


def _expert_kernel(blk_e_ref, blk_on_ref, blk_new_ref, x_ref, w1_ref, w3_ref, w2_ref, y_ref,
                   w1_sc, w3_sc, w2_sc):
    del blk_e_ref
    i = pl.program_id(0)

    @pl.when(blk_new_ref[i] > 0)
    def _():
        w1_sc[...] = w1_ref[...].astype(BF16)
        w3_sc[...] = w3_ref[...].astype(BF16)
        w2_sc[...] = w2_ref[...].astype(BF16)

    @pl.when(blk_on_ref[i] > 0)
    def _():
        half = D_MODEL // 2
        lo, hi = _unpack_halves(x_ref[...])
        xa, xb = lo.astype(BF16), hi.astype(BF16)

        def up(w_sc):
            return (jnp.dot(xa, w_sc[0:half, :], preferred_element_type=F32)
                    + jnp.dot(xb, w_sc[half:, :], preferred_element_type=F32))

        h1, h3 = up(w1_sc), up(w3_sc)
        act = (h1 * _sigmoid(h1) * h3).astype(BF16)
        y_ref[...] = _pack_halves(jnp.dot(act, w2_sc[...], preferred_element_type=F32))


def _experts(xs, blk_e, blk_on, blk_new, w1, w3, w2):
    n_rows, w = xs.shape
    bm = EXPERT_ROWS
    grid_spec = pltpu.PrefetchScalarGridSpec(
        num_scalar_prefetch=3, grid=(n_rows // bm,),
        in_specs=[pl.BlockSpec((bm, w), lambda i, e, on, nw: (i, 0)),
                  pl.BlockSpec((None, D_MODEL, EXPERT_DIM), lambda i, e, on, nw: (e[i], 0, 0)),
                  pl.BlockSpec((None, D_MODEL, EXPERT_DIM), lambda i, e, on, nw: (e[i], 0, 0)),
                  pl.BlockSpec((None, EXPERT_DIM, D_MODEL), lambda i, e, on, nw: (e[i], 0, 0))],
        out_specs=pl.BlockSpec((bm, w), lambda i, e, on, nw: (i, 0)),
        scratch_shapes=[pltpu.VMEM((D_MODEL, EXPERT_DIM), BF16), pltpu.VMEM((D_MODEL, EXPERT_DIM), BF16),
                        pltpu.VMEM((EXPERT_DIM, D_MODEL), BF16)])
    return pl.pallas_call(
        _expert_kernel, grid_spec=grid_spec,
        out_shape=jax.ShapeDtypeStruct((n_rows, w), U32),
        compiler_params=_cparams(1), name="moe_experts",
    )(blk_e, blk_on, blk_new, xs, w1, w3, w2)


def _shared_up_kernel(x_ref, w_ref, h_ref):
    h = jnp.dot(x_ref[...].astype(BF16), w_ref[...], preferred_element_type=F32)
    h1, h3 = h[:, :SHARED_DIM], h[:, SHARED_DIM:]
    h_ref[...] = (h1 * _sigmoid(h1) * h3).astype(BF16)


def _shared_up(x1, ws1, ws3):
    n, d = x1.shape
    tm = min(512, n)
    w13 = jnp.concatenate([ws1, ws3], axis=1).astype(BF16)
    return pl.pallas_call(
        _shared_up_kernel, grid=(n // tm,),
        in_specs=[pl.BlockSpec((tm, d), lambda i: (i, 0)),
                  pl.BlockSpec((d, 2 * SHARED_DIM), lambda i: (0, 0))],
        out_specs=pl.BlockSpec((tm, SHARED_DIM), lambda i: (i, 0)),
        out_shape=jax.ShapeDtypeStruct((n, SHARED_DIM), BF16),
        compiler_params=_cparams(1), name="shared_up",
    )(x1, w13)


def _combine_kernel(pos_ref, x1_ref, h_ref, ws2_ref, gate_ref, g_ref, b_ref, y_hbm, o_ref, ybuf, sem,
                    *, alpha):
    tm = x1_ref.shape[0]

    def row_copy(t, k):
        return pltpu.make_async_copy(y_hbm.at[pl.ds(pos_ref[k, t], 1), :], ybuf.at[k, pl.ds(t, 1), :], sem)

    def start(t, carry):
        for k in range(TOP_K):
            row_copy(t, k).start(priority=k % 2)
        return carry

    def wait(t, carry):
        for k in range(TOP_K):
            row_copy(t, k).wait()
        return carry

    lax.fori_loop(0, tm, start, 0)
    base = alpha * x1_ref[...] + jnp.dot(h_ref[...], ws2_ref[...], preferred_element_type=F32)
    lax.fori_loop(0, tm, wait, 0)

    gates = gate_ref[...]
    acc_lo = jnp.zeros((tm, D_MODEL // 2), F32)
    acc_hi = jnp.zeros((tm, D_MODEL // 2), F32)
    for k in range(TOP_K):
        lo, hi = _unpack_halves(ybuf[k])
        gk = gates[:, k:k + 1]
        acc_lo = acc_lo + gk * lo
        acc_hi = acc_hi + gk * hi
    f = jnp.concatenate([acc_lo, acc_hi], axis=1)
    o_ref[...] = _layer_norm(base + f, g_ref[...], b_ref[...])


def _combine(x1, hsh, ws2, pos, gates_t, ys, ln_g, ln_b, alpha):
    n, d = x1.shape
    tm = min(SCATTER_TOKENS, n)
    row = pl.BlockSpec((1, d), lambda i: (0, 0))
    return pl.pallas_call(
        functools.partial(_combine_kernel, alpha=alpha), grid=(n // tm,),
        in_specs=[pl.BlockSpec((TOP_K, tm), lambda i: (0, i), memory_space=pltpu.SMEM),
                  pl.BlockSpec((tm, d), lambda i: (i, 0)),
                  pl.BlockSpec((tm, SHARED_DIM), lambda i: (i, 0)),
                  pl.BlockSpec((SHARED_DIM, d), lambda i: (0, 0)),
                  pl.BlockSpec((tm, TOP_K), lambda i: (i, 0)),
                  row, row,
                  pl.BlockSpec(memory_space=pl.ANY)],
        out_specs=pl.BlockSpec((tm, d), lambda i: (i, 0)),
        out_shape=jax.ShapeDtypeStruct((n, d), F32),
        scratch_shapes=[pltpu.VMEM((TOP_K, tm, d // 2), U32), pltpu.SemaphoreType.DMA],
        compiler_params=_cparams(1), name="moe_combine",
    )(pos, x1, hsh, ws2.astype(BF16), gates_t, ln_g.astype(F32).reshape(1, d),
      ln_b.astype(F32).reshape(1, d), ys)


def _moe(x1, x1p, w_router, router_bias, w1, w3, w2, ws1, ws3, ws2, ln_g, ln_b, alpha):
    n = x1.shape[0]
    bm = EXPERT_ROWS
    eidx, gates, rank, cnt = _router(x1, w_router, router_bias)
    counts = cnt[:, 0].astype(I32)
    padded = (counts + bm - 1) // bm * bm
    pend = jnp.cumsum(padded)
    pstart = pend - padded
    pos = _positions(pstart, eidx, rank)
    n_rows = -(-(n * TOP_K) // bm) * bm + N_EXPERTS * bm
    blk_start = jnp.arange(n_rows // bm, dtype=I32) * bm
    blk_on = (blk_start < pend[-1]).astype(I32)
    last = jnp.maximum(pend[-1] - bm, 0)
    blk_e = jnp.sum((pend[None, :] <= jnp.minimum(blk_start, last)[:, None]).astype(I32), axis=1)
    blk_e = jnp.minimum(blk_e, N_EXPERTS - 1)
    changed = jnp.concatenate([jnp.ones((1,), I32), (blk_e[1:] != blk_e[:-1]).astype(I32)])
    blk_new = blk_on * changed
    xs = _scatter_rows(x1p, pos, pend, padded, n_rows)
    ys = _experts(xs, blk_e, blk_on, blk_new, w1, w3, w2)
    hsh = _shared_up(x1, ws1, ws3)
    return _combine(x1, hsh, ws2, pos, gates.T, ys, ln_g, ln_b, alpha)


def kernel(x, positions, w_in, conv_w, A_log, dt_bias, gdn_norm_w, idx_ln_g, idx_ln_b, w_o_attn, w_o_gdn, w_out, ln1_g, ln1_b, w_router, router_bias, w1, w3, w2, ws1, ws3, ws2, ln2_g, ln2_b):
    batch, seq, d = x.shape
    depth = w_in.shape[0]
    alpha = (2 * depth) ** 0.25
    n = batch * seq
    xf = x.reshape(n, d)
    tabs = _trig_tables(positions)
    for l in range(depth):
        proj = _project(xf, _pack_w_in(w_in[l]))
        q_r, k_r, v_ext, iq_r, ik_lo, ik_hi, iw = _att_prep(proj, tabs, idx_ln_g[l], idx_ln_b[l])
        att = _dsa_attention(q_r, k_r, v_ext, iq_r, ik_lo, ik_hi, iw, batch, seq)
        gdn = _gated_deltanet(proj, conv_w[l], A_log[l], dt_bias[l], gdn_norm_w[l], batch, seq)
        merged = _merge(att, gdn, w_o_attn[l], w_o_gdn[l], proj)
        x1, x1p = _outproj_ln(merged, w_out[l], xf, ln1_g[l], ln1_b[l], alpha)
        xf = _moe(x1, x1p, w_router[l], router_bias[l], w1[l], w3[l], w2[l], ws1[l], ws3[l], ws2[l],
                  ln2_g[l], ln2_b[l], alpha)
    return xf.reshape(batch, seq, d)
```

```python
import functools
import math

import jax
import jax.numpy as jnp
import numpy as np
from jax import lax
from jax.experimental import pallas as pl
from jax.experimental.pallas import tpu as pltpu

F32 = jnp.float32
BF16 = jnp.bfloat16
I32 = jnp.int32
U32 = jnp.uint32
HI = lax.Precision.HIGHEST

D_MODEL = 2048
ATT_HEADS = 16
ATT_KV_HEADS = 2
HEAD_DIM = 128
IDX_HEADS = 8
IDX_DIM = 64
TOPK_MAX = 256
Q_BLOCK = 128
ROPE_THETA = 10000.0
GDN_HEADS = 16
GDN_DK = 128
GDN_DV = 128
CONV_WIDTH = 4
CHUNK = 64
N_EXPERTS = 64
TOP_K = 8
N_GROUPS = 8
TOPK_GROUPS = 4
EXPERT_DIM = 512
SHARED_DIM = 512
ROUTED_SCALE = 2.5
LN_EPS = 1e-5
RMS_EPS = 1e-6

ATT_Q = ATT_HEADS * HEAD_DIM
ATT_KV = ATT_KV_HEADS * HEAD_DIM
IDX_Q = IDX_HEADS * IDX_DIM
GDN_QK = GDN_HEADS * GDN_DK
GDN_V = GDN_HEADS * GDN_DV
SPLITS = (ATT_Q, ATT_KV, ATT_KV, IDX_Q, IDX_DIM, IDX_HEADS, GDN_QK, GDN_QK, GDN_V, GDN_V,
          GDN_HEADS, GDN_HEADS, D_MODEL, D_MODEL)
SPLIT_OFFSETS = tuple(int(o) for o in np.cumsum(SPLITS)[:-1])

LANES = 128
VMEM_LIMIT_BYTES = 56 * 1024 * 1024

OFF_AQ = 0
OFF_GQ = OFF_AQ + ATT_Q
OFF_GK = OFF_GQ + GDN_QK
OFF_GV = OFF_GK + GDN_QK
OFF_GZ = OFF_GV + GDN_V
OFF_GTA = OFF_GZ + GDN_V
OFF_GTB = OFF_GTA + D_MODEL
OFF_AK = OFF_GTB + D_MODEL
OFF_AV = OFF_AK + ATT_KV
OFF_IQ = OFF_AV + ATT_KV
OFF_SM = OFF_IQ + IDX_Q
PROJ_W = 16384
SM_IK = 0
SM_IW = SM_IK + IDX_DIM
SM_BETA = SM_IW + IDX_HEADS
SM_DT = SM_BETA + GDN_HEADS

KEY_CHUNK = 512
EXPERT_ROWS = 512
INT_MIN = -2147483648
MASK_BIAS = -1e30
M_INIT = -1e29


def _cparams(n_axes):
    return pltpu.CompilerParams(dimension_semantics=("arbitrary",) * n_axes,
                                vmem_limit_bytes=VMEM_LIMIT_BYTES)


def _nt(a, b, precision=None):
    return lax.dot_general(a, b, (((1,), (1,)), ((), ())), precision=precision,
                           preferred_element_type=F32)


def _tn(a, b, precision=None):
    return lax.dot_general(a, b, (((0,), (0,)), ((), ())), precision=precision,
                           preferred_element_type=F32)


def _sigmoid(x):
    return jax.nn.sigmoid(x)


def _layer_norm(x, g, b):
    mu = jnp.mean(x, axis=-1, keepdims=True)
    xc = x - mu
    var = jnp.mean(xc * xc, axis=-1, keepdims=True)
    return xc * lax.rsqrt(var + LN_EPS) * g + b


def _trig_kernel(pos_ref, inv_a_ref, inv_i_ref, sgn_a_ref, sgn_i_ref, cos_a, sin_a, cos_i, sin_i):
    p = pos_ref[...]
    ang_a = p * inv_a_ref[...]
    cos_a[...] = jnp.cos(ang_a)
    sin_a[...] = jnp.sin(ang_a) * sgn_a_ref[...]
    ang_i = p * inv_i_ref[...]
    cos_i[...] = jnp.cos(ang_i)
    sin_i[...] = jnp.sin(ang_i) * sgn_i_ref[...]


def _trig_tables(positions):
    n = positions.size
    tm = min(512, n)
    pos = positions.reshape(n, 1).astype(F32)
    inv_a = ROPE_THETA ** (-jnp.arange(0, HEAD_DIM, 2, dtype=F32) / HEAD_DIM)
    inv_i = ROPE_THETA ** (-jnp.arange(0, IDX_DIM, 2, dtype=F32) / IDX_DIM)
    half_a, half_i = HEAD_DIM // 2, IDX_DIM // 2
    inv_a_row = jnp.tile(inv_a, 2).reshape(1, LANES)
    inv_i_row = jnp.tile(inv_i, 4).reshape(1, LANES)
    lane = np.arange(LANES)
    sgn_a = jnp.asarray(np.where(lane < half_a, -1.0, 1.0), F32).reshape(1, LANES)
    sgn_i = jnp.asarray(np.where(lane % IDX_DIM < half_i, -1.0, 1.0), F32).reshape(1, LANES)
    row = pl.BlockSpec((1, LANES), lambda i: (0, 0))
    tab = pl.BlockSpec((tm, LANES), lambda i: (i, 0))
    return pl.pallas_call(
        _trig_kernel, grid=(n // tm,),
        in_specs=[pl.BlockSpec((tm, 1), lambda i: (i, 0)), row, row, row, row],
        out_specs=[tab, tab, tab, tab],
        out_shape=[jax.ShapeDtypeStruct((n, LANES), F32)] * 4,
        compiler_params=_cparams(1), name="trig_tables",
    )(pos, inv_a_row, inv_i_row, sgn_a, sgn_i)


def _pack_w_in(w_in):
    (aq, ak, av, iq, ik, iw, gq, gk, gv, gz, gb, ga, gta, gtb) = jnp.split(w_in.astype(BF16), SPLIT_OFFSETS,
                                                                             axis=-1)
    used = OFF_SM + IDX_DIM + IDX_HEADS + 2 * GDN_HEADS
    pad = jnp.zeros((w_in.shape[0], PROJ_W - used), BF16)
    return jnp.concatenate([aq, gq, gk, gv, gz, gta, gtb, ak, av, iq, ik, iw, gb, ga, pad], axis=-1)


def _proj_kernel(x_ref, w_ref, o_ref, xb_ref):
    @pl.when(pl.program_id(1) == 0)
    def _():
        xb_ref[...] = x_ref[...].astype(BF16)

    o_ref[...] = jnp.dot(xb_ref[...], w_ref[...], preferred_element_type=F32)


def _project(x2d, w_packed):
    n, d = x2d.shape
    tm, tn = min(1024, n), 1024
    return pl.pallas_call(
        _proj_kernel, grid=(n // tm, PROJ_W // tn),
        in_specs=[pl.BlockSpec((tm, d), lambda i, j: (i, 0)),
                  pl.BlockSpec((d, tn), lambda i, j: (0, j))],
        out_specs=pl.BlockSpec((tm, tn), lambda i, j: (i, j)),
        out_shape=jax.ShapeDtypeStruct((n, PROJ_W), F32),
        scratch_shapes=[pltpu.VMEM((tm, d), BF16)],
        compiler_params=_cparams(2), name="in_proj",
    )(x2d, w_packed)


def _rope_head(x, cos, sin_signed):
    return x * cos + pltpu.roll(x, HEAD_DIM // 2, 1) * sin_signed


def _rope_idx(x, cos, sin_signed, first_half):
    half = IDX_DIM // 2
    partner = jnp.where(first_half, pltpu.roll(x, LANES - half, 1), pltpu.roll(x, half, 1))
    return x * cos + partner * sin_signed


def _att_prep_kernel(aq_ref, ak_ref, av_ref, iq_ref, sm_ref, cos_a_ref, sin_a_ref, cos_i_ref,
                     sin_i_ref, lng_ref, lnb_ref,
                     q_out, k_out, v_out, iq_out, iklo_out, ikhi_out, iw_out):
    cos_a, sin_a = cos_a_ref[...], sin_a_ref[...]
    cos_i, sin_i = cos_i_ref[...], sin_i_ref[...]
    rows = cos_a.shape[0]
    lane = lax.broadcasted_iota(I32, (rows, LANES), 1)
    first_half = (lane % IDX_DIM) < (IDX_DIM // 2)

    for h in range(ATT_HEADS):
        q_h = _rope_head(aq_ref[:, h * HEAD_DIM:(h + 1) * HEAD_DIM], cos_a, sin_a) * HEAD_DIM ** -0.5
        q_out[0, h] = q_h.astype(BF16)
    ones = jnp.ones((rows, HEAD_DIM), BF16)
    for g in range(ATT_KV_HEADS):
        sl = slice(g * HEAD_DIM, (g + 1) * HEAD_DIM)
        k_out[:, sl] = _rope_head(ak_ref[:, sl], cos_a, sin_a).astype(BF16)
        v_out[:, 2 * g * HEAD_DIM:(2 * g + 1) * HEAD_DIM] = av_ref[:, sl].astype(BF16)
        v_out[:, (2 * g + 1) * HEAD_DIM:(2 * g + 2) * HEAD_DIM] = ones
    for j in range(IDX_HEADS // 2):
        iq_out[0, j] = _rope_idx(iq_ref[:, j * LANES:(j + 1) * LANES], cos_i, sin_i, first_half).astype(BF16)

    sm = sm_ref[...]
    in_ik = lane < IDX_DIM
    mu = jnp.sum(jnp.where(in_ik, sm, 0.0), axis=1, keepdims=True) / IDX_DIM
    xc = sm - mu
    var = jnp.sum(jnp.where(in_ik, xc * xc, 0.0), axis=1, keepdims=True) / IDX_DIM
    ik = xc * lax.rsqrt(var + LN_EPS) * lng_ref[...] + lnb_ref[...]
    ik = jnp.where(in_ik, _rope_idx(ik, cos_i, sin_i, first_half), 0.0)
    iklo_out[...] = ik.astype(BF16)
    ikhi_out[...] = pltpu.roll(ik, IDX_DIM, 1).astype(BF16)
    iw_out[...] = (sm * IDX_HEADS ** -0.5) * IDX_DIM ** -0.5


def _att_prep(proj, tabs, idx_ln_g, idx_ln_b):
    n = proj.shape[0]
    tm = Q_BLOCK
    cos_a, sin_a, cos_i, sin_i = tabs
    pad = jnp.zeros((LANES - IDX_DIM,), F32)
    lng = jnp.concatenate([idx_ln_g.astype(F32), pad]).reshape(1, LANES)
    lnb = jnp.concatenate([idx_ln_b.astype(F32), pad]).reshape(1, LANES)
    tab = pl.BlockSpec((tm, LANES), lambda i: (i, 0))
    row = pl.BlockSpec((1, LANES), lambda i: (0, 0))

    def col(width, off):
        return pl.BlockSpec((tm, width), lambda i: (i, off // width))

    return pl.pallas_call(
        _att_prep_kernel, grid=(n // tm,),
        in_specs=[col(ATT_Q, OFF_AQ), col(ATT_KV, OFF_AK), col(ATT_KV, OFF_AV), col(IDX_Q, OFF_IQ),
                  col(LANES, OFF_SM), tab, tab, tab, tab, row, row],
        out_specs=[pl.BlockSpec((1, ATT_HEADS, tm, HEAD_DIM), lambda i: (i, 0, 0, 0)),
                   pl.BlockSpec((tm, ATT_KV), lambda i: (i, 0)),
                   pl.BlockSpec((tm, 2 * ATT_KV), lambda i: (i, 0)),
                   pl.BlockSpec((1, IDX_HEADS // 2, tm, LANES), lambda i: (i, 0, 0, 0)),
                   tab, tab, tab],
        out_shape=[jax.ShapeDtypeStruct((n // tm, ATT_HEADS, tm, HEAD_DIM), BF16),
                   jax.ShapeDtypeStruct((n, ATT_KV), BF16),
                   jax.ShapeDtypeStruct((n, 2 * ATT_KV), BF16),
                   jax.ShapeDtypeStruct((n // tm, IDX_HEADS // 2, tm, LANES), BF16),
                   jax.ShapeDtypeStruct((n, LANES), BF16),
                   jax.ShapeDtypeStruct((n, LANES), BF16),
                   jax.ShapeDtypeStruct((n, LANES), F32)],
        compiler_params=_cparams(1), name="att_prep",
    )(proj, proj, proj, proj, proj, cos_a, sin_a, cos_i, sin_i, lng, lnb)


def _dsa_kernel(q_ref, k_ref, v_ref, iq_ref, iklo_ref, ikhi_ref, iw_ref, o_ref,
                keys_sc, acc_sc, m_sc, wrep_sc, sel_sc, *, topk, seq):
    qb, kc = Q_BLOCK, KEY_CHUNK
    tiles = kc // LANES
    rep = ATT_HEADS // ATT_KV_HEADS
    i = pl.program_id(1)
    nch = (i * qb + qb + kc - 1) // kc
    row = lax.broadcasted_iota(I32, (qb, LANES), 0)
    lane = lax.broadcasted_iota(I32, (qb, LANES), 1)
    t_abs = i * qb + row

    iw = iw_ref[...]
    for h in range(IDX_HEADS):
        wrep_sc[h] = jnp.broadcast_to(iw[:, SM_IW + h:SM_IW + h + 1], (qb, LANES))
    qi = iq_ref[0].reshape((IDX_HEADS // 2) * qb, LANES)

    def score_chunk(c, carry):
        off = pl.multiple_of(c * kc, kc)
        s_lo = _nt(qi, iklo_ref[pl.ds(off, kc), :])
        s_hi = _nt(qi, ikhi_ref[pl.ds(off, kc), :])
        for ct in range(tiles):
            cs = slice(ct * LANES, (ct + 1) * LANES)
            acc = jnp.zeros((qb, LANES), F32)
            for j in range(IDX_HEADS // 2):
                rs = slice(j * qb, (j + 1) * qb)
                acc = acc + wrep_sc[2 * j] * jnp.maximum(s_lo[rs, cs], 0.0)
                acc = acc + wrep_sc[2 * j + 1] * jnp.maximum(s_hi[rs, cs], 0.0)
            bits = lax.bitcast_convert_type(acc, I32)
            key = bits ^ ((bits >> 31) & jnp.int32(0x7FFFFFFF))
            key = jnp.where(bits == jnp.int32(INT_MIN), 0, key)
            causal = (off + ct * LANES + lane) <= t_abs
            keys_sc[c, :, cs] = jnp.where(causal, key, jnp.int32(INT_MIN))
        return carry

    lax.fori_loop(0, nch, score_chunk, 0)

    kth = jnp.float32(topk)

    def count(pred):
        def body(c, cnt):
            kk = keys_sc[c]
            off = c * kc
            for ct in range(tiles):
                cs = slice(ct * LANES, (ct + 1) * LANES)
                cnt = cnt + jnp.where(pred(kk[:, cs], off + ct * LANES + lane), 1.0, 0.0)
            return cnt

        cnt = lax.fori_loop(0, nch, body, jnp.zeros((qb, LANES), F32))
        return jnp.sum(cnt, axis=1, keepdims=True)

    zero = jnp.zeros((qb, LANES), I32)
    n_pos = count(lambda k, s: k >= zero)
    base = jnp.where(n_pos >= kth, zero, jnp.int32(INT_MIN))
    n_base = jnp.where(n_pos >= kth, n_pos, -1.0)
    short = (i * qb + lax.broadcasted_iota(I32, (qb, 1), 0)) < topk

    def unresolved(n_base):
        return jnp.max(jnp.where((n_base == kth) | short, 0.0, 1.0)) > 0.0

    def bit_cond(st):
        it, _, n_base = st
        return (it < 31) & unresolved(n_base)

    def bit_step(st):
        it, base, n_base = st
        cand = base | (jnp.int32(1) << (30 - it))
        tot = count(lambda k, s: k >= cand)
        take = tot >= kth
        return it + 1, jnp.where(take, cand, base), jnp.where(take, tot, n_base)

    _, thr, n_ge = lax.while_loop(bit_cond, bit_step, (jnp.int32(0), base, n_base))
    sel_sc[0] = thr
    sel_sc[1] = jnp.full((qb, LANES), seq, I32)

    @pl.when(jnp.max(n_ge) > kth)
    def _():
        nbits = max(1, (seq - 1).bit_length())
        need = kth - count(lambda k, s: k > thr)

        def idx_step(it, posv):
            cand = posv + (jnp.int32(1) << (nbits - 1 - it))
            below = count(lambda k, s: (k == thr) & (s < cand))
            return jnp.where(below < need, cand, posv)

        sel_sc[1] = lax.fori_loop(0, nbits, idx_step, zero)

    thr = sel_sc[0]
    last_tie = sel_sc[1]

    m_sc[...] = jnp.full(m_sc.shape, M_INIT, F32)
    acc_sc[...] = jnp.zeros(acc_sc.shape, F32)

    def attend_chunk(c, carry):
        off = pl.multiple_of(c * kc, kc)
        kk = keys_sc[c]
        bias = []
        for ct in range(tiles):
            cs = slice(ct * LANES, (ct + 1) * LANES)
            sidx = off + ct * LANES + lane
            keep = (kk[:, cs] > thr) | ((kk[:, cs] == thr) & (sidx <= last_tie))
            keep = keep & (sidx <= t_abs)
            bias.append(jnp.where(keep, 0.0, MASK_BIAS))
        for g in range(ATT_KV_HEADS):
            kg = k_ref[pl.ds(off, kc), g * HEAD_DIM:(g + 1) * HEAD_DIM]
            vg = v_ref[pl.ds(off, kc), 2 * g * HEAD_DIM:(2 * g + 2) * HEAD_DIM]
            qg = q_ref[0, g * rep:(g + 1) * rep].reshape(rep * qb, HEAD_DIM)
            s = _nt(qg, kg)
            s_t = [(s[:, ct * LANES:(ct + 1) * LANES].reshape(rep, qb, LANES) + bias[ct][None]
                    ).reshape(rep * qb, LANES) for ct in range(tiles)]
            mx = s_t[0]
            for ct in range(1, tiles):
                mx = jnp.maximum(mx, s_t[ct])
            m_old = m_sc[g]
            m_new = jnp.maximum(m_old, jnp.max(mx, axis=1, keepdims=True))
            p = jnp.concatenate([jnp.exp(st - m_new) for st in s_t], axis=1).astype(BF16)
            alpha = jnp.exp(m_old - m_new)
            pv = jnp.dot(p, vg, preferred_element_type=F32)
            acc_sc[g] = acc_sc[g] * jnp.concatenate([alpha, alpha], axis=1) + pv
            m_sc[g] = m_new
        return carry

    lax.fori_loop(0, nch, attend_chunk, 0)

    for g in range(ATT_KV_HEADS):
        for r in range(rep):
            a = acc_sc[g, r * qb:(r + 1) * qb, :]
            h = g * rep + r
            o_ref[:, h * HEAD_DIM:(h + 1) * HEAD_DIM] = (a[:, :HEAD_DIM] / a[:, HEAD_DIM:]).astype(BF16)


def _dsa_attention(q_r, k_r, v_ext, iq_r, ik_lo, ik_hi, iw, batch, seq):
    n = batch * seq
    nqb = seq // Q_BLOCK
    topk = min(TOPK_MAX, seq // 4)
    rep = ATT_HEADS // ATT_KV_HEADS
    kern = functools.partial(_dsa_kernel, topk=topk, seq=seq)

    def per_batch(width):
        return pl.BlockSpec((None, seq, width), lambda b, i: (b, 0, 0))

    return pl.pallas_call(
        kern, grid=(batch, nqb),
        in_specs=[pl.BlockSpec((1, ATT_HEADS, Q_BLOCK, HEAD_DIM), lambda b, i: (b * nqb + i, 0, 0, 0)),
                  per_batch(ATT_KV), per_batch(2 * ATT_KV),
                  pl.BlockSpec((1, IDX_HEADS // 2, Q_BLOCK, LANES), lambda b, i: (b * nqb + i, 0, 0, 0)),
                  per_batch(LANES), per_batch(LANES),
                  pl.BlockSpec((Q_BLOCK, LANES), lambda b, i: (b * nqb + i, 0))],
        out_specs=pl.BlockSpec((Q_BLOCK, ATT_Q), lambda b, i: (b * nqb + i, 0)),
        out_shape=jax.ShapeDtypeStruct((n, ATT_Q), BF16),
        scratch_shapes=[pltpu.VMEM((seq // KEY_CHUNK, Q_BLOCK, KEY_CHUNK), I32),
                        pltpu.VMEM((ATT_KV_HEADS, rep * Q_BLOCK, 2 * HEAD_DIM), F32),
                        pltpu.VMEM((ATT_KV_HEADS, rep * Q_BLOCK, LANES), F32),
                        pltpu.VMEM((IDX_HEADS, Q_BLOCK, LANES), F32),
                        pltpu.VMEM((2, Q_BLOCK, LANES), I32)],
        compiler_params=_cparams(2), name="dsa_attention",
    )(q_r, k_r.reshape(batch, seq, ATT_KV), v_ext.reshape(batch, seq, 2 * ATT_KV), iq_r,
      ik_lo.reshape(batch, seq, LANES), ik_hi.reshape(batch, seq, LANES), iw)


GDN_PREP_HEADS = 8
CONV_HALO = 8


def _split2(x):
    hi = x.astype(BF16)
    return hi, (x - hi.astype(F32)).astype(BF16)


def _split3(x):
    hi = x.astype(BF16)
    r = x - hi.astype(F32)
    mid = r.astype(BF16)
    return hi, mid, (r - mid.astype(F32)).astype(BF16)


def _lhs3(hi, lo):
    return jnp.concatenate([hi, lo, hi], axis=1)


def _rhs3(hi, lo):
    return jnp.concatenate([hi, hi, lo], axis=0)


def _gdn_prep_kernel(q_ref, k_ref, v_ref, qp_ref, kp_ref, vp_ref, sm_ref, gat_ref,
                     cwq_ref, cwk_ref, cwv_ref, alog_row_ref, dtb_row_ref, alog_col_ref, dtb_col_ref,
                     u_ref, w_ref, qd_ref, kd_ref, a_ref, e_ref, xx_sc, gct_sc, *, nc):
    hb = GDN_PREP_HEADS
    c = CHUNK
    hg = pl.program_id(1)
    chunk = pl.program_id(0) % nc
    heads = range(hb)

    def conv_silu(slot, cur_ref, prev_ref, cw_ref):
        prev = prev_ref[...]
        at_start = jnp.full(prev.shape, chunk, I32) == 0
        xx_sc[slot, 0:CONV_HALO, :] = jnp.where(at_start, 0.0, prev)
        xx_sc[slot, CONV_HALO:CONV_HALO + c, :] = cur_ref[...]
        cw = cw_ref[...]
        start = CONV_HALO - (CONV_WIDTH - 1)
        acc = xx_sc[slot, pl.ds(start, c), :] * cw[0:1, :]
        for j in range(1, CONV_WIDTH):
            acc = acc + xx_sc[slot, pl.ds(start + j, c), :] * cw[j:j + 1, :]
        return acc * _sigmoid(acc)

    qa = conv_silu(0, q_ref, qp_ref, cwq_ref)
    ka = conv_silu(1, k_ref, kp_ref, cwk_ref)
    va = conv_silu(2, v_ref, vp_ref, cwv_ref)

    rowi = lax.broadcasted_iota(I32, (c, c), 0)
    coli = lax.broadcasted_iota(I32, (c, c), 1)
    lower = rowi >= coli
    strict = rowi > coli
    eye = jnp.where(rowi == coli, 1.0, 0.0)

    sm = sm_ref[...]
    beta_slab = _sigmoid(sm)
    g_slab = -jnp.exp(alog_row_ref[...]) * jax.nn.softplus(sm + dtb_row_ref[...])
    ones_l = jnp.where(lower, 1.0, 0.0).astype(BF16)
    ones_u = jnp.where(rowi <= coli, 1.0, 0.0).astype(BF16)
    gc_slab = jnp.dot(jnp.concatenate([ones_l] * 3, axis=1), jnp.concatenate(_split3(g_slab), axis=0),
                      preferred_element_type=F32)
    g_t = -jnp.exp(alog_col_ref[:, :c]) * jax.nn.softplus(gat_ref[...] + dtb_col_ref[:, :c])
    gct_sc[...] = jnp.dot(jnp.concatenate(_split3(g_t), axis=1), jnp.concatenate([ones_u] * 3, axis=0),
                          preferred_element_type=F32)
    lane = lax.broadcasted_iota(I32, (c, LANES), 1)
    hsl = [slice(r * GDN_DK, (r + 1) * GDN_DK) for r in heads]

    gcc = [jnp.sum(jnp.where(lane == SM_DT + hg * hb + r, gc_slab, 0.0), axis=1, keepdims=True) for r in heads]
    bcol = [jnp.sum(jnp.where(lane == SM_BETA + hg * hb + r, beta_slab, 0.0), axis=1, keepdims=True)
            for r in heads]
    gcr = [gct_sc[pl.ds(hg * hb + r, 1), :] for r in heads]
    glast = [g[:, c - 1:c] for g in gcr]
    qn = [qa[:, s] * lax.rsqrt(jnp.sum(qa[:, s] * qa[:, s], axis=1, keepdims=True) + RMS_EPS) * GDN_DK ** -0.5
          for s in hsl]
    kn = [ka[:, s] * lax.rsqrt(jnp.sum(ka[:, s] * ka[:, s], axis=1, keepdims=True) + RMS_EPS) for s in hsl]
    eg = [jnp.exp(g) for g in gcc]
    decay = [jnp.where(lower, jnp.exp(gcc[r] - gcr[r]), 0.0) for r in heads]
    kb = [kn[r] * bcol[r] for r in heads]

    def nt3(a, b):
        (ah, al), (bh, bl) = _split2(a), _split2(b)
        return _nt(_lhs3(ah, al), jnp.concatenate([bh, bh, bl], axis=1))

    kq = [nt3(jnp.concatenate([kb[r], qn[r]], axis=0), kn[r]) for r in heads]
    pairs = range(hb // 2)
    r2 = lax.broadcasted_iota(I32, (2 * c, 2 * c), 0)
    c2 = lax.broadcasted_iota(I32, (2 * c, 2 * c), 1)
    same_head = (r2 < c) == (c2 < c)

    def diag2(m):
        return jnp.where(same_head, jnp.concatenate([m, m], axis=0), jnp.zeros((), m.dtype))

    def diag_rhs3(hi, lo):
        return _rhs3(diag2(hi), diag2(lo))

    y = [jnp.concatenate([jnp.where(strict, -(kq[2 * p + s][:c] * decay[2 * p + s]), 0.0) for s in range(2)],
                         axis=1) for p in pairs]
    eye2 = jnp.concatenate([eye, eye], axis=1)
    t_inv = [eye2 + y[p] for p in pairs]
    ys = [_split2(v) for v in y]
    y = [jnp.dot(_lhs3(*ys[p]), diag_rhs3(*ys[p]), preferred_element_type=F32) for p in pairs]
    n_fac = int(math.log2(c))
    for j in range(1, n_fac):
        ys = [_split2(v) for v in y]
        ts = [_split2(v) for v in t_inv]
        if j < n_fac - 1:
            prod = [jnp.dot(jnp.concatenate([_lhs3(*ys[p]), _lhs3(*ts[p])], axis=0), diag_rhs3(*ys[p]),
                            preferred_element_type=F32) for p in pairs]
            y = [q[:c] for q in prod]
            t_inv = [t_inv[p] + prod[p][c:] for p in pairs]
        else:
            t_inv = [t_inv[p] + jnp.dot(_lhs3(*ts[p]), diag_rhs3(*ys[p]), preferred_element_type=F32)
                     for p in pairs]
    ts = [_split2(v) for v in t_inv]
    rhs = [jnp.concatenate([jnp.concatenate([va[:, hsl[r]] * bcol[r], kb[r] * eg[r]], axis=1)
                            for r in (2 * p, 2 * p + 1)], axis=0) for p in pairs]
    uw2 = [jnp.dot(_lhs3(diag2(ts[p][0]), diag2(ts[p][1])), _rhs3(*_split2(rhs[p])),
                   preferred_element_type=F32) for p in pairs]
    uw = [uw2[r // 2][(r % 2) * c:(r % 2 + 1) * c] for r in heads]

    for r in heads:
        u_ref[:, hsl[r]] = uw[r][:, :GDN_DV]
        w_ref[:, hsl[r]] = uw[r][:, GDN_DV:].astype(BF16)
        qd_ref[:, hsl[r]] = (qn[r] * eg[r]).astype(BF16)
        kd_ref[:, hsl[r]] = (kn[r] * jnp.exp(glast[r] - gcc[r])).astype(BF16)
        a_ref[:, r * c:(r + 1) * c] = (kq[r][c:] * decay[r]).astype(BF16)
        e_ref[:, hsl[r]] = jnp.broadcast_to(jnp.exp(glast[r]), (CONV_HALO, GDN_DK))


def _gdn_scan_kernel(u_ref, w_ref, qd_ref, kd_ref, a_ref, e_ref, z_ref, nw_ref, o_ref, st_sc):
    c = CHUNK
    heads = range(GDN_HEADS)
    hsl = [slice(h * GDN_DK, (h + 1) * GDN_DK) for h in heads]

    @pl.when(pl.program_id(1) == 0)
    def _():
        st_sc[...] = jnp.zeros(st_sc.shape, F32)

    st = [st_sc[h] for h in heads]
    ws_qs = [jnp.dot(jnp.concatenate([w_ref[:, hsl[h]], qd_ref[:, hsl[h]]], axis=0), st[h].astype(BF16),
                     preferred_element_type=F32) for h in heads]
    v_new = [(u_ref[:, hsl[h]] - ws_qs[h][:c]).astype(BF16) for h in heads]
    o = [ws_qs[h][c:] + jnp.dot(a_ref[:, h * c:(h + 1) * c], v_new[h], preferred_element_type=F32)
         for h in heads]
    for h in heads:
        st_sc[h] = st[h] * e_ref[0:1, hsl[h]] + _tn(kd_ref[:, hsl[h]], v_new[h])
    for h in heads:
        z = z_ref[:, hsl[h]]
        on = o[h] * lax.rsqrt(jnp.mean(o[h] * o[h], axis=1, keepdims=True) + RMS_EPS)
        o_ref[:, hsl[h]] = (on * nw_ref[...] * (z * _sigmoid(z))).astype(BF16)


def _gated_deltanet(proj, conv_w, a_log, dt_bias, norm_w, batch, seq):
    n = batch * seq
    nc = seq // CHUNK
    hb = GDN_PREP_HEADS
    w = hb * GDN_DK
    ga_t = proj[:, OFF_SM + SM_DT:OFF_SM + SM_DT + GDN_HEADS].reshape(batch * nc, CHUNK, GDN_HEADS)
    ga_t = ga_t.transpose(0, 2, 1)
    lane_pad = lambda vec, off: jnp.zeros((1, LANES), F32).at[0, off:off + GDN_HEADS].set(vec.astype(F32))
    alog_row, dtb_row = lane_pad(a_log, SM_DT), lane_pad(dt_bias, SM_DT)
    alog_col = jnp.broadcast_to(a_log.astype(F32)[:, None], (GDN_HEADS, LANES))
    dtb_col = jnp.broadcast_to(dt_bias.astype(F32)[:, None], (GDN_HEADS, LANES))
    nw = norm_w.astype(F32).reshape(1, GDN_DV)
    cw = conv_w.astype(F32)

    def cur(off):
        return pl.BlockSpec((CHUNK, w), lambda i, h: (i, off // w + h))

    def prev(off):
        per = CHUNK // CONV_HALO
        return pl.BlockSpec((CONV_HALO, w), lambda i, h: (jnp.maximum(i * per - 1, 0), off // w + h))

    def cwspec(off):
        return pl.BlockSpec((CONV_WIDTH, w), lambda i, h: (0, off // w + h))

    row = pl.BlockSpec((1, LANES), lambda i, h: (0, 0))
    colv = pl.BlockSpec((GDN_HEADS, LANES), lambda i, h: (0, 0))
    head_tile = pl.BlockSpec((CHUNK, w), lambda i, h: (i, h))
    u, wm, qd, kd, a_in, egl = pl.pallas_call(
        functools.partial(_gdn_prep_kernel, nc=nc), grid=(batch * nc, GDN_HEADS // hb),
        in_specs=[cur(OFF_GQ), cur(OFF_GK), cur(OFF_GV), prev(OFF_GQ), prev(OFF_GK), prev(OFF_GV),
                  pl.BlockSpec((CHUNK, LANES), lambda i, h: (i, OFF_SM // LANES)),
                  pl.BlockSpec((None, GDN_HEADS, CHUNK), lambda i, h: (i, 0, 0)),
                  cwspec(0), cwspec(GDN_QK), cwspec(2 * GDN_QK), row, row, colv, colv],
        out_specs=[head_tile, head_tile, head_tile, head_tile,
                   pl.BlockSpec((CHUNK, hb * CHUNK), lambda i, h: (i, h)),
                   pl.BlockSpec((CONV_HALO, w), lambda i, h: (i, h))],
        out_shape=[jax.ShapeDtypeStruct((n, GDN_V), F32), jax.ShapeDtypeStruct((n, GDN_V), BF16),
                   jax.ShapeDtypeStruct((n, GDN_QK), BF16), jax.ShapeDtypeStruct((n, GDN_QK), BF16),
                   jax.ShapeDtypeStruct((n, GDN_HEADS * CHUNK), BF16),
                   jax.ShapeDtypeStruct((batch * nc * CONV_HALO, GDN_V), F32)],
        scratch_shapes=[pltpu.VMEM((3, CONV_HALO + CHUNK, w), F32),
                        pltpu.VMEM((GDN_HEADS, CHUNK), F32)],
        compiler_params=_cparams(2), name="gdn_prep",
    )(proj, proj, proj, proj, proj, proj, proj, ga_t, cw, cw, cw, alog_row, dtb_row, alog_col, dtb_col)

    def full(width):
        return pl.BlockSpec((CHUNK, width), lambda b, t: (b * nc + t, 0))

    return pl.pallas_call(
        _gdn_scan_kernel, grid=(batch, nc),
        in_specs=[full(GDN_V), full(GDN_V), full(GDN_QK), full(GDN_QK), full(GDN_HEADS * CHUNK),
                  pl.BlockSpec((CONV_HALO, GDN_V), lambda b, t: (b * nc + t, 0)),
                  pl.BlockSpec((CHUNK, GDN_V), lambda b, t: (b * nc + t, OFF_GZ // GDN_V)),
                  pl.BlockSpec((1, GDN_DV), lambda b, t: (0, 0))],
        out_specs=full(GDN_V),
        out_shape=jax.ShapeDtypeStruct((n, GDN_V), BF16),
        scratch_shapes=[pltpu.VMEM((GDN_HEADS, GDN_DK, GDN_DV), F32)],
        compiler_params=_cparams(2), name="gdn_scan",
    )(u, wm, qd, kd, a_in, egl, proj, nw)


def _merge_kernel(att_ref, gdn_ref, wa_ref, wg_ref, ga_ref, gb_ref, o_ref):
    a = jnp.dot(att_ref[...], wa_ref[...], preferred_element_type=F32)
    g = jnp.dot(gdn_ref[...], wg_ref[...], preferred_element_type=F32)
    o_ref[...] = (_sigmoid(ga_ref[...]) * a + _sigmoid(gb_ref[...]) * g).astype(BF16)


def _merge(att, gdn, w_o_attn, w_o_gdn, proj):
    n = att.shape[0]
    tm, tn = min(512, n), 512
    return pl.pallas_call(
        _merge_kernel, grid=(n // tm, D_MODEL // tn),
        in_specs=[pl.BlockSpec((tm, ATT_Q), lambda i, j: (i, 0)),
                  pl.BlockSpec((tm, GDN_V), lambda i, j: (i, 0)),
                  pl.BlockSpec((ATT_Q, tn), lambda i, j: (0, j)),
                  pl.BlockSpec((GDN_V, tn), lambda i, j: (0, j)),
                  pl.BlockSpec((tm, tn), lambda i, j: (i, OFF_GTA // tn + j)),
                  pl.BlockSpec((tm, tn), lambda i, j: (i, OFF_GTB // tn + j))],
        out_specs=pl.BlockSpec((tm, tn), lambda i, j: (i, j)),
        out_shape=jax.ShapeDtypeStruct((n, D_MODEL), BF16),
        compiler_params=_cparams(2), name="mixer_merge",
    )(att, gdn, w_o_attn.astype(BF16), w_o_gdn.astype(BF16), proj, proj)


def _pack_halves(x):
    w = x.shape[1] // 2
    bits = lax.bitcast_convert_type(x.astype(BF16).astype(F32), U32)
    return (bits[:, :w] >> 16) | (bits[:, w:] & jnp.uint32(0xFFFF0000))


def _unpack_halves(u):
    lo = lax.bitcast_convert_type(u << 16, F32)
    hi = lax.bitcast_convert_type(u & jnp.uint32(0xFFFF0000), F32)
    return lo, hi


def _outproj_ln_kernel(m_ref, w_ref, x_ref, g_ref, b_ref, x1_ref, x1p_ref, *, alpha):
    y = jnp.dot(m_ref[...], w_ref[...], preferred_element_type=F32)
    x1 = _layer_norm(alpha * x_ref[...] + y, g_ref[...], b_ref[...])
    x1_ref[...] = x1
    x1p_ref[...] = _pack_halves(x1)


def _outproj_ln(merged, w_out, x2d, ln_g, ln_b, alpha):
    n, d = x2d.shape
    tm = min(256, n)
    row = pl.BlockSpec((1, d), lambda i: (0, 0))
    tile = pl.BlockSpec((tm, d), lambda i: (i, 0))
    return pl.pallas_call(
        functools.partial(_outproj_ln_kernel, alpha=alpha), grid=(n // tm,),
        in_specs=[tile, pl.BlockSpec((d, d), lambda i: (0, 0)), tile, row, row],
        out_specs=[tile, pl.BlockSpec((tm, d // 2), lambda i: (i, 0))],
        out_shape=[jax.ShapeDtypeStruct((n, d), F32), jax.ShapeDtypeStruct((n, d // 2), U32)],
        compiler_params=_cparams(1), name="outproj_ln",
    )(merged, w_out.astype(BF16), x2d, ln_g.astype(F32).reshape(1, d), ln_b.astype(F32).reshape(1, d))


def _first_max(x, iota, size):
    m = jnp.max(x, axis=0, keepdims=True)
    f = jnp.min(jnp.where(x == m, iota, float(size)), axis=0, keepdims=True)
    return m, f


def _router_kernel(x_ref, w_ref, b_ref, eidx_ref, gate_ref, rank_ref, cnt_ref, carry_sc):
    tm = x_ref.shape[0]
    per = N_EXPERTS // N_GROUPS
    neg = -jnp.inf

    @pl.when(pl.program_id(0) == 0)
    def _():
        carry_sc[...] = jnp.zeros(carry_sc.shape, F32)

    logits = _nt(w_ref[...], x_ref[...], precision=HI)
    s = _sigmoid(logits)
    choice = s + b_ref[:, 0:1]
    io_g = lax.broadcasted_iota(I32, (per, tm), 0).astype(F32)
    io_e = lax.broadcasted_iota(I32, (N_EXPERTS, tm), 0).astype(F32)

    gs = []
    for g in range(N_GROUPS):
        xg = choice[g * per:(g + 1) * per]
        m1, f1 = _first_max(xg, io_g, per)
        m2 = jnp.max(jnp.where(io_g == f1, neg, xg), axis=0, keepdims=True)
        gs.append(m1 + m2)
    gw = jnp.concatenate(gs, axis=0)
    gsel = jnp.zeros((N_GROUPS, tm), F32)
    for _ in range(TOPK_GROUPS):
        _, f = _first_max(gw, io_g, N_GROUPS)
        hit = io_g == f
        gsel = jnp.where(hit, 1.0, gsel)
        gw = jnp.where(hit, neg, gw)
    x = jnp.concatenate([jnp.where(gsel[g:g + 1] > 0.0, choice[g * per:(g + 1) * per], neg)
                         for g in range(N_GROUPS)], axis=0)

    mem = jnp.zeros((N_EXPERTS, tm), F32)
    picks, wsel = [], []
    for _ in range(TOP_K):
        _, f = _first_max(x, io_e, N_EXPERTS)
        hit = io_e == f
        picks.append(f)
        wsel.append(jnp.sum(jnp.where(hit, s, 0.0), axis=0, keepdims=True))
        mem = jnp.where(hit, 1.0, mem)
        x = jnp.where(hit, neg, x)
    wsel = jnp.concatenate(wsel, axis=0)
    gate_ref[...] = wsel / jnp.sum(wsel, axis=0, keepdims=True) * ROUTED_SCALE
    eidx_ref[...] = jnp.concatenate(picks, axis=0).astype(I32)

    tr = lax.broadcasted_iota(I32, (tm, tm), 0)
    tc = lax.broadcasted_iota(I32, (tm, tm), 1)
    before = jnp.where(tr < tc, 1.0, 0.0).astype(BF16)
    prefix = jnp.dot(mem.astype(BF16), before, preferred_element_type=F32) + carry_sc[:, 0:1]
    ranks = [jnp.sum(jnp.where(io_e == f, prefix, 0.0), axis=0, keepdims=True) for f in picks]
    rank_ref[...] = jnp.concatenate(ranks, axis=0).astype(I32)
    carry_sc[...] = carry_sc[...] + jnp.sum(mem, axis=1, keepdims=True)
    cnt_ref[...] = carry_sc[...]


def _router(x1, w_router, router_bias):
    n, d = x1.shape
    tm = min(512, n)
    out = pl.BlockSpec((TOP_K, tm), lambda i: (0, i))
    bias = jnp.broadcast_to(router_bias.astype(F32)[:, None], (N_EXPERTS, LANES))
    return pl.pallas_call(
        _router_kernel, grid=(n // tm,),
        in_specs=[pl.BlockSpec((tm, d), lambda i: (i, 0)),
                  pl.BlockSpec((N_EXPERTS, d), lambda i: (0, 0)),
                  pl.BlockSpec((N_EXPERTS, LANES), lambda i: (0, 0))],
        out_specs=[out, out, out, pl.BlockSpec((N_EXPERTS, LANES), lambda i: (0, 0))],
        out_shape=[jax.ShapeDtypeStruct((TOP_K, n), I32), jax.ShapeDtypeStruct((TOP_K, n), F32),
                   jax.ShapeDtypeStruct((TOP_K, n), I32), jax.ShapeDtypeStruct((N_EXPERTS, LANES), F32)],
        scratch_shapes=[pltpu.VMEM((N_EXPERTS, LANES), F32)],
        compiler_params=_cparams(1), name="router",
    )(x1, w_router.astype(F32).T, bias)


SCATTER_TOKENS = 128
ROW_COPY_UNROLL = 8


def _pos_kernel(pstart_ref, eidx_ref, rank_ref, pos_ref):
    e = eidx_ref[...]
    acc = rank_ref[...]
    for x in range(N_EXPERTS):
        acc = acc + jnp.where(e == x, pstart_ref[x], 0)
    pos_ref[...] = acc


def _positions(pstart, eidx, rank):
    n = eidx.shape[1]
    tm = min(2048, n)
    tile = pl.BlockSpec((TOP_K, tm), lambda i, ps: (0, i))
    return pl.pallas_call(
        _pos_kernel,
        grid_spec=pltpu.PrefetchScalarGridSpec(num_scalar_prefetch=1, grid=(n // tm,),
                                               in_specs=[tile, tile], out_specs=tile),
        out_shape=jax.ShapeDtypeStruct((TOP_K, n), I32),
        compiler_params=_cparams(1), name="moe_positions",
    )(pstart, eidx, rank)


def _scatter_kernel(pend_ref, padded_ref, pos_ref, x_ref, xs_ref, zero_sc, sem, zsem):
    tm = x_ref.shape[0]
    bm = EXPERT_ROWS

    @pl.when(pl.program_id(0) == 0)
    def _():
        zero_sc[...] = jnp.zeros(zero_sc.shape, U32)

        def tail_copy(e):
            tail = pl.multiple_of(pend_ref[e] - bm, bm)
            return pltpu.make_async_copy(zero_sc, xs_ref.at[pl.ds(tail, bm), :], zsem)

        def zstart(e, carry):
            @pl.when(padded_ref[e] > 0)
            def _():
                tail_copy(e).start()
            return carry

        def zwait(e, carry):
            @pl.when(padded_ref[e] > 0)
            def _():
                tail_copy(e).wait()
            return carry

        lax.fori_loop(0, N_EXPERTS, zstart, 0)
        lax.fori_loop(0, N_EXPERTS, zwait, 0)

    def row_copy(t, k):
        return pltpu.make_async_copy(x_ref.at[pl.ds(t, 1), :], xs_ref.at[pl.ds(pos_ref[k, t], 1), :], sem)

    def start(t, carry):
        for k in range(TOP_K):
            row_copy(t, k).start(priority=k % 2)
        return carry

    lax.fori_loop(0, tm, start, 0, unroll=ROW_COPY_UNROLL)
    rows = xs_ref.at[pl.ds(0, tm * TOP_K), :]
    pltpu.make_async_copy(rows, rows, sem).wait()


def _scatter_rows(x1p, pos, pend, padded, n_rows):
    n, w = x1p.shape
    tm = min(SCATTER_TOKENS, n)
    grid_spec = pltpu.PrefetchScalarGridSpec(
        num_scalar_prefetch=2, grid=(n // tm,),
        in_specs=[pl.BlockSpec((TOP_K, tm), lambda i, pe, pd: (0, i), memory_space=pltpu.SMEM),
                  pl.BlockSpec((tm, w), lambda i, pe, pd: (i, 0))],
        out_specs=pl.BlockSpec(memory_space=pl.ANY),
        scratch_shapes=[pltpu.VMEM((EXPERT_ROWS, w), U32), pltpu.SemaphoreType.DMA, pltpu.SemaphoreType.DMA])
    return pl.pallas_call(
        _scatter_kernel, grid_spec=grid_spec,
        out_shape=jax.ShapeDtypeStruct((n_rows, w), U32),
        compiler_params=_cparams(1), name="moe_scatter",
    )(pend, padded, pos, x1p)


def _expert_kernel(blk_e_ref, blk_on_ref, blk_new_ref, x_ref, w1_ref, w3_ref, w2_ref, y_ref,
                   w1_sc, w3_sc, w2_sc):
    del blk_e_ref
    i = pl.program_id(0)

    @pl.when(blk_new_ref[i] > 0)
    def _():
        w1_sc[...] = w1_ref[...].astype(BF16)
        w3_sc[...] = w3_ref[...].astype(BF16)
        w2_sc[...] = w2_ref[...].astype(BF16)

    @pl.when(blk_on_ref[i] > 0)
    def _():
        half = D_MODEL // 2
        lo, hi = _unpack_halves(x_ref[...])
        xa, xb = lo.astype(BF16), hi.astype(BF16)

        def up(w_sc):
            return (jnp.dot(xa, w_sc[0:half, :], preferred_element_type=F32)
                    + jnp.dot(xb, w_sc[half:, :], preferred_element_type=F32))

        h1, h3 = up(w1_sc), up(w3_sc)
        act = (h1 * _sigmoid(h1) * h3).astype(BF16)
        y_ref[...] = _pack_halves(jnp.dot(act, w2_sc[...], preferred_element_type=F32))


def _experts(xs, blk_e, blk_on, blk_new, w1, w3, w2):
    n_rows, w = xs.shape
    bm = EXPERT_ROWS
    grid_spec = pltpu.PrefetchScalarGridSpec(
        num_scalar_prefetch=3, grid=(n_rows // bm,),
        in_specs=[pl.BlockSpec((bm, w), lambda i, e, on, nw: (i, 0)),
                  pl.BlockSpec((None, D_MODEL, EXPERT_DIM), lambda i, e, on, nw: (e[i], 0, 0)),
                  pl.BlockSpec((None, D_MODEL, EXPERT_DIM), lambda i, e, on, nw: (e[i], 0, 0)),
                  pl.BlockSpec((None, EXPERT_DIM, D_MODEL), lambda i, e, on, nw: (e[i], 0, 0))],
        out_specs=pl.BlockSpec((bm, w), lambda i, e, on, nw: (i, 0)),
        scratch_shapes=[pltpu.VMEM((D_MODEL, EXPERT_DIM), BF16), pltpu.VMEM((D_MODEL, EXPERT_DIM), BF16),
                        pltpu.VMEM((EXPERT_DIM, D_MODEL), BF16)])
    return pl.pallas_call(
        _expert_kernel, grid_spec=grid_spec,
        out_shape=jax.ShapeDtypeStruct((n_rows, w), U32),
        compiler_params=_cparams(1), name="moe_experts",
    )(blk_e, blk_on, blk_new, xs, w1, w3, w2)


def _shared_up_kernel(x_ref, w_ref, h_ref):
    h = jnp.dot(x_ref[...].astype(BF16), w_ref[...], preferred_element_type=F32)
    h1, h3 = h[:, :SHARED_DIM], h[:, SHARED_DIM:]
    h_ref[...] = (h1 * _sigmoid(h1) * h3).astype(BF16)


def _shared_up(x1, ws1, ws3):
    n, d = x1.shape
    tm = min(512, n)
    w13 = jnp.concatenate([ws1, ws3], axis=1).astype(BF16)
    return pl.pallas_call(
        _shared_up_kernel, grid=(n // tm,),
        in_specs=[pl.BlockSpec((tm, d), lambda i: (i, 0)),
                  pl.BlockSpec((d, 2 * SHARED_DIM), lambda i: (0, 0))],
        out_specs=pl.BlockSpec((tm, SHARED_DIM), lambda i: (i, 0)),
        out_shape=jax.ShapeDtypeStruct((n, SHARED_DIM), BF16),
        compiler_params=_cparams(1), name="shared_up",
    )(x1, w13)


def _combine_kernel(pos_ref, x1_ref, h_ref, ws2_ref, gate_ref, g_ref, b_ref, y_hbm, o_ref, ybuf, sem,
                    *, alpha):
    tm = x1_ref.shape[0]

    def row_copy(t, k):
        return pltpu.make_async_copy(y_hbm.at[pl.ds(pos_ref[k, t], 1), :], ybuf.at[pl.ds(k * tm + t, 1), :], sem)

    def start(t, carry):
        for k in range(TOP_K):
            row_copy(t, k).start(priority=k % 2)
        return carry

    lax.fori_loop(0, tm, start, 0, unroll=ROW_COPY_UNROLL)
    base = alpha * x1_ref[...] + jnp.dot(h_ref[...], ws2_ref[...], preferred_element_type=F32)
    pltpu.make_async_copy(y_hbm.at[pl.ds(0, tm * TOP_K), :], ybuf, sem).wait()

    gates = gate_ref[...]
    acc_lo = jnp.zeros((tm, D_MODEL // 2), F32)
    acc_hi = jnp.zeros((tm, D_MODEL // 2), F32)
    for k in range(TOP_K):
        lo, hi = _unpack_halves(ybuf[k * tm:(k + 1) * tm, :])
        gk = gates[:, k:k + 1]
        acc_lo = acc_lo + gk * lo
        acc_hi = acc_hi + gk * hi
    f = jnp.concatenate([acc_lo, acc_hi], axis=1)
    o_ref[...] = _layer_norm(base + f, g_ref[...], b_ref[...])


def _combine(x1, hsh, ws2, pos, gates_t, ys, ln_g, ln_b, alpha):
    n, d = x1.shape
    tm = min(SCATTER_TOKENS, n)
    row = pl.BlockSpec((1, d), lambda i: (0, 0))
    return pl.pallas_call(
        functools.partial(_combine_kernel, alpha=alpha), grid=(n // tm,),
        in_specs=[pl.BlockSpec((TOP_K, tm), lambda i: (0, i), memory_space=pltpu.SMEM),
                  pl.BlockSpec((tm, d), lambda i: (i, 0)),
                  pl.BlockSpec((tm, SHARED_DIM), lambda i: (i, 0)),
                  pl.BlockSpec((SHARED_DIM, d), lambda i: (0, 0)),
                  pl.BlockSpec((tm, TOP_K), lambda i: (i, 0)),
                  row, row,
                  pl.BlockSpec(memory_space=pl.ANY)],
        out_specs=pl.BlockSpec((tm, d), lambda i: (i, 0)),
        out_shape=jax.ShapeDtypeStruct((n, d), F32),
        scratch_shapes=[pltpu.VMEM((TOP_K * tm, d // 2), U32), pltpu.SemaphoreType.DMA],
        compiler_params=_cparams(1), name="moe_combine",
    )(pos, x1, hsh, ws2.astype(BF16), gates_t, ln_g.astype(F32).reshape(1, d),
      ln_b.astype(F32).reshape(1, d), ys)


def _moe(x1, x1p, w_router, router_bias, w1, w3, w2, ws1, ws3, ws2, ln_g, ln_b, alpha):
    n = x1.shape[0]
    bm = EXPERT_ROWS
    eidx, gates, rank, cnt = _router(x1, w_router, router_bias)
    counts = cnt[:, 0].astype(I32)
    padded = (counts + bm - 1) // bm * bm
    pend = jnp.cumsum(padded)
    pstart = pend - padded
    pos = _positions(pstart, eidx, rank)
    n_rows = -(-(n * TOP_K) // bm) * bm + N_EXPERTS * bm
    blk_start = jnp.arange(n_rows // bm, dtype=I32) * bm
    blk_on = (blk_start < pend[-1]).astype(I32)
    last = jnp.maximum(pend[-1] - bm, 0)
    blk_e = jnp.sum((pend[None, :] <= jnp.minimum(blk_start, last)[:, None]).astype(I32), axis=1)
    blk_e = jnp.minimum(blk_e, N_EXPERTS - 1)
    changed = jnp.concatenate([jnp.ones((1,), I32), (blk_e[1:] != blk_e[:-1]).astype(I32)])
    blk_new = blk_on * changed
    xs = _scatter_rows(x1p, pos, pend, padded, n_rows)
    ys = _experts(xs, blk_e, blk_on, blk_new, w1, w3, w2)
    hsh = _shared_up(x1, ws1, ws3)
    return _combine(x1, hsh, ws2, pos, gates.T, ys, ln_g, ln_b, alpha)


def kernel(x, positions, w_in, conv_w, A_log, dt_bias, gdn_norm_w, idx_ln_g, idx_ln_b, w_o_attn, w_o_gdn, w_out, ln1_g, ln1_b, w_router, router_bias, w1, w3, w2, ws1, ws3, ws2, ln2_g, ln2_b):
    batch, seq, d = x.shape
    depth = w_in.shape[0]
    alpha = (2 * depth) ** 0.25
    n = batch * seq
    xf = x.reshape(n, d)
    tabs = _trig_tables(positions)
    for l in range(depth):
        proj = _project(xf, _pack_w_in(w_in[l]))
        q_r, k_r, v_ext, iq_r, ik_lo, ik_hi, iw = _att_prep(proj, tabs, idx_ln_g[l], idx_ln_b[l])
        att = _dsa_attention(q_r, k_r, v_ext, iq_r, ik_lo, ik_hi, iw, batch, seq)
        gdn = _gated_deltanet(proj, conv_w[l], A_log[l], dt_bias[l], gdn_norm_w[l], batch, seq)
        merged = _merge(att, gdn, w_o_attn[l], w_o_gdn[l], proj)
        x1, x1p = _outproj_ln(merged, w_out[l], xf, ln1_g[l], ln1_b[l], alpha)
        xf = _moe(x1, x1p, w_router[l], router_bias[l], w1[l], w3[l], w2[l], ws1[l], ws3[l], ws2[l],
                  ln2_g[l], ln2_b[l], alpha)
    return xf.reshape(batch, seq, d)
```

```python
import functools
import math

import jax
import jax.numpy as jnp
import numpy as np
from jax import lax
from jax.experimental import pallas as pl
from jax.experimental.pallas import tpu as pltpu

F32 = jnp.float32
BF16 = jnp.bfloat16
I32 = jnp.int32
U32 = jnp.uint32
HI = lax.Precision.HIGHEST

D_MODEL = 2048
ATT_HEADS = 16
ATT_KV_HEADS = 2
HEAD_DIM = 128
IDX_HEADS = 8
IDX_DIM = 64
TOPK_MAX = 256
Q_BLOCK = 128
ROPE_THETA = 10000.0
GDN_HEADS = 16
GDN_DK = 128
GDN_DV = 128
CONV_WIDTH = 4
CHUNK = 64
N_EXPERTS = 64
TOP_K = 8
N_GROUPS = 8
TOPK_GROUPS = 4
EXPERT_DIM = 512
SHARED_DIM = 512
ROUTED_SCALE = 2.5
LN_EPS = 1e-5
RMS_EPS = 1e-6

ATT_Q = ATT_HEADS * HEAD_DIM
ATT_KV = ATT_KV_HEADS * HEAD_DIM
IDX_Q = IDX_HEADS * IDX_DIM
GDN_QK = GDN_HEADS * GDN_DK
GDN_V = GDN_HEADS * GDN_DV
SPLITS = (ATT_Q, ATT_KV, ATT_KV, IDX_Q, IDX_DIM, IDX_HEADS, GDN_QK, GDN_QK, GDN_V, GDN_V,
          GDN_HEADS, GDN_HEADS, D_MODEL, D_MODEL)
SPLIT_OFFSETS = tuple(int(o) for o in np.cumsum(SPLITS)[:-1])

LANES = 128
VMEM_LIMIT_BYTES = 56 * 1024 * 1024

OFF_AQ = 0
OFF_GQ = OFF_AQ + ATT_Q
OFF_GK = OFF_GQ + GDN_QK
OFF_GV = OFF_GK + GDN_QK
OFF_GZ = OFF_GV + GDN_V
OFF_GTA = OFF_GZ + GDN_V
OFF_GTB = OFF_GTA + D_MODEL
OFF_AK = OFF_GTB + D_MODEL
OFF_AV = OFF_AK + ATT_KV
OFF_IQ = OFF_AV + ATT_KV
OFF_SM = OFF_IQ + IDX_Q
PROJ_W = 16384
SM_IK = 0
SM_IW = SM_IK + IDX_DIM
SM_BETA = SM_IW + IDX_HEADS
SM_DT = SM_BETA + GDN_HEADS

KEY_CHUNK = 512
EXPERT_ROWS = 512
INT_MIN = -2147483648
MASK_BIAS = -1e30
M_INIT = -1e29


def _cparams(n_axes):
    return pltpu.CompilerParams(dimension_semantics=("arbitrary",) * n_axes,
                                vmem_limit_bytes=VMEM_LIMIT_BYTES)


def _nt(a, b, precision=None):
    return lax.dot_general(a, b, (((1,), (1,)), ((), ())), precision=precision,
                           preferred_element_type=F32)


def _tn(a, b, precision=None):
    return lax.dot_general(a, b, (((0,), (0,)), ((), ())), precision=precision,
                           preferred_element_type=F32)


def _sigmoid(x):
    return jax.nn.sigmoid(x)


def _layer_norm(x, g, b):
    mu = jnp.mean(x, axis=-1, keepdims=True)
    xc = x - mu
    var = jnp.mean(xc * xc, axis=-1, keepdims=True)
    return xc * lax.rsqrt(var + LN_EPS) * g + b


def _trig_kernel(pos_ref, inv_a_ref, inv_i_ref, sgn_a_ref, sgn_i_ref, cos_a, sin_a, cos_i, sin_i):
    p = pos_ref[...]
    ang_a = p * inv_a_ref[...]
    cos_a[...] = jnp.cos(ang_a)
    sin_a[...] = jnp.sin(ang_a) * sgn_a_ref[...]
    ang_i = p * inv_i_ref[...]
    cos_i[...] = jnp.cos(ang_i)
    sin_i[...] = jnp.sin(ang_i) * sgn_i_ref[...]


def _trig_tables(positions):
    n = positions.size
    tm = min(512, n)
    pos = positions.reshape(n, 1).astype(F32)
    inv_a = ROPE_THETA ** (-jnp.arange(0, HEAD_DIM, 2, dtype=F32) / HEAD_DIM)
    inv_i = ROPE_THETA ** (-jnp.arange(0, IDX_DIM, 2, dtype=F32) / IDX_DIM)
    half_a, half_i = HEAD_DIM // 2, IDX_DIM // 2
    inv_a_row = jnp.tile(inv_a, 2).reshape(1, LANES)
    inv_i_row = jnp.tile(inv_i, 4).reshape(1, LANES)
    lane = np.arange(LANES)
    sgn_a = jnp.asarray(np.where(lane < half_a, -1.0, 1.0), F32).reshape(1, LANES)
    sgn_i = jnp.asarray(np.where(lane % IDX_DIM < half_i, -1.0, 1.0), F32).reshape(1, LANES)
    row = pl.BlockSpec((1, LANES), lambda i: (0, 0))
    tab = pl.BlockSpec((tm, LANES), lambda i: (i, 0))
    return pl.pallas_call(
        _trig_kernel, grid=(n // tm,),
        in_specs=[pl.BlockSpec((tm, 1), lambda i: (i, 0)), row, row, row, row],
        out_specs=[tab, tab, tab, tab],
        out_shape=[jax.ShapeDtypeStruct((n, LANES), F32)] * 4,
        compiler_params=_cparams(1), name="trig_tables",
    )(pos, inv_a_row, inv_i_row, sgn_a, sgn_i)


def _pack_w_in(w_in):
    (aq, ak, av, iq, ik, iw, gq, gk, gv, gz, gb, ga, gta, gtb) = jnp.split(w_in.astype(BF16), SPLIT_OFFSETS,
                                                                             axis=-1)
    used = OFF_SM + IDX_DIM + IDX_HEADS + 2 * GDN_HEADS
    pad = jnp.zeros((w_in.shape[0], PROJ_W - used), BF16)
    return jnp.concatenate([aq, gq, gk, gv, gz, gta, gtb, ak, av, iq, ik, iw, gb, ga, pad], axis=-1)


def _proj_kernel(x_ref, w_ref, o_ref, xb_ref):
    @pl.when(pl.program_id(1) == 0)
    def _():
        xb_ref[...] = x_ref[...].astype(BF16)

    o_ref[...] = jnp.dot(xb_ref[...], w_ref[...], preferred_element_type=F32)


def _project(x2d, w_packed):
    n, d = x2d.shape
    tm, tn = min(1024, n), 1024
    return pl.pallas_call(
        _proj_kernel, grid=(n // tm, PROJ_W // tn),
        in_specs=[pl.BlockSpec((tm, d), lambda i, j: (i, 0)),
                  pl.BlockSpec((d, tn), lambda i, j: (0, j))],
        out_specs=pl.BlockSpec((tm, tn), lambda i, j: (i, j)),
        out_shape=jax.ShapeDtypeStruct((n, PROJ_W), F32),
        scratch_shapes=[pltpu.VMEM((tm, d), BF16)],
        compiler_params=_cparams(2), name="in_proj",
    )(x2d, w_packed)


def _rope_head(x, cos, sin_signed):
    return x * cos + pltpu.roll(x, HEAD_DIM // 2, 1) * sin_signed


def _rope_idx(x, cos, sin_signed, first_half):
    half = IDX_DIM // 2
    partner = jnp.where(first_half, pltpu.roll(x, LANES - half, 1), pltpu.roll(x, half, 1))
    return x * cos + partner * sin_signed


def _att_prep_kernel(aq_ref, ak_ref, av_ref, iq_ref, sm_ref, cos_a_ref, sin_a_ref, cos_i_ref,
                     sin_i_ref, lng_ref, lnb_ref,
                     q_out, k_out, v_out, iq_out, iklo_out, ikhi_out, iw_out):
    cos_a, sin_a = cos_a_ref[...], sin_a_ref[...]
    cos_i, sin_i = cos_i_ref[...], sin_i_ref[...]
    rows = cos_a.shape[0]
    lane = lax.broadcasted_iota(I32, (rows, LANES), 1)
    first_half = (lane % IDX_DIM) < (IDX_DIM // 2)

    for h in range(ATT_HEADS):
        q_h = _rope_head(aq_ref[:, h * HEAD_DIM:(h + 1) * HEAD_DIM], cos_a, sin_a) * (
            HEAD_DIM ** -0.5 * math.log2(math.e))
        q_out[0, h] = q_h.astype(BF16)
    ones = jnp.ones((rows, HEAD_DIM), BF16)
    for g in range(ATT_KV_HEADS):
        sl = slice(g * HEAD_DIM, (g + 1) * HEAD_DIM)
        k_out[:, sl] = _rope_head(ak_ref[:, sl], cos_a, sin_a).astype(BF16)
        v_out[:, 2 * g * HEAD_DIM:(2 * g + 1) * HEAD_DIM] = av_ref[:, sl].astype(BF16)
        v_out[:, (2 * g + 1) * HEAD_DIM:(2 * g + 2) * HEAD_DIM] = ones
    for j in range(IDX_HEADS // 2):
        iq_out[0, j] = _rope_idx(iq_ref[:, j * LANES:(j + 1) * LANES], cos_i, sin_i, first_half).astype(BF16)

    sm = sm_ref[...]
    in_ik = lane < IDX_DIM
    mu = jnp.sum(jnp.where(in_ik, sm, 0.0), axis=1, keepdims=True) / IDX_DIM
    xc = sm - mu
    var = jnp.sum(jnp.where(in_ik, xc * xc, 0.0), axis=1, keepdims=True) / IDX_DIM
    ik = xc * lax.rsqrt(var + LN_EPS) * lng_ref[...] + lnb_ref[...]
    ik = jnp.where(in_ik, _rope_idx(ik, cos_i, sin_i, first_half), 0.0)
    iklo_out[...] = ik.astype(BF16)
    ikhi_out[...] = pltpu.roll(ik, IDX_DIM, 1).astype(BF16)
    iw_out[...] = (sm * IDX_HEADS ** -0.5) * IDX_DIM ** -0.5


def _att_prep(proj, tabs, idx_ln_g, idx_ln_b):
    n = proj.shape[0]
    tm = Q_BLOCK
    cos_a, sin_a, cos_i, sin_i = tabs
    pad = jnp.zeros((LANES - IDX_DIM,), F32)
    lng = jnp.concatenate([idx_ln_g.astype(F32), pad]).reshape(1, LANES)
    lnb = jnp.concatenate([idx_ln_b.astype(F32), pad]).reshape(1, LANES)
    tab = pl.BlockSpec((tm, LANES), lambda i: (i, 0))
    row = pl.BlockSpec((1, LANES), lambda i: (0, 0))

    def col(width, off):
        return pl.BlockSpec((tm, width), lambda i: (i, off // width))

    return pl.pallas_call(
        _att_prep_kernel, grid=(n // tm,),
        in_specs=[col(ATT_Q, OFF_AQ), col(ATT_KV, OFF_AK), col(ATT_KV, OFF_AV), col(IDX_Q, OFF_IQ),
                  col(LANES, OFF_SM), tab, tab, tab, tab, row, row],
        out_specs=[pl.BlockSpec((1, ATT_HEADS, tm, HEAD_DIM), lambda i: (i, 0, 0, 0)),
                   pl.BlockSpec((tm, ATT_KV), lambda i: (i, 0)),
                   pl.BlockSpec((tm, 2 * ATT_KV), lambda i: (i, 0)),
                   pl.BlockSpec((1, IDX_HEADS // 2, tm, LANES), lambda i: (i, 0, 0, 0)),
                   tab, tab, tab],
        out_shape=[jax.ShapeDtypeStruct((n // tm, ATT_HEADS, tm, HEAD_DIM), BF16),
                   jax.ShapeDtypeStruct((n, ATT_KV), BF16),
                   jax.ShapeDtypeStruct((n, 2 * ATT_KV), BF16),
                   jax.ShapeDtypeStruct((n // tm, IDX_HEADS // 2, tm, LANES), BF16),
                   jax.ShapeDtypeStruct((n, LANES), BF16),
                   jax.ShapeDtypeStruct((n, LANES), BF16),
                   jax.ShapeDtypeStruct((n, LANES), F32)],
        compiler_params=_cparams(1), name="att_prep",
    )(proj, proj, proj, proj, proj, cos_a, sin_a, cos_i, sin_i, lng, lnb)


def _dsa_kernel(q_ref, k_ref, v_ref, iq_ref, iklo_ref, ikhi_ref, iw_ref, o_ref,
                keys_sc, acc_sc, m_sc, wrep_sc, sel_sc, *, topk, seq):
    qb, kc = Q_BLOCK, KEY_CHUNK
    tiles = kc // LANES
    rep = ATT_HEADS // ATT_KV_HEADS
    i = pl.program_id(1)
    nch = (i * qb + qb + kc - 1) // kc
    row = lax.broadcasted_iota(I32, (qb, LANES), 0)
    lane = lax.broadcasted_iota(I32, (qb, LANES), 1)
    t_abs = i * qb + row

    iw = iw_ref[...]
    for h in range(IDX_HEADS):
        wrep_sc[h] = jnp.broadcast_to(iw[:, SM_IW + h:SM_IW + h + 1], (qb, LANES))
    qi = iq_ref[0].reshape((IDX_HEADS // 2) * qb, LANES)

    def score_chunk(c, carry):
        off = pl.multiple_of(c * kc, kc)
        s_lo = _nt(qi, iklo_ref[pl.ds(off, kc), :])
        s_hi = _nt(qi, ikhi_ref[pl.ds(off, kc), :])
        for ct in range(tiles):
            cs = slice(ct * LANES, (ct + 1) * LANES)
            acc = jnp.zeros((qb, LANES), F32)
            for j in range(IDX_HEADS // 2):
                rs = slice(j * qb, (j + 1) * qb)
                acc = acc + wrep_sc[2 * j] * jnp.maximum(s_lo[rs, cs], 0.0)
                acc = acc + wrep_sc[2 * j + 1] * jnp.maximum(s_hi[rs, cs], 0.0)
            bits = lax.bitcast_convert_type(acc, I32)
            key = bits ^ ((bits >> 31) & jnp.int32(0x7FFFFFFF))
            key = jnp.where(bits == jnp.int32(INT_MIN), 0, key)
            causal = (off + ct * LANES + lane) <= t_abs
            keys_sc[c, :, cs] = jnp.where(causal, key, jnp.int32(INT_MIN))
        return carry

    lax.fori_loop(0, nch, score_chunk, 0)

    kth = jnp.float32(topk)

    def count(pred):
        def body(c, cnt):
            kk = keys_sc[c]
            off = c * kc
            for ct in range(tiles):
                cs = slice(ct * LANES, (ct + 1) * LANES)
                cnt = cnt + jnp.where(pred(kk[:, cs], off + ct * LANES + lane), 1.0, 0.0)
            return cnt

        cnt = lax.fori_loop(0, nch, body, jnp.zeros((qb, LANES), F32))
        return jnp.sum(cnt, axis=1, keepdims=True)

    zero = jnp.zeros((qb, LANES), I32)
    n_pos = count(lambda k, s: k >= zero)
    base = jnp.where(n_pos >= kth, zero, jnp.int32(INT_MIN))
    n_base = jnp.where(n_pos >= kth, n_pos, -1.0)
    short = (i * qb + lax.broadcasted_iota(I32, (qb, 1), 0)) < topk

    def unresolved(n_base):
        return jnp.max(jnp.where((n_base == kth) | short, 0.0, 1.0)) > 0.0

    def bit_cond(st):
        it, _, n_base = st
        return (it < 31) & unresolved(n_base)

    def bit_step(st):
        it, base, n_base = st
        cand = base | (jnp.int32(1) << (30 - it))
        tot = count(lambda k, s: k >= cand)
        take = tot >= kth
        return it + 1, jnp.where(take, cand, base), jnp.where(take, tot, n_base)

    _, thr, n_ge = lax.while_loop(bit_cond, bit_step, (jnp.int32(0), base, n_base))
    sel_sc[0] = thr
    sel_sc[1] = jnp.full((qb, LANES), seq, I32)

    @pl.when(jnp.max(n_ge) > kth)
    def _():
        nbits = max(1, (seq - 1).bit_length())
        need = kth - count(lambda k, s: k > thr)

        def idx_step(it, posv):
            cand = posv + (jnp.int32(1) << (nbits - 1 - it))
            below = count(lambda k, s: (k == thr) & (s < cand))
            return jnp.where(below < need, cand, posv)

        sel_sc[1] = lax.fori_loop(0, nbits, idx_step, zero)

    thr = sel_sc[0]
    last_tie = sel_sc[1]

    m_sc[...] = jnp.full(m_sc.shape, M_INIT, F32)
    acc_sc[...] = jnp.zeros(acc_sc.shape, F32)

    def attend_chunk(c, carry):
        off = pl.multiple_of(c * kc, kc)
        kk = keys_sc[c]
        bias = []
        for ct in range(tiles):
            cs = slice(ct * LANES, (ct + 1) * LANES)
            sidx = off + ct * LANES + lane
            keep = (kk[:, cs] > thr) | ((kk[:, cs] == thr) & (sidx <= last_tie))
            keep = keep & (sidx <= t_abs)
            bias.append(jnp.where(keep, 0.0, MASK_BIAS))
        for g in range(ATT_KV_HEADS):
            kg = k_ref[pl.ds(off, kc), g * HEAD_DIM:(g + 1) * HEAD_DIM]
            vg = v_ref[pl.ds(off, kc), 2 * g * HEAD_DIM:(2 * g + 2) * HEAD_DIM]
            qg = q_ref[0, g * rep:(g + 1) * rep].reshape(rep * qb, HEAD_DIM)
            s = _nt(qg, kg)
            s_t = [(s[:, ct * LANES:(ct + 1) * LANES].reshape(rep, qb, LANES) + bias[ct][None]
                    ).reshape(rep * qb, LANES) for ct in range(tiles)]
            mx = s_t[0]
            for ct in range(1, tiles):
                mx = jnp.maximum(mx, s_t[ct])
            m_old = m_sc[g]
            m_new = jnp.maximum(m_old, jnp.max(mx, axis=1, keepdims=True))
            p = jnp.concatenate([jnp.exp2(st - m_new) for st in s_t], axis=1).astype(BF16)
            alpha = jnp.exp2(m_old - m_new)
            pv = jnp.dot(p, vg, preferred_element_type=F32)
            acc_sc[g] = acc_sc[g] * jnp.concatenate([alpha, alpha], axis=1) + pv
            m_sc[g] = m_new
        return carry

    lax.fori_loop(0, nch, attend_chunk, 0)

    for g in range(ATT_KV_HEADS):
        for r in range(rep):
            a = acc_sc[g, r * qb:(r + 1) * qb, :]
            h = g * rep + r
            o_ref[:, h * HEAD_DIM:(h + 1) * HEAD_DIM] = (a[:, :HEAD_DIM] / a[:, HEAD_DIM:]).astype(BF16)


def _dsa_attention(q_r, k_r, v_ext, iq_r, ik_lo, ik_hi, iw, batch, seq):
    n = batch * seq
    nqb = seq // Q_BLOCK
    topk = min(TOPK_MAX, seq // 4)
    rep = ATT_HEADS // ATT_KV_HEADS
    kern = functools.partial(_dsa_kernel, topk=topk, seq=seq)

    def per_batch(width):
        return pl.BlockSpec((None, seq, width), lambda b, i: (b, 0, 0))

    return pl.pallas_call(
        kern, grid=(batch, nqb),
        in_specs=[pl.BlockSpec((1, ATT_HEADS, Q_BLOCK, HEAD_DIM), lambda b, i: (b * nqb + i, 0, 0, 0)),
                  per_batch(ATT_KV), per_batch(2 * ATT_KV),
                  pl.BlockSpec((1, IDX_HEADS // 2, Q_BLOCK, LANES), lambda b, i: (b * nqb + i, 0, 0, 0)),
                  per_batch(LANES), per_batch(LANES),
                  pl.BlockSpec((Q_BLOCK, LANES), lambda b, i: (b * nqb + i, 0))],
        out_specs=pl.BlockSpec((Q_BLOCK, ATT_Q), lambda b, i: (b * nqb + i, 0)),
        out_shape=jax.ShapeDtypeStruct((n, ATT_Q), BF16),
        scratch_shapes=[pltpu.VMEM((seq // KEY_CHUNK, Q_BLOCK, KEY_CHUNK), I32),
                        pltpu.VMEM((ATT_KV_HEADS, rep * Q_BLOCK, 2 * HEAD_DIM), F32),
                        pltpu.VMEM((ATT_KV_HEADS, rep * Q_BLOCK, LANES), F32),
                        pltpu.VMEM((IDX_HEADS, Q_BLOCK, LANES), F32),
                        pltpu.VMEM((2, Q_BLOCK, LANES), I32)],
        compiler_params=_cparams(2), name="dsa_attention",
    )(q_r, k_r.reshape(batch, seq, ATT_KV), v_ext.reshape(batch, seq, 2 * ATT_KV), iq_r,
      ik_lo.reshape(batch, seq, LANES), ik_hi.reshape(batch, seq, LANES), iw)


GDN_PREP_HEADS = 8
CONV_HALO = 8


def _split2(x):
    hi = x.astype(BF16)
    return hi, (x - hi.astype(F32)).astype(BF16)


def _split3(x):
    hi = x.astype(BF16)
    r = x - hi.astype(F32)
    mid = r.astype(BF16)
    return hi, mid, (r - mid.astype(F32)).astype(BF16)


def _lhs3(hi, lo):
    return jnp.concatenate([hi, lo, hi], axis=1)


def _rhs3(hi, lo):
    return jnp.concatenate([hi, hi, lo], axis=0)


def _gdn_prep_kernel(q_ref, k_ref, v_ref, qp_ref, kp_ref, vp_ref, sm_ref, gat_ref,
                     cwq_ref, cwk_ref, cwv_ref, alog_row_ref, dtb_row_ref, alog_col_ref, dtb_col_ref,
                     u_ref, w_ref, qd_ref, kd_ref, a_ref, e_ref, xx_sc, gct_sc, *, nc):
    hb = GDN_PREP_HEADS
    c = CHUNK
    hg = pl.program_id(1)
    chunk = pl.program_id(0) % nc
    heads = range(hb)

    def conv_silu(slot, cur_ref, prev_ref, cw_ref):
        prev = prev_ref[...]
        at_start = jnp.full(prev.shape, chunk, I32) == 0
        xx_sc[slot, 0:CONV_HALO, :] = jnp.where(at_start, 0.0, prev)
        xx_sc[slot, CONV_HALO:CONV_HALO + c, :] = cur_ref[...]
        cw = cw_ref[...]
        start = CONV_HALO - (CONV_WIDTH - 1)
        acc = xx_sc[slot, pl.ds(start, c), :] * cw[0:1, :]
        for j in range(1, CONV_WIDTH):
            acc = acc + xx_sc[slot, pl.ds(start + j, c), :] * cw[j:j + 1, :]
        return acc * _sigmoid(acc)

    qa = conv_silu(0, q_ref, qp_ref, cwq_ref)
    ka = conv_silu(1, k_ref, kp_ref, cwk_ref)
    va = conv_silu(2, v_ref, vp_ref, cwv_ref)

    rowi = lax.broadcasted_iota(I32, (c, c), 0)
    coli = lax.broadcasted_iota(I32, (c, c), 1)
    lower = rowi >= coli
    strict = rowi > coli
    eye = jnp.where(rowi == coli, 1.0, 0.0)

    sm = sm_ref[...]
    beta_slab = _sigmoid(sm)
    g_slab = -jnp.exp(alog_row_ref[...]) * jax.nn.softplus(sm + dtb_row_ref[...])
    ones_l = jnp.where(lower, 1.0, 0.0).astype(BF16)
    ones_u = jnp.where(rowi <= coli, 1.0, 0.0).astype(BF16)
    gc_slab = jnp.dot(jnp.concatenate([ones_l] * 3, axis=1), jnp.concatenate(_split3(g_slab), axis=0),
                      preferred_element_type=F32)
    g_t = -jnp.exp(alog_col_ref[:, :c]) * jax.nn.softplus(gat_ref[...] + dtb_col_ref[:, :c])
    gct_sc[...] = jnp.dot(jnp.concatenate(_split3(g_t), axis=1), jnp.concatenate([ones_u] * 3, axis=0),
                          preferred_element_type=F32)
    lane = lax.broadcasted_iota(I32, (c, LANES), 1)
    hsl = [slice(r * GDN_DK, (r + 1) * GDN_DK) for r in heads]

    gcc = [jnp.sum(jnp.where(lane == SM_DT + hg * hb + r, gc_slab, 0.0), axis=1, keepdims=True) for r in heads]
    bcol = [jnp.sum(jnp.where(lane == SM_BETA + hg * hb + r, beta_slab, 0.0), axis=1, keepdims=True)
            for r in heads]
    gcr = [gct_sc[pl.ds(hg * hb + r, 1), :] for r in heads]
    glast = [g[:, c - 1:c] for g in gcr]
    qn = [qa[:, s] * lax.rsqrt(jnp.sum(qa[:, s] * qa[:, s], axis=1, keepdims=True) + RMS_EPS) * GDN_DK ** -0.5
          for s in hsl]
    kn = [ka[:, s] * lax.rsqrt(jnp.sum(ka[:, s] * ka[:, s], axis=1, keepdims=True) + RMS_EPS) for s in hsl]
    eg = [jnp.exp(g) for g in gcc]
    decay = [jnp.where(lower, jnp.exp(gcc[r] - gcr[r]), 0.0) for r in heads]
    kb = [kn[r] * bcol[r] for r in heads]

    def nt3(a, b):
        (ah, al), (bh, bl) = _split2(a), _split2(b)
        return _nt(_lhs3(ah, al), jnp.concatenate([bh, bh, bl], axis=1))

    kq = [nt3(jnp.concatenate([kb[r], qn[r]], axis=0), kn[r]) for r in heads]
    pairs = range(hb // 2)
    r2 = lax.broadcasted_iota(I32, (2 * c, 2 * c), 0)
    c2 = lax.broadcasted_iota(I32, (2 * c, 2 * c), 1)
    same_head = (r2 < c) == (c2 < c)

    def diag2(m):
        return jnp.where(same_head, jnp.concatenate([m, m], axis=0), jnp.zeros((), m.dtype))

    def diag_rhs3(hi, lo):
        return _rhs3(diag2(hi), diag2(lo))

    y = [jnp.concatenate([jnp.where(strict, -(kq[2 * p + s][:c] * decay[2 * p + s]), 0.0) for s in range(2)],
                         axis=1) for p in pairs]
    eye2 = jnp.concatenate([eye, eye], axis=1)
    t_inv = [eye2 + y[p] for p in pairs]
    ys = [_split2(v) for v in y]
    y = [jnp.dot(_lhs3(*ys[p]), diag_rhs3(*ys[p]), preferred_element_type=F32) for p in pairs]
    n_fac = int(math.log2(c))
    for j in range(1, n_fac):
        ys = [_split2(v) for v in y]
        ts = [_split2(v) for v in t_inv]
        if j < n_fac - 1:
            prod = [jnp.dot(jnp.concatenate([_lhs3(*ys[p]), _lhs3(*ts[p])], axis=0), diag_rhs3(*ys[p]),
                            preferred_element_type=F32) for p in pairs]
            y = [q[:c] for q in prod]
            t_inv = [t_inv[p] + prod[p][c:] for p in pairs]
        else:
            t_inv = [t_inv[p] + jnp.dot(_lhs3(*ts[p]), diag_rhs3(*ys[p]), preferred_element_type=F32)
                     for p in pairs]
    ts = [_split2(v) for v in t_inv]
    rhs = [jnp.concatenate([jnp.concatenate([va[:, hsl[r]] * bcol[r], kb[r] * eg[r]], axis=1)
                            for r in (2 * p, 2 * p + 1)], axis=0) for p in pairs]
    uw2 = [jnp.dot(_lhs3(diag2(ts[p][0]), diag2(ts[p][1])), _rhs3(*_split2(rhs[p])),
                   preferred_element_type=F32) for p in pairs]
    uw = [uw2[r // 2][(r % 2) * c:(r % 2 + 1) * c] for r in heads]

    for r in heads:
        u_ref[:, hsl[r]] = uw[r][:, :GDN_DV]
        w_ref[:, hsl[r]] = uw[r][:, GDN_DV:].astype(BF16)
        qd_ref[:, hsl[r]] = (qn[r] * eg[r]).astype(BF16)
        kd_ref[:, hsl[r]] = (kn[r] * jnp.exp(glast[r] - gcc[r])).astype(BF16)
        a_ref[:, r * c:(r + 1) * c] = (kq[r][c:] * decay[r]).astype(BF16)
        e_ref[:, hsl[r]] = jnp.broadcast_to(jnp.exp(glast[r]), (CONV_HALO, GDN_DK))


def _gdn_scan_kernel(u_ref, w_ref, qd_ref, kd_ref, a_ref, e_ref, z_ref, nw_ref, o_ref, st_sc):
    c = CHUNK
    heads = range(GDN_HEADS)
    hsl = [slice(h * GDN_DK, (h + 1) * GDN_DK) for h in heads]

    @pl.when(pl.program_id(1) == 0)
    def _():
        st_sc[...] = jnp.zeros(st_sc.shape, F32)

    st = [st_sc[h] for h in heads]
    ws_qs = [jnp.dot(jnp.concatenate([w_ref[:, hsl[h]], qd_ref[:, hsl[h]]], axis=0), st[h].astype(BF16),
                     preferred_element_type=F32) for h in heads]
    v_new = [(u_ref[:, hsl[h]] - ws_qs[h][:c]).astype(BF16) for h in heads]
    o = [ws_qs[h][c:] + jnp.dot(a_ref[:, h * c:(h + 1) * c], v_new[h], preferred_element_type=F32)
         for h in heads]
    for h in heads:
        st_sc[h] = st[h] * e_ref[0:1, hsl[h]] + _tn(kd_ref[:, hsl[h]], v_new[h])
    for h in heads:
        z = z_ref[:, hsl[h]]
        on = o[h] * lax.rsqrt(jnp.mean(o[h] * o[h], axis=1, keepdims=True) + RMS_EPS)
        o_ref[:, hsl[h]] = (on * nw_ref[...] * (z * _sigmoid(z))).astype(BF16)


def _gated_deltanet(proj, conv_w, a_log, dt_bias, norm_w, batch, seq):
    n = batch * seq
    nc = seq // CHUNK
    hb = GDN_PREP_HEADS
    w = hb * GDN_DK
    ga_t = proj[:, OFF_SM + SM_DT:OFF_SM + SM_DT + GDN_HEADS].reshape(batch * nc, CHUNK, GDN_HEADS)
    ga_t = ga_t.transpose(0, 2, 1)
    lane_pad = lambda vec, off: jnp.zeros((1, LANES), F32).at[0, off:off + GDN_HEADS].set(vec.astype(F32))
    alog_row, dtb_row = lane_pad(a_log, SM_DT), lane_pad(dt_bias, SM_DT)
    alog_col = jnp.broadcast_to(a_log.astype(F32)[:, None], (GDN_HEADS, LANES))
    dtb_col = jnp.broadcast_to(dt_bias.astype(F32)[:, None], (GDN_HEADS, LANES))
    nw = norm_w.astype(F32).reshape(1, GDN_DV)
    cw = conv_w.astype(F32)

    def cur(off):
        return pl.BlockSpec((CHUNK, w), lambda i, h: (i, off // w + h))

    def prev(off):
        per = CHUNK // CONV_HALO
        return pl.BlockSpec((CONV_HALO, w), lambda i, h: (jnp.maximum(i * per - 1, 0), off // w + h))

    def cwspec(off):
        return pl.BlockSpec((CONV_WIDTH, w), lambda i, h: (0, off // w + h))

    row = pl.BlockSpec((1, LANES), lambda i, h: (0, 0))
    colv = pl.BlockSpec((GDN_HEADS, LANES), lambda i, h: (0, 0))
    head_tile = pl.BlockSpec((CHUNK, w), lambda i, h: (i, h))
    u, wm, qd, kd, a_in, egl = pl.pallas_call(
        functools.partial(_gdn_prep_kernel, nc=nc), grid=(batch * nc, GDN_HEADS // hb),
        in_specs=[cur(OFF_GQ), cur(OFF_GK), cur(OFF_GV), prev(OFF_GQ), prev(OFF_GK), prev(OFF_GV),
                  pl.BlockSpec((CHUNK, LANES), lambda i, h: (i, OFF_SM // LANES)),
                  pl.BlockSpec((None, GDN_HEADS, CHUNK), lambda i, h: (i, 0, 0)),
                  cwspec(0), cwspec(GDN_QK), cwspec(2 * GDN_QK), row, row, colv, colv],
        out_specs=[head_tile, head_tile, head_tile, head_tile,
                   pl.BlockSpec((CHUNK, hb * CHUNK), lambda i, h: (i, h)),
                   pl.BlockSpec((CONV_HALO, w), lambda i, h: (i, h))],
        out_shape=[jax.ShapeDtypeStruct((n, GDN_V), F32), jax.ShapeDtypeStruct((n, GDN_V), BF16),
                   jax.ShapeDtypeStruct((n, GDN_QK), BF16), jax.ShapeDtypeStruct((n, GDN_QK), BF16),
                   jax.ShapeDtypeStruct((n, GDN_HEADS * CHUNK), BF16),
                   jax.ShapeDtypeStruct((batch * nc * CONV_HALO, GDN_V), F32)],
        scratch_shapes=[pltpu.VMEM((3, CONV_HALO + CHUNK, w), F32),
                        pltpu.VMEM((GDN_HEADS, CHUNK), F32)],
        compiler_params=_cparams(2), name="gdn_prep",
    )(proj, proj, proj, proj, proj, proj, proj, ga_t, cw, cw, cw, alog_row, dtb_row, alog_col, dtb_col)

    def full(width):
        return pl.BlockSpec((CHUNK, width), lambda b, t: (b * nc + t, 0))

    return pl.pallas_call(
        _gdn_scan_kernel, grid=(batch, nc),
        in_specs=[full(GDN_V), full(GDN_V), full(GDN_QK), full(GDN_QK), full(GDN_HEADS * CHUNK),
                  pl.BlockSpec((CONV_HALO, GDN_V), lambda b, t: (b * nc + t, 0)),
                  pl.BlockSpec((CHUNK, GDN_V), lambda b, t: (b * nc + t, OFF_GZ // GDN_V)),
                  pl.BlockSpec((1, GDN_DV), lambda b, t: (0, 0))],
        out_specs=full(GDN_V),
        out_shape=jax.ShapeDtypeStruct((n, GDN_V), BF16),
        scratch_shapes=[pltpu.VMEM((GDN_HEADS, GDN_DK, GDN_DV), F32)],
        compiler_params=_cparams(2), name="gdn_scan",
    )(u, wm, qd, kd, a_in, egl, proj, nw)


def _merge_kernel(att_ref, gdn_ref, wa_ref, wg_ref, ga_ref, gb_ref, o_ref):
    a = jnp.dot(att_ref[...], wa_ref[...], preferred_element_type=F32)
    g = jnp.dot(gdn_ref[...], wg_ref[...], preferred_element_type=F32)
    o_ref[...] = (_sigmoid(ga_ref[...]) * a + _sigmoid(gb_ref[...]) * g).astype(BF16)


def _merge(att, gdn, w_o_attn, w_o_gdn, proj):
    n = att.shape[0]
    tm, tn = min(512, n), 512
    return pl.pallas_call(
        _merge_kernel, grid=(n // tm, D_MODEL // tn),
        in_specs=[pl.BlockSpec((tm, ATT_Q), lambda i, j: (i, 0)),
                  pl.BlockSpec((tm, GDN_V), lambda i, j: (i, 0)),
                  pl.BlockSpec((ATT_Q, tn), lambda i, j: (0, j)),
                  pl.BlockSpec((GDN_V, tn), lambda i, j: (0, j)),
                  pl.BlockSpec((tm, tn), lambda i, j: (i, OFF_GTA // tn + j)),
                  pl.BlockSpec((tm, tn), lambda i, j: (i, OFF_GTB // tn + j))],
        out_specs=pl.BlockSpec((tm, tn), lambda i, j: (i, j)),
        out_shape=jax.ShapeDtypeStruct((n, D_MODEL), BF16),
        compiler_params=_cparams(2), name="mixer_merge",
    )(att, gdn, w_o_attn.astype(BF16), w_o_gdn.astype(BF16), proj, proj)


def _pack_halves(x):
    w = x.shape[1] // 2
    bits = lax.bitcast_convert_type(x.astype(BF16).astype(F32), U32)
    return (bits[:, :w] >> 16) | (bits[:, w:] & jnp.uint32(0xFFFF0000))


def _unpack_halves(u):
    lo = lax.bitcast_convert_type(u << 16, F32)
    hi = lax.bitcast_convert_type(u & jnp.uint32(0xFFFF0000), F32)
    return lo, hi


def _outproj_ln_kernel(m_ref, w_ref, x_ref, g_ref, b_ref, x1_ref, x1p_ref, *, alpha):
    y = jnp.dot(m_ref[...], w_ref[...], preferred_element_type=F32)
    x1 = _layer_norm(alpha * x_ref[...] + y, g_ref[...], b_ref[...])
    x1_ref[...] = x1
    x1p_ref[...] = _pack_halves(x1)


def _outproj_ln(merged, w_out, x2d, ln_g, ln_b, alpha):
    n, d = x2d.shape
    tm = min(256, n)
    row = pl.BlockSpec((1, d), lambda i: (0, 0))
    tile = pl.BlockSpec((tm, d), lambda i: (i, 0))
    return pl.pallas_call(
        functools.partial(_outproj_ln_kernel, alpha=alpha), grid=(n // tm,),
        in_specs=[tile, pl.BlockSpec((d, d), lambda i: (0, 0)), tile, row, row],
        out_specs=[tile, pl.BlockSpec((tm, d // 2), lambda i: (i, 0))],
        out_shape=[jax.ShapeDtypeStruct((n, d), F32), jax.ShapeDtypeStruct((n, d // 2), U32)],
        compiler_params=_cparams(1), name="outproj_ln",
    )(merged, w_out.astype(BF16), x2d, ln_g.astype(F32).reshape(1, d), ln_b.astype(F32).reshape(1, d))


def _first_max(x, iota, size):
    m = jnp.max(x, axis=0, keepdims=True)
    f = jnp.min(jnp.where(x == m, iota, float(size)), axis=0, keepdims=True)
    return m, f


def _router_kernel(x_ref, w_ref, b_ref, eidx_ref, gate_ref, rank_ref, cnt_ref, carry_sc):
    tm = x_ref.shape[0]
    per = N_EXPERTS // N_GROUPS
    neg = -jnp.inf

    @pl.when(pl.program_id(0) == 0)
    def _():
        carry_sc[...] = jnp.zeros(carry_sc.shape, F32)

    logits = _nt(w_ref[...], x_ref[...], precision=HI)
    s = _sigmoid(logits)
    choice = s + b_ref[:, 0:1]
    io_g = lax.broadcasted_iota(I32, (per, tm), 0).astype(F32)
    io_e = lax.broadcasted_iota(I32, (N_EXPERTS, tm), 0).astype(F32)

    gs = []
    for g in range(N_GROUPS):
        xg = choice[g * per:(g + 1) * per]
        m1, f1 = _first_max(xg, io_g, per)
        m2 = jnp.max(jnp.where(io_g == f1, neg, xg), axis=0, keepdims=True)
        gs.append(m1 + m2)
    gw = jnp.concatenate(gs, axis=0)
    gsel = jnp.zeros((N_GROUPS, tm), F32)
    for _ in range(TOPK_GROUPS):
        _, f = _first_max(gw, io_g, N_GROUPS)
        hit = io_g == f
        gsel = jnp.where(hit, 1.0, gsel)
        gw = jnp.where(hit, neg, gw)
    x = jnp.concatenate([jnp.where(gsel[g:g + 1] > 0.0, choice[g * per:(g + 1) * per], neg)
                         for g in range(N_GROUPS)], axis=0)

    mem = jnp.zeros((N_EXPERTS, tm), F32)
    picks, wsel = [], []
    for _ in range(TOP_K):
        _, f = _first_max(x, io_e, N_EXPERTS)
        hit = io_e == f
        picks.append(f)
        wsel.append(jnp.sum(jnp.where(hit, s, 0.0), axis=0, keepdims=True))
        mem = jnp.where(hit, 1.0, mem)
        x = jnp.where(hit, neg, x)
    wsel = jnp.concatenate(wsel, axis=0)
    gate_ref[...] = wsel / jnp.sum(wsel, axis=0, keepdims=True) * ROUTED_SCALE
    eidx_ref[...] = jnp.concatenate(picks, axis=0).astype(I32)

    tr = lax.broadcasted_iota(I32, (tm, tm), 0)
    tc = lax.broadcasted_iota(I32, (tm, tm), 1)
    before = jnp.where(tr < tc, 1.0, 0.0).astype(BF16)
    prefix = jnp.dot(mem.astype(BF16), before, preferred_element_type=F32) + carry_sc[:, 0:1]
    ranks = [jnp.sum(jnp.where(io_e == f, prefix, 0.0), axis=0, keepdims=True) for f in picks]
    rank_ref[...] = jnp.concatenate(ranks, axis=0).astype(I32)
    carry_sc[...] = carry_sc[...] + jnp.sum(mem, axis=1, keepdims=True)
    cnt_ref[...] = carry_sc[...]


def _router(x1, w_router, router_bias):
    n, d = x1.shape
    tm = min(512, n)
    out = pl.BlockSpec((TOP_K, tm), lambda i: (0, i))
    bias = jnp.broadcast_to(router_bias.astype(F32)[:, None], (N_EXPERTS, LANES))
    return pl.pallas_call(
        _router_kernel, grid=(n // tm,),
        in_specs=[pl.BlockSpec((tm, d), lambda i: (i, 0)),
                  pl.BlockSpec((N_EXPERTS, d), lambda i: (0, 0)),
                  pl.BlockSpec((N_EXPERTS, LANES), lambda i: (0, 0))],
        out_specs=[out, out, out, pl.BlockSpec((N_EXPERTS, LANES), lambda i: (0, 0))],
        out_shape=[jax.ShapeDtypeStruct((TOP_K, n), I32), jax.ShapeDtypeStruct((TOP_K, n), F32),
                   jax.ShapeDtypeStruct((TOP_K, n), I32), jax.ShapeDtypeStruct((N_EXPERTS, LANES), F32)],
        scratch_shapes=[pltpu.VMEM((N_EXPERTS, LANES), F32)],
        compiler_params=_cparams(1), name="router",
    )(x1, w_router.astype(F32).T, bias)


SCATTER_TOKENS = 512
COMBINE_TOKENS = 128
ROW_COPY_UNROLL = 8


def _pos_kernel(pstart_ref, eidx_ref, rank_ref, pos_ref):
    e = eidx_ref[...]
    acc = rank_ref[...]
    for x in range(N_EXPERTS):
        acc = acc + jnp.where(e == x, pstart_ref[x], 0)
    pos_ref[...] = acc


def _positions(pstart, eidx, rank):
    n = eidx.shape[1]
    tm = min(2048, n)
    tile = pl.BlockSpec((TOP_K, tm), lambda i, ps: (0, i))
    return pl.pallas_call(
        _pos_kernel,
        grid_spec=pltpu.PrefetchScalarGridSpec(num_scalar_prefetch=1, grid=(n // tm,),
                                               in_specs=[tile, tile], out_specs=tile),
        out_shape=jax.ShapeDtypeStruct((TOP_K, n), I32),
        compiler_params=_cparams(1), name="moe_positions",
    )(pstart, eidx, rank)


def _scatter_kernel(pend_ref, padded_ref, pos_ref, x_ref, xs_ref, zero_sc, sem, zsem):
    tm = x_ref.shape[0]
    bm = EXPERT_ROWS

    @pl.when(pl.program_id(0) == 0)
    def _():
        zero_sc[...] = jnp.zeros(zero_sc.shape, U32)

        def tail_copy(e):
            tail = pl.multiple_of(pend_ref[e] - bm, bm)
            return pltpu.make_async_copy(zero_sc, xs_ref.at[pl.ds(tail, bm), :], zsem)

        def zstart(e, carry):
            @pl.when(padded_ref[e] > 0)
            def _():
                tail_copy(e).start()
            return carry

        def zwait(e, carry):
            @pl.when(padded_ref[e] > 0)
            def _():
                tail_copy(e).wait()
            return carry

        lax.fori_loop(0, N_EXPERTS, zstart, 0)
        lax.fori_loop(0, N_EXPERTS, zwait, 0)

    def row_copy(t, k):
        return pltpu.make_async_copy(x_ref.at[pl.ds(t, 1), :], xs_ref.at[pl.ds(pos_ref[k, t], 1), :], sem)

    def start(t, carry):
        for k in range(TOP_K):
            row_copy(t, k).start(priority=k % 2)
        return carry

    lax.fori_loop(0, tm, start, 0, unroll=ROW_COPY_UNROLL)
    rows = xs_ref.at[pl.ds(0, tm * TOP_K), :]
    pltpu.make_async_copy(rows, rows, sem).wait()


def _scatter_rows(x1p, pos, pend, padded, n_rows):
    n, w = x1p.shape
    tm = min(SCATTER_TOKENS, n)
    grid_spec = pltpu.PrefetchScalarGridSpec(
        num_scalar_prefetch=2, grid=(n // tm,),
        in_specs=[pl.BlockSpec((TOP_K, tm), lambda i, pe, pd: (0, i), memory_space=pltpu.SMEM),
                  pl.BlockSpec((tm, w), lambda i, pe, pd: (i, 0))],
        out_specs=pl.BlockSpec(memory_space=pl.ANY),
        scratch_shapes=[pltpu.VMEM((EXPERT_ROWS, w), U32), pltpu.SemaphoreType.DMA, pltpu.SemaphoreType.DMA])
    return pl.pallas_call(
        _scatter_kernel, grid_spec=grid_spec,
        out_shape=jax.ShapeDtypeStruct((n_rows, w), U32),
        compiler_params=_cparams(1), name="moe_scatter",
    )(pend, padded, pos, x1p)


def _expert_kernel(blk_e_ref, blk_on_ref, blk_new_ref, x_ref, w1_ref, w3_ref, w2_ref, y_ref,
                   w1_sc, w3_sc, w2_sc):
    del blk_e_ref
    i = pl.program_id(0)

    @pl.when(blk_new_ref[i] > 0)
    def _():
        w1_sc[...] = w1_ref[...].astype(BF16)
        w3_sc[...] = w3_ref[...].astype(BF16)
        w2_sc[...] = w2_ref[...].astype(BF16)

    @pl.when(blk_on_ref[i] > 0)
    def _():
        half = D_MODEL // 2
        lo, hi = _unpack_halves(x_ref[...])
        xa, xb = lo.astype(BF16), hi.astype(BF16)

        def up(w_sc):
            return (jnp.dot(xa, w_sc[0:half, :], preferred_element_type=F32)
                    + jnp.dot(xb, w_sc[half:, :], preferred_element_type=F32))

        h1, h3 = up(w1_sc), up(w3_sc)
        act = (h1 * _sigmoid(h1) * h3).astype(BF16)
        y_ref[...] = _pack_halves(jnp.dot(act, w2_sc[...], preferred_element_type=F32))


def _experts(xs, blk_e, blk_on, blk_new, w1, w3, w2):
    n_rows, w = xs.shape
    bm = EXPERT_ROWS
    grid_spec = pltpu.PrefetchScalarGridSpec(
        num_scalar_prefetch=3, grid=(n_rows // bm,),
        in_specs=[pl.BlockSpec((bm, w), lambda i, e, on, nw: (i, 0)),
                  pl.BlockSpec((None, D_MODEL, EXPERT_DIM), lambda i, e, on, nw: (e[i], 0, 0)),
                  pl.BlockSpec((None, D_MODEL, EXPERT_DIM), lambda i, e, on, nw: (e[i], 0, 0)),
                  pl.BlockSpec((None, EXPERT_DIM, D_MODEL), lambda i, e, on, nw: (e[i], 0, 0))],
        out_specs=pl.BlockSpec((bm, w), lambda i, e, on, nw: (i, 0)),
        scratch_shapes=[pltpu.VMEM((D_MODEL, EXPERT_DIM), BF16), pltpu.VMEM((D_MODEL, EXPERT_DIM), BF16),
                        pltpu.VMEM((EXPERT_DIM, D_MODEL), BF16)])
    return pl.pallas_call(
        _expert_kernel, grid_spec=grid_spec,
        out_shape=jax.ShapeDtypeStruct((n_rows, w), U32),
        compiler_params=_cparams(1), name="moe_experts",
    )(blk_e, blk_on, blk_new, xs, w1, w3, w2)


def _shared_up_kernel(x_ref, w_ref, h_ref):
    h = jnp.dot(x_ref[...].astype(BF16), w_ref[...], preferred_element_type=F32)
    h1, h3 = h[:, :SHARED_DIM], h[:, SHARED_DIM:]
    h_ref[...] = (h1 * _sigmoid(h1) * h3).astype(BF16)


def _shared_up(x1, ws1, ws3):
    n, d = x1.shape
    tm = min(512, n)
    w13 = jnp.concatenate([ws1, ws3], axis=1).astype(BF16)
    return pl.pallas_call(
        _shared_up_kernel, grid=(n // tm,),
        in_specs=[pl.BlockSpec((tm, d), lambda i: (i, 0)),
                  pl.BlockSpec((d, 2 * SHARED_DIM), lambda i: (0, 0))],
        out_specs=pl.BlockSpec((tm, SHARED_DIM), lambda i: (i, 0)),
        out_shape=jax.ShapeDtypeStruct((n, SHARED_DIM), BF16),
        compiler_params=_cparams(1), name="shared_up",
    )(x1, w13)


def _combine_kernel(pos_ref, pos_next_ref, x1_ref, h_ref, ws2_ref, gate_ref, g_ref, b_ref, y_hbm, o_ref,
                    ybuf, sem, *, alpha):
    tm = x1_ref.shape[0]
    i = pl.program_id(0)
    slot = i % 2

    def issue(p_ref, s):
        def start(t, carry):
            for k in range(TOP_K):
                pltpu.make_async_copy(y_hbm.at[pl.ds(p_ref[k, t], 1), :],
                                      ybuf.at[s, pl.ds(k * tm + t, 1), :], sem.at[s]).start(priority=k % 2)
            return carry

        lax.fori_loop(0, tm, start, 0, unroll=ROW_COPY_UNROLL)

    @pl.when(i == 0)
    def _():
        issue(pos_ref, 0)

    @pl.when(i + 1 < pl.num_programs(0))
    def _():
        issue(pos_next_ref, 1 - slot)

    base = alpha * x1_ref[...] + jnp.dot(h_ref[...], ws2_ref[...], preferred_element_type=F32)
    pltpu.make_async_copy(y_hbm.at[pl.ds(0, tm * TOP_K), :], ybuf.at[slot], sem.at[slot]).wait()

    gates = gate_ref[...]
    acc_lo = jnp.zeros((tm, D_MODEL // 2), F32)
    acc_hi = jnp.zeros((tm, D_MODEL // 2), F32)
    for k in range(TOP_K):
        lo, hi = _unpack_halves(ybuf[slot, k * tm:(k + 1) * tm, :])
        gk = gates[:, k:k + 1]
        acc_lo = acc_lo + gk * lo
        acc_hi = acc_hi + gk * hi
    f = jnp.concatenate([acc_lo, acc_hi], axis=1)
    o_ref[...] = _layer_norm(base + f, g_ref[...], b_ref[...])


def _combine(x1, hsh, ws2, pos, gates_t, ys, ln_g, ln_b, alpha):
    n, d = x1.shape
    tm = min(COMBINE_TOKENS, n)
    row = pl.BlockSpec((1, d), lambda i: (0, 0))
    last = n // tm - 1
    return pl.pallas_call(
        functools.partial(_combine_kernel, alpha=alpha), grid=(n // tm,),
        in_specs=[pl.BlockSpec((TOP_K, tm), lambda i: (0, i), memory_space=pltpu.SMEM),
                  pl.BlockSpec((TOP_K, tm), lambda i: (0, jnp.minimum(i + 1, last)), memory_space=pltpu.SMEM),
                  pl.BlockSpec((tm, d), lambda i: (i, 0)),
                  pl.BlockSpec((tm, SHARED_DIM), lambda i: (i, 0)),
                  pl.BlockSpec((SHARED_DIM, d), lambda i: (0, 0)),
                  pl.BlockSpec((tm, TOP_K), lambda i: (i, 0)),
                  row, row,
                  pl.BlockSpec(memory_space=pl.ANY)],
        out_specs=pl.BlockSpec((tm, d), lambda i: (i, 0)),
        out_shape=jax.ShapeDtypeStruct((n, d), F32),
        scratch_shapes=[pltpu.VMEM((2, TOP_K * tm, d // 2), U32), pltpu.SemaphoreType.DMA((2,))],
        compiler_params=_cparams(1), name="moe_combine",
    )(pos, pos, x1, hsh, ws2.astype(BF16), gates_t, ln_g.astype(F32).reshape(1, d),
      ln_b.astype(F32).reshape(1, d), ys)


def _moe(x1, x1p, w_router, router_bias, w1, w3, w2, ws1, ws3, ws2, ln_g, ln_b, alpha):
    n = x1.shape[0]
    bm = EXPERT_ROWS
    eidx, gates, rank, cnt = _router(x1, w_router, router_bias)
    counts = cnt[:, 0].astype(I32)
    padded = (counts + bm - 1) // bm * bm
    pend = jnp.cumsum(padded)
    pstart = pend - padded
    pos = _positions(pstart, eidx, rank)
    n_rows = -(-(n * TOP_K) // bm) * bm + N_EXPERTS * bm
    blk_start = jnp.arange(n_rows // bm, dtype=I32) * bm
    blk_on = (blk_start < pend[-1]).astype(I32)
    last = jnp.maximum(pend[-1] - bm, 0)
    blk_e = jnp.sum((pend[None, :] <= jnp.minimum(blk_start, last)[:, None]).astype(I32), axis=1)
    blk_e = jnp.minimum(blk_e, N_EXPERTS - 1)
    changed = jnp.concatenate([jnp.ones((1,), I32), (blk_e[1:] != blk_e[:-1]).astype(I32)])
    blk_new = blk_on * changed
    xs = _scatter_rows(x1p, pos, pend, padded, n_rows)
    ys = _experts(xs, blk_e, blk_on, blk_new, w1, w3, w2)
    hsh = _shared_up(x1, ws1, ws3)
    return _combine(x1, hsh, ws2, pos, gates.T, ys, ln_g, ln_b, alpha)


def kernel(x, positions, w_in, conv_w, A_log, dt_bias, gdn_norm_w, idx_ln_g, idx_ln_b, w_o_attn, w_o_gdn, w_out, ln1_g, ln1_b, w_router, router_bias, w1, w3, w2, ws1, ws3, ws2, ln2_g, ln2_b):
    batch, seq, d = x.shape
    depth = w_in.shape[0]
    alpha = (2 * depth) ** 0.25
    n = batch * seq
    xf = x.reshape(n, d)
    tabs = _trig_tables(positions)
    for l in range(depth):
        proj = _project(xf, _pack_w_in(w_in[l]))
        q_r, k_r, v_ext, iq_r, ik_lo, ik_hi, iw = _att_prep(proj, tabs, idx_ln_g[l], idx_ln_b[l])
        att = _dsa_attention(q_r, k_r, v_ext, iq_r, ik_lo, ik_hi, iw, batch, seq)
        gdn = _gated_deltanet(proj, conv_w[l], A_log[l], dt_bias[l], gdn_norm_w[l], batch, seq)
        merged = _merge(att, gdn, w_o_attn[l], w_o_gdn[l], proj)
        x1, x1p = _outproj_ln(merged, w_out[l], xf, ln1_g[l], ln1_b[l], alpha)
        xf = _moe(x1, x1p, w_router[l], router_bias[l], w1[l], w3[l], w2[l], ws1[l], ws3[l], ws2[l],
                  ln2_g[l], ln2_b[l], alpha)
    return xf.reshape(batch, seq, d)
```

```python
import functools
import math

import jax
import jax.numpy as jnp
import numpy as np
from jax import lax
from jax.experimental import pallas as pl
from jax.experimental.pallas import tpu as pltpu

F32 = jnp.float32
BF16 = jnp.bfloat16
I32 = jnp.int32
U32 = jnp.uint32
HI = lax.Precision.HIGHEST

D_MODEL = 2048
ATT_HEADS = 16
ATT_KV_HEADS = 2
HEAD_DIM = 128
IDX_HEADS = 8
IDX_DIM = 64
TOPK_MAX = 256
Q_BLOCK = 128
ROPE_THETA = 10000.0
GDN_HEADS = 16
GDN_DK = 128
GDN_DV = 128
CONV_WIDTH = 4
CHUNK = 64
N_EXPERTS = 64
TOP_K = 8
N_GROUPS = 8
TOPK_GROUPS = 4
EXPERT_DIM = 512
SHARED_DIM = 512
ROUTED_SCALE = 2.5
LN_EPS = 1e-5
RMS_EPS = 1e-6

ATT_Q = ATT_HEADS * HEAD_DIM
ATT_KV = ATT_KV_HEADS * HEAD_DIM
IDX_Q = IDX_HEADS * IDX_DIM
GDN_QK = GDN_HEADS * GDN_DK
GDN_V = GDN_HEADS * GDN_DV
SPLITS = (ATT_Q, ATT_KV, ATT_KV, IDX_Q, IDX_DIM, IDX_HEADS, GDN_QK, GDN_QK, GDN_V, GDN_V,
          GDN_HEADS, GDN_HEADS, D_MODEL, D_MODEL)
SPLIT_OFFSETS = tuple(int(o) for o in np.cumsum(SPLITS)[:-1])

LANES = 128
VMEM_LIMIT_BYTES = 56 * 1024 * 1024

OFF_AQ = 0
OFF_GQ = OFF_AQ + ATT_Q
OFF_GK = OFF_GQ + GDN_QK
OFF_GV = OFF_GK + GDN_QK
OFF_GZ = OFF_GV + GDN_V
OFF_GTA = OFF_GZ + GDN_V
OFF_GTB = OFF_GTA + D_MODEL
OFF_AK = OFF_GTB + D_MODEL
OFF_AV = OFF_AK + ATT_KV
OFF_IQ = OFF_AV + ATT_KV
OFF_SM = OFF_IQ + IDX_Q
PROJ_W = 16384
SM_IK = 0
SM_IW = SM_IK + IDX_DIM
SM_BETA = SM_IW + IDX_HEADS
SM_DT = SM_BETA + GDN_HEADS

KEY_CHUNK = 512
ATT_ROWS = 256
EXPERT_ROWS = 512
INT_MIN = -2147483648
MASK_BIAS = -1e30
M_INIT = -1e29


def _cparams(n_axes):
    return pltpu.CompilerParams(dimension_semantics=("arbitrary",) * n_axes,
                                vmem_limit_bytes=VMEM_LIMIT_BYTES)


def _nt(a, b, precision=None):
    return lax.dot_general(a, b, (((1,), (1,)), ((), ())), precision=precision,
                           preferred_element_type=F32)


def _tn(a, b, precision=None):
    return lax.dot_general(a, b, (((0,), (0,)), ((), ())), precision=precision,
                           preferred_element_type=F32)


def _sigmoid(x):
    return jax.nn.sigmoid(x)


def _layer_norm(x, g, b):
    mu = jnp.mean(x, axis=-1, keepdims=True)
    xc = x - mu
    var = jnp.mean(xc * xc, axis=-1, keepdims=True)
    return xc * lax.rsqrt(var + LN_EPS) * g + b


def _trig_kernel(pos_ref, inv_a_ref, inv_i_ref, sgn_a_ref, sgn_i_ref, cos_a, sin_a, cos_i, sin_i):
    p = pos_ref[...]
    ang_a = p * inv_a_ref[...]
    cos_a[...] = jnp.cos(ang_a)
    sin_a[...] = jnp.sin(ang_a) * sgn_a_ref[...]
    ang_i = p * inv_i_ref[...]
    cos_i[...] = jnp.cos(ang_i)
    sin_i[...] = jnp.sin(ang_i) * sgn_i_ref[...]


def _trig_tables(positions):
    n = positions.size
    tm = min(512, n)
    pos = positions.reshape(n, 1).astype(F32)
    inv_a = ROPE_THETA ** (-jnp.arange(0, HEAD_DIM, 2, dtype=F32) / HEAD_DIM)
    inv_i = ROPE_THETA ** (-jnp.arange(0, IDX_DIM, 2, dtype=F32) / IDX_DIM)
    half_a, half_i = HEAD_DIM // 2, IDX_DIM // 2
    inv_a_row = jnp.tile(inv_a, 2).reshape(1, LANES)
    inv_i_row = jnp.tile(inv_i, 4).reshape(1, LANES)
    lane = np.arange(LANES)
    sgn_a = jnp.asarray(np.where(lane < half_a, -1.0, 1.0), F32).reshape(1, LANES)
    sgn_i = jnp.asarray(np.where(lane % IDX_DIM < half_i, -1.0, 1.0), F32).reshape(1, LANES)
    row = pl.BlockSpec((1, LANES), lambda i: (0, 0))
    tab = pl.BlockSpec((tm, LANES), lambda i: (i, 0))
    return pl.pallas_call(
        _trig_kernel, grid=(n // tm,),
        in_specs=[pl.BlockSpec((tm, 1), lambda i: (i, 0)), row, row, row, row],
        out_specs=[tab, tab, tab, tab],
        out_shape=[jax.ShapeDtypeStruct((n, LANES), F32)] * 4,
        compiler_params=_cparams(1), name="trig_tables",
    )(pos, inv_a_row, inv_i_row, sgn_a, sgn_i)


def _pack_w_in(w_in):
    (aq, ak, av, iq, ik, iw, gq, gk, gv, gz, gb, ga, gta, gtb) = jnp.split(w_in.astype(BF16), SPLIT_OFFSETS,
                                                                             axis=-1)
    used = OFF_SM + IDX_DIM + IDX_HEADS + 2 * GDN_HEADS
    pad = jnp.zeros((w_in.shape[0], PROJ_W - used), BF16)
    return jnp.concatenate([aq, gq, gk, gv, gz, gta, gtb, ak, av, iq, ik, iw, gb, ga, pad], axis=-1)


def _proj_kernel(x_ref, w_ref, o_ref, xb_ref):
    @pl.when(pl.program_id(1) == 0)
    def _():
        xb_ref[...] = x_ref[...].astype(BF16)

    o_ref[...] = jnp.dot(xb_ref[...], w_ref[...], preferred_element_type=F32)


def _project(x2d, w_packed):
    n, d = x2d.shape
    tm, tn = min(1024, n), 1024
    return pl.pallas_call(
        _proj_kernel, grid=(n // tm, PROJ_W // tn),
        in_specs=[pl.BlockSpec((tm, d), lambda i, j: (i, 0)),
                  pl.BlockSpec((d, tn), lambda i, j: (0, j))],
        out_specs=pl.BlockSpec((tm, tn), lambda i, j: (i, j)),
        out_shape=jax.ShapeDtypeStruct((n, PROJ_W), F32),
        scratch_shapes=[pltpu.VMEM((tm, d), BF16)],
        compiler_params=_cparams(2), name="in_proj",
    )(x2d, w_packed)


def _rope_head(x, cos, sin_signed):
    return x * cos + pltpu.roll(x, HEAD_DIM // 2, 1) * sin_signed


def _rope_idx(x, cos, sin_signed, first_half):
    half = IDX_DIM // 2
    partner = jnp.where(first_half, pltpu.roll(x, LANES - half, 1), pltpu.roll(x, half, 1))
    return x * cos + partner * sin_signed


def _att_prep_kernel(aq_ref, ak_ref, av_ref, iq_ref, sm_ref, cos_a_ref, sin_a_ref, cos_i_ref,
                     sin_i_ref, lng_ref, lnb_ref,
                     q_out, k_out, v_out, iq_out, iklo_out, ikhi_out, iw_out):
    cos_a, sin_a = cos_a_ref[...], sin_a_ref[...]
    cos_i, sin_i = cos_i_ref[...], sin_i_ref[...]
    rows = cos_a.shape[0]
    lane = lax.broadcasted_iota(I32, (rows, LANES), 1)
    first_half = (lane % IDX_DIM) < (IDX_DIM // 2)

    for h in range(ATT_HEADS):
        q_h = _rope_head(aq_ref[:, h * HEAD_DIM:(h + 1) * HEAD_DIM], cos_a, sin_a) * (
            HEAD_DIM ** -0.5 * math.log2(math.e))
        q_out[0, h] = q_h.astype(BF16)
    ones = jnp.ones((rows, HEAD_DIM), BF16)
    for g in range(ATT_KV_HEADS):
        sl = slice(g * HEAD_DIM, (g + 1) * HEAD_DIM)
        k_out[:, sl] = _rope_head(ak_ref[:, sl], cos_a, sin_a).astype(BF16)
        v_out[:, 2 * g * HEAD_DIM:(2 * g + 1) * HEAD_DIM] = av_ref[:, sl].astype(BF16)
        v_out[:, (2 * g + 1) * HEAD_DIM:(2 * g + 2) * HEAD_DIM] = ones
    for j in range(IDX_HEADS // 2):
        iq_out[0, j] = _rope_idx(iq_ref[:, j * LANES:(j + 1) * LANES], cos_i, sin_i, first_half).astype(BF16)

    sm = sm_ref[...]
    in_ik = lane < IDX_DIM
    mu = jnp.sum(jnp.where(in_ik, sm, 0.0), axis=1, keepdims=True) / IDX_DIM
    xc = sm - mu
    var = jnp.sum(jnp.where(in_ik, xc * xc, 0.0), axis=1, keepdims=True) / IDX_DIM
    ik = xc * lax.rsqrt(var + LN_EPS) * lng_ref[...] + lnb_ref[...]
    ik = jnp.where(in_ik, _rope_idx(ik, cos_i, sin_i, first_half), 0.0)
    iklo_out[...] = ik.astype(BF16)
    ikhi_out[...] = pltpu.roll(ik, IDX_DIM, 1).astype(BF16)
    iw_out[...] = (sm * IDX_HEADS ** -0.5) * IDX_DIM ** -0.5


def _att_prep(proj, tabs, idx_ln_g, idx_ln_b):
    n = proj.shape[0]
    tm = min(ATT_ROWS, n)
    cos_a, sin_a, cos_i, sin_i = tabs
    pad = jnp.zeros((LANES - IDX_DIM,), F32)
    lng = jnp.concatenate([idx_ln_g.astype(F32), pad]).reshape(1, LANES)
    lnb = jnp.concatenate([idx_ln_b.astype(F32), pad]).reshape(1, LANES)
    tab = pl.BlockSpec((tm, LANES), lambda i: (i, 0))
    row = pl.BlockSpec((1, LANES), lambda i: (0, 0))

    def col(width, off):
        return pl.BlockSpec((tm, width), lambda i: (i, off // width))

    return pl.pallas_call(
        _att_prep_kernel, grid=(n // tm,),
        in_specs=[col(ATT_Q, OFF_AQ), col(ATT_KV, OFF_AK), col(ATT_KV, OFF_AV), col(IDX_Q, OFF_IQ),
                  col(LANES, OFF_SM), tab, tab, tab, tab, row, row],
        out_specs=[pl.BlockSpec((1, ATT_HEADS, tm, HEAD_DIM), lambda i: (i, 0, 0, 0)),
                   pl.BlockSpec((tm, ATT_KV), lambda i: (i, 0)),
                   pl.BlockSpec((tm, 2 * ATT_KV), lambda i: (i, 0)),
                   pl.BlockSpec((1, IDX_HEADS // 2, tm, LANES), lambda i: (i, 0, 0, 0)),
                   tab, tab, tab],
        out_shape=[jax.ShapeDtypeStruct((n // tm, ATT_HEADS, tm, HEAD_DIM), BF16),
                   jax.ShapeDtypeStruct((n, ATT_KV), BF16),
                   jax.ShapeDtypeStruct((n, 2 * ATT_KV), BF16),
                   jax.ShapeDtypeStruct((n // tm, IDX_HEADS // 2, tm, LANES), BF16),
                   jax.ShapeDtypeStruct((n, LANES), BF16),
                   jax.ShapeDtypeStruct((n, LANES), BF16),
                   jax.ShapeDtypeStruct((n, LANES), F32)],
        compiler_params=_cparams(1), name="att_prep",
    )(proj, proj, proj, proj, proj, cos_a, sin_a, cos_i, sin_i, lng, lnb)


def _dsa_kernel(q_ref, k_ref, v_ref, iq_ref, iklo_ref, ikhi_ref, iw_ref, o_ref,
                keys_sc, acc_sc, m_sc, wrep_sc, sel_sc, *, topk, seq):
    qb, kc = q_ref.shape[2], KEY_CHUNK
    tiles = kc // LANES
    rep = ATT_HEADS // ATT_KV_HEADS
    i = pl.program_id(1)
    nch = (i * qb + qb + kc - 1) // kc
    row = lax.broadcasted_iota(I32, (qb, LANES), 0)
    lane = lax.broadcasted_iota(I32, (qb, LANES), 1)
    t_abs = i * qb + row

    iw = iw_ref[...]
    for h in range(IDX_HEADS):
        wrep_sc[h] = jnp.broadcast_to(iw[:, SM_IW + h:SM_IW + h + 1], (qb, LANES))
    qi = iq_ref[0].reshape((IDX_HEADS // 2) * qb, LANES)

    def score_chunk(c, carry):
        off = pl.multiple_of(c * kc, kc)
        s_lo = _nt(qi, iklo_ref[pl.ds(off, kc), :])
        s_hi = _nt(qi, ikhi_ref[pl.ds(off, kc), :])
        for ct in range(tiles):
            cs = slice(ct * LANES, (ct + 1) * LANES)
            acc = jnp.zeros((qb, LANES), F32)
            for j in range(IDX_HEADS // 2):
                rs = slice(j * qb, (j + 1) * qb)
                acc = acc + wrep_sc[2 * j] * jnp.maximum(s_lo[rs, cs], 0.0)
                acc = acc + wrep_sc[2 * j + 1] * jnp.maximum(s_hi[rs, cs], 0.0)
            bits = lax.bitcast_convert_type(acc, I32)
            key = bits ^ ((bits >> 31) & jnp.int32(0x7FFFFFFF))
            key = jnp.where(bits == jnp.int32(INT_MIN), 0, key)
            causal = (off + ct * LANES + lane) <= t_abs
            keys_sc[c, :, cs] = jnp.where(causal, key, jnp.int32(INT_MIN))
        return carry

    lax.fori_loop(0, nch, score_chunk, 0)

    kth = jnp.float32(topk)

    def count(pred):
        lane_g = lax.broadcasted_iota(I32, (Q_BLOCK, LANES), 1)
        parts = []
        for r0 in range(0, qb, Q_BLOCK):
            rs = slice(r0, r0 + Q_BLOCK)

            def body(c, cnt, rs=rs):
                off = c * kc
                for ct in range(tiles):
                    kt = keys_sc[c, rs, ct * LANES:(ct + 1) * LANES]
                    cnt = cnt + jnp.where(pred(kt, off + ct * LANES + lane_g, rs), 1.0, 0.0)
                return cnt

            parts.append(lax.fori_loop(0, nch, body, jnp.zeros((Q_BLOCK, LANES), F32)))
        return jnp.concatenate([jnp.broadcast_to(jnp.sum(cnt, axis=1, keepdims=True), (Q_BLOCK, LANES))
                                for cnt in parts], axis=0)

    zero = jnp.zeros((qb, LANES), I32)
    n_pos = count(lambda k, s, rs: k >= zero[rs])
    base = jnp.where(n_pos >= kth, zero, jnp.int32(INT_MIN))
    n_base = jnp.where(n_pos >= kth, n_pos, -1.0)
    short = t_abs < topk

    def unresolved(n_base):
        return jnp.max(jnp.where((n_base == kth) | short, 0.0, 1.0)) > 0.0

    def bit_cond(st):
        it, _, n_base = st
        return (it < 31) & unresolved(n_base)

    def bit_step(st):
        it, base, n_base = st
        cand = base | (jnp.int32(1) << (30 - it))
        tot = count(lambda k, s, rs: k >= cand[rs])
        take = tot >= kth
        return it + 1, jnp.where(take, cand, base), jnp.where(take, tot, n_base)

    _, thr, n_ge = lax.while_loop(bit_cond, bit_step, (jnp.int32(0), base, n_base))
    sel_sc[0] = thr
    sel_sc[1] = jnp.full((qb, LANES), seq, I32)

    @pl.when(jnp.max(n_ge) > kth)
    def _():
        nbits = max(1, (seq - 1).bit_length())
        need = kth - count(lambda k, s, rs: k > thr[rs])

        def idx_step(it, posv):
            cand = posv + (jnp.int32(1) << (nbits - 1 - it))
            below = count(lambda k, s, rs: (k == thr[rs]) & (s < cand[rs]))
            return jnp.where(below < need, cand, posv)

        sel_sc[1] = lax.fori_loop(0, nbits, idx_step, zero)

    thr = sel_sc[0]
    last_tie = sel_sc[1]

    m_sc[...] = jnp.full(m_sc.shape, M_INIT, F32)
    acc_sc[...] = jnp.zeros(acc_sc.shape, F32)

    def attend_chunk(c, carry):
        off = pl.multiple_of(c * kc, kc)
        kk = keys_sc[c]
        bias = []
        for ct in range(tiles):
            cs = slice(ct * LANES, (ct + 1) * LANES)
            sidx = off + ct * LANES + lane
            keep = (kk[:, cs] > thr) | ((kk[:, cs] == thr) & (sidx <= last_tie))
            keep = keep & (sidx <= t_abs)
            bias.append(jnp.where(keep, 0.0, MASK_BIAS))
        for g in range(ATT_KV_HEADS):
            kg = k_ref[pl.ds(off, kc), g * HEAD_DIM:(g + 1) * HEAD_DIM]
            vg = v_ref[pl.ds(off, kc), 2 * g * HEAD_DIM:(2 * g + 2) * HEAD_DIM]
            qg = q_ref[0, g * rep:(g + 1) * rep].reshape(rep * qb, HEAD_DIM)
            s = _nt(qg, kg)
            s_t = [(s[:, ct * LANES:(ct + 1) * LANES].reshape(rep, qb, LANES) + bias[ct][None]
                    ).reshape(rep * qb, LANES) for ct in range(tiles)]
            mx = s_t[0]
            for ct in range(1, tiles):
                mx = jnp.maximum(mx, s_t[ct])
            m_old = m_sc[g]
            m_new = jnp.maximum(m_old, jnp.max(mx, axis=1, keepdims=True))
            p = jnp.concatenate([jnp.exp2(st - m_new) for st in s_t], axis=1).astype(BF16)
            alpha = jnp.exp2(m_old - m_new)
            pv = jnp.dot(p, vg, preferred_element_type=F32)
            acc_sc[g] = acc_sc[g] * jnp.concatenate([alpha, alpha], axis=1) + pv
            m_sc[g] = m_new
        return carry

    lax.fori_loop(0, nch, attend_chunk, 0)

    for g in range(ATT_KV_HEADS):
        for r in range(rep):
            a = acc_sc[g, r * qb:(r + 1) * qb, :]
            h = g * rep + r
            o_ref[:, h * HEAD_DIM:(h + 1) * HEAD_DIM] = (a[:, :HEAD_DIM] / a[:, HEAD_DIM:]).astype(BF16)


def _dsa_attention(q_r, k_r, v_ext, iq_r, ik_lo, ik_hi, iw, batch, seq):
    n = batch * seq
    qb = q_r.shape[2]
    nqb = seq // qb
    topk = min(TOPK_MAX, seq // 4)
    rep = ATT_HEADS // ATT_KV_HEADS
    kern = functools.partial(_dsa_kernel, topk=topk, seq=seq)

    def per_batch(width):
        return pl.BlockSpec((None, seq, width), lambda b, i: (b, 0, 0))

    return pl.pallas_call(
        kern, grid=(batch, nqb),
        in_specs=[pl.BlockSpec((1, ATT_HEADS, qb, HEAD_DIM), lambda b, i: (b * nqb + i, 0, 0, 0)),
                  per_batch(ATT_KV), per_batch(2 * ATT_KV),
                  pl.BlockSpec((1, IDX_HEADS // 2, qb, LANES), lambda b, i: (b * nqb + i, 0, 0, 0)),
                  per_batch(LANES), per_batch(LANES),
                  pl.BlockSpec((qb, LANES), lambda b, i: (b * nqb + i, 0))],
        out_specs=pl.BlockSpec((qb, ATT_Q), lambda b, i: (b * nqb + i, 0)),
        out_shape=jax.ShapeDtypeStruct((n, ATT_Q), BF16),
        scratch_shapes=[pltpu.VMEM((seq // KEY_CHUNK, qb, KEY_CHUNK), I32),
                        pltpu.VMEM((ATT_KV_HEADS, rep * qb, 2 * HEAD_DIM), F32),
                        pltpu.VMEM((ATT_KV_HEADS, rep * qb, LANES), F32),
                        pltpu.VMEM((IDX_HEADS, qb, LANES), F32),
                        pltpu.VMEM((2, qb, LANES), I32)],
        compiler_params=_cparams(2), name="dsa_attention",
    )(q_r, k_r.reshape(batch, seq, ATT_KV), v_ext.reshape(batch, seq, 2 * ATT_KV), iq_r,
      ik_lo.reshape(batch, seq, LANES), ik_hi.reshape(batch, seq, LANES), iw)


GDN_PREP_HEADS = 8
CONV_HALO = 8


def _split2(x):
    hi = x.astype(BF16)
    return hi, (x - hi.astype(F32)).astype(BF16)


def _split3(x):
    hi = x.astype(BF16)
    r = x - hi.astype(F32)
    mid = r.astype(BF16)
    return hi, mid, (r - mid.astype(F32)).astype(BF16)


def _lhs3(hi, lo):
    return jnp.concatenate([hi, lo, hi], axis=1)


def _rhs3(hi, lo):
    return jnp.concatenate([hi, hi, lo], axis=0)


def _gdn_prep_kernel(q_ref, k_ref, v_ref, qp_ref, kp_ref, vp_ref, sm_ref, gat_ref,
                     cwq_ref, cwk_ref, cwv_ref, alog_row_ref, dtb_row_ref, alog_col_ref, dtb_col_ref,
                     u_ref, w_ref, qd_ref, kd_ref, a_ref, e_ref, xx_sc, gct_sc, *, nc):
    hb = GDN_PREP_HEADS
    c = CHUNK
    hg = pl.program_id(1)
    chunk = pl.program_id(0) % nc
    heads = range(hb)

    def conv_silu(slot, cur_ref, prev_ref, cw_ref):
        prev = prev_ref[...]
        at_start = jnp.full(prev.shape, chunk, I32) == 0
        xx_sc[slot, 0:CONV_HALO, :] = jnp.where(at_start, 0.0, prev)
        xx_sc[slot, CONV_HALO:CONV_HALO + c, :] = cur_ref[...]
        cw = cw_ref[...]
        start = CONV_HALO - (CONV_WIDTH - 1)
        acc = xx_sc[slot, pl.ds(start, c), :] * cw[0:1, :]
        for j in range(1, CONV_WIDTH):
            acc = acc + xx_sc[slot, pl.ds(start + j, c), :] * cw[j:j + 1, :]
        return acc * _sigmoid(acc)

    qa = conv_silu(0, q_ref, qp_ref, cwq_ref)
    ka = conv_silu(1, k_ref, kp_ref, cwk_ref)
    va = conv_silu(2, v_ref, vp_ref, cwv_ref)

    rowi = lax.broadcasted_iota(I32, (c, c), 0)
    coli = lax.broadcasted_iota(I32, (c, c), 1)
    lower = rowi >= coli
    strict = rowi > coli
    eye = jnp.where(rowi == coli, 1.0, 0.0)

    sm = sm_ref[...]
    beta_slab = _sigmoid(sm)
    g_slab = -jnp.exp(alog_row_ref[...]) * jax.nn.softplus(sm + dtb_row_ref[...])
    ones_l = jnp.where(lower, 1.0, 0.0).astype(BF16)
    ones_u = jnp.where(rowi <= coli, 1.0, 0.0).astype(BF16)
    gc_slab = jnp.dot(jnp.concatenate([ones_l] * 3, axis=1), jnp.concatenate(_split3(g_slab), axis=0),
                      preferred_element_type=F32)
    g_t = -jnp.exp(alog_col_ref[:, :c]) * jax.nn.softplus(gat_ref[...] + dtb_col_ref[:, :c])
    gct_sc[...] = jnp.dot(jnp.concatenate(_split3(g_t), axis=1), jnp.concatenate([ones_u] * 3, axis=0),
                          preferred_element_type=F32)
    lane = lax.broadcasted_iota(I32, (c, LANES), 1)
    hsl = [slice(r * GDN_DK, (r + 1) * GDN_DK) for r in heads]

    gcc = [jnp.sum(jnp.where(lane == SM_DT + hg * hb + r, gc_slab, 0.0), axis=1, keepdims=True) for r in heads]
    bcol = [jnp.sum(jnp.where(lane == SM_BETA + hg * hb + r, beta_slab, 0.0), axis=1, keepdims=True)
            for r in heads]
    gcr = [gct_sc[pl.ds(hg * hb + r, 1), :] for r in heads]
    glast = [g[:, c - 1:c] for g in gcr]
    qn = [qa[:, s] * lax.rsqrt(jnp.sum(qa[:, s] * qa[:, s], axis=1, keepdims=True) + RMS_EPS) * GDN_DK ** -0.5
          for s in hsl]
    kn = [ka[:, s] * lax.rsqrt(jnp.sum(ka[:, s] * ka[:, s], axis=1, keepdims=True) + RMS_EPS) for s in hsl]
    eg = [jnp.exp(g) for g in gcc]
    decay = [jnp.where(lower, jnp.exp(gcc[r] - gcr[r]), 0.0) for r in heads]
    kb = [kn[r] * bcol[r] for r in heads]

    def nt3(a, b):
        (ah, al), (bh, bl) = _split2(a), _split2(b)
        return _nt(_lhs3(ah, al), jnp.concatenate([bh, bh, bl], axis=1))

    kq = [nt3(jnp.concatenate([kb[r], qn[r]], axis=0), kn[r]) for r in heads]
    pairs = range(hb // 2)
    r2 = lax.broadcasted_iota(I32, (2 * c, 2 * c), 0)
    c2 = lax.broadcasted_iota(I32, (2 * c, 2 * c), 1)
    same_head = (r2 < c) == (c2 < c)

    def diag2(m):
        return jnp.where(same_head, jnp.concatenate([m, m], axis=0), jnp.zeros((), m.dtype))

    def diag_rhs3(hi, lo):
        return _rhs3(diag2(hi), diag2(lo))

    y = [jnp.concatenate([jnp.where(strict, -(kq[2 * p + s][:c] * decay[2 * p + s]), 0.0) for s in range(2)],
                         axis=1) for p in pairs]
    eye2 = jnp.concatenate([eye, eye], axis=1)
    t_inv = [eye2 + y[p] for p in pairs]
    ys = [_split2(v) for v in y]
    y = [jnp.dot(_lhs3(*ys[p]), diag_rhs3(*ys[p]), preferred_element_type=F32) for p in pairs]
    n_fac = int(math.log2(c))
    for j in range(1, n_fac):
        ys = [_split2(v) for v in y]
        ts = [_split2(v) for v in t_inv]
        if j < n_fac - 1:
            prod = [jnp.dot(jnp.concatenate([_lhs3(*ys[p]), _lhs3(*ts[p])], axis=0), diag_rhs3(*ys[p]),
                            preferred_element_type=F32) for p in pairs]
            y = [q[:c] for q in prod]
            t_inv = [t_inv[p] + prod[p][c:] for p in pairs]
        else:
            t_inv = [t_inv[p] + jnp.dot(_lhs3(*ts[p]), diag_rhs3(*ys[p]), preferred_element_type=F32)
                     for p in pairs]
    ts = [_split2(v) for v in t_inv]
    rhs = [jnp.concatenate([jnp.concatenate([va[:, hsl[r]] * bcol[r], kb[r] * eg[r]], axis=1)
                            for r in (2 * p, 2 * p + 1)], axis=0) for p in pairs]
    uw2 = [jnp.dot(_lhs3(diag2(ts[p][0]), diag2(ts[p][1])), _rhs3(*_split2(rhs[p])),
                   preferred_element_type=F32) for p in pairs]
    uw = [uw2[r // 2][(r % 2) * c:(r % 2 + 1) * c] for r in heads]

    for r in heads:
        u_ref[:, hsl[r]] = uw[r][:, :GDN_DV]
        w_ref[:, hsl[r]] = uw[r][:, GDN_DV:].astype(BF16)
        qd_ref[:, hsl[r]] = (qn[r] * eg[r]).astype(BF16)
        kd_ref[:, hsl[r]] = (kn[r] * jnp.exp(glast[r] - gcc[r])).astype(BF16)
        a_ref[:, r * c:(r + 1) * c] = (kq[r][c:] * decay[r]).astype(BF16)
        e_ref[:, hsl[r]] = jnp.broadcast_to(jnp.exp(glast[r]), (CONV_HALO, GDN_DK))


def _gdn_scan_kernel(u_ref, w_ref, qd_ref, kd_ref, a_ref, e_ref, z_ref, nw_ref, o_ref, st_sc):
    c = CHUNK
    heads = range(GDN_HEADS)
    hsl = [slice(h * GDN_DK, (h + 1) * GDN_DK) for h in heads]

    @pl.when(pl.program_id(1) == 0)
    def _():
        st_sc[...] = jnp.zeros(st_sc.shape, F32)

    st = [st_sc[h] for h in heads]
    ws_qs = [jnp.dot(jnp.concatenate([w_ref[:, hsl[h]], qd_ref[:, hsl[h]]], axis=0), st[h].astype(BF16),
                     preferred_element_type=F32) for h in heads]
    v_new = [(u_ref[:, hsl[h]] - ws_qs[h][:c]).astype(BF16) for h in heads]
    o = [ws_qs[h][c:] + jnp.dot(a_ref[:, h * c:(h + 1) * c], v_new[h], preferred_element_type=F32)
         for h in heads]
    for h in heads:
        st_sc[h] = st[h] * e_ref[0:1, hsl[h]] + _tn(kd_ref[:, hsl[h]], v_new[h])
    for h in heads:
        z = z_ref[:, hsl[h]]
        on = o[h] * lax.rsqrt(jnp.mean(o[h] * o[h], axis=1, keepdims=True) + RMS_EPS)
        o_ref[:, hsl[h]] = (on * nw_ref[...] * (z * _sigmoid(z))).astype(BF16)


def _gated_deltanet(proj, conv_w, a_log, dt_bias, norm_w, batch, seq):
    n = batch * seq
    nc = seq // CHUNK
    hb = GDN_PREP_HEADS
    w = hb * GDN_DK
    ga_t = proj[:, OFF_SM + SM_DT:OFF_SM + SM_DT + GDN_HEADS].reshape(batch * nc, CHUNK, GDN_HEADS)
    ga_t = ga_t.transpose(0, 2, 1)
    lane_pad = lambda vec, off: jnp.zeros((1, LANES), F32).at[0, off:off + GDN_HEADS].set(vec.astype(F32))
    alog_row, dtb_row = lane_pad(a_log, SM_DT), lane_pad(dt_bias, SM_DT)
    alog_col = jnp.broadcast_to(a_log.astype(F32)[:, None], (GDN_HEADS, LANES))
    dtb_col = jnp.broadcast_to(dt_bias.astype(F32)[:, None], (GDN_HEADS, LANES))
    nw = norm_w.astype(F32).reshape(1, GDN_DV)
    cw = conv_w.astype(F32)

    def cur(off):
        return pl.BlockSpec((CHUNK, w), lambda i, h: (i, off // w + h))

    def prev(off):
        per = CHUNK // CONV_HALO
        return pl.BlockSpec((CONV_HALO, w), lambda i, h: (jnp.maximum(i * per - 1, 0), off // w + h))

    def cwspec(off):
        return pl.BlockSpec((CONV_WIDTH, w), lambda i, h: (0, off // w + h))

    row = pl.BlockSpec((1, LANES), lambda i, h: (0, 0))
    colv = pl.BlockSpec((GDN_HEADS, LANES), lambda i, h: (0, 0))
    head_tile = pl.BlockSpec((CHUNK, w), lambda i, h: (i, h))
    u, wm, qd, kd, a_in, egl = pl.pallas_call(
        functools.partial(_gdn_prep_kernel, nc=nc), grid=(batch * nc, GDN_HEADS // hb),
        in_specs=[cur(OFF_GQ), cur(OFF_GK), cur(OFF_GV), prev(OFF_GQ), prev(OFF_GK), prev(OFF_GV),
                  pl.BlockSpec((CHUNK, LANES), lambda i, h: (i, OFF_SM // LANES)),
                  pl.BlockSpec((None, GDN_HEADS, CHUNK), lambda i, h: (i, 0, 0)),
                  cwspec(0), cwspec(GDN_QK), cwspec(2 * GDN_QK), row, row, colv, colv],
        out_specs=[head_tile, head_tile, head_tile, head_tile,
                   pl.BlockSpec((CHUNK, hb * CHUNK), lambda i, h: (i, h)),
                   pl.BlockSpec((CONV_HALO, w), lambda i, h: (i, h))],
        out_shape=[jax.ShapeDtypeStruct((n, GDN_V), F32), jax.ShapeDtypeStruct((n, GDN_V), BF16),
                   jax.ShapeDtypeStruct((n, GDN_QK), BF16), jax.ShapeDtypeStruct((n, GDN_QK), BF16),
                   jax.ShapeDtypeStruct((n, GDN_HEADS * CHUNK), BF16),
                   jax.ShapeDtypeStruct((batch * nc * CONV_HALO, GDN_V), F32)],
        scratch_shapes=[pltpu.VMEM((3, CONV_HALO + CHUNK, w), F32),
                        pltpu.VMEM((GDN_HEADS, CHUNK), F32)],
        compiler_params=_cparams(2), name="gdn_prep",
    )(proj, proj, proj, proj, proj, proj, proj, ga_t, cw, cw, cw, alog_row, dtb_row, alog_col, dtb_col)

    def full(width):
        return pl.BlockSpec((CHUNK, width), lambda b, t: (b * nc + t, 0))

    return pl.pallas_call(
        _gdn_scan_kernel, grid=(batch, nc),
        in_specs=[full(GDN_V), full(GDN_V), full(GDN_QK), full(GDN_QK), full(GDN_HEADS * CHUNK),
                  pl.BlockSpec((CONV_HALO, GDN_V), lambda b, t: (b * nc + t, 0)),
                  pl.BlockSpec((CHUNK, GDN_V), lambda b, t: (b * nc + t, OFF_GZ // GDN_V)),
                  pl.BlockSpec((1, GDN_DV), lambda b, t: (0, 0))],
        out_specs=full(GDN_V),
        out_shape=jax.ShapeDtypeStruct((n, GDN_V), BF16),
        scratch_shapes=[pltpu.VMEM((GDN_HEADS, GDN_DK, GDN_DV), F32)],
        compiler_params=_cparams(2), name="gdn_scan",
    )(u, wm, qd, kd, a_in, egl, proj, nw)


def _merge_kernel(att_ref, gdn_ref, wa_ref, wg_ref, ga_ref, gb_ref, o_ref):
    a = jnp.dot(att_ref[...], wa_ref[...], preferred_element_type=F32)
    g = jnp.dot(gdn_ref[...], wg_ref[...], preferred_element_type=F32)
    o_ref[...] = (_sigmoid(ga_ref[...]) * a + _sigmoid(gb_ref[...]) * g).astype(BF16)


def _merge(att, gdn, w_o_attn, w_o_gdn, proj):
    n = att.shape[0]
    tm, tn = min(512, n), 1024
    return pl.pallas_call(
        _merge_kernel, grid=(n // tm, D_MODEL // tn),
        in_specs=[pl.BlockSpec((tm, ATT_Q), lambda i, j: (i, 0)),
                  pl.BlockSpec((tm, GDN_V), lambda i, j: (i, 0)),
                  pl.BlockSpec((ATT_Q, tn), lambda i, j: (0, j)),
                  pl.BlockSpec((GDN_V, tn), lambda i, j: (0, j)),
                  pl.BlockSpec((tm, tn), lambda i, j: (i, OFF_GTA // tn + j)),
                  pl.BlockSpec((tm, tn), lambda i, j: (i, OFF_GTB // tn + j))],
        out_specs=pl.BlockSpec((tm, tn), lambda i, j: (i, j)),
        out_shape=jax.ShapeDtypeStruct((n, D_MODEL), BF16),
        compiler_params=_cparams(2), name="mixer_merge",
    )(att, gdn, w_o_attn.astype(BF16), w_o_gdn.astype(BF16), proj, proj)


def _pack_halves(x):
    w = x.shape[1] // 2
    bits = lax.bitcast_convert_type(x.astype(BF16).astype(F32), U32)
    return (bits[:, :w] >> 16) | (bits[:, w:] & jnp.uint32(0xFFFF0000))


def _unpack_halves(u):
    lo = lax.bitcast_convert_type(u << 16, F32)
    hi = lax.bitcast_convert_type(u & jnp.uint32(0xFFFF0000), F32)
    return lo, hi


def _outproj_ln_kernel(m_ref, w_ref, x_ref, g_ref, b_ref, x1_ref, x1p_ref, *, alpha):
    y = jnp.dot(m_ref[...], w_ref[...], preferred_element_type=F32)
    x1 = _layer_norm(alpha * x_ref[...] + y, g_ref[...], b_ref[...])
    x1_ref[...] = x1
    x1p_ref[...] = _pack_halves(x1)


def _outproj_ln(merged, w_out, x2d, ln_g, ln_b, alpha):
    n, d = x2d.shape
    tm = min(256, n)
    row = pl.BlockSpec((1, d), lambda i: (0, 0))
    tile = pl.BlockSpec((tm, d), lambda i: (i, 0))
    return pl.pallas_call(
        functools.partial(_outproj_ln_kernel, alpha=alpha), grid=(n // tm,),
        in_specs=[tile, pl.BlockSpec((d, d), lambda i: (0, 0)), tile, row, row],
        out_specs=[tile, pl.BlockSpec((tm, d // 2), lambda i: (i, 0))],
        out_shape=[jax.ShapeDtypeStruct((n, d), F32), jax.ShapeDtypeStruct((n, d // 2), U32)],
        compiler_params=_cparams(1), name="outproj_ln",
    )(merged, w_out.astype(BF16), x2d, ln_g.astype(F32).reshape(1, d), ln_b.astype(F32).reshape(1, d))


def _first_max(x, iota, size):
    m = jnp.max(x, axis=0, keepdims=True)
    f = jnp.min(jnp.where(x == m, iota, float(size)), axis=0, keepdims=True)
    return m, f


def _router_kernel(x_ref, w_ref, b_ref, eidx_ref, gate_ref, rank_ref, cnt_ref, carry_sc):
    tm = x_ref.shape[0]
    per = N_EXPERTS // N_GROUPS
    neg = -jnp.inf

    @pl.when(pl.program_id(0) == 0)
    def _():
        carry_sc[...] = jnp.zeros(carry_sc.shape, F32)

    logits = _nt(w_ref[...], x_ref[...], precision=HI)
    s = _sigmoid(logits)
    choice = s + b_ref[:, 0:1]
    io_g = lax.broadcasted_iota(I32, (per, tm), 0).astype(F32)
    io_e = lax.broadcasted_iota(I32, (N_EXPERTS, tm), 0).astype(F32)

    gs = []
    for g in range(N_GROUPS):
        xg = choice[g * per:(g + 1) * per]
        m1, f1 = _first_max(xg, io_g, per)
        m2 = jnp.max(jnp.where(io_g == f1, neg, xg), axis=0, keepdims=True)
        gs.append(m1 + m2)
    gw = jnp.concatenate(gs, axis=0)
    gsel = jnp.zeros((N_GROUPS, tm), F32)
    for _ in range(TOPK_GROUPS):
        _, f = _first_max(gw, io_g, N_GROUPS)
        hit = io_g == f
        gsel = jnp.where(hit, 1.0, gsel)
        gw = jnp.where(hit, neg, gw)
    x = jnp.concatenate([jnp.where(gsel[g:g + 1] > 0.0, choice[g * per:(g + 1) * per], neg)
                         for g in range(N_GROUPS)], axis=0)

    mem = jnp.zeros((N_EXPERTS, tm), F32)
    picks, wsel = [], []
    for _ in range(TOP_K):
        _, f = _first_max(x, io_e, N_EXPERTS)
        hit = io_e == f
        picks.append(f)
        wsel.append(jnp.sum(jnp.where(hit, s, 0.0), axis=0, keepdims=True))
        mem = jnp.where(hit, 1.0, mem)
        x = jnp.where(hit, neg, x)
    wsel = jnp.concatenate(wsel, axis=0)
    gate_ref[...] = wsel / jnp.sum(wsel, axis=0, keepdims=True) * ROUTED_SCALE
    eidx_ref[...] = jnp.concatenate(picks, axis=0).astype(I32)

    tr = lax.broadcasted_iota(I32, (tm, tm), 0)
    tc = lax.broadcasted_iota(I32, (tm, tm), 1)
    before = jnp.where(tr < tc, 1.0, 0.0).astype(BF16)
    prefix = jnp.dot(mem.astype(BF16), before, preferred_element_type=F32) + carry_sc[:, 0:1]
    ranks = [jnp.sum(jnp.where(io_e == f, prefix, 0.0), axis=0, keepdims=True) for f in picks]
    rank_ref[...] = jnp.concatenate(ranks, axis=0).astype(I32)
    carry_sc[...] = carry_sc[...] + jnp.sum(mem, axis=1, keepdims=True)
    cnt_ref[...] = carry_sc[...]


def _router(x1, w_router, router_bias):
    n, d = x1.shape
    tm = min(512, n)
    out = pl.BlockSpec((TOP_K, tm), lambda i: (0, i))
    bias = jnp.broadcast_to(router_bias.astype(F32)[:, None], (N_EXPERTS, LANES))
    return pl.pallas_call(
        _router_kernel, grid=(n // tm,),
        in_specs=[pl.BlockSpec((tm, d), lambda i: (i, 0)),
                  pl.BlockSpec((N_EXPERTS, d), lambda i: (0, 0)),
                  pl.BlockSpec((N_EXPERTS, LANES), lambda i: (0, 0))],
        out_specs=[out, out, out, pl.BlockSpec((N_EXPERTS, LANES), lambda i: (0, 0))],
        out_shape=[jax.ShapeDtypeStruct((TOP_K, n), I32), jax.ShapeDtypeStruct((TOP_K, n), F32),
                   jax.ShapeDtypeStruct((TOP_K, n), I32), jax.ShapeDtypeStruct((N_EXPERTS, LANES), F32)],
        scratch_shapes=[pltpu.VMEM((N_EXPERTS, LANES), F32)],
        compiler_params=_cparams(1), name="router",
    )(x1, w_router.astype(F32).T, bias)


SCATTER_TOKENS = 512
COMBINE_TOKENS = 128
ROW_COPY_UNROLL = 8


def _pos_kernel(pstart_ref, eidx_ref, rank_ref, pos_ref):
    e = eidx_ref[...]
    acc = rank_ref[...]
    for x in range(N_EXPERTS):
        acc = acc + jnp.where(e == x, pstart_ref[x], 0)
    pos_ref[...] = acc


def _positions(pstart, eidx, rank):
    n = eidx.shape[1]
    tm = min(2048, n)
    tile = pl.BlockSpec((TOP_K, tm), lambda i, ps: (0, i))
    return pl.pallas_call(
        _pos_kernel,
        grid_spec=pltpu.PrefetchScalarGridSpec(num_scalar_prefetch=1, grid=(n // tm,),
                                               in_specs=[tile, tile], out_specs=tile),
        out_shape=jax.ShapeDtypeStruct((TOP_K, n), I32),
        compiler_params=_cparams(1), name="moe_positions",
    )(pstart, eidx, rank)


def _scatter_kernel(pend_ref, padded_ref, pos_ref, x_ref, xs_ref, zero_sc, sem, zsem):
    tm = x_ref.shape[0]
    bm = EXPERT_ROWS

    @pl.when(pl.program_id(0) == 0)
    def _():
        zero_sc[...] = jnp.zeros(zero_sc.shape, U32)

        def tail_copy(e):
            tail = pl.multiple_of(pend_ref[e] - bm, bm)
            return pltpu.make_async_copy(zero_sc, xs_ref.at[pl.ds(tail, bm), :], zsem)

        def zstart(e, carry):
            @pl.when(padded_ref[e] > 0)
            def _():
                tail_copy(e).start()
            return carry

        def zwait(e, carry):
            @pl.when(padded_ref[e] > 0)
            def _():
                tail_copy(e).wait()
            return carry

        lax.fori_loop(0, N_EXPERTS, zstart, 0)
        lax.fori_loop(0, N_EXPERTS, zwait, 0)

    def row_copy(t, k):
        return pltpu.make_async_copy(x_ref.at[pl.ds(t, 1), :], xs_ref.at[pl.ds(pos_ref[k, t], 1), :], sem)

    def start(t, carry):
        for k in range(TOP_K):
            row_copy(t, k).start(priority=k % 2)
        return carry

    lax.fori_loop(0, tm, start, 0, unroll=ROW_COPY_UNROLL)
    rows = xs_ref.at[pl.ds(0, tm * TOP_K), :]
    pltpu.make_async_copy(rows, rows, sem).wait()


def _scatter_rows(x1p, pos, pend, padded, n_rows):
    n, w = x1p.shape
    tm = min(SCATTER_TOKENS, n)
    grid_spec = pltpu.PrefetchScalarGridSpec(
        num_scalar_prefetch=2, grid=(n // tm,),
        in_specs=[pl.BlockSpec((TOP_K, tm), lambda i, pe, pd: (0, i), memory_space=pltpu.SMEM),
                  pl.BlockSpec((tm, w), lambda i, pe, pd: (i, 0))],
        out_specs=pl.BlockSpec(memory_space=pl.ANY),
        scratch_shapes=[pltpu.VMEM((EXPERT_ROWS, w), U32), pltpu.SemaphoreType.DMA, pltpu.SemaphoreType.DMA])
    return pl.pallas_call(
        _scatter_kernel, grid_spec=grid_spec,
        out_shape=jax.ShapeDtypeStruct((n_rows, w), U32),
        compiler_params=_cparams(1), name="moe_scatter",
    )(pend, padded, pos, x1p)


def _expert_kernel(blk_e_ref, blk_on_ref, blk_new_ref, x_ref, w1_ref, w3_ref, w2_ref, y_ref,
                   w1_sc, w3_sc, w2_sc):
    del blk_e_ref
    i = pl.program_id(0)

    @pl.when(blk_new_ref[i] > 0)
    def _():
        w1_sc[...] = w1_ref[...].astype(BF16)
        w3_sc[...] = w3_ref[...].astype(BF16)
        w2_sc[...] = w2_ref[...].astype(BF16)

    @pl.when(blk_on_ref[i] > 0)
    def _():
        half = D_MODEL // 2
        lo, hi = _unpack_halves(x_ref[...])
        xa, xb = lo.astype(BF16), hi.astype(BF16)

        def up(w_sc):
            return (jnp.dot(xa, w_sc[0:half, :], preferred_element_type=F32)
                    + jnp.dot(xb, w_sc[half:, :], preferred_element_type=F32))

        h1, h3 = up(w1_sc), up(w3_sc)
        act = (h1 * _sigmoid(h1) * h3).astype(BF16)
        y_ref[...] = _pack_halves(jnp.dot(act, w2_sc[...], preferred_element_type=F32))


def _experts(xs, blk_e, blk_on, blk_new, w1, w3, w2):
    n_rows, w = xs.shape
    bm = EXPERT_ROWS
    grid_spec = pltpu.PrefetchScalarGridSpec(
        num_scalar_prefetch=3, grid=(n_rows // bm,),
        in_specs=[pl.BlockSpec((bm, w), lambda i, e, on, nw: (i, 0)),
                  pl.BlockSpec((None, D_MODEL, EXPERT_DIM), lambda i, e, on, nw: (e[i], 0, 0)),
                  pl.BlockSpec((None, D_MODEL, EXPERT_DIM), lambda i, e, on, nw: (e[i], 0, 0)),
                  pl.BlockSpec((None, EXPERT_DIM, D_MODEL), lambda i, e, on, nw: (e[i], 0, 0))],
        out_specs=pl.BlockSpec((bm, w), lambda i, e, on, nw: (i, 0)),
        scratch_shapes=[pltpu.VMEM((D_MODEL, EXPERT_DIM), BF16), pltpu.VMEM((D_MODEL, EXPERT_DIM), BF16),
                        pltpu.VMEM((EXPERT_DIM, D_MODEL), BF16)])
    return pl.pallas_call(
        _expert_kernel, grid_spec=grid_spec,
        out_shape=jax.ShapeDtypeStruct((n_rows, w), U32),
        compiler_params=_cparams(1), name="moe_experts",
    )(blk_e, blk_on, blk_new, xs, w1, w3, w2)


def _shared_up_kernel(x_ref, w_ref, h_ref):
    h = jnp.dot(x_ref[...].astype(BF16), w_ref[...], preferred_element_type=F32)
    h1, h3 = h[:, :SHARED_DIM], h[:, SHARED_DIM:]
    h_ref[...] = (h1 * _sigmoid(h1) * h3).astype(BF16)


def _shared_up(x1, ws1, ws3):
    n, d = x1.shape
    tm = min(512, n)
    w13 = jnp.concatenate([ws1, ws3], axis=1).astype(BF16)
    return pl.pallas_call(
        _shared_up_kernel, grid=(n // tm,),
        in_specs=[pl.BlockSpec((tm, d), lambda i: (i, 0)),
                  pl.BlockSpec((d, 2 * SHARED_DIM), lambda i: (0, 0))],
        out_specs=pl.BlockSpec((tm, SHARED_DIM), lambda i: (i, 0)),
        out_shape=jax.ShapeDtypeStruct((n, SHARED_DIM), BF16),
        compiler_params=_cparams(1), name="shared_up",
    )(x1, w13)


def _combine_kernel(pos_ref, pos_next_ref, x1_ref, h_ref, ws2_ref, gate_ref, g_ref, b_ref, y_hbm, o_ref,
                    ybuf, sem, *, alpha):
    tm = x1_ref.shape[0]
    i = pl.program_id(0)
    slot = i % 2

    def issue(p_ref, s):
        def start(t, carry):
            for k in range(TOP_K):
                pltpu.make_async_copy(y_hbm.at[pl.ds(p_ref[k, t], 1), :],
                                      ybuf.at[s, pl.ds(k * tm + t, 1), :], sem.at[s]).start(priority=k % 2)
            return carry

        lax.fori_loop(0, tm, start, 0, unroll=ROW_COPY_UNROLL)

    @pl.when(i == 0)
    def _():
        issue(pos_ref, 0)

    @pl.when(i + 1 < pl.num_programs(0))
    def _():
        issue(pos_next_ref, 1 - slot)

    base = alpha * x1_ref[...] + jnp.dot(h_ref[...], ws2_ref[...], preferred_element_type=F32)
    pltpu.make_async_copy(y_hbm.at[pl.ds(0, tm * TOP_K), :], ybuf.at[slot], sem.at[slot]).wait()

    gates = gate_ref[...]
    acc_lo = jnp.zeros((tm, D_MODEL // 2), F32)
    acc_hi = jnp.zeros((tm, D_MODEL // 2), F32)
    for k in range(TOP_K):
        lo, hi = _unpack_halves(ybuf[slot, k * tm:(k + 1) * tm, :])
        gk = gates[:, k:k + 1]
        acc_lo = acc_lo + gk * lo
        acc_hi = acc_hi + gk * hi
    f = jnp.concatenate([acc_lo, acc_hi], axis=1)
    o_ref[...] = _layer_norm(base + f, g_ref[...], b_ref[...])


def _combine(x1, hsh, ws2, pos, gates_t, ys, ln_g, ln_b, alpha):
    n, d = x1.shape
    tm = min(COMBINE_TOKENS, n)
    row = pl.BlockSpec((1, d), lambda i: (0, 0))
    last = n // tm - 1
    return pl.pallas_call(
        functools.partial(_combine_kernel, alpha=alpha), grid=(n // tm,),
        in_specs=[pl.BlockSpec((TOP_K, tm), lambda i: (0, i), memory_space=pltpu.SMEM),
                  pl.BlockSpec((TOP_K, tm), lambda i: (0, jnp.minimum(i + 1, last)), memory_space=pltpu.SMEM),
                  pl.BlockSpec((tm, d), lambda i: (i, 0)),
                  pl.BlockSpec((tm, SHARED_DIM), lambda i: (i, 0)),
                  pl.BlockSpec((SHARED_DIM, d), lambda i: (0, 0)),
                  pl.BlockSpec((tm, TOP_K), lambda i: (i, 0)),
                  row, row,
                  pl.BlockSpec(memory_space=pl.ANY)],
        out_specs=pl.BlockSpec((tm, d), lambda i: (i, 0)),
        out_shape=jax.ShapeDtypeStruct((n, d), F32),
        scratch_shapes=[pltpu.VMEM((2, TOP_K * tm, d // 2), U32), pltpu.SemaphoreType.DMA((2,))],
        compiler_params=_cparams(1), name="moe_combine",
    )(pos, pos, x1, hsh, ws2.astype(BF16), gates_t, ln_g.astype(F32).reshape(1, d),
      ln_b.astype(F32).reshape(1, d), ys)


def _moe(x1, x1p, w_router, router_bias, w1, w3, w2, ws1, ws3, ws2, ln_g, ln_b, alpha):
    n = x1.shape[0]
    bm = EXPERT_ROWS
    eidx, gates, rank, cnt = _router(x1, w_router, router_bias)
    counts = cnt[:, 0].astype(I32)
    padded = (counts + bm - 1) // bm * bm
    pend = jnp.cumsum(padded)
    pstart = pend - padded
    pos = _positions(pstart, eidx, rank)
    n_rows = -(-(n * TOP_K) // bm) * bm + N_EXPERTS * bm
    blk_start = jnp.arange(n_rows // bm, dtype=I32) * bm
    blk_on = (blk_start < pend[-1]).astype(I32)
    last = jnp.maximum(pend[-1] - bm, 0)
    blk_e = jnp.sum((pend[None, :] <= jnp.minimum(blk_start, last)[:, None]).astype(I32), axis=1)
    blk_e = jnp.minimum(blk_e, N_EXPERTS - 1)
    changed = jnp.concatenate([jnp.ones((1,), I32), (blk_e[1:] != blk_e[:-1]).astype(I32)])
    blk_new = blk_on * changed
    xs = _scatter_rows(x1p, pos, pend, padded, n_rows)
    ys = _experts(xs, blk_e, blk_on, blk_new, w1, w3, w2)
    hsh = _shared_up(x1, ws1, ws3)
    return _combine(x1, hsh, ws2, pos, gates.T, ys, ln_g, ln_b, alpha)


def kernel(x, positions, w_in, conv_w, A_log, dt_bias, gdn_norm_w, idx_ln_g, idx_ln_b, w_o_attn, w_o_gdn, w_out, ln1_g, ln1_b, w_router, router_bias, w1, w3, w2, ws1, ws3, ws2, ln2_g, ln2_b):
    batch, seq, d = x.shape
    depth = w_in.shape[0]
    alpha = (2 * depth) ** 0.25
    n = batch * seq
    xf = x.reshape(n, d)
    tabs = _trig_tables(positions)
    for l in range(depth):
        proj = _project(xf, _pack_w_in(w_in[l]))
        q_r, k_r, v_ext, iq_r, ik_lo, ik_hi, iw = _att_prep(proj, tabs, idx_ln_g[l], idx_ln_b[l])
        att = _dsa_attention(q_r, k_r, v_ext, iq_r, ik_lo, ik_hi, iw, batch, seq)
        gdn = _gated_deltanet(proj, conv_w[l], A_log[l], dt_bias[l], gdn_norm_w[l], batch, seq)
        merged = _merge(att, gdn, w_o_attn[l], w_o_gdn[l], proj)
        x1, x1p = _outproj_ln(merged, w_out[l], xf, ln1_g[l], ln1_b[l], alpha)
        xf = _moe(x1, x1p, w_router[l], router_bias[l], w1[l], w3[l], w2[l], ws1[l], ws3[l], ws2[l],
                  ln2_g[l], ln2_b[l], alpha)
    return xf.reshape(batch, seq, d)
```

```python
import functools
import math

import jax
import jax.numpy as jnp
import numpy as np
from jax import lax
from jax.experimental import pallas as pl
from jax.experimental.pallas import tpu as pltpu

F32 = jnp.float32
BF16 = jnp.bfloat16
I32 = jnp.int32
U32 = jnp.uint32
HI = lax.Precision.HIGHEST

D_MODEL = 2048
ATT_HEADS = 16
ATT_KV_HEADS = 2
HEAD_DIM = 128
IDX_HEADS = 8
IDX_DIM = 64
TOPK_MAX = 256
Q_BLOCK = 128
ROPE_THETA = 10000.0
GDN_HEADS = 16
GDN_DK = 128
GDN_DV = 128
CONV_WIDTH = 4
CHUNK = 64
N_EXPERTS = 64
TOP_K = 8
N_GROUPS = 8
TOPK_GROUPS = 4
EXPERT_DIM = 512
SHARED_DIM = 512
ROUTED_SCALE = 2.5
LN_EPS = 1e-5
RMS_EPS = 1e-6

ATT_Q = ATT_HEADS * HEAD_DIM
ATT_KV = ATT_KV_HEADS * HEAD_DIM
IDX_Q = IDX_HEADS * IDX_DIM
GDN_QK = GDN_HEADS * GDN_DK
GDN_V = GDN_HEADS * GDN_DV
SPLITS = (ATT_Q, ATT_KV, ATT_KV, IDX_Q, IDX_DIM, IDX_HEADS, GDN_QK, GDN_QK, GDN_V, GDN_V,
          GDN_HEADS, GDN_HEADS, D_MODEL, D_MODEL)
SPLIT_OFFSETS = tuple(int(o) for o in np.cumsum(SPLITS)[:-1])

LANES = 128
VMEM_LIMIT_BYTES = 56 * 1024 * 1024

OFF_AQ = 0
OFF_GQ = OFF_AQ + ATT_Q
OFF_GK = OFF_GQ + GDN_QK
OFF_GV = OFF_GK + GDN_QK
OFF_GZ = OFF_GV + GDN_V
OFF_GTA = OFF_GZ + GDN_V
OFF_GTB = OFF_GTA + D_MODEL
OFF_AK = OFF_GTB + D_MODEL
OFF_AV = OFF_AK + ATT_KV
OFF_IQ = OFF_AV + ATT_KV
OFF_SM = OFF_IQ + IDX_Q
PROJ_W = 16384
SM_IK = 0
SM_IW = SM_IK + IDX_DIM
SM_BETA = SM_IW + IDX_HEADS
SM_DT = SM_BETA + GDN_HEADS

KEY_CHUNK = 512
ATT_ROWS = 256
EXPERT_ROWS = 512
INT_MIN = -2147483648
MASK_BIAS = -1e30
M_INIT = -1e29


def _cparams(n_axes):
    return pltpu.CompilerParams(dimension_semantics=("arbitrary",) * n_axes,
                                vmem_limit_bytes=VMEM_LIMIT_BYTES)


def _nt(a, b, precision=None):
    return lax.dot_general(a, b, (((1,), (1,)), ((), ())), precision=precision,
                           preferred_element_type=F32)


def _tn(a, b, precision=None):
    return lax.dot_general(a, b, (((0,), (0,)), ((), ())), precision=precision,
                           preferred_element_type=F32)


def _sigmoid(x):
    return jax.nn.sigmoid(x)


def _layer_norm(x, g, b):
    mu = jnp.mean(x, axis=-1, keepdims=True)
    xc = x - mu
    var = jnp.mean(xc * xc, axis=-1, keepdims=True)
    return xc * lax.rsqrt(var + LN_EPS) * g + b


def _trig_kernel(pos_ref, inv_a_ref, inv_i_ref, sgn_a_ref, sgn_i_ref, cos_a, sin_a, cos_i, sin_i):
    p = pos_ref[...]
    ang_a = p * inv_a_ref[...]
    cos_a[...] = jnp.cos(ang_a)
    sin_a[...] = jnp.sin(ang_a) * sgn_a_ref[...]
    ang_i = p * inv_i_ref[...]
    cos_i[...] = jnp.cos(ang_i)
    sin_i[...] = jnp.sin(ang_i) * sgn_i_ref[...]


def _trig_tables(positions):
    n = positions.size
    tm = min(512, n)
    pos = positions.reshape(n, 1).astype(F32)
    inv_a = ROPE_THETA ** (-jnp.arange(0, HEAD_DIM, 2, dtype=F32) / HEAD_DIM)
    inv_i = ROPE_THETA ** (-jnp.arange(0, IDX_DIM, 2, dtype=F32) / IDX_DIM)
    half_a, half_i = HEAD_DIM // 2, IDX_DIM // 2
    inv_a_row = jnp.tile(inv_a, 2).reshape(1, LANES)
    inv_i_row = jnp.tile(inv_i, 4).reshape(1, LANES)
    lane = np.arange(LANES)
    sgn_a = jnp.asarray(np.where(lane < half_a, -1.0, 1.0), F32).reshape(1, LANES)
    sgn_i = jnp.asarray(np.where(lane % IDX_DIM < half_i, -1.0, 1.0), F32).reshape(1, LANES)
    row = pl.BlockSpec((1, LANES), lambda i: (0, 0))
    tab = pl.BlockSpec((tm, LANES), lambda i: (i, 0))
    return pl.pallas_call(
        _trig_kernel, grid=(n // tm,),
        in_specs=[pl.BlockSpec((tm, 1), lambda i: (i, 0)), row, row, row, row],
        out_specs=[tab, tab, tab, tab],
        out_shape=[jax.ShapeDtypeStruct((n, LANES), F32)] * 4,
        compiler_params=_cparams(1), name="trig_tables",
    )(pos, inv_a_row, inv_i_row, sgn_a, sgn_i)


def _pack_w_in(w_in):
    (aq, ak, av, iq, ik, iw, gq, gk, gv, gz, gb, ga, gta, gtb) = jnp.split(w_in.astype(BF16), SPLIT_OFFSETS,
                                                                             axis=-1)
    used = OFF_SM + IDX_DIM + IDX_HEADS + 2 * GDN_HEADS
    pad = jnp.zeros((w_in.shape[0], PROJ_W - used), BF16)
    return jnp.concatenate([aq, gq, gk, gv, gz, gta, gtb, ak, av, iq, ik, iw, gb, ga, pad], axis=-1)


def _proj_kernel(x_ref, w_ref, o_ref, xb_ref):
    @pl.when(pl.program_id(1) == 0)
    def _():
        xb_ref[...] = x_ref[...].astype(BF16)

    o_ref[...] = jnp.dot(xb_ref[...], w_ref[...], preferred_element_type=F32)


def _project(x2d, w_packed):
    n, d = x2d.shape
    tm, tn = min(1024, n), 1024
    return pl.pallas_call(
        _proj_kernel, grid=(n // tm, PROJ_W // tn),
        in_specs=[pl.BlockSpec((tm, d), lambda i, j: (i, 0)),
                  pl.BlockSpec((d, tn), lambda i, j: (0, j))],
        out_specs=pl.BlockSpec((tm, tn), lambda i, j: (i, j)),
        out_shape=jax.ShapeDtypeStruct((n, PROJ_W), F32),
        scratch_shapes=[pltpu.VMEM((tm, d), BF16)],
        compiler_params=_cparams(2), name="in_proj",
    )(x2d, w_packed)


def _rope_head(x, cos, sin_signed):
    return x * cos + pltpu.roll(x, HEAD_DIM // 2, 1) * sin_signed


def _rope_idx(x, cos, sin_signed, first_half):
    half = IDX_DIM // 2
    partner = jnp.where(first_half, pltpu.roll(x, LANES - half, 1), pltpu.roll(x, half, 1))
    return x * cos + partner * sin_signed


def _att_prep_kernel(aq_ref, ak_ref, av_ref, iq_ref, sm_ref, cos_a_ref, sin_a_ref, cos_i_ref,
                     sin_i_ref, lng_ref, lnb_ref,
                     q_out, k_out, v_out, iq_out, iklo_out, ikhi_out, iw_out):
    cos_a, sin_a = cos_a_ref[...], sin_a_ref[...]
    cos_i, sin_i = cos_i_ref[...], sin_i_ref[...]
    rows = cos_a.shape[0]
    lane = lax.broadcasted_iota(I32, (rows, LANES), 1)
    first_half = (lane % IDX_DIM) < (IDX_DIM // 2)

    for h in range(ATT_HEADS):
        q_h = _rope_head(aq_ref[:, h * HEAD_DIM:(h + 1) * HEAD_DIM], cos_a, sin_a) * (
            HEAD_DIM ** -0.5 * math.log2(math.e))
        q_out[0, h] = q_h.astype(BF16)
    ones = jnp.ones((rows, HEAD_DIM), BF16)
    for g in range(ATT_KV_HEADS):
        sl = slice(g * HEAD_DIM, (g + 1) * HEAD_DIM)
        k_out[:, sl] = _rope_head(ak_ref[:, sl], cos_a, sin_a).astype(BF16)
        v_out[:, 2 * g * HEAD_DIM:(2 * g + 1) * HEAD_DIM] = av_ref[:, sl].astype(BF16)
        v_out[:, (2 * g + 1) * HEAD_DIM:(2 * g + 2) * HEAD_DIM] = ones
    for j in range(IDX_HEADS // 2):
        iq_out[0, j] = _rope_idx(iq_ref[:, j * LANES:(j + 1) * LANES], cos_i, sin_i, first_half).astype(BF16)

    sm = sm_ref[...]
    in_ik = lane < IDX_DIM
    mu = jnp.sum(jnp.where(in_ik, sm, 0.0), axis=1, keepdims=True) / IDX_DIM
    xc = sm - mu
    var = jnp.sum(jnp.where(in_ik, xc * xc, 0.0), axis=1, keepdims=True) / IDX_DIM
    ik = xc * lax.rsqrt(var + LN_EPS) * lng_ref[...] + lnb_ref[...]
    ik = jnp.where(in_ik, _rope_idx(ik, cos_i, sin_i, first_half), 0.0)
    iklo_out[...] = ik.astype(BF16)
    ikhi_out[...] = pltpu.roll(ik, IDX_DIM, 1).astype(BF16)
    iw_out[...] = (sm * IDX_HEADS ** -0.5) * IDX_DIM ** -0.5


def _att_prep(proj, tabs, idx_ln_g, idx_ln_b):
    n = proj.shape[0]
    tm = min(ATT_ROWS, n)
    cos_a, sin_a, cos_i, sin_i = tabs
    pad = jnp.zeros((LANES - IDX_DIM,), F32)
    lng = jnp.concatenate([idx_ln_g.astype(F32), pad]).reshape(1, LANES)
    lnb = jnp.concatenate([idx_ln_b.astype(F32), pad]).reshape(1, LANES)
    tab = pl.BlockSpec((tm, LANES), lambda i: (i, 0))
    row = pl.BlockSpec((1, LANES), lambda i: (0, 0))

    def col(width, off):
        return pl.BlockSpec((tm, width), lambda i: (i, off // width))

    return pl.pallas_call(
        _att_prep_kernel, grid=(n // tm,),
        in_specs=[col(ATT_Q, OFF_AQ), col(ATT_KV, OFF_AK), col(ATT_KV, OFF_AV), col(IDX_Q, OFF_IQ),
                  col(LANES, OFF_SM), tab, tab, tab, tab, row, row],
        out_specs=[pl.BlockSpec((1, ATT_HEADS, tm, HEAD_DIM), lambda i: (i, 0, 0, 0)),
                   pl.BlockSpec((tm, ATT_KV), lambda i: (i, 0)),
                   pl.BlockSpec((tm, 2 * ATT_KV), lambda i: (i, 0)),
                   pl.BlockSpec((1, IDX_HEADS // 2, tm, LANES), lambda i: (i, 0, 0, 0)),
                   tab, tab, tab],
        out_shape=[jax.ShapeDtypeStruct((n // tm, ATT_HEADS, tm, HEAD_DIM), BF16),
                   jax.ShapeDtypeStruct((n, ATT_KV), BF16),
                   jax.ShapeDtypeStruct((n, 2 * ATT_KV), BF16),
                   jax.ShapeDtypeStruct((n // tm, IDX_HEADS // 2, tm, LANES), BF16),
                   jax.ShapeDtypeStruct((n, LANES), BF16),
                   jax.ShapeDtypeStruct((n, LANES), BF16),
                   jax.ShapeDtypeStruct((n, LANES), F32)],
        compiler_params=_cparams(1), name="att_prep",
    )(proj, proj, proj, proj, proj, cos_a, sin_a, cos_i, sin_i, lng, lnb)


def _dsa_kernel(q_ref, k_ref, v_ref, iq_ref, iklo_ref, ikhi_ref, iw_ref, o_ref,
                keys_sc, acc_sc, m_sc, wrep_sc, sel_sc, *, topk, seq):
    qb, kc = q_ref.shape[2], KEY_CHUNK
    tiles = kc // LANES
    rep = ATT_HEADS // ATT_KV_HEADS
    i = pl.program_id(1)
    nch = (i * qb + qb + kc - 1) // kc
    row = lax.broadcasted_iota(I32, (qb, LANES), 0)
    lane = lax.broadcasted_iota(I32, (qb, LANES), 1)
    t_abs = i * qb + row

    iw = iw_ref[...]
    for h in range(IDX_HEADS):
        wrep_sc[h] = jnp.broadcast_to(iw[:, SM_IW + h:SM_IW + h + 1], (qb, LANES))
    qi = iq_ref[0].reshape((IDX_HEADS // 2) * qb, LANES)

    def score_chunk(c, carry):
        off = pl.multiple_of(c * kc, kc)
        s_lo = _nt(qi, iklo_ref[pl.ds(off, kc), :])
        s_hi = _nt(qi, ikhi_ref[pl.ds(off, kc), :])
        for ct in range(tiles):
            cs = slice(ct * LANES, (ct + 1) * LANES)
            acc = jnp.zeros((qb, LANES), F32)
            for j in range(IDX_HEADS // 2):
                rs = slice(j * qb, (j + 1) * qb)
                acc = acc + wrep_sc[2 * j] * jnp.maximum(s_lo[rs, cs], 0.0)
                acc = acc + wrep_sc[2 * j + 1] * jnp.maximum(s_hi[rs, cs], 0.0)
            bits = lax.bitcast_convert_type(acc, I32)
            key = bits ^ ((bits >> 31) & jnp.int32(0x7FFFFFFF))
            key = jnp.where(bits == jnp.int32(INT_MIN), 0, key)
            causal = (off + ct * LANES + lane) <= t_abs
            keys_sc[c, :, cs] = jnp.where(causal, key, jnp.int32(INT_MIN))
        return carry

    lax.fori_loop(0, nch, score_chunk, 0)

    kth = jnp.float32(topk)

    def count(pred):
        lane_g = lax.broadcasted_iota(I32, (Q_BLOCK, LANES), 1)
        parts = []
        for r0 in range(0, qb, Q_BLOCK):
            rs = slice(r0, r0 + Q_BLOCK)

            def body(c, cnt, rs=rs):
                off = c * kc
                for ct in range(tiles):
                    kt = keys_sc[c, rs, ct * LANES:(ct + 1) * LANES]
                    cnt = cnt + jnp.where(pred(kt, off + ct * LANES + lane_g, rs), 1.0, 0.0)
                return cnt

            parts.append(lax.fori_loop(0, nch, body, jnp.zeros((Q_BLOCK, LANES), F32)))
        return jnp.concatenate([jnp.broadcast_to(jnp.sum(cnt, axis=1, keepdims=True), (Q_BLOCK, LANES))
                                for cnt in parts], axis=0)

    zero = jnp.zeros((qb, LANES), I32)
    n_pos = count(lambda k, s, rs: k >= zero[rs])
    base = jnp.where(n_pos >= kth, zero, jnp.int32(INT_MIN))
    n_base = jnp.where(n_pos >= kth, n_pos, -1.0)
    short = t_abs < topk

    def unresolved(n_base):
        return jnp.max(jnp.where((n_base == kth) | short, 0.0, 1.0)) > 0.0

    def bit_cond(st):
        it, _, n_base = st
        return (it < 31) & unresolved(n_base)

    def bit_step(st):
        it, base, n_base = st
        cand = base | (jnp.int32(1) << (30 - it))
        tot = count(lambda k, s, rs: k >= cand[rs])
        take = tot >= kth
        return it + 1, jnp.where(take, cand, base), jnp.where(take, tot, n_base)

    _, thr, n_ge = lax.while_loop(bit_cond, bit_step, (jnp.int32(0), base, n_base))
    sel_sc[0] = thr
    sel_sc[1] = jnp.full((qb, LANES), seq, I32)

    @pl.when(jnp.max(n_ge) > kth)
    def _():
        nbits = max(1, (seq - 1).bit_length())
        need = kth - count(lambda k, s, rs: k > thr[rs])

        def idx_step(it, posv):
            cand = posv + (jnp.int32(1) << (nbits - 1 - it))
            below = count(lambda k, s, rs: (k == thr[rs]) & (s < cand[rs]))
            return jnp.where(below < need, cand, posv)

        sel_sc[1] = lax.fori_loop(0, nbits, idx_step, zero)

    thr = sel_sc[0]
    last_tie = sel_sc[1]

    m_sc[...] = jnp.full(m_sc.shape, M_INIT, F32)
    acc_sc[...] = jnp.zeros(acc_sc.shape, F32)

    def attend_chunk(c, carry):
        off = pl.multiple_of(c * kc, kc)
        kk = keys_sc[c]
        bias = []
        for ct in range(tiles):
            cs = slice(ct * LANES, (ct + 1) * LANES)
            sidx = off + ct * LANES + lane
            keep = (kk[:, cs] > thr) | ((kk[:, cs] == thr) & (sidx <= last_tie))
            keep = keep & (sidx <= t_abs)
            bias.append(jnp.where(keep, 0.0, MASK_BIAS))
        for g in range(ATT_KV_HEADS):
            kg = k_ref[pl.ds(off, kc), g * HEAD_DIM:(g + 1) * HEAD_DIM]
            vg = v_ref[pl.ds(off, kc), 2 * g * HEAD_DIM:(2 * g + 2) * HEAD_DIM]
            qg = q_ref[0, g * rep:(g + 1) * rep].reshape(rep * qb, HEAD_DIM)
            s = _nt(qg, kg)
            s_t = [(s[:, ct * LANES:(ct + 1) * LANES].reshape(rep, qb, LANES) + bias[ct][None]
                    ).reshape(rep * qb, LANES) for ct in range(tiles)]
            mx = s_t[0]
            for ct in range(1, tiles):
                mx = jnp.maximum(mx, s_t[ct])
            m_old = m_sc[g]
            m_new = jnp.maximum(m_old, jnp.max(mx, axis=1, keepdims=True))
            p = jnp.concatenate([jnp.exp2(st - m_new) for st in s_t], axis=1).astype(BF16)
            alpha = jnp.exp2(m_old - m_new)
            pv = jnp.dot(p, vg, preferred_element_type=F32)
            acc_sc[g] = acc_sc[g] * jnp.concatenate([alpha, alpha], axis=1) + pv
            m_sc[g] = m_new
        return carry

    lax.fori_loop(0, nch, attend_chunk, 0)

    for g in range(ATT_KV_HEADS):
        for r in range(rep):
            a = acc_sc[g, r * qb:(r + 1) * qb, :]
            h = g * rep + r
            o_ref[:, h * HEAD_DIM:(h + 1) * HEAD_DIM] = (a[:, :HEAD_DIM] / a[:, HEAD_DIM:]).astype(BF16)


def _dsa_attention(q_r, k_r, v_ext, iq_r, ik_lo, ik_hi, iw, batch, seq):
    n = batch * seq
    qb = q_r.shape[2]
    nqb = seq // qb
    topk = min(TOPK_MAX, seq // 4)
    rep = ATT_HEADS // ATT_KV_HEADS
    kern = functools.partial(_dsa_kernel, topk=topk, seq=seq)

    def per_batch(width):
        return pl.BlockSpec((None, seq, width), lambda b, i: (b, 0, 0))

    return pl.pallas_call(
        kern, grid=(batch, nqb),
        in_specs=[pl.BlockSpec((1, ATT_HEADS, qb, HEAD_DIM), lambda b, i: (b * nqb + i, 0, 0, 0)),
                  per_batch(ATT_KV), per_batch(2 * ATT_KV),
                  pl.BlockSpec((1, IDX_HEADS // 2, qb, LANES), lambda b, i: (b * nqb + i, 0, 0, 0)),
                  per_batch(LANES), per_batch(LANES),
                  pl.BlockSpec((qb, LANES), lambda b, i: (b * nqb + i, 0))],
        out_specs=pl.BlockSpec((qb, ATT_Q), lambda b, i: (b * nqb + i, 0)),
        out_shape=jax.ShapeDtypeStruct((n, ATT_Q), BF16),
        scratch_shapes=[pltpu.VMEM((seq // KEY_CHUNK, qb, KEY_CHUNK), I32),
                        pltpu.VMEM((ATT_KV_HEADS, rep * qb, 2 * HEAD_DIM), F32),
                        pltpu.VMEM((ATT_KV_HEADS, rep * qb, LANES), F32),
                        pltpu.VMEM((IDX_HEADS, qb, LANES), F32),
                        pltpu.VMEM((2, qb, LANES), I32)],
        compiler_params=_cparams(2), name="dsa_attention",
    )(q_r, k_r.reshape(batch, seq, ATT_KV), v_ext.reshape(batch, seq, 2 * ATT_KV), iq_r,
      ik_lo.reshape(batch, seq, LANES), ik_hi.reshape(batch, seq, LANES), iw)


GDN_PREP_HEADS = 8
CONV_HALO = 8


def _split2(x):
    hi = x.astype(BF16)
    return hi, (x - hi.astype(F32)).astype(BF16)


def _split3(x):
    hi = x.astype(BF16)
    r = x - hi.astype(F32)
    mid = r.astype(BF16)
    return hi, mid, (r - mid.astype(F32)).astype(BF16)


def _lhs3(hi, lo):
    return jnp.concatenate([hi, lo, hi], axis=1)


def _rhs3(hi, lo):
    return jnp.concatenate([hi, hi, lo], axis=0)


def _gdn_prep_kernel(q_ref, k_ref, v_ref, qp_ref, kp_ref, vp_ref, sm_ref, gat_ref,
                     cwq_ref, cwk_ref, cwv_ref, alog_row_ref, dtb_row_ref, alog_col_ref, dtb_col_ref,
                     u_ref, w_ref, qd_ref, kd_ref, a_ref, e_ref, xx_sc, gct_sc, *, nc):
    hb = GDN_PREP_HEADS
    c = CHUNK
    hg = pl.program_id(1)
    chunk = pl.program_id(0) % nc
    heads = range(hb)

    def conv_silu(slot, cur_ref, prev_ref, cw_ref):
        prev = prev_ref[...]
        at_start = jnp.full(prev.shape, chunk, I32) == 0
        xx_sc[slot, 0:CONV_HALO, :] = jnp.where(at_start, 0.0, prev)
        xx_sc[slot, CONV_HALO:CONV_HALO + c, :] = cur_ref[...]
        cw = cw_ref[...]
        start = CONV_HALO - (CONV_WIDTH - 1)
        acc = xx_sc[slot, pl.ds(start, c), :] * cw[0:1, :]
        for j in range(1, CONV_WIDTH):
            acc = acc + xx_sc[slot, pl.ds(start + j, c), :] * cw[j:j + 1, :]
        return acc * _sigmoid(acc)

    qa = conv_silu(0, q_ref, qp_ref, cwq_ref)
    ka = conv_silu(1, k_ref, kp_ref, cwk_ref)
    va = conv_silu(2, v_ref, vp_ref, cwv_ref)

    rowi = lax.broadcasted_iota(I32, (c, c), 0)
    coli = lax.broadcasted_iota(I32, (c, c), 1)
    lower = rowi >= coli
    strict = rowi > coli
    eye = jnp.where(rowi == coli, 1.0, 0.0)

    sm = sm_ref[...]
    beta_slab = _sigmoid(sm)
    g_slab = -jnp.exp(alog_row_ref[...]) * jax.nn.softplus(sm + dtb_row_ref[...])
    ones_l = jnp.where(lower, 1.0, 0.0).astype(BF16)
    ones_u = jnp.where(rowi <= coli, 1.0, 0.0).astype(BF16)
    gc_slab = jnp.dot(jnp.concatenate([ones_l] * 3, axis=1), jnp.concatenate(_split3(g_slab), axis=0),
                      preferred_element_type=F32)
    g_t = -jnp.exp(alog_col_ref[:, :c]) * jax.nn.softplus(gat_ref[...] + dtb_col_ref[:, :c])
    gct_sc[...] = jnp.dot(jnp.concatenate(_split3(g_t), axis=1), jnp.concatenate([ones_u] * 3, axis=0),
                          preferred_element_type=F32)
    lane = lax.broadcasted_iota(I32, (c, LANES), 1)
    hsl = [slice(r * GDN_DK, (r + 1) * GDN_DK) for r in heads]

    gcc = [jnp.sum(jnp.where(lane == SM_DT + hg * hb + r, gc_slab, 0.0), axis=1, keepdims=True) for r in heads]
    bcol = [jnp.sum(jnp.where(lane == SM_BETA + hg * hb + r, beta_slab, 0.0), axis=1, keepdims=True)
            for r in heads]
    gcr = [gct_sc[pl.ds(hg * hb + r, 1), :] for r in heads]
    glast = [g[:, c - 1:c] for g in gcr]
    qn = [qa[:, s] * lax.rsqrt(jnp.sum(qa[:, s] * qa[:, s], axis=1, keepdims=True) + RMS_EPS) * GDN_DK ** -0.5
          for s in hsl]
    kn = [ka[:, s] * lax.rsqrt(jnp.sum(ka[:, s] * ka[:, s], axis=1, keepdims=True) + RMS_EPS) for s in hsl]
    eg = [jnp.exp(g) for g in gcc]
    decay = [jnp.where(lower, jnp.exp(gcc[r] - gcr[r]), 0.0) for r in heads]
    kb = [kn[r] * bcol[r] for r in heads]

    def nt3(a, b):
        (ah, al), (bh, bl) = _split2(a), _split2(b)
        return _nt(_lhs3(ah, al), jnp.concatenate([bh, bh, bl], axis=1))

    kq = [nt3(jnp.concatenate([kb[r], qn[r]], axis=0), kn[r]) for r in heads]
    pairs = range(hb // 2)
    r2 = lax.broadcasted_iota(I32, (2 * c, 2 * c), 0)
    c2 = lax.broadcasted_iota(I32, (2 * c, 2 * c), 1)
    same_head = (r2 < c) == (c2 < c)

    def diag2(m):
        return jnp.where(same_head, jnp.concatenate([m, m], axis=0), jnp.zeros((), m.dtype))

    def diag_rhs3(hi, lo):
        return _rhs3(diag2(hi), diag2(lo))

    y = [jnp.concatenate([jnp.where(strict, -(kq[2 * p + s][:c] * decay[2 * p + s]), 0.0) for s in range(2)],
                         axis=1) for p in pairs]
    eye2 = jnp.concatenate([eye, eye], axis=1)
    t_inv = [eye2 + y[p] for p in pairs]
    ys = [_split2(v) for v in y]
    y = [jnp.dot(_lhs3(*ys[p]), diag_rhs3(*ys[p]), preferred_element_type=F32) for p in pairs]
    n_fac = int(math.log2(c))
    for j in range(1, n_fac):
        ys = [_split2(v) for v in y]
        ts = [_split2(v) for v in t_inv]
        if j < n_fac - 1:
            prod = [jnp.dot(jnp.concatenate([_lhs3(*ys[p]), _lhs3(*ts[p])], axis=0), diag_rhs3(*ys[p]),
                            preferred_element_type=F32) for p in pairs]
            y = [q[:c] for q in prod]
            t_inv = [t_inv[p] + prod[p][c:] for p in pairs]
        else:
            t_inv = [t_inv[p] + jnp.dot(_lhs3(*ts[p]), diag_rhs3(*ys[p]), preferred_element_type=F32)
                     for p in pairs]
    ts = [_split2(v) for v in t_inv]
    rhs = [jnp.concatenate([jnp.concatenate([va[:, hsl[r]] * bcol[r], kb[r] * eg[r]], axis=1)
                            for r in (2 * p, 2 * p + 1)], axis=0) for p in pairs]
    uw2 = [jnp.dot(_lhs3(diag2(ts[p][0]), diag2(ts[p][1])), _rhs3(*_split2(rhs[p])),
                   preferred_element_type=F32) for p in pairs]
    uw = [uw2[r // 2][(r % 2) * c:(r % 2 + 1) * c] for r in heads]

    for r in heads:
        u_ref[:, hsl[r]] = uw[r][:, :GDN_DV]
        w_ref[:, hsl[r]] = uw[r][:, GDN_DV:].astype(BF16)
        qd_ref[:, hsl[r]] = (qn[r] * eg[r]).astype(BF16)
        kd_ref[:, hsl[r]] = (kn[r] * jnp.exp(glast[r] - gcc[r])).astype(BF16)
        a_ref[:, r * c:(r + 1) * c] = (kq[r][c:] * decay[r]).astype(BF16)
        e_ref[:, hsl[r]] = jnp.broadcast_to(jnp.exp(glast[r]), (CONV_HALO, GDN_DK))


def _gdn_scan_kernel(u_ref, w_ref, qd_ref, kd_ref, a_ref, e_ref, z_ref, nw_ref, o_ref, st_sc):
    c = CHUNK
    heads = range(GDN_HEADS)
    hsl = [slice(h * GDN_DK, (h + 1) * GDN_DK) for h in heads]

    @pl.when(pl.program_id(1) == 0)
    def _():
        st_sc[...] = jnp.zeros(st_sc.shape, F32)

    st = [st_sc[h] for h in heads]
    ws_qs = [jnp.dot(jnp.concatenate([w_ref[:, hsl[h]], qd_ref[:, hsl[h]]], axis=0), st[h].astype(BF16),
                     preferred_element_type=F32) for h in heads]
    v_new = [(u_ref[:, hsl[h]] - ws_qs[h][:c]).astype(BF16) for h in heads]
    o = [ws_qs[h][c:] + jnp.dot(a_ref[:, h * c:(h + 1) * c], v_new[h], preferred_element_type=F32)
         for h in heads]
    for h in heads:
        st_sc[h] = st[h] * e_ref[0:1, hsl[h]] + _tn(kd_ref[:, hsl[h]], v_new[h])
    for h in heads:
        z = z_ref[:, hsl[h]]
        on = o[h] * lax.rsqrt(jnp.mean(o[h] * o[h], axis=1, keepdims=True) + RMS_EPS)
        o_ref[:, hsl[h]] = (on * nw_ref[...] * (z * _sigmoid(z))).astype(BF16)


def _gated_deltanet(proj, conv_w, a_log, dt_bias, norm_w, batch, seq):
    n = batch * seq
    nc = seq // CHUNK
    hb = GDN_PREP_HEADS
    w = hb * GDN_DK
    ga_t = proj[:, OFF_SM + SM_DT:OFF_SM + SM_DT + GDN_HEADS].reshape(batch * nc, CHUNK, GDN_HEADS)
    ga_t = ga_t.transpose(0, 2, 1)
    lane_pad = lambda vec, off: jnp.zeros((1, LANES), F32).at[0, off:off + GDN_HEADS].set(vec.astype(F32))
    alog_row, dtb_row = lane_pad(a_log, SM_DT), lane_pad(dt_bias, SM_DT)
    alog_col = jnp.broadcast_to(a_log.astype(F32)[:, None], (GDN_HEADS, LANES))
    dtb_col = jnp.broadcast_to(dt_bias.astype(F32)[:, None], (GDN_HEADS, LANES))
    nw = norm_w.astype(F32).reshape(1, GDN_DV)
    cw = conv_w.astype(F32)

    def cur(off):
        return pl.BlockSpec((CHUNK, w), lambda i, h: (i, off // w + h))

    def prev(off):
        per = CHUNK // CONV_HALO
        return pl.BlockSpec((CONV_HALO, w), lambda i, h: (jnp.maximum(i * per - 1, 0), off // w + h))

    def cwspec(off):
        return pl.BlockSpec((CONV_WIDTH, w), lambda i, h: (0, off // w + h))

    row = pl.BlockSpec((1, LANES), lambda i, h: (0, 0))
    colv = pl.BlockSpec((GDN_HEADS, LANES), lambda i, h: (0, 0))
    head_tile = pl.BlockSpec((CHUNK, w), lambda i, h: (i, h))
    u, wm, qd, kd, a_in, egl = pl.pallas_call(
        functools.partial(_gdn_prep_kernel, nc=nc), grid=(batch * nc, GDN_HEADS // hb),
        in_specs=[cur(OFF_GQ), cur(OFF_GK), cur(OFF_GV), prev(OFF_GQ), prev(OFF_GK), prev(OFF_GV),
                  pl.BlockSpec((CHUNK, LANES), lambda i, h: (i, OFF_SM // LANES)),
                  pl.BlockSpec((None, GDN_HEADS, CHUNK), lambda i, h: (i, 0, 0)),
                  cwspec(0), cwspec(GDN_QK), cwspec(2 * GDN_QK), row, row, colv, colv],
        out_specs=[head_tile, head_tile, head_tile, head_tile,
                   pl.BlockSpec((CHUNK, hb * CHUNK), lambda i, h: (i, h)),
                   pl.BlockSpec((CONV_HALO, w), lambda i, h: (i, h))],
        out_shape=[jax.ShapeDtypeStruct((n, GDN_V), F32), jax.ShapeDtypeStruct((n, GDN_V), BF16),
                   jax.ShapeDtypeStruct((n, GDN_QK), BF16), jax.ShapeDtypeStruct((n, GDN_QK), BF16),
                   jax.ShapeDtypeStruct((n, GDN_HEADS * CHUNK), BF16),
                   jax.ShapeDtypeStruct((batch * nc * CONV_HALO, GDN_V), F32)],
        scratch_shapes=[pltpu.VMEM((3, CONV_HALO + CHUNK, w), F32),
                        pltpu.VMEM((GDN_HEADS, CHUNK), F32)],
        compiler_params=_cparams(2), name="gdn_prep",
    )(proj, proj, proj, proj, proj, proj, proj, ga_t, cw, cw, cw, alog_row, dtb_row, alog_col, dtb_col)

    def full(width):
        return pl.BlockSpec((CHUNK, width), lambda b, t: (b * nc + t, 0))

    return pl.pallas_call(
        _gdn_scan_kernel, grid=(batch, nc),
        in_specs=[full(GDN_V), full(GDN_V), full(GDN_QK), full(GDN_QK), full(GDN_HEADS * CHUNK),
                  pl.BlockSpec((CONV_HALO, GDN_V), lambda b, t: (b * nc + t, 0)),
                  pl.BlockSpec((CHUNK, GDN_V), lambda b, t: (b * nc + t, OFF_GZ // GDN_V)),
                  pl.BlockSpec((1, GDN_DV), lambda b, t: (0, 0))],
        out_specs=full(GDN_V),
        out_shape=jax.ShapeDtypeStruct((n, GDN_V), BF16),
        scratch_shapes=[pltpu.VMEM((GDN_HEADS, GDN_DK, GDN_DV), F32)],
        compiler_params=_cparams(2), name="gdn_scan",
    )(u, wm, qd, kd, a_in, egl, proj, nw)


def _merge_kernel(att_ref, gdn_ref, wa_ref, wg_ref, ga_ref, gb_ref, o_ref):
    a = jnp.dot(att_ref[...], wa_ref[...], preferred_element_type=F32)
    g = jnp.dot(gdn_ref[...], wg_ref[...], preferred_element_type=F32)
    o_ref[...] = (_sigmoid(ga_ref[...]) * a + _sigmoid(gb_ref[...]) * g).astype(BF16)


def _merge(att, gdn, w_o_attn, w_o_gdn, proj):
    n = att.shape[0]
    tm, tn = min(512, n), 1024
    return pl.pallas_call(
        _merge_kernel, grid=(n // tm, D_MODEL // tn),
        in_specs=[pl.BlockSpec((tm, ATT_Q), lambda i, j: (i, 0)),
                  pl.BlockSpec((tm, GDN_V), lambda i, j: (i, 0)),
                  pl.BlockSpec((ATT_Q, tn), lambda i, j: (0, j)),
                  pl.BlockSpec((GDN_V, tn), lambda i, j: (0, j)),
                  pl.BlockSpec((tm, tn), lambda i, j: (i, OFF_GTA // tn + j)),
                  pl.BlockSpec((tm, tn), lambda i, j: (i, OFF_GTB // tn + j))],
        out_specs=pl.BlockSpec((tm, tn), lambda i, j: (i, j)),
        out_shape=jax.ShapeDtypeStruct((n, D_MODEL), BF16),
        compiler_params=_cparams(2), name="mixer_merge",
    )(att, gdn, w_o_attn.astype(BF16), w_o_gdn.astype(BF16), proj, proj)


def _pack_halves(x):
    w = x.shape[1] // 2
    bits = lax.bitcast_convert_type(x.astype(BF16).astype(F32), U32)
    return (bits[:, :w] >> 16) | (bits[:, w:] & jnp.uint32(0xFFFF0000))


def _unpack_halves(u):
    lo = lax.bitcast_convert_type(u << 16, F32)
    hi = lax.bitcast_convert_type(u & jnp.uint32(0xFFFF0000), F32)
    return lo, hi


SLAB = 8


def _store_slabs(ref, lead, u):
    m = u.shape[0]
    for s in range(SLAB):
        ref[lead + (pl.ds(s, m, stride=SLAB), slice(None))] = u[:, s * LANES:(s + 1) * LANES]


def _load_slabs(ref, lead, start, m):
    return jnp.concatenate([ref[lead + (pl.ds(start + s, m, stride=SLAB), slice(None))] for s in range(SLAB)],
                           axis=1)


def _outproj_ln_kernel(m_ref, w_ref, x_ref, g_ref, b_ref, x1_ref, x1p_ref, *, alpha):
    y = jnp.dot(m_ref[...], w_ref[...], preferred_element_type=F32)
    x1 = _layer_norm(alpha * x_ref[...] + y, g_ref[...], b_ref[...])
    x1_ref[...] = x1
    _store_slabs(x1p_ref, (), _pack_halves(x1))


def _outproj_ln(merged, w_out, x2d, ln_g, ln_b, alpha):
    n, d = x2d.shape
    tm = min(256, n)
    row = pl.BlockSpec((1, d), lambda i: (0, 0))
    tile = pl.BlockSpec((tm, d), lambda i: (i, 0))
    return pl.pallas_call(
        functools.partial(_outproj_ln_kernel, alpha=alpha), grid=(n // tm,),
        in_specs=[tile, pl.BlockSpec((d, d), lambda i: (0, 0)), tile, row, row],
        out_specs=[tile, pl.BlockSpec((tm * SLAB, LANES), lambda i: (i, 0))],
        out_shape=[jax.ShapeDtypeStruct((n, d), F32), jax.ShapeDtypeStruct((n * SLAB, LANES), U32)],
        compiler_params=_cparams(1), name="outproj_ln",
    )(merged, w_out.astype(BF16), x2d, ln_g.astype(F32).reshape(1, d), ln_b.astype(F32).reshape(1, d))


def _first_max(x, iota, size):
    m = jnp.max(x, axis=0, keepdims=True)
    f = jnp.min(jnp.where(x == m, iota, float(size)), axis=0, keepdims=True)
    return m, f


def _router_kernel(x_ref, w_ref, b_ref, eidx_ref, gate_ref, rank_ref, cnt_ref, carry_sc):
    tm = x_ref.shape[0]
    per = N_EXPERTS // N_GROUPS
    neg = -jnp.inf

    @pl.when(pl.program_id(0) == 0)
    def _():
        carry_sc[...] = jnp.zeros(carry_sc.shape, F32)

    logits = _nt(w_ref[...], x_ref[...], precision=HI)
    s = _sigmoid(logits)
    choice = s + b_ref[:, 0:1]
    io_g = lax.broadcasted_iota(I32, (per, tm), 0).astype(F32)
    io_e = lax.broadcasted_iota(I32, (N_EXPERTS, tm), 0).astype(F32)

    gs = []
    for g in range(N_GROUPS):
        xg = choice[g * per:(g + 1) * per]
        m1, f1 = _first_max(xg, io_g, per)
        m2 = jnp.max(jnp.where(io_g == f1, neg, xg), axis=0, keepdims=True)
        gs.append(m1 + m2)
    gw = jnp.concatenate(gs, axis=0)
    gsel = jnp.zeros((N_GROUPS, tm), F32)
    for _ in range(TOPK_GROUPS):
        _, f = _first_max(gw, io_g, N_GROUPS)
        hit = io_g == f
        gsel = jnp.where(hit, 1.0, gsel)
        gw = jnp.where(hit, neg, gw)
    x = jnp.concatenate([jnp.where(gsel[g:g + 1] > 0.0, choice[g * per:(g + 1) * per], neg)
                         for g in range(N_GROUPS)], axis=0)

    mem = jnp.zeros((N_EXPERTS, tm), F32)
    picks, wsel = [], []
    for _ in range(TOP_K):
        _, f = _first_max(x, io_e, N_EXPERTS)
        hit = io_e == f
        picks.append(f)
        wsel.append(jnp.sum(jnp.where(hit, s, 0.0), axis=0, keepdims=True))
        mem = jnp.where(hit, 1.0, mem)
        x = jnp.where(hit, neg, x)
    wsel = jnp.concatenate(wsel, axis=0)
    gate_ref[...] = wsel / jnp.sum(wsel, axis=0, keepdims=True) * ROUTED_SCALE
    eidx_ref[...] = jnp.concatenate(picks, axis=0).astype(I32)

    tr = lax.broadcasted_iota(I32, (tm, tm), 0)
    tc = lax.broadcasted_iota(I32, (tm, tm), 1)
    before = jnp.where(tr < tc, 1.0, 0.0).astype(BF16)
    prefix = jnp.dot(mem.astype(BF16), before, preferred_element_type=F32) + carry_sc[:, 0:1]
    ranks = [jnp.sum(jnp.where(io_e == f, prefix, 0.0), axis=0, keepdims=True) for f in picks]
    rank_ref[...] = jnp.concatenate(ranks, axis=0).astype(I32)
    carry_sc[...] = carry_sc[...] + jnp.sum(mem, axis=1, keepdims=True)
    cnt_ref[...] = carry_sc[...]


def _router(x1, w_router, router_bias):
    n, d = x1.shape
    tm = min(512, n)
    out = pl.BlockSpec((TOP_K, tm), lambda i: (0, i))
    bias = jnp.broadcast_to(router_bias.astype(F32)[:, None], (N_EXPERTS, LANES))
    return pl.pallas_call(
        _router_kernel, grid=(n // tm,),
        in_specs=[pl.BlockSpec((tm, d), lambda i: (i, 0)),
                  pl.BlockSpec((N_EXPERTS, d), lambda i: (0, 0)),
                  pl.BlockSpec((N_EXPERTS, LANES), lambda i: (0, 0))],
        out_specs=[out, out, out, pl.BlockSpec((N_EXPERTS, LANES), lambda i: (0, 0))],
        out_shape=[jax.ShapeDtypeStruct((TOP_K, n), I32), jax.ShapeDtypeStruct((TOP_K, n), F32),
                   jax.ShapeDtypeStruct((TOP_K, n), I32), jax.ShapeDtypeStruct((N_EXPERTS, LANES), F32)],
        scratch_shapes=[pltpu.VMEM((N_EXPERTS, LANES), F32)],
        compiler_params=_cparams(1), name="router",
    )(x1, w_router.astype(F32).T, bias)


SCATTER_TOKENS = 512
COMBINE_TOKENS = 128
ROW_COPY_UNROLL = 8


def _pos_kernel(pstart_ref, eidx_ref, rank_ref, pos_ref):
    e = eidx_ref[...]
    acc = rank_ref[...]
    for x in range(N_EXPERTS):
        acc = acc + jnp.where(e == x, pstart_ref[x], 0)
    pos_ref[...] = acc


def _positions(pstart, eidx, rank):
    n = eidx.shape[1]
    tm = min(2048, n)
    tile = pl.BlockSpec((TOP_K, tm), lambda i, ps: (0, i))
    return pl.pallas_call(
        _pos_kernel,
        grid_spec=pltpu.PrefetchScalarGridSpec(num_scalar_prefetch=1, grid=(n // tm,),
                                               in_specs=[tile, tile], out_specs=tile),
        out_shape=jax.ShapeDtypeStruct((TOP_K, n), I32),
        compiler_params=_cparams(1), name="moe_positions",
    )(pstart, eidx, rank)


def _scatter_kernel(pend_ref, padded_ref, pos_ref, x_ref, xs_ref, zero_sc, sem, zsem):
    tm = x_ref.shape[0] // SLAB
    bm = EXPERT_ROWS

    @pl.when(pl.program_id(0) == 0)
    def _():
        zero_sc[...] = jnp.zeros(zero_sc.shape, U32)

        def tail_copy(e):
            tail = pl.multiple_of((pend_ref[e] - bm) * SLAB, bm * SLAB)
            return pltpu.make_async_copy(zero_sc, xs_ref.at[pl.ds(tail, bm * SLAB), :], zsem)

        def zstart(e, carry):
            @pl.when(padded_ref[e] > 0)
            def _():
                tail_copy(e).start()
            return carry

        def zwait(e, carry):
            @pl.when(padded_ref[e] > 0)
            def _():
                tail_copy(e).wait()
            return carry

        lax.fori_loop(0, N_EXPERTS, zstart, 0)
        lax.fori_loop(0, N_EXPERTS, zwait, 0)

    def row_copy(t, k):
        src = x_ref.at[pl.ds(pl.multiple_of(t * SLAB, SLAB), SLAB), :]
        dst = xs_ref.at[pl.ds(pl.multiple_of(pos_ref[k, t] * SLAB, SLAB), SLAB), :]
        return pltpu.make_async_copy(src, dst, sem)

    def start(t, carry):
        for k in range(TOP_K):
            row_copy(t, k).start(priority=k % 2)
        return carry

    lax.fori_loop(0, tm, start, 0, unroll=ROW_COPY_UNROLL)
    rows = xs_ref.at[pl.ds(0, tm * TOP_K * SLAB), :]
    pltpu.make_async_copy(rows, rows, sem).wait()


def _scatter_rows(x1p, pos, pend, padded, n_rows):
    n = x1p.shape[0] // SLAB
    tm = min(SCATTER_TOKENS, n)
    grid_spec = pltpu.PrefetchScalarGridSpec(
        num_scalar_prefetch=2, grid=(n // tm,),
        in_specs=[pl.BlockSpec((TOP_K, tm), lambda i, pe, pd: (0, i), memory_space=pltpu.SMEM),
                  pl.BlockSpec((tm * SLAB, LANES), lambda i, pe, pd: (i, 0))],
        out_specs=pl.BlockSpec(memory_space=pl.ANY),
        scratch_shapes=[pltpu.VMEM((EXPERT_ROWS * SLAB, LANES), U32), pltpu.SemaphoreType.DMA,
                        pltpu.SemaphoreType.DMA])
    return pl.pallas_call(
        _scatter_kernel, grid_spec=grid_spec,
        out_shape=jax.ShapeDtypeStruct((n_rows * SLAB, LANES), U32),
        compiler_params=_cparams(1), name="moe_scatter",
    )(pend, padded, pos, x1p)


def _expert_kernel(blk_e_ref, blk_on_ref, blk_new_ref, x_ref, w1_ref, w3_ref, w2_ref, y_ref,
                   w1_sc, w3_sc, w2_sc):
    del blk_e_ref
    i = pl.program_id(0)

    @pl.when(blk_new_ref[i] > 0)
    def _():
        w1_sc[...] = w1_ref[...].astype(BF16)
        w3_sc[...] = w3_ref[...].astype(BF16)
        w2_sc[...] = w2_ref[...].astype(BF16)

    @pl.when(blk_on_ref[i] > 0)
    def _():
        half = D_MODEL // 2
        lo, hi = _unpack_halves(_load_slabs(x_ref, (), 0, EXPERT_ROWS))
        xa, xb = lo.astype(BF16), hi.astype(BF16)

        def up(w_sc):
            return (jnp.dot(xa, w_sc[0:half, :], preferred_element_type=F32)
                    + jnp.dot(xb, w_sc[half:, :], preferred_element_type=F32))

        h1, h3 = up(w1_sc), up(w3_sc)
        act = (h1 * _sigmoid(h1) * h3).astype(BF16)
        _store_slabs(y_ref, (), _pack_halves(jnp.dot(act, w2_sc[...], preferred_element_type=F32)))


def _experts(xs, blk_e, blk_on, blk_new, w1, w3, w2):
    n_rows, w = xs.shape
    bm = EXPERT_ROWS * SLAB
    grid_spec = pltpu.PrefetchScalarGridSpec(
        num_scalar_prefetch=3, grid=(n_rows // bm,),
        in_specs=[pl.BlockSpec((bm, w), lambda i, e, on, nw: (i, 0)),
                  pl.BlockSpec((None, D_MODEL, EXPERT_DIM), lambda i, e, on, nw: (e[i], 0, 0)),
                  pl.BlockSpec((None, D_MODEL, EXPERT_DIM), lambda i, e, on, nw: (e[i], 0, 0)),
                  pl.BlockSpec((None, EXPERT_DIM, D_MODEL), lambda i, e, on, nw: (e[i], 0, 0))],
        out_specs=pl.BlockSpec((bm, w), lambda i, e, on, nw: (i, 0)),
        scratch_shapes=[pltpu.VMEM((D_MODEL, EXPERT_DIM), BF16), pltpu.VMEM((D_MODEL, EXPERT_DIM), BF16),
                        pltpu.VMEM((EXPERT_DIM, D_MODEL), BF16)])
    return pl.pallas_call(
        _expert_kernel, grid_spec=grid_spec,
        out_shape=jax.ShapeDtypeStruct((n_rows, w), U32),
        compiler_params=_cparams(1), name="moe_experts",
    )(blk_e, blk_on, blk_new, xs, w1, w3, w2)


def _shared_up_kernel(x_ref, w_ref, h_ref):
    h = jnp.dot(x_ref[...].astype(BF16), w_ref[...], preferred_element_type=F32)
    h1, h3 = h[:, :SHARED_DIM], h[:, SHARED_DIM:]
    h_ref[...] = (h1 * _sigmoid(h1) * h3).astype(BF16)


def _shared_up(x1, ws1, ws3):
    n, d = x1.shape
    tm = min(512, n)
    w13 = jnp.concatenate([ws1, ws3], axis=1).astype(BF16)
    return pl.pallas_call(
        _shared_up_kernel, grid=(n // tm,),
        in_specs=[pl.BlockSpec((tm, d), lambda i: (i, 0)),
                  pl.BlockSpec((d, 2 * SHARED_DIM), lambda i: (0, 0))],
        out_specs=pl.BlockSpec((tm, SHARED_DIM), lambda i: (i, 0)),
        out_shape=jax.ShapeDtypeStruct((n, SHARED_DIM), BF16),
        compiler_params=_cparams(1), name="shared_up",
    )(x1, w13)


def _combine_kernel(pos_ref, pos_next_ref, x1_ref, h_ref, ws2_ref, gate_ref, g_ref, b_ref, y_hbm, o_ref,
                    ybuf, sem, *, alpha):
    tm = x1_ref.shape[0]
    i = pl.program_id(0)
    slot = i % 2

    def issue(p_ref, s):
        def start(t, carry):
            for k in range(TOP_K):
                src = y_hbm.at[pl.ds(pl.multiple_of(p_ref[k, t] * SLAB, SLAB), SLAB), :]
                dst = ybuf.at[s, pl.ds(pl.multiple_of((k * tm + t) * SLAB, SLAB), SLAB), :]
                pltpu.make_async_copy(src, dst, sem.at[s]).start(priority=k % 2)
            return carry

        lax.fori_loop(0, tm, start, 0, unroll=ROW_COPY_UNROLL)

    @pl.when(i == 0)
    def _():
        issue(pos_ref, 0)

    @pl.when(i + 1 < pl.num_programs(0))
    def _():
        issue(pos_next_ref, 1 - slot)

    base = alpha * x1_ref[...] + jnp.dot(h_ref[...], ws2_ref[...], preferred_element_type=F32)
    pltpu.make_async_copy(y_hbm.at[pl.ds(0, tm * TOP_K * SLAB), :], ybuf.at[slot], sem.at[slot]).wait()

    gates = gate_ref[...]
    acc_lo = jnp.zeros((tm, D_MODEL // 2), F32)
    acc_hi = jnp.zeros((tm, D_MODEL // 2), F32)
    for k in range(TOP_K):
        lo, hi = _unpack_halves(_load_slabs(ybuf, (slot,), k * tm * SLAB, tm))
        gk = gates[:, k:k + 1]
        acc_lo = acc_lo + gk * lo
        acc_hi = acc_hi + gk * hi
    f = jnp.concatenate([acc_lo, acc_hi], axis=1)
    o_ref[...] = _layer_norm(base + f, g_ref[...], b_ref[...])


def _combine(x1, hsh, ws2, pos, gates_t, ys, ln_g, ln_b, alpha):
    n, d = x1.shape
    tm = min(COMBINE_TOKENS, n)
    row = pl.BlockSpec((1, d), lambda i: (0, 0))
    last = n // tm - 1
    return pl.pallas_call(
        functools.partial(_combine_kernel, alpha=alpha), grid=(n // tm,),
        in_specs=[pl.BlockSpec((TOP_K, tm), lambda i: (0, i), memory_space=pltpu.SMEM),
                  pl.BlockSpec((TOP_K, tm), lambda i: (0, jnp.minimum(i + 1, last)), memory_space=pltpu.SMEM),
                  pl.BlockSpec((tm, d), lambda i: (i, 0)),
                  pl.BlockSpec((tm, SHARED_DIM), lambda i: (i, 0)),
                  pl.BlockSpec((SHARED_DIM, d), lambda i: (0, 0)),
                  pl.BlockSpec((tm, TOP_K), lambda i: (i, 0)),
                  row, row,
                  pl.BlockSpec(memory_space=pl.ANY)],
        out_specs=pl.BlockSpec((tm, d), lambda i: (i, 0)),
        out_shape=jax.ShapeDtypeStruct((n, d), F32),
        scratch_shapes=[pltpu.VMEM((2, TOP_K * tm * SLAB, LANES), U32), pltpu.SemaphoreType.DMA((2,))],
        compiler_params=_cparams(1), name="moe_combine",
    )(pos, pos, x1, hsh, ws2.astype(BF16), gates_t, ln_g.astype(F32).reshape(1, d),
      ln_b.astype(F32).reshape(1, d), ys)


def _moe(x1, x1p, w_router, router_bias, w1, w3, w2, ws1, ws3, ws2, ln_g, ln_b, alpha):
    n = x1.shape[0]
    bm = EXPERT_ROWS
    eidx, gates, rank, cnt = _router(x1, w_router, router_bias)
    counts = cnt[:, 0].astype(I32)
    padded = (counts + bm - 1) // bm * bm
    pend = jnp.cumsum(padded)
    pstart = pend - padded
    pos = _positions(pstart, eidx, rank)
    n_rows = -(-(n * TOP_K) // bm) * bm + N_EXPERTS * bm
    blk_start = jnp.arange(n_rows // bm, dtype=I32) * bm
    blk_on = (blk_start < pend[-1]).astype(I32)
    last = jnp.maximum(pend[-1] - bm, 0)
    blk_e = jnp.sum((pend[None, :] <= jnp.minimum(blk_start, last)[:, None]).astype(I32), axis=1)
    blk_e = jnp.minimum(blk_e, N_EXPERTS - 1)
    changed = jnp.concatenate([jnp.ones((1,), I32), (blk_e[1:] != blk_e[:-1]).astype(I32)])
    blk_new = blk_on * changed
    xs = _scatter_rows(x1p, pos, pend, padded, n_rows)
    ys = _experts(xs, blk_e, blk_on, blk_new, w1, w3, w2)
    hsh = _shared_up(x1, ws1, ws3)
    return _combine(x1, hsh, ws2, pos, gates.T, ys, ln_g, ln_b, alpha)


def kernel(x, positions, w_in, conv_w, A_log, dt_bias, gdn_norm_w, idx_ln_g, idx_ln_b, w_o_attn, w_o_gdn, w_out, ln1_g, ln1_b, w_router, router_bias, w1, w3, w2, ws1, ws3, ws2, ln2_g, ln2_b):
    batch, seq, d = x.shape
    depth = w_in.shape[0]
    alpha = (2 * depth) ** 0.25
    n = batch * seq
    xf = x.reshape(n, d)
    tabs = _trig_tables(positions)
    for l in range(depth):
        proj = _project(xf, _pack_w_in(w_in[l]))
        q_r, k_r, v_ext, iq_r, ik_lo, ik_hi, iw = _att_prep(proj, tabs, idx_ln_g[l], idx_ln_b[l])
        att = _dsa_attention(q_r, k_r, v_ext, iq_r, ik_lo, ik_hi, iw, batch, seq)
        gdn = _gated_deltanet(proj, conv_w[l], A_log[l], dt_bias[l], gdn_norm_w[l], batch, seq)
        merged = _merge(att, gdn, w_o_attn[l], w_o_gdn[l], proj)
        x1, x1p = _outproj_ln(merged, w_out[l], xf, ln1_g[l], ln1_b[l], alpha)
        xf = _moe(x1, x1p, w_router[l], router_bias[l], w1[l], w3[l], w2[l], ws1[l], ws3[l], ws2[l],
                  ln2_g[l], ln2_b[l], alpha)
    return xf.reshape(batch, seq, d)
```

```python
import functools
import math

import jax
import jax.numpy as jnp
import numpy as np
from jax import lax
from jax.experimental import pallas as pl
from jax.experimental.pallas import tpu as pltpu

F32 = jnp.float32
BF16 = jnp.bfloat16
I32 = jnp.int32
U32 = jnp.uint32
HI = lax.Precision.HIGHEST

D_MODEL = 2048
ATT_HEADS = 16
ATT_KV_HEADS = 2
HEAD_DIM = 128
IDX_HEADS = 8
IDX_DIM = 64
TOPK_MAX = 256
Q_BLOCK = 128
ROPE_THETA = 10000.0
GDN_HEADS = 16
GDN_DK = 128
GDN_DV = 128
CONV_WIDTH = 4
CHUNK = 64
N_EXPERTS = 64
TOP_K = 8
N_GROUPS = 8
TOPK_GROUPS = 4
EXPERT_DIM = 512
SHARED_DIM = 512
ROUTED_SCALE = 2.5
LN_EPS = 1e-5
RMS_EPS = 1e-6

ATT_Q = ATT_HEADS * HEAD_DIM
ATT_KV = ATT_KV_HEADS * HEAD_DIM
IDX_Q = IDX_HEADS * IDX_DIM
GDN_QK = GDN_HEADS * GDN_DK
GDN_V = GDN_HEADS * GDN_DV
SPLITS = (ATT_Q, ATT_KV, ATT_KV, IDX_Q, IDX_DIM, IDX_HEADS, GDN_QK, GDN_QK, GDN_V, GDN_V,
          GDN_HEADS, GDN_HEADS, D_MODEL, D_MODEL)
SPLIT_OFFSETS = tuple(int(o) for o in np.cumsum(SPLITS)[:-1])

LANES = 128
VMEM_LIMIT_BYTES = 56 * 1024 * 1024

OFF_AQ = 0
OFF_GQ = OFF_AQ + ATT_Q
OFF_GK = OFF_GQ + GDN_QK
OFF_GV = OFF_GK + GDN_QK
OFF_GZ = OFF_GV + GDN_V
OFF_GTA = OFF_GZ + GDN_V
OFF_GTB = OFF_GTA + D_MODEL
OFF_AK = OFF_GTB + D_MODEL
OFF_AV = OFF_AK + ATT_KV
OFF_IQ = OFF_AV + ATT_KV
OFF_SM = OFF_IQ + IDX_Q
PROJ_W = 16384
SM_IK = 0
SM_IW = SM_IK + IDX_DIM
SM_BETA = SM_IW + IDX_HEADS
SM_DT = SM_BETA + GDN_HEADS

KEY_CHUNK = 512
ATT_ROWS = 256
EXPERT_ROWS = 512
INT_MIN = -2147483648
MASK_BIAS = -1e30
M_INIT = -1e29


def _cparams(n_axes):
    return pltpu.CompilerParams(dimension_semantics=("arbitrary",) * n_axes,
                                vmem_limit_bytes=VMEM_LIMIT_BYTES)


def _nt(a, b, precision=None):
    return lax.dot_general(a, b, (((1,), (1,)), ((), ())), precision=precision,
                           preferred_element_type=F32)


def _tn(a, b, precision=None):
    return lax.dot_general(a, b, (((0,), (0,)), ((), ())), precision=precision,
                           preferred_element_type=F32)


def _sigmoid(x):
    return jax.nn.sigmoid(x)


def _layer_norm(x, g, b):
    mu = jnp.mean(x, axis=-1, keepdims=True)
    xc = x - mu
    var = jnp.mean(xc * xc, axis=-1, keepdims=True)
    return xc * lax.rsqrt(var + LN_EPS) * g + b


def _trig_kernel(pos_ref, inv_a_ref, inv_i_ref, sgn_a_ref, sgn_i_ref, cos_a, sin_a, cos_i, sin_i):
    p = pos_ref[...]
    ang_a = p * inv_a_ref[...]
    cos_a[...] = jnp.cos(ang_a)
    sin_a[...] = jnp.sin(ang_a) * sgn_a_ref[...]
    ang_i = p * inv_i_ref[...]
    cos_i[...] = jnp.cos(ang_i)
    sin_i[...] = jnp.sin(ang_i) * sgn_i_ref[...]


def _trig_tables(positions):
    n = positions.size
    tm = min(512, n)
    pos = positions.reshape(n, 1).astype(F32)
    inv_a = ROPE_THETA ** (-jnp.arange(0, HEAD_DIM, 2, dtype=F32) / HEAD_DIM)
    inv_i = ROPE_THETA ** (-jnp.arange(0, IDX_DIM, 2, dtype=F32) / IDX_DIM)
    half_a, half_i = HEAD_DIM // 2, IDX_DIM // 2
    inv_a_row = jnp.tile(inv_a, 2).reshape(1, LANES)
    inv_i_row = jnp.tile(inv_i, 4).reshape(1, LANES)
    lane = np.arange(LANES)
    sgn_a = jnp.asarray(np.where(lane < half_a, -1.0, 1.0), F32).reshape(1, LANES)
    sgn_i = jnp.asarray(np.where(lane % IDX_DIM < half_i, -1.0, 1.0), F32).reshape(1, LANES)
    row = pl.BlockSpec((1, LANES), lambda i: (0, 0))
    tab = pl.BlockSpec((tm, LANES), lambda i: (i, 0))
    return pl.pallas_call(
        _trig_kernel, grid=(n // tm,),
        in_specs=[pl.BlockSpec((tm, 1), lambda i: (i, 0)), row, row, row, row],
        out_specs=[tab, tab, tab, tab],
        out_shape=[jax.ShapeDtypeStruct((n, LANES), F32)] * 4,
        compiler_params=_cparams(1), name="trig_tables",
    )(pos, inv_a_row, inv_i_row, sgn_a, sgn_i)


def _pack_plan():
    src_off = dict(zip(("aq", "ak", "av", "iq", "ik", "iw", "gq", "gk", "gv", "gz", "gb", "ga", "gta", "gtb"),
                       (0,) + SPLIT_OFFSETS))
    order = (("aq", OFF_AQ, ATT_Q), ("gq", OFF_GQ, GDN_QK), ("gk", OFF_GK, GDN_QK), ("gv", OFF_GV, GDN_V),
             ("gz", OFF_GZ, GDN_V), ("gta", OFF_GTA, D_MODEL), ("gtb", OFF_GTB, D_MODEL), ("ak", OFF_AK, ATT_KV),
             ("av", OFF_AV, ATT_KV), ("iq", OFF_IQ, IDX_Q), ("ik", OFF_SM + SM_IK, IDX_DIM),
             ("iw", OFF_SM + SM_IW, IDX_HEADS), ("gb", OFF_SM + SM_BETA, GDN_HEADS), ("ga", OFF_SM + SM_DT, GDN_HEADS))
    src_col = np.full((PROJ_W,), -1, np.int64)
    for name, dst, width in order:
        src_col[dst:dst + width] = src_off[name] + np.arange(width)
    plan, shifts = [], set()
    for j in range(PROJ_W // LANES):
        cols = src_col[j * LANES:(j + 1) * LANES]
        if j == OFF_SM // LANES:
            a, b = src_off["ik"], src_off["gb"] - SM_BETA
            assert a % LANES == 0 and b % LANES == 0
            plan.append((3, a // LANES, b // LANES, 0))
        elif (cols < 0).all():
            plan.append((2, 0, 0, 0))
        else:
            assert (np.diff(cols) == 1).all()
            shift = int(cols[0] % LANES)
            if shift == 0:
                plan.append((0, cols[0] // LANES, cols[0] // LANES, 0))
            else:
                shifts.add(shift)
                plan.append((1, cols[0] // LANES, cols[0] // LANES + 1, shift))
    return np.asarray(plan, np.int32), tuple(sorted(shifts))


def _pack_kernel(mode_ref, ta_ref, tb_ref, shift_ref, a_ref, b_ref, o_ref, *, shifts):
    del ta_ref, tb_ref
    mode = mode_ref[pl.program_id(0)]
    lane = lax.broadcasted_iota(I32, a_ref.shape, 1)

    @pl.when(mode == 0)
    def _():
        o_ref[...] = a_ref[...].astype(BF16)

    for shift in shifts:
        @pl.when((mode == 1) & (shift_ref[pl.program_id(0)] == shift))
        def _(shift=shift):
            o_ref[...] = jnp.concatenate([a_ref[:, shift:], b_ref[:, :shift]], axis=1).astype(BF16)

    @pl.when(mode == 2)
    def _():
        o_ref[...] = jnp.zeros(o_ref.shape, BF16)

    @pl.when(mode == 3)
    def _():
        small = jnp.where(lane < SM_BETA, a_ref[...], jnp.where(lane < SM_DT + GDN_HEADS, b_ref[...], 0.0))
        o_ref[...] = small.astype(BF16)


def _pack_w_in(w_in):
    d, width = w_in.shape
    plan, shifts = _pack_plan()
    last = (width - 1) // LANES
    grid_spec = pltpu.PrefetchScalarGridSpec(
        num_scalar_prefetch=4, grid=(PROJ_W // LANES,),
        in_specs=[pl.BlockSpec((d, LANES), lambda j, m, ta, tb, sh: (0, ta[j])),
                  pl.BlockSpec((d, LANES), lambda j, m, ta, tb, sh: (0, jnp.minimum(tb[j], last)))],
        out_specs=pl.BlockSpec((d, LANES), lambda j, m, ta, tb, sh: (0, j)))
    return pl.pallas_call(
        functools.partial(_pack_kernel, shifts=shifts), grid_spec=grid_spec,
        out_shape=jax.ShapeDtypeStruct((d, PROJ_W), BF16),
        compiler_params=_cparams(1), name="pack_w_in",
    )(*(jnp.asarray(plan[:, c]) for c in range(4)), w_in, w_in)


def _proj_kernel(x_ref, w_ref, o_ref, xb_ref):
    @pl.when(pl.program_id(1) == 0)
    def _():
        xb_ref[...] = x_ref[...].astype(BF16)

    o_ref[...] = jnp.dot(xb_ref[...], w_ref[...], preferred_element_type=F32)


def _project(x2d, w_packed):
    n, d = x2d.shape
    tm, tn = min(1024, n), 1024
    return pl.pallas_call(
        _proj_kernel, grid=(n // tm, PROJ_W // tn),
        in_specs=[pl.BlockSpec((tm, d), lambda i, j: (i, 0)),
                  pl.BlockSpec((d, tn), lambda i, j: (0, j))],
        out_specs=pl.BlockSpec((tm, tn), lambda i, j: (i, j)),
        out_shape=jax.ShapeDtypeStruct((n, PROJ_W), F32),
        scratch_shapes=[pltpu.VMEM((tm, d), BF16)],
        compiler_params=_cparams(2), name="in_proj",
    )(x2d, w_packed)


def _rope_head(x, cos, sin_signed):
    return x * cos + pltpu.roll(x, HEAD_DIM // 2, 1) * sin_signed


def _rope_idx(x, cos, sin_signed, first_half):
    half = IDX_DIM // 2
    partner = jnp.where(first_half, pltpu.roll(x, LANES - half, 1), pltpu.roll(x, half, 1))
    return x * cos + partner * sin_signed


def _att_prep_kernel(aq_ref, ak_ref, av_ref, iq_ref, sm_ref, cos_a_ref, sin_a_ref, cos_i_ref,
                     sin_i_ref, lng_ref, lnb_ref,
                     q_out, k_out, v_out, iq_out, iklo_out, ikhi_out, iw_out):
    cos_a, sin_a = cos_a_ref[...], sin_a_ref[...]
    cos_i, sin_i = cos_i_ref[...], sin_i_ref[...]
    rows = cos_a.shape[0]
    lane = lax.broadcasted_iota(I32, (rows, LANES), 1)
    first_half = (lane % IDX_DIM) < (IDX_DIM // 2)

    for h in range(ATT_HEADS):
        q_h = _rope_head(aq_ref[:, h * HEAD_DIM:(h + 1) * HEAD_DIM], cos_a, sin_a) * (
            HEAD_DIM ** -0.5 * math.log2(math.e))
        q_out[0, h] = q_h.astype(BF16)
    ones = jnp.ones((rows, HEAD_DIM), BF16)
    for g in range(ATT_KV_HEADS):
        sl = slice(g * HEAD_DIM, (g + 1) * HEAD_DIM)
        k_out[:, sl] = _rope_head(ak_ref[:, sl], cos_a, sin_a).astype(BF16)
        v_out[:, 2 * g * HEAD_DIM:(2 * g + 1) * HEAD_DIM] = av_ref[:, sl].astype(BF16)
        v_out[:, (2 * g + 1) * HEAD_DIM:(2 * g + 2) * HEAD_DIM] = ones
    for j in range(IDX_HEADS // 2):
        iq_out[0, j] = _rope_idx(iq_ref[:, j * LANES:(j + 1) * LANES], cos_i, sin_i, first_half).astype(BF16)

    sm = sm_ref[...]
    in_ik = lane < IDX_DIM
    mu = jnp.sum(jnp.where(in_ik, sm, 0.0), axis=1, keepdims=True) / IDX_DIM
    xc = sm - mu
    var = jnp.sum(jnp.where(in_ik, xc * xc, 0.0), axis=1, keepdims=True) / IDX_DIM
    ik = xc * lax.rsqrt(var + LN_EPS) * lng_ref[...] + lnb_ref[...]
    ik = jnp.where(in_ik, _rope_idx(ik, cos_i, sin_i, first_half), 0.0)
    iklo_out[...] = ik.astype(BF16)
    ikhi_out[...] = pltpu.roll(ik, IDX_DIM, 1).astype(BF16)
    iw_out[...] = (sm * IDX_HEADS ** -0.5) * IDX_DIM ** -0.5


def _att_prep(proj, tabs, idx_ln_g, idx_ln_b):
    n = proj.shape[0]
    tm = min(ATT_ROWS, n)
    cos_a, sin_a, cos_i, sin_i = tabs
    pad = jnp.zeros((LANES - IDX_DIM,), F32)
    lng = jnp.concatenate([idx_ln_g.astype(F32), pad]).reshape(1, LANES)
    lnb = jnp.concatenate([idx_ln_b.astype(F32), pad]).reshape(1, LANES)
    tab = pl.BlockSpec((tm, LANES), lambda i: (i, 0))
    row = pl.BlockSpec((1, LANES), lambda i: (0, 0))

    def col(width, off):
        return pl.BlockSpec((tm, width), lambda i: (i, off // width))

    return pl.pallas_call(
        _att_prep_kernel, grid=(n // tm,),
        in_specs=[col(ATT_Q, OFF_AQ), col(ATT_KV, OFF_AK), col(ATT_KV, OFF_AV), col(IDX_Q, OFF_IQ),
                  col(LANES, OFF_SM), tab, tab, tab, tab, row, row],
        out_specs=[pl.BlockSpec((1, ATT_HEADS, tm, HEAD_DIM), lambda i: (i, 0, 0, 0)),
                   pl.BlockSpec((tm, ATT_KV), lambda i: (i, 0)),
                   pl.BlockSpec((tm, 2 * ATT_KV), lambda i: (i, 0)),
                   pl.BlockSpec((1, IDX_HEADS // 2, tm, LANES), lambda i: (i, 0, 0, 0)),
                   tab, tab, tab],
        out_shape=[jax.ShapeDtypeStruct((n // tm, ATT_HEADS, tm, HEAD_DIM), BF16),
                   jax.ShapeDtypeStruct((n, ATT_KV), BF16),
                   jax.ShapeDtypeStruct((n, 2 * ATT_KV), BF16),
                   jax.ShapeDtypeStruct((n // tm, IDX_HEADS // 2, tm, LANES), BF16),
                   jax.ShapeDtypeStruct((n, LANES), BF16),
                   jax.ShapeDtypeStruct((n, LANES), BF16),
                   jax.ShapeDtypeStruct((n, LANES), F32)],
        compiler_params=_cparams(1), name="att_prep",
    )(proj, proj, proj, proj, proj, cos_a, sin_a, cos_i, sin_i, lng, lnb)


def _dsa_kernel(q_ref, k_ref, v_ref, iq_ref, iklo_ref, ikhi_ref, iw_ref, o_ref,
                keys_sc, acc_sc, m_sc, wrep_sc, sel_sc, *, topk, seq):
    qb, kc = q_ref.shape[2], KEY_CHUNK
    tiles = kc // LANES
    rep = ATT_HEADS // ATT_KV_HEADS
    i = pl.program_id(1)
    nch = (i * qb + qb + kc - 1) // kc
    row = lax.broadcasted_iota(I32, (qb, LANES), 0)
    lane = lax.broadcasted_iota(I32, (qb, LANES), 1)
    t_abs = i * qb + row

    iw = iw_ref[...]
    for h in range(IDX_HEADS):
        wrep_sc[h] = jnp.broadcast_to(iw[:, SM_IW + h:SM_IW + h + 1], (qb, LANES))
    qi = iq_ref[0].reshape((IDX_HEADS // 2) * qb, LANES)

    def score_chunk(c, carry):
        off = pl.multiple_of(c * kc, kc)
        s_lo = _nt(qi, iklo_ref[pl.ds(off, kc), :])
        s_hi = _nt(qi, ikhi_ref[pl.ds(off, kc), :])
        for ct in range(tiles):
            cs = slice(ct * LANES, (ct + 1) * LANES)
            acc = jnp.zeros((qb, LANES), F32)
            for j in range(IDX_HEADS // 2):
                rs = slice(j * qb, (j + 1) * qb)
                acc = acc + wrep_sc[2 * j] * jnp.maximum(s_lo[rs, cs], 0.0)
                acc = acc + wrep_sc[2 * j + 1] * jnp.maximum(s_hi[rs, cs], 0.0)
            bits = lax.bitcast_convert_type(acc, I32)
            key = bits ^ ((bits >> 31) & jnp.int32(0x7FFFFFFF))
            key = jnp.where(bits == jnp.int32(INT_MIN), 0, key)
            causal = (off + ct * LANES + lane) <= t_abs
            keys_sc[c, :, cs] = jnp.where(causal, key, jnp.int32(INT_MIN))
        return carry

    lax.fori_loop(0, nch, score_chunk, 0)

    kth = jnp.float32(topk)

    def count(pred):
        lane_g = lax.broadcasted_iota(I32, (Q_BLOCK, LANES), 1)
        parts = []
        for r0 in range(0, qb, Q_BLOCK):
            rs = slice(r0, r0 + Q_BLOCK)

            def body(c, cnt, rs=rs):
                off = c * kc
                for ct in range(tiles):
                    kt = keys_sc[c, rs, ct * LANES:(ct + 1) * LANES]
                    cnt = cnt + jnp.where(pred(kt, off + ct * LANES + lane_g, rs), 1.0, 0.0)
                return cnt

            parts.append(lax.fori_loop(0, nch, body, jnp.zeros((Q_BLOCK, LANES), F32)))
        return jnp.concatenate([jnp.broadcast_to(jnp.sum(cnt, axis=1, keepdims=True), (Q_BLOCK, LANES))
                                for cnt in parts], axis=0)

    zero = jnp.zeros((qb, LANES), I32)
    n_pos = count(lambda k, s, rs: k >= zero[rs])
    base = jnp.where(n_pos >= kth, zero, jnp.int32(INT_MIN))
    n_base = jnp.where(n_pos >= kth, n_pos, -1.0)
    short = t_abs < topk

    def unresolved(n_base):
        return jnp.max(jnp.where((n_base == kth) | short, 0.0, 1.0)) > 0.0

    def bit_cond(st):
        it, _, n_base = st
        return (it < 31) & unresolved(n_base)

    def bit_step(st):
        it, base, n_base = st
        cand = base | (jnp.int32(1) << (30 - it))
        tot = count(lambda k, s, rs: k >= cand[rs])
        take = tot >= kth
        return it + 1, jnp.where(take, cand, base), jnp.where(take, tot, n_base)

    _, thr, n_ge = lax.while_loop(bit_cond, bit_step, (jnp.int32(0), base, n_base))
    sel_sc[0] = thr
    sel_sc[1] = jnp.full((qb, LANES), seq, I32)

    @pl.when(jnp.max(n_ge) > kth)
    def _():
        nbits = max(1, (seq - 1).bit_length())
        need = kth - count(lambda k, s, rs: k > thr[rs])

        def idx_step(it, posv):
            cand = posv + (jnp.int32(1) << (nbits - 1 - it))
            below = count(lambda k, s, rs: (k == thr[rs]) & (s < cand[rs]))
            return jnp.where(below < need, cand, posv)

        sel_sc[1] = lax.fori_loop(0, nbits, idx_step, zero)

    thr = sel_sc[0]
    last_tie = sel_sc[1]

    m_sc[...] = jnp.full(m_sc.shape, M_INIT, F32)
    acc_sc[...] = jnp.zeros(acc_sc.shape, F32)

    def attend_chunk(c, carry):
        off = pl.multiple_of(c * kc, kc)
        kk = keys_sc[c]
        bias = []
        for ct in range(tiles):
            cs = slice(ct * LANES, (ct + 1) * LANES)
            sidx = off + ct * LANES + lane
            keep = (kk[:, cs] > thr) | ((kk[:, cs] == thr) & (sidx <= last_tie))
            keep = keep & (sidx <= t_abs)
            bias.append(jnp.where(keep, 0.0, MASK_BIAS))
        for g in range(ATT_KV_HEADS):
            kg = k_ref[pl.ds(off, kc), g * HEAD_DIM:(g + 1) * HEAD_DIM]
            vg = v_ref[pl.ds(off, kc), 2 * g * HEAD_DIM:(2 * g + 2) * HEAD_DIM]
            qg = q_ref[0, g * rep:(g + 1) * rep].reshape(rep * qb, HEAD_DIM)
            s = _nt(qg, kg)
            s_t = [(s[:, ct * LANES:(ct + 1) * LANES].reshape(rep, qb, LANES) + bias[ct][None]
                    ).reshape(rep * qb, LANES) for ct in range(tiles)]
            mx = s_t[0]
            for ct in range(1, tiles):
                mx = jnp.maximum(mx, s_t[ct])
            m_old = m_sc[g]
            m_new = jnp.maximum(m_old, jnp.max(mx, axis=1, keepdims=True))
            p = jnp.concatenate([jnp.exp2(st - m_new) for st in s_t], axis=1).astype(BF16)
            alpha = jnp.exp2(m_old - m_new)
            pv = jnp.dot(p, vg, preferred_element_type=F32)
            acc_sc[g] = acc_sc[g] * jnp.concatenate([alpha, alpha], axis=1) + pv
            m_sc[g] = m_new
        return carry

    lax.fori_loop(0, nch, attend_chunk, 0)

    for g in range(ATT_KV_HEADS):
        for r in range(rep):
            a = acc_sc[g, r * qb:(r + 1) * qb, :]
            h = g * rep + r
            o_ref[:, h * HEAD_DIM:(h + 1) * HEAD_DIM] = (a[:, :HEAD_DIM] / a[:, HEAD_DIM:]).astype(BF16)


def _dsa_attention(q_r, k_r, v_ext, iq_r, ik_lo, ik_hi, iw, batch, seq):
    n = batch * seq
    qb = q_r.shape[2]
    nqb = seq // qb
    topk = min(TOPK_MAX, seq // 4)
    rep = ATT_HEADS // ATT_KV_HEADS
    kern = functools.partial(_dsa_kernel, topk=topk, seq=seq)

    def per_batch(width):
        return pl.BlockSpec((None, seq, width), lambda b, i: (b, 0, 0))

    return pl.pallas_call(
        kern, grid=(batch, nqb),
        in_specs=[pl.BlockSpec((1, ATT_HEADS, qb, HEAD_DIM), lambda b, i: (b * nqb + i, 0, 0, 0)),
                  per_batch(ATT_KV), per_batch(2 * ATT_KV),
                  pl.BlockSpec((1, IDX_HEADS // 2, qb, LANES), lambda b, i: (b * nqb + i, 0, 0, 0)),
                  per_batch(LANES), per_batch(LANES),
                  pl.BlockSpec((qb, LANES), lambda b, i: (b * nqb + i, 0))],
        out_specs=pl.BlockSpec((qb, ATT_Q), lambda b, i: (b * nqb + i, 0)),
        out_shape=jax.ShapeDtypeStruct((n, ATT_Q), BF16),
        scratch_shapes=[pltpu.VMEM((seq // KEY_CHUNK, qb, KEY_CHUNK), I32),
                        pltpu.VMEM((ATT_KV_HEADS, rep * qb, 2 * HEAD_DIM), F32),
                        pltpu.VMEM((ATT_KV_HEADS, rep * qb, LANES), F32),
                        pltpu.VMEM((IDX_HEADS, qb, LANES), F32),
                        pltpu.VMEM((2, qb, LANES), I32)],
        compiler_params=_cparams(2), name="dsa_attention",
    )(q_r, k_r.reshape(batch, seq, ATT_KV), v_ext.reshape(batch, seq, 2 * ATT_KV), iq_r,
      ik_lo.reshape(batch, seq, LANES), ik_hi.reshape(batch, seq, LANES), iw)


GDN_PREP_HEADS = 8
CONV_HALO = 8


def _split2(x):
    hi = x.astype(BF16)
    return hi, (x - hi.astype(F32)).astype(BF16)


def _split3(x):
    hi = x.astype(BF16)
    r = x - hi.astype(F32)
    mid = r.astype(BF16)
    return hi, mid, (r - mid.astype(F32)).astype(BF16)


def _lhs3(hi, lo):
    return jnp.concatenate([hi, lo, hi], axis=1)


def _rhs3(hi, lo):
    return jnp.concatenate([hi, hi, lo], axis=0)


def _gdn_prep_kernel(q_ref, k_ref, v_ref, qp_ref, kp_ref, vp_ref, sm_ref, gat_ref,
                     cwq_ref, cwk_ref, cwv_ref, alog_row_ref, dtb_row_ref, alog_col_ref, dtb_col_ref,
                     u_ref, w_ref, qd_ref, kd_ref, a_ref, e_ref, xx_sc, gct_sc, *, nc):
    hb = GDN_PREP_HEADS
    c = CHUNK
    hg = pl.program_id(1)
    chunk = pl.program_id(0) % nc
    heads = range(hb)

    def conv_silu(slot, cur_ref, prev_ref, cw_ref):
        prev = prev_ref[...]
        at_start = jnp.full(prev.shape, chunk, I32) == 0
        xx_sc[slot, 0:CONV_HALO, :] = jnp.where(at_start, 0.0, prev)
        xx_sc[slot, CONV_HALO:CONV_HALO + c, :] = cur_ref[...]
        cw = cw_ref[...]
        start = CONV_HALO - (CONV_WIDTH - 1)
        acc = xx_sc[slot, pl.ds(start, c), :] * cw[0:1, :]
        for j in range(1, CONV_WIDTH):
            acc = acc + xx_sc[slot, pl.ds(start + j, c), :] * cw[j:j + 1, :]
        return acc * _sigmoid(acc)

    qa = conv_silu(0, q_ref, qp_ref, cwq_ref)
    ka = conv_silu(1, k_ref, kp_ref, cwk_ref)
    va = conv_silu(2, v_ref, vp_ref, cwv_ref)

    rowi = lax.broadcasted_iota(I32, (c, c), 0)
    coli = lax.broadcasted_iota(I32, (c, c), 1)
    lower = rowi >= coli
    strict = rowi > coli
    eye = jnp.where(rowi == coli, 1.0, 0.0)

    sm = sm_ref[...]
    beta_slab = _sigmoid(sm)
    g_slab = -jnp.exp(alog_row_ref[...]) * jax.nn.softplus(sm + dtb_row_ref[...])
    ones_l = jnp.where(lower, 1.0, 0.0).astype(BF16)
    ones_u = jnp.where(rowi <= coli, 1.0, 0.0).astype(BF16)
    gc_slab = jnp.dot(jnp.concatenate([ones_l] * 3, axis=1), jnp.concatenate(_split3(g_slab), axis=0),
                      preferred_element_type=F32)
    g_t = -jnp.exp(alog_col_ref[:, :c]) * jax.nn.softplus(gat_ref[...] + dtb_col_ref[:, :c])
    gct_sc[...] = jnp.dot(jnp.concatenate(_split3(g_t), axis=1), jnp.concatenate([ones_u] * 3, axis=0),
                          preferred_element_type=F32)
    lane = lax.broadcasted_iota(I32, (c, LANES), 1)
    hsl = [slice(r * GDN_DK, (r + 1) * GDN_DK) for r in heads]

    gcc = [jnp.sum(jnp.where(lane == SM_DT + hg * hb + r, gc_slab, 0.0), axis=1, keepdims=True) for r in heads]
    bcol = [jnp.sum(jnp.where(lane == SM_BETA + hg * hb + r, beta_slab, 0.0), axis=1, keepdims=True)
            for r in heads]
    gcr = [gct_sc[pl.ds(hg * hb + r, 1), :] for r in heads]
    glast = [g[:, c - 1:c] for g in gcr]
    qn = [qa[:, s] * lax.rsqrt(jnp.sum(qa[:, s] * qa[:, s], axis=1, keepdims=True) + RMS_EPS) * GDN_DK ** -0.5
          for s in hsl]
    kn = [ka[:, s] * lax.rsqrt(jnp.sum(ka[:, s] * ka[:, s], axis=1, keepdims=True) + RMS_EPS) for s in hsl]
    eg = [jnp.exp(g) for g in gcc]
    decay = [jnp.where(lower, jnp.exp(gcc[r] - gcr[r]), 0.0) for r in heads]
    kb = [kn[r] * bcol[r] for r in heads]

    def nt3(a, b):
        (ah, al), (bh, bl) = _split2(a), _split2(b)
        return _nt(_lhs3(ah, al), jnp.concatenate([bh, bh, bl], axis=1))

    kq = [nt3(jnp.concatenate([kb[r], qn[r]], axis=0), kn[r]) for r in heads]
    pairs = range(hb // 2)
    r2 = lax.broadcasted_iota(I32, (2 * c, 2 * c), 0)
    c2 = lax.broadcasted_iota(I32, (2 * c, 2 * c), 1)
    same_head = (r2 < c) == (c2 < c)

    def diag2(m):
        return jnp.where(same_head, jnp.concatenate([m, m], axis=0), jnp.zeros((), m.dtype))

    def diag_rhs3(hi, lo):
        return _rhs3(diag2(hi), diag2(lo))

    y = [jnp.concatenate([jnp.where(strict, -(kq[2 * p + s][:c] * decay[2 * p + s]), 0.0) for s in range(2)],
                         axis=1) for p in pairs]
    eye2 = jnp.concatenate([eye, eye], axis=1)
    t_inv = [eye2 + y[p] for p in pairs]
    ys = [_split2(v) for v in y]
    y = [jnp.dot(_lhs3(*ys[p]), diag_rhs3(*ys[p]), preferred_element_type=F32) for p in pairs]
    n_fac = int(math.log2(c))
    for j in range(1, n_fac):
        ys = [_split2(v) for v in y]
        ts = [_split2(v) for v in t_inv]
        if j < n_fac - 1:
            prod = [jnp.dot(jnp.concatenate([_lhs3(*ys[p]), _lhs3(*ts[p])], axis=0), diag_rhs3(*ys[p]),
                            preferred_element_type=F32) for p in pairs]
            y = [q[:c] for q in prod]
            t_inv = [t_inv[p] + prod[p][c:] for p in pairs]
        else:
            t_inv = [t_inv[p] + jnp.dot(_lhs3(*ts[p]), diag_rhs3(*ys[p]), preferred_element_type=F32)
                     for p in pairs]
    ts = [_split2(v) for v in t_inv]
    rhs = [jnp.concatenate([jnp.concatenate([va[:, hsl[r]] * bcol[r], kb[r] * eg[r]], axis=1)
                            for r in (2 * p, 2 * p + 1)], axis=0) for p in pairs]
    uw2 = [jnp.dot(_lhs3(diag2(ts[p][0]), diag2(ts[p][1])), _rhs3(*_split2(rhs[p])),
                   preferred_element_type=F32) for p in pairs]
    uw = [uw2[r // 2][(r % 2) * c:(r % 2 + 1) * c] for r in heads]

    for r in heads:
        u_ref[:, hsl[r]] = uw[r][:, :GDN_DV]
        w_ref[:, hsl[r]] = uw[r][:, GDN_DV:].astype(BF16)
        qd_ref[:, hsl[r]] = (qn[r] * eg[r]).astype(BF16)
        kd_ref[:, hsl[r]] = (kn[r] * jnp.exp(glast[r] - gcc[r])).astype(BF16)
        a_ref[:, r * c:(r + 1) * c] = (kq[r][c:] * decay[r]).astype(BF16)
        e_ref[:, hsl[r]] = jnp.broadcast_to(jnp.exp(glast[r]), (CONV_HALO, GDN_DK))


def _gdn_scan_kernel(u_ref, w_ref, qd_ref, kd_ref, a_ref, e_ref, z_ref, nw_ref, o_ref, st_sc):
    c = CHUNK
    heads = range(GDN_HEADS)
    hsl = [slice(h * GDN_DK, (h + 1) * GDN_DK) for h in heads]

    @pl.when(pl.program_id(1) == 0)
    def _():
        st_sc[...] = jnp.zeros(st_sc.shape, F32)

    st = [st_sc[h] for h in heads]
    ws_qs = [jnp.dot(jnp.concatenate([w_ref[:, hsl[h]], qd_ref[:, hsl[h]]], axis=0), st[h].astype(BF16),
                     preferred_element_type=F32) for h in heads]
    v_new = [(u_ref[:, hsl[h]] - ws_qs[h][:c]).astype(BF16) for h in heads]
    o = [ws_qs[h][c:] + jnp.dot(a_ref[:, h * c:(h + 1) * c], v_new[h], preferred_element_type=F32)
         for h in heads]
    for h in heads:
        st_sc[h] = st[h] * e_ref[0:1, hsl[h]] + _tn(kd_ref[:, hsl[h]], v_new[h])
    for h in heads:
        z = z_ref[:, hsl[h]]
        on = o[h] * lax.rsqrt(jnp.mean(o[h] * o[h], axis=1, keepdims=True) + RMS_EPS)
        o_ref[:, hsl[h]] = (on * nw_ref[...] * (z * _sigmoid(z))).astype(BF16)


def _gated_deltanet(proj, conv_w, a_log, dt_bias, norm_w, batch, seq):
    n = batch * seq
    nc = seq // CHUNK
    hb = GDN_PREP_HEADS
    w = hb * GDN_DK
    ga_t = proj[:, OFF_SM + SM_DT:OFF_SM + SM_DT + GDN_HEADS].reshape(batch * nc, CHUNK, GDN_HEADS)
    ga_t = ga_t.transpose(0, 2, 1)
    lane_pad = lambda vec, off: jnp.zeros((1, LANES), F32).at[0, off:off + GDN_HEADS].set(vec.astype(F32))
    alog_row, dtb_row = lane_pad(a_log, SM_DT), lane_pad(dt_bias, SM_DT)
    alog_col = jnp.broadcast_to(a_log.astype(F32)[:, None], (GDN_HEADS, LANES))
    dtb_col = jnp.broadcast_to(dt_bias.astype(F32)[:, None], (GDN_HEADS, LANES))
    nw = norm_w.astype(F32).reshape(1, GDN_DV)
    cw = conv_w.astype(F32)

    def cur(off):
        return pl.BlockSpec((CHUNK, w), lambda i, h: (i, off // w + h))

    def prev(off):
        per = CHUNK // CONV_HALO
        return pl.BlockSpec((CONV_HALO, w), lambda i, h: (jnp.maximum(i * per - 1, 0), off // w + h))

    def cwspec(off):
        return pl.BlockSpec((CONV_WIDTH, w), lambda i, h: (0, off // w + h))

    row = pl.BlockSpec((1, LANES), lambda i, h: (0, 0))
    colv = pl.BlockSpec((GDN_HEADS, LANES), lambda i, h: (0, 0))
    head_tile = pl.BlockSpec((CHUNK, w), lambda i, h: (i, h))
    u, wm, qd, kd, a_in, egl = pl.pallas_call(
        functools.partial(_gdn_prep_kernel, nc=nc), grid=(batch * nc, GDN_HEADS // hb),
        in_specs=[cur(OFF_GQ), cur(OFF_GK), cur(OFF_GV), prev(OFF_GQ), prev(OFF_GK), prev(OFF_GV),
                  pl.BlockSpec((CHUNK, LANES), lambda i, h: (i, OFF_SM // LANES)),
                  pl.BlockSpec((None, GDN_HEADS, CHUNK), lambda i, h: (i, 0, 0)),
                  cwspec(0), cwspec(GDN_QK), cwspec(2 * GDN_QK), row, row, colv, colv],
        out_specs=[head_tile, head_tile, head_tile, head_tile,
                   pl.BlockSpec((CHUNK, hb * CHUNK), lambda i, h: (i, h)),
                   pl.BlockSpec((CONV_HALO, w), lambda i, h: (i, h))],
        out_shape=[jax.ShapeDtypeStruct((n, GDN_V), F32), jax.ShapeDtypeStruct((n, GDN_V), BF16),
                   jax.ShapeDtypeStruct((n, GDN_QK), BF16), jax.ShapeDtypeStruct((n, GDN_QK), BF16),
                   jax.ShapeDtypeStruct((n, GDN_HEADS * CHUNK), BF16),
                   jax.ShapeDtypeStruct((batch * nc * CONV_HALO, GDN_V), F32)],
        scratch_shapes=[pltpu.VMEM((3, CONV_HALO + CHUNK, w), F32),
                        pltpu.VMEM((GDN_HEADS, CHUNK), F32)],
        compiler_params=_cparams(2), name="gdn_prep",
    )(proj, proj, proj, proj, proj, proj, proj, ga_t, cw, cw, cw, alog_row, dtb_row, alog_col, dtb_col)

    def full(width):
        return pl.BlockSpec((CHUNK, width), lambda b, t: (b * nc + t, 0))

    return pl.pallas_call(
        _gdn_scan_kernel, grid=(batch, nc),
        in_specs=[full(GDN_V), full(GDN_V), full(GDN_QK), full(GDN_QK), full(GDN_HEADS * CHUNK),
                  pl.BlockSpec((CONV_HALO, GDN_V), lambda b, t: (b * nc + t, 0)),
                  pl.BlockSpec((CHUNK, GDN_V), lambda b, t: (b * nc + t, OFF_GZ // GDN_V)),
                  pl.BlockSpec((1, GDN_DV), lambda b, t: (0, 0))],
        out_specs=full(GDN_V),
        out_shape=jax.ShapeDtypeStruct((n, GDN_V), BF16),
        scratch_shapes=[pltpu.VMEM((GDN_HEADS, GDN_DK, GDN_DV), F32)],
        compiler_params=_cparams(2), name="gdn_scan",
    )(u, wm, qd, kd, a_in, egl, proj, nw)


def _merge_kernel(att_ref, gdn_ref, wa_ref, wg_ref, ga_ref, gb_ref, o_ref):
    a = jnp.dot(att_ref[...], wa_ref[...], preferred_element_type=F32)
    g = jnp.dot(gdn_ref[...], wg_ref[...], preferred_element_type=F32)
    o_ref[...] = (_sigmoid(ga_ref[...]) * a + _sigmoid(gb_ref[...]) * g).astype(BF16)


def _merge(att, gdn, w_o_attn, w_o_gdn, proj):
    n = att.shape[0]
    tm, tn = min(512, n), 1024
    return pl.pallas_call(
        _merge_kernel, grid=(n // tm, D_MODEL // tn),
        in_specs=[pl.BlockSpec((tm, ATT_Q), lambda i, j: (i, 0)),
                  pl.BlockSpec((tm, GDN_V), lambda i, j: (i, 0)),
                  pl.BlockSpec((ATT_Q, tn), lambda i, j: (0, j)),
                  pl.BlockSpec((GDN_V, tn), lambda i, j: (0, j)),
                  pl.BlockSpec((tm, tn), lambda i, j: (i, OFF_GTA // tn + j)),
                  pl.BlockSpec((tm, tn), lambda i, j: (i, OFF_GTB // tn + j))],
        out_specs=pl.BlockSpec((tm, tn), lambda i, j: (i, j)),
        out_shape=jax.ShapeDtypeStruct((n, D_MODEL), BF16),
        compiler_params=_cparams(2), name="mixer_merge",
    )(att, gdn, w_o_attn.astype(BF16), w_o_gdn.astype(BF16), proj, proj)


def _pack_halves(x):
    w = x.shape[1] // 2
    bits = lax.bitcast_convert_type(x.astype(BF16).astype(F32), U32)
    return (bits[:, :w] >> 16) | (bits[:, w:] & jnp.uint32(0xFFFF0000))


def _unpack_halves(u):
    lo = lax.bitcast_convert_type(u << 16, F32)
    hi = lax.bitcast_convert_type(u & jnp.uint32(0xFFFF0000), F32)
    return lo, hi


SLAB = 8


def _store_slabs(ref, lead, u):
    m = u.shape[0]
    for s in range(SLAB):
        ref[lead + (pl.ds(s, m, stride=SLAB), slice(None))] = u[:, s * LANES:(s + 1) * LANES]


def _load_slabs(ref, lead, start, m):
    return jnp.concatenate([ref[lead + (pl.ds(start + s, m, stride=SLAB), slice(None))] for s in range(SLAB)],
                           axis=1)


def _outproj_ln_kernel(m_ref, w_ref, x_ref, g_ref, b_ref, x1_ref, x1p_ref, *, alpha):
    y = jnp.dot(m_ref[...], w_ref[...], preferred_element_type=F32)
    x1 = _layer_norm(alpha * x_ref[...] + y, g_ref[...], b_ref[...])
    x1_ref[...] = x1
    _store_slabs(x1p_ref, (), _pack_halves(x1))


def _outproj_ln(merged, w_out, x2d, ln_g, ln_b, alpha):
    n, d = x2d.shape
    tm = min(256, n)
    row = pl.BlockSpec((1, d), lambda i: (0, 0))
    tile = pl.BlockSpec((tm, d), lambda i: (i, 0))
    return pl.pallas_call(
        functools.partial(_outproj_ln_kernel, alpha=alpha), grid=(n // tm,),
        in_specs=[tile, pl.BlockSpec((d, d), lambda i: (0, 0)), tile, row, row],
        out_specs=[tile, pl.BlockSpec((tm * SLAB, LANES), lambda i: (i, 0))],
        out_shape=[jax.ShapeDtypeStruct((n, d), F32), jax.ShapeDtypeStruct((n * SLAB, LANES), U32)],
        compiler_params=_cparams(1), name="outproj_ln",
    )(merged, w_out.astype(BF16), x2d, ln_g.astype(F32).reshape(1, d), ln_b.astype(F32).reshape(1, d))


def _first_max(x, iota, size):
    m = jnp.max(x, axis=0, keepdims=True)
    f = jnp.min(jnp.where(x == m, iota, float(size)), axis=0, keepdims=True)
    return m, f


def _router_kernel(x_ref, w_ref, b_ref, eidx_ref, gate_ref, rank_ref, cnt_ref, carry_sc):
    tm = x_ref.shape[0]
    per = N_EXPERTS // N_GROUPS
    neg = -jnp.inf

    @pl.when(pl.program_id(0) == 0)
    def _():
        carry_sc[...] = jnp.zeros(carry_sc.shape, F32)

    logits = _nt(w_ref[...], x_ref[...], precision=HI)
    s = _sigmoid(logits)
    choice = s + b_ref[:, 0:1]
    io_g = lax.broadcasted_iota(I32, (per, tm), 0).astype(F32)
    io_e = lax.broadcasted_iota(I32, (N_EXPERTS, tm), 0).astype(F32)

    gs = []
    for g in range(N_GROUPS):
        xg = choice[g * per:(g + 1) * per]
        m1, f1 = _first_max(xg, io_g, per)
        m2 = jnp.max(jnp.where(io_g == f1, neg, xg), axis=0, keepdims=True)
        gs.append(m1 + m2)
    gw = jnp.concatenate(gs, axis=0)
    gsel = jnp.zeros((N_GROUPS, tm), F32)
    for _ in range(TOPK_GROUPS):
        _, f = _first_max(gw, io_g, N_GROUPS)
        hit = io_g == f
        gsel = jnp.where(hit, 1.0, gsel)
        gw = jnp.where(hit, neg, gw)
    x = jnp.concatenate([jnp.where(gsel[g:g + 1] > 0.0, choice[g * per:(g + 1) * per], neg)
                         for g in range(N_GROUPS)], axis=0)

    mem = jnp.zeros((N_EXPERTS, tm), F32)
    picks, wsel = [], []
    for _ in range(TOP_K):
        _, f = _first_max(x, io_e, N_EXPERTS)
        hit = io_e == f
        picks.append(f)
        wsel.append(jnp.sum(jnp.where(hit, s, 0.0), axis=0, keepdims=True))
        mem = jnp.where(hit, 1.0, mem)
        x = jnp.where(hit, neg, x)
    wsel = jnp.concatenate(wsel, axis=0)
    gate_ref[...] = wsel / jnp.sum(wsel, axis=0, keepdims=True) * ROUTED_SCALE
    eidx_ref[...] = jnp.concatenate(picks, axis=0).astype(I32)

    tr = lax.broadcasted_iota(I32, (tm, tm), 0)
    tc = lax.broadcasted_iota(I32, (tm, tm), 1)
    before = jnp.where(tr < tc, 1.0, 0.0).astype(BF16)
    prefix = jnp.dot(mem.astype(BF16), before, preferred_element_type=F32) + carry_sc[:, 0:1]
    ranks = [jnp.sum(jnp.where(io_e == f, prefix, 0.0), axis=0, keepdims=True) for f in picks]
    rank_ref[...] = jnp.concatenate(ranks, axis=0).astype(I32)
    carry_sc[...] = carry_sc[...] + jnp.sum(mem, axis=1, keepdims=True)
    cnt_ref[...] = carry_sc[...]


def _router(x1, w_router, router_bias):
    n, d = x1.shape
    tm = min(512, n)
    out = pl.BlockSpec((TOP_K, tm), lambda i: (0, i))
    bias = jnp.broadcast_to(router_bias.astype(F32)[:, None], (N_EXPERTS, LANES))
    return pl.pallas_call(
        _router_kernel, grid=(n // tm,),
        in_specs=[pl.BlockSpec((tm, d), lambda i: (i, 0)),
                  pl.BlockSpec((N_EXPERTS, d), lambda i: (0, 0)),
                  pl.BlockSpec((N_EXPERTS, LANES), lambda i: (0, 0))],
        out_specs=[out, out, out, pl.BlockSpec((N_EXPERTS, LANES), lambda i: (0, 0))],
        out_shape=[jax.ShapeDtypeStruct((TOP_K, n), I32), jax.ShapeDtypeStruct((TOP_K, n), F32),
                   jax.ShapeDtypeStruct((TOP_K, n), I32), jax.ShapeDtypeStruct((N_EXPERTS, LANES), F32)],
        scratch_shapes=[pltpu.VMEM((N_EXPERTS, LANES), F32)],
        compiler_params=_cparams(1), name="router",
    )(x1, w_router.astype(F32).T, bias)


SCATTER_TOKENS = 512
COMBINE_TOKENS = 128
ROW_COPY_UNROLL = 8


def _pos_kernel(pstart_ref, eidx_ref, rank_ref, pos_ref):
    e = eidx_ref[...]
    acc = rank_ref[...]
    for x in range(N_EXPERTS):
        acc = acc + jnp.where(e == x, pstart_ref[x], 0)
    pos_ref[...] = acc


def _positions(pstart, eidx, rank):
    n = eidx.shape[1]
    tm = min(2048, n)
    tile = pl.BlockSpec((TOP_K, tm), lambda i, ps: (0, i))
    return pl.pallas_call(
        _pos_kernel,
        grid_spec=pltpu.PrefetchScalarGridSpec(num_scalar_prefetch=1, grid=(n // tm,),
                                               in_specs=[tile, tile], out_specs=tile),
        out_shape=jax.ShapeDtypeStruct((TOP_K, n), I32),
        compiler_params=_cparams(1), name="moe_positions",
    )(pstart, eidx, rank)


def _scatter_kernel(pend_ref, padded_ref, pos_ref, x_ref, xs_ref, zero_sc, sem, zsem):
    tm = x_ref.shape[0] // SLAB
    bm = EXPERT_ROWS

    @pl.when(pl.program_id(0) == 0)
    def _():
        zero_sc[...] = jnp.zeros(zero_sc.shape, U32)

        def tail_copy(e):
            tail = pl.multiple_of((pend_ref[e] - bm) * SLAB, bm * SLAB)
            return pltpu.make_async_copy(zero_sc, xs_ref.at[pl.ds(tail, bm * SLAB), :], zsem)

        def zstart(e, carry):
            @pl.when(padded_ref[e] > 0)
            def _():
                tail_copy(e).start()
            return carry

        def zwait(e, carry):
            @pl.when(padded_ref[e] > 0)
            def _():
                tail_copy(e).wait()
            return carry

        lax.fori_loop(0, N_EXPERTS, zstart, 0)
        lax.fori_loop(0, N_EXPERTS, zwait, 0)

    def row_copy(t, k):
        src = x_ref.at[pl.ds(pl.multiple_of(t * SLAB, SLAB), SLAB), :]
        dst = xs_ref.at[pl.ds(pl.multiple_of(pos_ref[k, t] * SLAB, SLAB), SLAB), :]
        return pltpu.make_async_copy(src, dst, sem)

    def start(t, carry):
        for k in range(TOP_K):
            row_copy(t, k).start(priority=k % 2)
        return carry

    lax.fori_loop(0, tm, start, 0, unroll=ROW_COPY_UNROLL)
    rows = xs_ref.at[pl.ds(0, tm * TOP_K * SLAB), :]
    pltpu.make_async_copy(rows, rows, sem).wait()


def _scatter_rows(x1p, pos, pend, padded, n_rows):
    n = x1p.shape[0] // SLAB
    tm = min(SCATTER_TOKENS, n)
    grid_spec = pltpu.PrefetchScalarGridSpec(
        num_scalar_prefetch=2, grid=(n // tm,),
        in_specs=[pl.BlockSpec((TOP_K, tm), lambda i, pe, pd: (0, i), memory_space=pltpu.SMEM),
                  pl.BlockSpec((tm * SLAB, LANES), lambda i, pe, pd: (i, 0))],
        out_specs=pl.BlockSpec(memory_space=pl.ANY),
        scratch_shapes=[pltpu.VMEM((EXPERT_ROWS * SLAB, LANES), U32), pltpu.SemaphoreType.DMA,
                        pltpu.SemaphoreType.DMA])
    return pl.pallas_call(
        _scatter_kernel, grid_spec=grid_spec,
        out_shape=jax.ShapeDtypeStruct((n_rows * SLAB, LANES), U32),
        compiler_params=_cparams(1), name="moe_scatter",
    )(pend, padded, pos, x1p)


def _expert_kernel(blk_e_ref, blk_on_ref, blk_new_ref, x_ref, w1_ref, w3_ref, w2_ref, y_ref,
                   w1_sc, w3_sc, w2_sc):
    del blk_e_ref
    i = pl.program_id(0)

    @pl.when(blk_new_ref[i] > 0)
    def _():
        w1_sc[...] = w1_ref[...].astype(BF16)
        w3_sc[...] = w3_ref[...].astype(BF16)
        w2_sc[...] = w2_ref[...].astype(BF16)

    @pl.when(blk_on_ref[i] > 0)
    def _():
        half = D_MODEL // 2
        lo, hi = _unpack_halves(_load_slabs(x_ref, (), 0, EXPERT_ROWS))
        xa, xb = lo.astype(BF16), hi.astype(BF16)

        def up(w_sc):
            return (jnp.dot(xa, w_sc[0:half, :], preferred_element_type=F32)
                    + jnp.dot(xb, w_sc[half:, :], preferred_element_type=F32))

        h1, h3 = up(w1_sc), up(w3_sc)
        act = (h1 * _sigmoid(h1) * h3).astype(BF16)
        _store_slabs(y_ref, (), _pack_halves(jnp.dot(act, w2_sc[...], preferred_element_type=F32)))


def _experts(xs, blk_e, blk_on, blk_new, w1, w3, w2):
    n_rows, w = xs.shape
    bm = EXPERT_ROWS * SLAB
    grid_spec = pltpu.PrefetchScalarGridSpec(
        num_scalar_prefetch=3, grid=(n_rows // bm,),
        in_specs=[pl.BlockSpec((bm, w), lambda i, e, on, nw: (i, 0)),
                  pl.BlockSpec((None, D_MODEL, EXPERT_DIM), lambda i, e, on, nw: (e[i], 0, 0)),
                  pl.BlockSpec((None, D_MODEL, EXPERT_DIM), lambda i, e, on, nw: (e[i], 0, 0)),
                  pl.BlockSpec((None, EXPERT_DIM, D_MODEL), lambda i, e, on, nw: (e[i], 0, 0))],
        out_specs=pl.BlockSpec((bm, w), lambda i, e, on, nw: (i, 0)),
        scratch_shapes=[pltpu.VMEM((D_MODEL, EXPERT_DIM), BF16), pltpu.VMEM((D_MODEL, EXPERT_DIM), BF16),
                        pltpu.VMEM((EXPERT_DIM, D_MODEL), BF16)])
    return pl.pallas_call(
        _expert_kernel, grid_spec=grid_spec,
        out_shape=jax.ShapeDtypeStruct((n_rows, w), U32),
        compiler_params=_cparams(1), name="moe_experts",
    )(blk_e, blk_on, blk_new, xs, w1, w3, w2)


def _shared_up_kernel(x_ref, w_ref, h_ref):
    h = jnp.dot(x_ref[...].astype(BF16), w_ref[...], preferred_element_type=F32)
    h1, h3 = h[:, :SHARED_DIM], h[:, SHARED_DIM:]
    h_ref[...] = (h1 * _sigmoid(h1) * h3).astype(BF16)


def _shared_up(x1, ws1, ws3):
    n, d = x1.shape
    tm = min(512, n)
    w13 = jnp.concatenate([ws1, ws3], axis=1).astype(BF16)
    return pl.pallas_call(
        _shared_up_kernel, grid=(n // tm,),
        in_specs=[pl.BlockSpec((tm, d), lambda i: (i, 0)),
                  pl.BlockSpec((d, 2 * SHARED_DIM), lambda i: (0, 0))],
        out_specs=pl.BlockSpec((tm, SHARED_DIM), lambda i: (i, 0)),
        out_shape=jax.ShapeDtypeStruct((n, SHARED_DIM), BF16),
        compiler_params=_cparams(1), name="shared_up",
    )(x1, w13)


def _combine_kernel(pos_ref, pos_next_ref, x1_ref, h_ref, ws2_ref, gate_ref, g_ref, b_ref, y_hbm, o_ref,
                    ybuf, sem, *, alpha):
    tm = x1_ref.shape[0]
    i = pl.program_id(0)
    slot = i % 2

    def issue(p_ref, s):
        def start(t, carry):
            for k in range(TOP_K):
                src = y_hbm.at[pl.ds(pl.multiple_of(p_ref[k, t] * SLAB, SLAB), SLAB), :]
                dst = ybuf.at[s, pl.ds(pl.multiple_of((k * tm + t) * SLAB, SLAB), SLAB), :]
                pltpu.make_async_copy(src, dst, sem.at[s]).start(priority=k % 2)
            return carry

        lax.fori_loop(0, tm, start, 0, unroll=ROW_COPY_UNROLL)

    @pl.when(i == 0)
    def _():
        issue(pos_ref, 0)

    @pl.when(i + 1 < pl.num_programs(0))
    def _():
        issue(pos_next_ref, 1 - slot)

    base = alpha * x1_ref[...] + jnp.dot(h_ref[...], ws2_ref[...], preferred_element_type=F32)
    pltpu.make_async_copy(y_hbm.at[pl.ds(0, tm * TOP_K * SLAB), :], ybuf.at[slot], sem.at[slot]).wait()

    gates = gate_ref[...]
    acc_lo = jnp.zeros((tm, D_MODEL // 2), F32)
    acc_hi = jnp.zeros((tm, D_MODEL // 2), F32)
    for k in range(TOP_K):
        lo, hi = _unpack_halves(_load_slabs(ybuf, (slot,), k * tm * SLAB, tm))
        gk = gates[:, k:k + 1]
        acc_lo = acc_lo + gk * lo
        acc_hi = acc_hi + gk * hi
    f = jnp.concatenate([acc_lo, acc_hi], axis=1)
    o_ref[...] = _layer_norm(base + f, g_ref[...], b_ref[...])


def _combine(x1, hsh, ws2, pos, gates_t, ys, ln_g, ln_b, alpha):
    n, d = x1.shape
    tm = min(COMBINE_TOKENS, n)
    row = pl.BlockSpec((1, d), lambda i: (0, 0))
    last = n // tm - 1
    return pl.pallas_call(
        functools.partial(_combine_kernel, alpha=alpha), grid=(n // tm,),
        in_specs=[pl.BlockSpec((TOP_K, tm), lambda i: (0, i), memory_space=pltpu.SMEM),
                  pl.BlockSpec((TOP_K, tm), lambda i: (0, jnp.minimum(i + 1, last)), memory_space=pltpu.SMEM),
                  pl.BlockSpec((tm, d), lambda i: (i, 0)),
                  pl.BlockSpec((tm, SHARED_DIM), lambda i: (i, 0)),
                  pl.BlockSpec((SHARED_DIM, d), lambda i: (0, 0)),
                  pl.BlockSpec((tm, TOP_K), lambda i: (i, 0)),
                  row, row,
                  pl.BlockSpec(memory_space=pl.ANY)],
        out_specs=pl.BlockSpec((tm, d), lambda i: (i, 0)),
        out_shape=jax.ShapeDtypeStruct((n, d), F32),
        scratch_shapes=[pltpu.VMEM((2, TOP_K * tm * SLAB, LANES), U32), pltpu.SemaphoreType.DMA((2,))],
        compiler_params=_cparams(1), name="moe_combine",
    )(pos, pos, x1, hsh, ws2.astype(BF16), gates_t, ln_g.astype(F32).reshape(1, d),
      ln_b.astype(F32).reshape(1, d), ys)


def _moe(x1, x1p, w_router, router_bias, w1, w3, w2, ws1, ws3, ws2, ln_g, ln_b, alpha):
    n = x1.shape[0]
    bm = EXPERT_ROWS
    eidx, gates, rank, cnt = _router(x1, w_router, router_bias)
    counts = cnt[:, 0].astype(I32)
    padded = (counts + bm - 1) // bm * bm
    pend = jnp.cumsum(padded)
    pstart = pend - padded
    pos = _positions(pstart, eidx, rank)
    n_rows = -(-(n * TOP_K) // bm) * bm + N_EXPERTS * bm
    blk_start = jnp.arange(n_rows // bm, dtype=I32) * bm
    blk_on = (blk_start < pend[-1]).astype(I32)
    last = jnp.maximum(pend[-1] - bm, 0)
    blk_e = jnp.sum((pend[None, :] <= jnp.minimum(blk_start, last)[:, None]).astype(I32), axis=1)
    blk_e = jnp.minimum(blk_e, N_EXPERTS - 1)
    changed = jnp.concatenate([jnp.ones((1,), I32), (blk_e[1:] != blk_e[:-1]).astype(I32)])
    blk_new = blk_on * changed
    xs = _scatter_rows(x1p, pos, pend, padded, n_rows)
    ys = _experts(xs, blk_e, blk_on, blk_new, w1, w3, w2)
    hsh = _shared_up(x1, ws1, ws3)
    return _combine(x1, hsh, ws2, pos, gates.T, ys, ln_g, ln_b, alpha)


def kernel(x, positions, w_in, conv_w, A_log, dt_bias, gdn_norm_w, idx_ln_g, idx_ln_b, w_o_attn, w_o_gdn, w_out, ln1_g, ln1_b, w_router, router_bias, w1, w3, w2, ws1, ws3, ws2, ln2_g, ln2_b):
    batch, seq, d = x.shape
    depth = w_in.shape[0]
    alpha = (2 * depth) ** 0.25
    n = batch * seq
    xf = x.reshape(n, d)
    tabs = _trig_tables(positions)
    for l in range(depth):
        proj = _project(xf, _pack_w_in(w_in[l]))
        q_r, k_r, v_ext, iq_r, ik_lo, ik_hi, iw = _att_prep(proj, tabs, idx_ln_g[l], idx_ln_b[l])
        att = _dsa_attention(q_r, k_r, v_ext, iq_r, ik_lo, ik_hi, iw, batch, seq)
        gdn = _gated_deltanet(proj, conv_w[l], A_log[l], dt_bias[l], gdn_norm_w[l], batch, seq)
        merged = _merge(att, gdn, w_o_attn[l], w_o_gdn[l], proj)
        x1, x1p = _outproj_ln(merged, w_out[l], xf, ln1_g[l], ln1_b[l], alpha)
        xf = _moe(x1, x1p, w_router[l], router_bias[l], w1[l], w3[l], w2[l], ws1[l], ws3[l], ws2[l],
                  ln2_g[l], ln2_b[l], alpha)
    return xf.reshape(batch, seq, d)
```

```python
import functools
import math

import jax
import jax.numpy as jnp
import numpy as np
from jax import lax
from jax.experimental import pallas as pl
from jax.experimental.pallas import tpu as pltpu

F32 = jnp.float32
BF16 = jnp.bfloat16
I32 = jnp.int32
U32 = jnp.uint32
HI = lax.Precision.HIGHEST

D_MODEL = 2048
ATT_HEADS = 16
ATT_KV_HEADS = 2
HEAD_DIM = 128
IDX_HEADS = 8
IDX_DIM = 64
TOPK_MAX = 256
Q_BLOCK = 128
ROPE_THETA = 10000.0
GDN_HEADS = 16
GDN_DK = 128
GDN_DV = 128
CONV_WIDTH = 4
CHUNK = 64
N_EXPERTS = 64
TOP_K = 8
N_GROUPS = 8
TOPK_GROUPS = 4
EXPERT_DIM = 512
SHARED_DIM = 512
ROUTED_SCALE = 2.5
LN_EPS = 1e-5
RMS_EPS = 1e-6

ATT_Q = ATT_HEADS * HEAD_DIM
ATT_KV = ATT_KV_HEADS * HEAD_DIM
IDX_Q = IDX_HEADS * IDX_DIM
GDN_QK = GDN_HEADS * GDN_DK
GDN_V = GDN_HEADS * GDN_DV
SPLITS = (ATT_Q, ATT_KV, ATT_KV, IDX_Q, IDX_DIM, IDX_HEADS, GDN_QK, GDN_QK, GDN_V, GDN_V,
          GDN_HEADS, GDN_HEADS, D_MODEL, D_MODEL)
SPLIT_OFFSETS = tuple(int(o) for o in np.cumsum(SPLITS)[:-1])

LANES = 128
VMEM_LIMIT_BYTES = 56 * 1024 * 1024

OFF_AQ = 0
OFF_GQ = OFF_AQ + ATT_Q
OFF_GK = OFF_GQ + GDN_QK
OFF_GV = OFF_GK + GDN_QK
OFF_GZ = OFF_GV + GDN_V
OFF_GTA = OFF_GZ + GDN_V
OFF_GTB = OFF_GTA + D_MODEL
OFF_AK = OFF_GTB + D_MODEL
OFF_AV = OFF_AK + ATT_KV
OFF_IQ = OFF_AV + ATT_KV
OFF_SM = OFF_IQ + IDX_Q
PROJ_W = 16384
SM_IK = 0
SM_IW = SM_IK + IDX_DIM
SM_BETA = SM_IW + IDX_HEADS
SM_DT = SM_BETA + GDN_HEADS

KEY_CHUNK = 256
ATT_ROWS = 256
EXPERT_ROWS = 512
INT_MIN = -2147483648
MASK_BIAS = -1e30
M_INIT = -1e29


def _cparams(n_axes):
    return pltpu.CompilerParams(dimension_semantics=("arbitrary",) * n_axes,
                                vmem_limit_bytes=VMEM_LIMIT_BYTES)


def _nt(a, b, precision=None):
    return lax.dot_general(a, b, (((1,), (1,)), ((), ())), precision=precision,
                           preferred_element_type=F32)


def _tn(a, b, precision=None):
    return lax.dot_general(a, b, (((0,), (0,)), ((), ())), precision=precision,
                           preferred_element_type=F32)


def _sigmoid(x):
    return jax.nn.sigmoid(x)


def _layer_norm(x, g, b):
    mu = jnp.mean(x, axis=-1, keepdims=True)
    xc = x - mu
    var = jnp.mean(xc * xc, axis=-1, keepdims=True)
    return xc * lax.rsqrt(var + LN_EPS) * g + b


def _trig_kernel(pos_ref, inv_a_ref, inv_i_ref, sgn_a_ref, sgn_i_ref, cos_a, sin_a, cos_i, sin_i):
    p = pos_ref[...]
    ang_a = p * inv_a_ref[...]
    cos_a[...] = jnp.cos(ang_a)
    sin_a[...] = jnp.sin(ang_a) * sgn_a_ref[...]
    ang_i = p * inv_i_ref[...]
    cos_i[...] = jnp.cos(ang_i)
    sin_i[...] = jnp.sin(ang_i) * sgn_i_ref[...]


def _trig_tables(positions):
    n = positions.size
    tm = min(512, n)
    pos = positions.reshape(n, 1).astype(F32)
    inv_a = ROPE_THETA ** (-jnp.arange(0, HEAD_DIM, 2, dtype=F32) / HEAD_DIM)
    inv_i = ROPE_THETA ** (-jnp.arange(0, IDX_DIM, 2, dtype=F32) / IDX_DIM)
    half_a, half_i = HEAD_DIM // 2, IDX_DIM // 2
    inv_a_row = jnp.tile(inv_a, 2).reshape(1, LANES)
    inv_i_row = jnp.tile(inv_i, 4).reshape(1, LANES)
    lane = np.arange(LANES)
    sgn_a = jnp.asarray(np.where(lane < half_a, -1.0, 1.0), F32).reshape(1, LANES)
    sgn_i = jnp.asarray(np.where(lane % IDX_DIM < half_i, -1.0, 1.0), F32).reshape(1, LANES)
    row = pl.BlockSpec((1, LANES), lambda i: (0, 0))
    tab = pl.BlockSpec((tm, LANES), lambda i: (i, 0))
    return pl.pallas_call(
        _trig_kernel, grid=(n // tm,),
        in_specs=[pl.BlockSpec((tm, 1), lambda i: (i, 0)), row, row, row, row],
        out_specs=[tab, tab, tab, tab],
        out_shape=[jax.ShapeDtypeStruct((n, LANES), F32)] * 4,
        compiler_params=_cparams(1), name="trig_tables",
    )(pos, inv_a_row, inv_i_row, sgn_a, sgn_i)


def _pack_plan():
    src_off = dict(zip(("aq", "ak", "av", "iq", "ik", "iw", "gq", "gk", "gv", "gz", "gb", "ga", "gta", "gtb"),
                       (0,) + SPLIT_OFFSETS))
    order = (("aq", OFF_AQ, ATT_Q), ("gq", OFF_GQ, GDN_QK), ("gk", OFF_GK, GDN_QK), ("gv", OFF_GV, GDN_V),
             ("gz", OFF_GZ, GDN_V), ("gta", OFF_GTA, D_MODEL), ("gtb", OFF_GTB, D_MODEL), ("ak", OFF_AK, ATT_KV),
             ("av", OFF_AV, ATT_KV), ("iq", OFF_IQ, IDX_Q), ("ik", OFF_SM + SM_IK, IDX_DIM),
             ("iw", OFF_SM + SM_IW, IDX_HEADS), ("gb", OFF_SM + SM_BETA, GDN_HEADS), ("ga", OFF_SM + SM_DT, GDN_HEADS))
    src_col = np.full((PROJ_W,), -1, np.int64)
    for name, dst, width in order:
        src_col[dst:dst + width] = src_off[name] + np.arange(width)
    plan, shifts = [], set()
    for j in range(PROJ_W // LANES):
        cols = src_col[j * LANES:(j + 1) * LANES]
        if j == OFF_SM // LANES:
            a, b = src_off["ik"], src_off["gb"] - SM_BETA
            assert a % LANES == 0 and b % LANES == 0
            plan.append((3, a // LANES, b // LANES, 0))
        elif (cols < 0).all():
            plan.append((2, 0, 0, 0))
        else:
            assert (np.diff(cols) == 1).all()
            shift = int(cols[0] % LANES)
            if shift == 0:
                plan.append((0, cols[0] // LANES, cols[0] // LANES, 0))
            else:
                shifts.add(shift)
                plan.append((1, cols[0] // LANES, cols[0] // LANES + 1, shift))
    return np.asarray(plan, np.int32), tuple(sorted(shifts))


def _pack_kernel(mode_ref, ta_ref, tb_ref, shift_ref, a_ref, b_ref, o_ref, *, shifts):
    del ta_ref, tb_ref
    mode = mode_ref[pl.program_id(0)]
    lane = lax.broadcasted_iota(I32, a_ref.shape, 1)

    @pl.when(mode == 0)
    def _():
        o_ref[...] = a_ref[...].astype(BF16)

    for shift in shifts:
        @pl.when((mode == 1) & (shift_ref[pl.program_id(0)] == shift))
        def _(shift=shift):
            o_ref[...] = jnp.concatenate([a_ref[:, shift:], b_ref[:, :shift]], axis=1).astype(BF16)

    @pl.when(mode == 2)
    def _():
        o_ref[...] = jnp.zeros(o_ref.shape, BF16)

    @pl.when(mode == 3)
    def _():
        small = jnp.where(lane < SM_BETA, a_ref[...], jnp.where(lane < SM_DT + GDN_HEADS, b_ref[...], 0.0))
        o_ref[...] = small.astype(BF16)


def _pack_w_in(w_in, layer):
    _, d, width = w_in.shape
    plan, shifts = _pack_plan()
    last = (width - 1) // LANES
    grid_spec = pltpu.PrefetchScalarGridSpec(
        num_scalar_prefetch=4, grid=(PROJ_W // LANES,),
        in_specs=[pl.BlockSpec((None, d, LANES), lambda j, m, ta, tb, sh: (layer, 0, ta[j])),
                  pl.BlockSpec((None, d, LANES), lambda j, m, ta, tb, sh: (layer, 0, jnp.minimum(tb[j], last)))],
        out_specs=pl.BlockSpec((d, LANES), lambda j, m, ta, tb, sh: (0, j)))
    return pl.pallas_call(
        functools.partial(_pack_kernel, shifts=shifts), grid_spec=grid_spec,
        out_shape=jax.ShapeDtypeStruct((d, PROJ_W), BF16),
        compiler_params=_cparams(1), name="pack_w_in",
    )(*(jnp.asarray(plan[:, c]) for c in range(4)), w_in, w_in)


def _proj_kernel(x_ref, w_ref, o_ref, xb_ref):
    @pl.when(pl.program_id(1) == 0)
    def _():
        xb_ref[...] = x_ref[...].astype(BF16)

    o_ref[...] = jnp.dot(xb_ref[...], w_ref[...], preferred_element_type=F32)


def _project(x2d, w_packed):
    n, d = x2d.shape
    tm, tn = min(1024, n), 1024
    return pl.pallas_call(
        _proj_kernel, grid=(n // tm, PROJ_W // tn),
        in_specs=[pl.BlockSpec((tm, d), lambda i, j: (i, 0)),
                  pl.BlockSpec((d, tn), lambda i, j: (0, j))],
        out_specs=pl.BlockSpec((tm, tn), lambda i, j: (i, j)),
        out_shape=jax.ShapeDtypeStruct((n, PROJ_W), F32),
        scratch_shapes=[pltpu.VMEM((tm, d), BF16)],
        compiler_params=_cparams(2), name="in_proj",
    )(x2d, w_packed)


def _rope_head(x, cos, sin_signed):
    return x * cos + pltpu.roll(x, HEAD_DIM // 2, 1) * sin_signed


def _rope_idx(x, cos, sin_signed, first_half):
    half = IDX_DIM // 2
    partner = jnp.where(first_half, pltpu.roll(x, LANES - half, 1), pltpu.roll(x, half, 1))
    return x * cos + partner * sin_signed


def _att_prep_kernel(aq_ref, ak_ref, av_ref, iq_ref, sm_ref, cos_a_ref, sin_a_ref, cos_i_ref,
                     sin_i_ref, lng_ref, lnb_ref,
                     q_out, k_out, v_out, iq_out, iklo_out, ikhi_out, iw_out):
    cos_a, sin_a = cos_a_ref[...], sin_a_ref[...]
    cos_i, sin_i = cos_i_ref[...], sin_i_ref[...]
    rows = cos_a.shape[0]
    lane = lax.broadcasted_iota(I32, (rows, LANES), 1)
    first_half = (lane % IDX_DIM) < (IDX_DIM // 2)

    for h in range(ATT_HEADS):
        q_h = _rope_head(aq_ref[:, h * HEAD_DIM:(h + 1) * HEAD_DIM], cos_a, sin_a) * (
            HEAD_DIM ** -0.5 * math.log2(math.e))
        q_out[0, h] = q_h.astype(BF16)
    ones = jnp.ones((rows, HEAD_DIM), BF16)
    for g in range(ATT_KV_HEADS):
        sl = slice(g * HEAD_DIM, (g + 1) * HEAD_DIM)
        k_out[:, sl] = _rope_head(ak_ref[:, sl], cos_a, sin_a).astype(BF16)
        v_out[:, 2 * g * HEAD_DIM:(2 * g + 1) * HEAD_DIM] = av_ref[:, sl].astype(BF16)
        v_out[:, (2 * g + 1) * HEAD_DIM:(2 * g + 2) * HEAD_DIM] = ones
    for j in range(IDX_HEADS // 2):
        iq_out[0, j] = _rope_idx(iq_ref[:, j * LANES:(j + 1) * LANES], cos_i, sin_i, first_half).astype(BF16)

    sm = sm_ref[...]
    in_ik = lane < IDX_DIM
    mu = jnp.sum(jnp.where(in_ik, sm, 0.0), axis=1, keepdims=True) / IDX_DIM
    xc = sm - mu
    var = jnp.sum(jnp.where(in_ik, xc * xc, 0.0), axis=1, keepdims=True) / IDX_DIM
    ik = xc * lax.rsqrt(var + LN_EPS) * lng_ref[...] + lnb_ref[...]
    ik = jnp.where(in_ik, _rope_idx(ik, cos_i, sin_i, first_half), 0.0)
    iklo_out[...] = ik.astype(BF16)
    ikhi_out[...] = pltpu.roll(ik, IDX_DIM, 1).astype(BF16)
    iw_out[...] = (sm * IDX_HEADS ** -0.5) * IDX_DIM ** -0.5


def _att_prep(proj, tabs, idx_ln_g, idx_ln_b):
    n = proj.shape[0]
    tm = min(ATT_ROWS, n)
    cos_a, sin_a, cos_i, sin_i = tabs
    pad = jnp.zeros((LANES - IDX_DIM,), F32)
    lng = jnp.concatenate([idx_ln_g.astype(F32), pad]).reshape(1, LANES)
    lnb = jnp.concatenate([idx_ln_b.astype(F32), pad]).reshape(1, LANES)
    tab = pl.BlockSpec((tm, LANES), lambda i: (i, 0))
    row = pl.BlockSpec((1, LANES), lambda i: (0, 0))

    def col(width, off):
        return pl.BlockSpec((tm, width), lambda i: (i, off // width))

    return pl.pallas_call(
        _att_prep_kernel, grid=(n // tm,),
        in_specs=[col(ATT_Q, OFF_AQ), col(ATT_KV, OFF_AK), col(ATT_KV, OFF_AV), col(IDX_Q, OFF_IQ),
                  col(LANES, OFF_SM), tab, tab, tab, tab, row, row],
        out_specs=[pl.BlockSpec((1, ATT_HEADS, tm, HEAD_DIM), lambda i: (i, 0, 0, 0)),
                   pl.BlockSpec((tm, ATT_KV), lambda i: (i, 0)),
                   pl.BlockSpec((tm, 2 * ATT_KV), lambda i: (i, 0)),
                   pl.BlockSpec((1, IDX_HEADS // 2, tm, LANES), lambda i: (i, 0, 0, 0)),
                   tab, tab, tab],
        out_shape=[jax.ShapeDtypeStruct((n // tm, ATT_HEADS, tm, HEAD_DIM), BF16),
                   jax.ShapeDtypeStruct((n, ATT_KV), BF16),
                   jax.ShapeDtypeStruct((n, 2 * ATT_KV), BF16),
                   jax.ShapeDtypeStruct((n // tm, IDX_HEADS // 2, tm, LANES), BF16),
                   jax.ShapeDtypeStruct((n, LANES), BF16),
                   jax.ShapeDtypeStruct((n, LANES), BF16),
                   jax.ShapeDtypeStruct((n, LANES), F32)],
        compiler_params=_cparams(1), name="att_prep",
    )(proj, proj, proj, proj, proj, cos_a, sin_a, cos_i, sin_i, lng, lnb)


def _dsa_kernel(q_ref, k_ref, v_ref, iq_ref, iklo_ref, ikhi_ref, iw_ref, o_ref,
                keys_sc, acc_sc, m_sc, wrep_sc, sel_sc, *, topk, seq):
    qb, kc = q_ref.shape[2], KEY_CHUNK
    tiles = kc // LANES
    rep = ATT_HEADS // ATT_KV_HEADS
    i = pl.program_id(1)
    nch = (i * qb + qb + kc - 1) // kc
    row = lax.broadcasted_iota(I32, (qb, LANES), 0)
    lane = lax.broadcasted_iota(I32, (qb, LANES), 1)
    t_abs = i * qb + row

    iw = iw_ref[...]
    for h in range(IDX_HEADS):
        wrep_sc[h] = jnp.broadcast_to(iw[:, SM_IW + h:SM_IW + h + 1], (qb, LANES))
    qi = iq_ref[0].reshape((IDX_HEADS // 2) * qb, LANES)

    def score_chunk(c, carry):
        off = pl.multiple_of(c * kc, kc)
        s_lo = _nt(qi, iklo_ref[pl.ds(off, kc), :])
        s_hi = _nt(qi, ikhi_ref[pl.ds(off, kc), :])
        for ct in range(tiles):
            cs = slice(ct * LANES, (ct + 1) * LANES)
            acc = jnp.zeros((qb, LANES), F32)
            for j in range(IDX_HEADS // 2):
                rs = slice(j * qb, (j + 1) * qb)
                acc = acc + wrep_sc[2 * j] * jnp.maximum(s_lo[rs, cs], 0.0)
                acc = acc + wrep_sc[2 * j + 1] * jnp.maximum(s_hi[rs, cs], 0.0)
            bits = lax.bitcast_convert_type(acc, I32)
            key = bits ^ ((bits >> 31) & jnp.int32(0x7FFFFFFF))
            key = jnp.where(bits == jnp.int32(INT_MIN), 0, key)
            causal = (off + ct * LANES + lane) <= t_abs
            keys_sc[c, :, cs] = jnp.where(causal, key, jnp.int32(INT_MIN))
        return carry

    lax.fori_loop(0, nch, score_chunk, 0)

    kth = jnp.float32(topk)

    def count(pred):
        lane_g = lax.broadcasted_iota(I32, (Q_BLOCK, LANES), 1)
        parts = []
        for r0 in range(0, qb, Q_BLOCK):
            rs = slice(r0, r0 + Q_BLOCK)

            def body(c, cnt, rs=rs):
                off = c * kc
                for ct in range(tiles):
                    kt = keys_sc[c, rs, ct * LANES:(ct + 1) * LANES]
                    cnt = cnt + jnp.where(pred(kt, off + ct * LANES + lane_g, rs), 1.0, 0.0)
                return cnt

            parts.append(lax.fori_loop(0, nch, body, jnp.zeros((Q_BLOCK, LANES), F32)))
        return jnp.concatenate([jnp.broadcast_to(jnp.sum(cnt, axis=1, keepdims=True), (Q_BLOCK, LANES))
                                for cnt in parts], axis=0)

    zero = jnp.zeros((qb, LANES), I32)
    n_pos = count(lambda k, s, rs: k >= zero[rs])
    base = jnp.where(n_pos >= kth, zero, jnp.int32(INT_MIN))
    n_base = jnp.where(n_pos >= kth, n_pos, -1.0)
    short = t_abs < topk

    def unresolved(n_base):
        return jnp.max(jnp.where((n_base == kth) | short, 0.0, 1.0)) > 0.0

    def bit_cond(st):
        it, _, n_base = st
        return (it < 31) & unresolved(n_base)

    def bit_step(st):
        it, base, n_base = st
        cand = base | (jnp.int32(1) << (30 - it))
        tot = count(lambda k, s, rs: k >= cand[rs])
        take = tot >= kth
        return it + 1, jnp.where(take, cand, base), jnp.where(take, tot, n_base)

    _, thr, n_ge = lax.while_loop(bit_cond, bit_step, (jnp.int32(0), base, n_base))
    sel_sc[0] = thr
    sel_sc[1] = jnp.full((qb, LANES), seq, I32)

    @pl.when(jnp.max(n_ge) > kth)
    def _():
        nbits = max(1, (seq - 1).bit_length())
        need = kth - count(lambda k, s, rs: k > thr[rs])

        def idx_step(it, posv):
            cand = posv + (jnp.int32(1) << (nbits - 1 - it))
            below = count(lambda k, s, rs: (k == thr[rs]) & (s < cand[rs]))
            return jnp.where(below < need, cand, posv)

        sel_sc[1] = lax.fori_loop(0, nbits, idx_step, zero)

    thr = sel_sc[0]
    last_tie = sel_sc[1]

    m_sc[...] = jnp.full(m_sc.shape, M_INIT, F32)
    acc_sc[...] = jnp.zeros(acc_sc.shape, F32)

    def attend_chunk(c, carry):
        off = pl.multiple_of(c * kc, kc)
        kk = keys_sc[c]
        bias = []
        for ct in range(tiles):
            cs = slice(ct * LANES, (ct + 1) * LANES)
            sidx = off + ct * LANES + lane
            keep = (kk[:, cs] > thr) | ((kk[:, cs] == thr) & (sidx <= last_tie))
            keep = keep & (sidx <= t_abs)
            bias.append(jnp.where(keep, 0.0, MASK_BIAS))
        for g in range(ATT_KV_HEADS):
            kg = k_ref[pl.ds(off, kc), g * HEAD_DIM:(g + 1) * HEAD_DIM]
            vg = v_ref[pl.ds(off, kc), 2 * g * HEAD_DIM:(2 * g + 2) * HEAD_DIM]
            qg = q_ref[0, g * rep:(g + 1) * rep].reshape(rep * qb, HEAD_DIM)
            s = _nt(qg, kg)
            s_t = [(s[:, ct * LANES:(ct + 1) * LANES].reshape(rep, qb, LANES) + bias[ct][None]
                    ).reshape(rep * qb, LANES) for ct in range(tiles)]
            mx = s_t[0]
            for ct in range(1, tiles):
                mx = jnp.maximum(mx, s_t[ct])
            m_old = m_sc[g]
            m_new = jnp.maximum(m_old, jnp.max(mx, axis=1, keepdims=True))
            p = jnp.concatenate([jnp.exp2(st - m_new) for st in s_t], axis=1).astype(BF16)
            alpha = jnp.exp2(m_old - m_new)
            pv = jnp.dot(p, vg, preferred_element_type=F32)
            acc_sc[g] = acc_sc[g] * jnp.concatenate([alpha, alpha], axis=1) + pv
            m_sc[g] = m_new
        return carry

    lax.fori_loop(0, nch, attend_chunk, 0)

    for g in range(ATT_KV_HEADS):
        for r in range(rep):
            a = acc_sc[g, r * qb:(r + 1) * qb, :]
            h = g * rep + r
            o_ref[:, h * HEAD_DIM:(h + 1) * HEAD_DIM] = (a[:, :HEAD_DIM] / a[:, HEAD_DIM:]).astype(BF16)


def _dsa_attention(q_r, k_r, v_ext, iq_r, ik_lo, ik_hi, iw, batch, seq):
    n = batch * seq
    qb = q_r.shape[2]
    nqb = seq // qb
    topk = min(TOPK_MAX, seq // 4)
    rep = ATT_HEADS // ATT_KV_HEADS
    kern = functools.partial(_dsa_kernel, topk=topk, seq=seq)

    def per_batch(width):
        return pl.BlockSpec((None, seq, width), lambda b, i: (b, 0, 0))

    return pl.pallas_call(
        kern, grid=(batch, nqb),
        in_specs=[pl.BlockSpec((1, ATT_HEADS, qb, HEAD_DIM), lambda b, i: (b * nqb + i, 0, 0, 0)),
                  per_batch(ATT_KV), per_batch(2 * ATT_KV),
                  pl.BlockSpec((1, IDX_HEADS // 2, qb, LANES), lambda b, i: (b * nqb + i, 0, 0, 0)),
                  per_batch(LANES), per_batch(LANES),
                  pl.BlockSpec((qb, LANES), lambda b, i: (b * nqb + i, 0))],
        out_specs=pl.BlockSpec((qb, ATT_Q), lambda b, i: (b * nqb + i, 0)),
        out_shape=jax.ShapeDtypeStruct((n, ATT_Q), BF16),
        scratch_shapes=[pltpu.VMEM((seq // KEY_CHUNK, qb, KEY_CHUNK), I32),
                        pltpu.VMEM((ATT_KV_HEADS, rep * qb, 2 * HEAD_DIM), F32),
                        pltpu.VMEM((ATT_KV_HEADS, rep * qb, LANES), F32),
                        pltpu.VMEM((IDX_HEADS, qb, LANES), F32),
                        pltpu.VMEM((2, qb, LANES), I32)],
        compiler_params=_cparams(2), name="dsa_attention",
    )(q_r, k_r.reshape(batch, seq, ATT_KV), v_ext.reshape(batch, seq, 2 * ATT_KV), iq_r,
      ik_lo.reshape(batch, seq, LANES), ik_hi.reshape(batch, seq, LANES), iw)


GDN_PREP_HEADS = 8
GDN_SCAN_CHUNKS = 2
CONV_HALO = 8


def _split2(x):
    hi = x.astype(BF16)
    return hi, (x - hi.astype(F32)).astype(BF16)


def _split3(x):
    hi = x.astype(BF16)
    r = x - hi.astype(F32)
    mid = r.astype(BF16)
    return hi, mid, (r - mid.astype(F32)).astype(BF16)


def _lhs3(hi, lo):
    return jnp.concatenate([hi, lo, hi], axis=1)


def _rhs3(hi, lo):
    return jnp.concatenate([hi, hi, lo], axis=0)


def _gdn_prep_kernel(q_ref, k_ref, v_ref, qp_ref, kp_ref, vp_ref, sm_ref, gat_ref,
                     cwq_ref, cwk_ref, cwv_ref, alog_row_ref, dtb_row_ref, alog_col_ref, dtb_col_ref,
                     u_ref, w_ref, qd_ref, kd_ref, a_ref, e_ref, xx_sc, gct_sc, *, nc):
    hb = GDN_PREP_HEADS
    c = CHUNK
    hg = pl.program_id(1)
    chunk = pl.program_id(0) % nc
    heads = range(hb)

    def conv_silu(slot, cur_ref, prev_ref, cw_ref):
        prev = prev_ref[...]
        at_start = jnp.full(prev.shape, chunk, I32) == 0
        xx_sc[slot, 0:CONV_HALO, :] = jnp.where(at_start, 0.0, prev)
        xx_sc[slot, CONV_HALO:CONV_HALO + c, :] = cur_ref[...]
        cw = cw_ref[...]
        start = CONV_HALO - (CONV_WIDTH - 1)
        acc = xx_sc[slot, pl.ds(start, c), :] * cw[0:1, :]
        for j in range(1, CONV_WIDTH):
            acc = acc + xx_sc[slot, pl.ds(start + j, c), :] * cw[j:j + 1, :]
        return acc * _sigmoid(acc)

    qa = conv_silu(0, q_ref, qp_ref, cwq_ref)
    ka = conv_silu(1, k_ref, kp_ref, cwk_ref)
    va = conv_silu(2, v_ref, vp_ref, cwv_ref)

    rowi = lax.broadcasted_iota(I32, (c, c), 0)
    coli = lax.broadcasted_iota(I32, (c, c), 1)
    lower = rowi >= coli
    strict = rowi > coli
    eye = jnp.where(rowi == coli, 1.0, 0.0)

    sm = sm_ref[...]
    beta_slab = _sigmoid(sm)
    g_slab = -jnp.exp(alog_row_ref[...]) * jax.nn.softplus(sm + dtb_row_ref[...])
    ones_l = jnp.where(lower, 1.0, 0.0).astype(BF16)
    ones_u = jnp.where(rowi <= coli, 1.0, 0.0).astype(BF16)
    gc_slab = jnp.dot(jnp.concatenate([ones_l] * 3, axis=1), jnp.concatenate(_split3(g_slab), axis=0),
                      preferred_element_type=F32)
    g_t = -jnp.exp(alog_col_ref[:, :c]) * jax.nn.softplus(gat_ref[...] + dtb_col_ref[:, :c])
    gct_sc[...] = jnp.dot(jnp.concatenate(_split3(g_t), axis=1), jnp.concatenate([ones_u] * 3, axis=0),
                          preferred_element_type=F32)
    lane = lax.broadcasted_iota(I32, (c, LANES), 1)
    hsl = [slice(r * GDN_DK, (r + 1) * GDN_DK) for r in heads]

    gcc = [jnp.sum(jnp.where(lane == SM_DT + hg * hb + r, gc_slab, 0.0), axis=1, keepdims=True) for r in heads]
    bcol = [jnp.sum(jnp.where(lane == SM_BETA + hg * hb + r, beta_slab, 0.0), axis=1, keepdims=True)
            for r in heads]
    gcr = [gct_sc[pl.ds(hg * hb + r, 1), :] for r in heads]
    glast = [g[:, c - 1:c] for g in gcr]
    qn = [qa[:, s] * lax.rsqrt(jnp.sum(qa[:, s] * qa[:, s], axis=1, keepdims=True) + RMS_EPS) * GDN_DK ** -0.5
          for s in hsl]
    kn = [ka[:, s] * lax.rsqrt(jnp.sum(ka[:, s] * ka[:, s], axis=1, keepdims=True) + RMS_EPS) for s in hsl]
    eg = [jnp.exp(g) for g in gcc]
    decay = [jnp.where(lower, jnp.exp(gcc[r] - gcr[r]), 0.0) for r in heads]
    kb = [kn[r] * bcol[r] for r in heads]

    def nt3(a, b):
        (ah, al), (bh, bl) = _split2(a), _split2(b)
        return _nt(_lhs3(ah, al), jnp.concatenate([bh, bh, bl], axis=1))

    kq = [nt3(jnp.concatenate([kb[r], qn[r]], axis=0), kn[r]) for r in heads]
    pairs = range(hb // 2)
    r2 = lax.broadcasted_iota(I32, (2 * c, 2 * c), 0)
    c2 = lax.broadcasted_iota(I32, (2 * c, 2 * c), 1)
    same_head = (r2 < c) == (c2 < c)

    def diag2(m):
        return jnp.where(same_head, jnp.concatenate([m, m], axis=0), jnp.zeros((), m.dtype))

    def diag_rhs3(hi, lo):
        return _rhs3(diag2(hi), diag2(lo))

    y = [jnp.concatenate([jnp.where(strict, -(kq[2 * p + s][:c] * decay[2 * p + s]), 0.0) for s in range(2)],
                         axis=1) for p in pairs]
    eye2 = jnp.concatenate([eye, eye], axis=1)
    t_inv = [eye2 + y[p] for p in pairs]
    ys = [_split2(v) for v in y]
    y = [jnp.dot(_lhs3(*ys[p]), diag_rhs3(*ys[p]), preferred_element_type=F32) for p in pairs]
    n_fac = int(math.log2(c))
    for j in range(1, n_fac):
        ys = [_split2(v) for v in y]
        ts = [_split2(v) for v in t_inv]
        if j < n_fac - 1:
            prod = [jnp.dot(jnp.concatenate([_lhs3(*ys[p]), _lhs3(*ts[p])], axis=0), diag_rhs3(*ys[p]),
                            preferred_element_type=F32) for p in pairs]
            y = [q[:c] for q in prod]
            t_inv = [t_inv[p] + prod[p][c:] for p in pairs]
        else:
            t_inv = [t_inv[p] + jnp.dot(_lhs3(*ts[p]), diag_rhs3(*ys[p]), preferred_element_type=F32)
                     for p in pairs]
    ts = [_split2(v) for v in t_inv]
    rhs = [jnp.concatenate([jnp.concatenate([va[:, hsl[r]] * bcol[r], kb[r] * eg[r]], axis=1)
                            for r in (2 * p, 2 * p + 1)], axis=0) for p in pairs]
    uw2 = [jnp.dot(_lhs3(diag2(ts[p][0]), diag2(ts[p][1])), _rhs3(*_split2(rhs[p])),
                   preferred_element_type=F32) for p in pairs]
    uw = [uw2[r // 2][(r % 2) * c:(r % 2 + 1) * c] for r in heads]

    for r in heads:
        u_ref[:, hsl[r]] = uw[r][:, :GDN_DV]
        w_ref[:, hsl[r]] = uw[r][:, GDN_DV:].astype(BF16)
        qd_ref[:, hsl[r]] = (qn[r] * eg[r]).astype(BF16)
        kd_ref[:, hsl[r]] = (kn[r] * jnp.exp(glast[r] - gcc[r])).astype(BF16)
        a_ref[:, r * c:(r + 1) * c] = (kq[r][c:] * decay[r]).astype(BF16)
        e_ref[:, hsl[r]] = jnp.broadcast_to(jnp.exp(glast[r]), (CONV_HALO, GDN_DK))


def _gdn_scan_kernel(u_ref, w_ref, qd_ref, kd_ref, a_ref, e_ref, z_ref, nw_ref, o_ref, st_sc):
    c = CHUNK
    heads = range(GDN_HEADS)
    hsl = [slice(h * GDN_DK, (h + 1) * GDN_DK) for h in heads]

    @pl.when(pl.program_id(1) == 0)
    def _():
        st_sc[...] = jnp.zeros(st_sc.shape, F32)

    st = [st_sc[h] for h in heads]
    for sub in range(u_ref.shape[0] // c):
        rows = slice(sub * c, (sub + 1) * c)
        ws_qs = [jnp.dot(jnp.concatenate([w_ref[rows, hsl[h]], qd_ref[rows, hsl[h]]], axis=0), st[h].astype(BF16),
                         preferred_element_type=F32) for h in heads]
        v_new = [(u_ref[rows, hsl[h]] - ws_qs[h][:c]).astype(BF16) for h in heads]
        o = [ws_qs[h][c:] + jnp.dot(a_ref[rows, h * c:(h + 1) * c], v_new[h], preferred_element_type=F32)
             for h in heads]
        e_row = slice(sub * CONV_HALO, sub * CONV_HALO + 1)
        st = [st[h] * e_ref[e_row, hsl[h]] + _tn(kd_ref[rows, hsl[h]], v_new[h]) for h in heads]
        for h in heads:
            z = z_ref[rows, hsl[h]]
            on = o[h] * lax.rsqrt(jnp.mean(o[h] * o[h], axis=1, keepdims=True) + RMS_EPS)
            o_ref[rows, hsl[h]] = (on * nw_ref[...] * (z * _sigmoid(z))).astype(BF16)
    for h in heads:
        st_sc[h] = st[h]


def _gated_deltanet(proj, conv_w, a_log, dt_bias, norm_w, batch, seq):
    n = batch * seq
    nc = seq // CHUNK
    hb = GDN_PREP_HEADS
    w = hb * GDN_DK
    ga_t = proj[:, OFF_SM + SM_DT:OFF_SM + SM_DT + GDN_HEADS].reshape(batch * nc, CHUNK, GDN_HEADS)
    ga_t = ga_t.transpose(0, 2, 1)
    lane_pad = lambda vec, off: jnp.zeros((1, LANES), F32).at[0, off:off + GDN_HEADS].set(vec.astype(F32))
    alog_row, dtb_row = lane_pad(a_log, SM_DT), lane_pad(dt_bias, SM_DT)
    alog_col = jnp.broadcast_to(a_log.astype(F32)[:, None], (GDN_HEADS, LANES))
    dtb_col = jnp.broadcast_to(dt_bias.astype(F32)[:, None], (GDN_HEADS, LANES))
    nw = norm_w.astype(F32).reshape(1, GDN_DV)
    cw = conv_w.astype(F32)

    def cur(off):
        return pl.BlockSpec((CHUNK, w), lambda i, h: (i, off // w + h))

    def prev(off):
        per = CHUNK // CONV_HALO
        return pl.BlockSpec((CONV_HALO, w), lambda i, h: (jnp.maximum(i * per - 1, 0), off // w + h))

    def cwspec(off):
        return pl.BlockSpec((CONV_WIDTH, w), lambda i, h: (0, off // w + h))

    row = pl.BlockSpec((1, LANES), lambda i, h: (0, 0))
    colv = pl.BlockSpec((GDN_HEADS, LANES), lambda i, h: (0, 0))
    head_tile = pl.BlockSpec((CHUNK, w), lambda i, h: (i, h))
    u, wm, qd, kd, a_in, egl = pl.pallas_call(
        functools.partial(_gdn_prep_kernel, nc=nc), grid=(batch * nc, GDN_HEADS // hb),
        in_specs=[cur(OFF_GQ), cur(OFF_GK), cur(OFF_GV), prev(OFF_GQ), prev(OFF_GK), prev(OFF_GV),
                  pl.BlockSpec((CHUNK, LANES), lambda i, h: (i, OFF_SM // LANES)),
                  pl.BlockSpec((None, GDN_HEADS, CHUNK), lambda i, h: (i, 0, 0)),
                  cwspec(0), cwspec(GDN_QK), cwspec(2 * GDN_QK), row, row, colv, colv],
        out_specs=[head_tile, head_tile, head_tile, head_tile,
                   pl.BlockSpec((CHUNK, hb * CHUNK), lambda i, h: (i, h)),
                   pl.BlockSpec((CONV_HALO, w), lambda i, h: (i, h))],
        out_shape=[jax.ShapeDtypeStruct((n, GDN_V), F32), jax.ShapeDtypeStruct((n, GDN_V), BF16),
                   jax.ShapeDtypeStruct((n, GDN_QK), BF16), jax.ShapeDtypeStruct((n, GDN_QK), BF16),
                   jax.ShapeDtypeStruct((n, GDN_HEADS * CHUNK), BF16),
                   jax.ShapeDtypeStruct((batch * nc * CONV_HALO, GDN_V), F32)],
        scratch_shapes=[pltpu.VMEM((3, CONV_HALO + CHUNK, w), F32),
                        pltpu.VMEM((GDN_HEADS, CHUNK), F32)],
        compiler_params=_cparams(2), name="gdn_prep",
    )(proj, proj, proj, proj, proj, proj, proj, ga_t, cw, cw, cw, alog_row, dtb_row, alog_col, dtb_col)

    per = GDN_SCAN_CHUNKS if nc % GDN_SCAN_CHUNKS == 0 else 1
    steps = nc // per

    def full(width):
        return pl.BlockSpec((per * CHUNK, width), lambda b, t: (b * steps + t, 0))

    return pl.pallas_call(
        _gdn_scan_kernel, grid=(batch, steps),
        in_specs=[full(GDN_V), full(GDN_V), full(GDN_QK), full(GDN_QK), full(GDN_HEADS * CHUNK),
                  pl.BlockSpec((per * CONV_HALO, GDN_V), lambda b, t: (b * steps + t, 0)),
                  pl.BlockSpec((per * CHUNK, GDN_V), lambda b, t: (b * steps + t, OFF_GZ // GDN_V)),
                  pl.BlockSpec((1, GDN_DV), lambda b, t: (0, 0))],
        out_specs=full(GDN_V),
        out_shape=jax.ShapeDtypeStruct((n, GDN_V), BF16),
        scratch_shapes=[pltpu.VMEM((GDN_HEADS, GDN_DK, GDN_DV), F32)],
        compiler_params=_cparams(2), name="gdn_scan",
    )(u, wm, qd, kd, a_in, egl, proj, nw)


def _merge_kernel(att_ref, gdn_ref, wa_ref, wg_ref, ga_ref, gb_ref, o_ref):
    a = jnp.dot(att_ref[...], wa_ref[...], preferred_element_type=F32)
    g = jnp.dot(gdn_ref[...], wg_ref[...], preferred_element_type=F32)
    o_ref[...] = (_sigmoid(ga_ref[...]) * a + _sigmoid(gb_ref[...]) * g).astype(BF16)


def _merge(att, gdn, w_o_attn, w_o_gdn, proj):
    n = att.shape[0]
    tm, tn = min(512, n), 1024
    return pl.pallas_call(
        _merge_kernel, grid=(n // tm, D_MODEL // tn),
        in_specs=[pl.BlockSpec((tm, ATT_Q), lambda i, j: (i, 0)),
                  pl.BlockSpec((tm, GDN_V), lambda i, j: (i, 0)),
                  pl.BlockSpec((ATT_Q, tn), lambda i, j: (0, j)),
                  pl.BlockSpec((GDN_V, tn), lambda i, j: (0, j)),
                  pl.BlockSpec((tm, tn), lambda i, j: (i, OFF_GTA // tn + j)),
                  pl.BlockSpec((tm, tn), lambda i, j: (i, OFF_GTB // tn + j))],
        out_specs=pl.BlockSpec((tm, tn), lambda i, j: (i, j)),
        out_shape=jax.ShapeDtypeStruct((n, D_MODEL), BF16),
        compiler_params=_cparams(2), name="mixer_merge",
    )(att, gdn, w_o_attn.astype(BF16), w_o_gdn.astype(BF16), proj, proj)


def _pack_halves(x):
    w = x.shape[1] // 2
    bits = lax.bitcast_convert_type(x.astype(BF16).astype(F32), U32)
    return (bits[:, :w] >> 16) | (bits[:, w:] & jnp.uint32(0xFFFF0000))


def _unpack_halves(u):
    lo = lax.bitcast_convert_type(u << 16, F32)
    hi = lax.bitcast_convert_type(u & jnp.uint32(0xFFFF0000), F32)
    return lo, hi


SLAB = 8


def _store_slabs(ref, lead, u):
    m = u.shape[0]
    for s in range(SLAB):
        ref[lead + (pl.ds(s, m, stride=SLAB), slice(None))] = u[:, s * LANES:(s + 1) * LANES]


def _load_slabs(ref, lead, start, m):
    return jnp.concatenate([ref[lead + (pl.ds(start + s, m, stride=SLAB), slice(None))] for s in range(SLAB)],
                           axis=1)


def _outproj_ln_kernel(m_ref, w_ref, x_ref, g_ref, b_ref, x1_ref, x1p_ref, *, alpha):
    y = jnp.dot(m_ref[...], w_ref[...], preferred_element_type=F32)
    x1 = _layer_norm(alpha * x_ref[...] + y, g_ref[...], b_ref[...])
    x1_ref[...] = x1
    _store_slabs(x1p_ref, (), _pack_halves(x1))


def _outproj_ln(merged, w_out, x2d, ln_g, ln_b, alpha):
    n, d = x2d.shape
    tm = min(256, n)
    row = pl.BlockSpec((1, d), lambda i: (0, 0))
    tile = pl.BlockSpec((tm, d), lambda i: (i, 0))
    return pl.pallas_call(
        functools.partial(_outproj_ln_kernel, alpha=alpha), grid=(n // tm,),
        in_specs=[tile, pl.BlockSpec((d, d), lambda i: (0, 0)), tile, row, row],
        out_specs=[tile, pl.BlockSpec((tm * SLAB, LANES), lambda i: (i, 0))],
        out_shape=[jax.ShapeDtypeStruct((n, d), F32), jax.ShapeDtypeStruct((n * SLAB, LANES), U32)],
        compiler_params=_cparams(1), name="outproj_ln",
    )(merged, w_out.astype(BF16), x2d, ln_g.astype(F32).reshape(1, d), ln_b.astype(F32).reshape(1, d))


def _first_max(x, iota, size):
    m = jnp.max(x, axis=0, keepdims=True)
    f = jnp.min(jnp.where(x == m, iota, float(size)), axis=0, keepdims=True)
    return m, f


def _router_kernel(x_ref, w_ref, b_ref, eidx_ref, gate_ref, rank_ref, cnt_ref, carry_sc):
    tm = x_ref.shape[0]
    per = N_EXPERTS // N_GROUPS
    neg = -jnp.inf

    @pl.when(pl.program_id(0) == 0)
    def _():
        carry_sc[...] = jnp.zeros(carry_sc.shape, F32)

    logits = _nt(w_ref[...], x_ref[...], precision=HI)
    s = _sigmoid(logits)
    choice = s + b_ref[:, 0:1]
    io_g = lax.broadcasted_iota(I32, (per, tm), 0).astype(F32)
    io_e = lax.broadcasted_iota(I32, (N_EXPERTS, tm), 0).astype(F32)

    gs = []
    for g in range(N_GROUPS):
        xg = choice[g * per:(g + 1) * per]
        m1, f1 = _first_max(xg, io_g, per)
        m2 = jnp.max(jnp.where(io_g == f1, neg, xg), axis=0, keepdims=True)
        gs.append(m1 + m2)
    gw = jnp.concatenate(gs, axis=0)
    gsel = jnp.zeros((N_GROUPS, tm), F32)
    for _ in range(TOPK_GROUPS):
        _, f = _first_max(gw, io_g, N_GROUPS)
        hit = io_g == f
        gsel = jnp.where(hit, 1.0, gsel)
        gw = jnp.where(hit, neg, gw)
    x = jnp.concatenate([jnp.where(gsel[g:g + 1] > 0.0, choice[g * per:(g + 1) * per], neg)
                         for g in range(N_GROUPS)], axis=0)

    mem = jnp.zeros((N_EXPERTS, tm), F32)
    picks, wsel = [], []
    for _ in range(TOP_K):
        _, f = _first_max(x, io_e, N_EXPERTS)
        hit = io_e == f
        picks.append(f)
        wsel.append(jnp.sum(jnp.where(hit, s, 0.0), axis=0, keepdims=True))
        mem = jnp.where(hit, 1.0, mem)
        x = jnp.where(hit, neg, x)
    wsel = jnp.concatenate(wsel, axis=0)
    gate_ref[...] = wsel / jnp.sum(wsel, axis=0, keepdims=True) * ROUTED_SCALE
    eidx_ref[...] = jnp.concatenate(picks, axis=0).astype(I32)

    tr = lax.broadcasted_iota(I32, (tm, tm), 0)
    tc = lax.broadcasted_iota(I32, (tm, tm), 1)
    before = jnp.where(tr < tc, 1.0, 0.0).astype(BF16)
    prefix = jnp.dot(mem.astype(BF16), before, preferred_element_type=F32) + carry_sc[:, 0:1]
    ranks = [jnp.sum(jnp.where(io_e == f, prefix, 0.0), axis=0, keepdims=True) for f in picks]
    rank_ref[...] = jnp.concatenate(ranks, axis=0).astype(I32)
    carry_sc[...] = carry_sc[...] + jnp.sum(mem, axis=1, keepdims=True)
    cnt_ref[...] = carry_sc[...]


def _router(x1, w_router, router_bias):
    n, d = x1.shape
    tm = min(512, n)
    out = pl.BlockSpec((TOP_K, tm), lambda i: (0, i))
    bias = jnp.broadcast_to(router_bias.astype(F32)[:, None], (N_EXPERTS, LANES))
    return pl.pallas_call(
        _router_kernel, grid=(n // tm,),
        in_specs=[pl.BlockSpec((tm, d), lambda i: (i, 0)),
                  pl.BlockSpec((N_EXPERTS, d), lambda i: (0, 0)),
                  pl.BlockSpec((N_EXPERTS, LANES), lambda i: (0, 0))],
        out_specs=[out, out, out, pl.BlockSpec((N_EXPERTS, LANES), lambda i: (0, 0))],
        out_shape=[jax.ShapeDtypeStruct((TOP_K, n), I32), jax.ShapeDtypeStruct((TOP_K, n), F32),
                   jax.ShapeDtypeStruct((TOP_K, n), I32), jax.ShapeDtypeStruct((N_EXPERTS, LANES), F32)],
        scratch_shapes=[pltpu.VMEM((N_EXPERTS, LANES), F32)],
        compiler_params=_cparams(1), name="router",
    )(x1, w_router.astype(F32).T, bias)


SCATTER_TOKENS = 512
COMBINE_TOKENS = 128
ROW_COPY_UNROLL = 8


def _pos_kernel(pstart_ref, eidx_ref, rank_ref, pos_ref):
    e = eidx_ref[...]
    acc = rank_ref[...]
    for x in range(N_EXPERTS):
        acc = acc + jnp.where(e == x, pstart_ref[x], 0)
    pos_ref[...] = acc


def _positions(pstart, eidx, rank):
    n = eidx.shape[1]
    tm = min(2048, n)
    tile = pl.BlockSpec((TOP_K, tm), lambda i, ps: (0, i))
    return pl.pallas_call(
        _pos_kernel,
        grid_spec=pltpu.PrefetchScalarGridSpec(num_scalar_prefetch=1, grid=(n // tm,),
                                               in_specs=[tile, tile], out_specs=tile),
        out_shape=jax.ShapeDtypeStruct((TOP_K, n), I32),
        compiler_params=_cparams(1), name="moe_positions",
    )(pstart, eidx, rank)


def _scatter_kernel(pend_ref, padded_ref, pos_ref, x_ref, xs_ref, zero_sc, sem, zsem):
    tm = x_ref.shape[0] // SLAB
    bm = EXPERT_ROWS

    @pl.when(pl.program_id(0) == 0)
    def _():
        zero_sc[...] = jnp.zeros(zero_sc.shape, U32)

        def tail_copy(e):
            tail = pl.multiple_of((pend_ref[e] - bm) * SLAB, bm * SLAB)
            return pltpu.make_async_copy(zero_sc, xs_ref.at[pl.ds(tail, bm * SLAB), :], zsem)

        def zstart(e, carry):
            @pl.when(padded_ref[e] > 0)
            def _():
                tail_copy(e).start()
            return carry

        def zwait(e, carry):
            @pl.when(padded_ref[e] > 0)
            def _():
                tail_copy(e).wait()
            return carry

        lax.fori_loop(0, N_EXPERTS, zstart, 0)
        lax.fori_loop(0, N_EXPERTS, zwait, 0)

    def row_copy(t, k):
        src = x_ref.at[pl.ds(pl.multiple_of(t * SLAB, SLAB), SLAB), :]
        dst = xs_ref.at[pl.ds(pl.multiple_of(pos_ref[k, t] * SLAB, SLAB), SLAB), :]
        return pltpu.make_async_copy(src, dst, sem)

    def start(t, carry):
        for k in range(TOP_K):
            row_copy(t, k).start(priority=k % 2)
        return carry

    lax.fori_loop(0, tm, start, 0, unroll=ROW_COPY_UNROLL)
    rows = xs_ref.at[pl.ds(0, tm * TOP_K * SLAB), :]
    pltpu.make_async_copy(rows, rows, sem).wait()


def _scatter_rows(x1p, pos, pend, padded, n_rows):
    n = x1p.shape[0] // SLAB
    tm = min(SCATTER_TOKENS, n)
    grid_spec = pltpu.PrefetchScalarGridSpec(
        num_scalar_prefetch=2, grid=(n // tm,),
        in_specs=[pl.BlockSpec((TOP_K, tm), lambda i, pe, pd: (0, i), memory_space=pltpu.SMEM),
                  pl.BlockSpec((tm * SLAB, LANES), lambda i, pe, pd: (i, 0))],
        out_specs=pl.BlockSpec(memory_space=pl.ANY),
        scratch_shapes=[pltpu.VMEM((EXPERT_ROWS * SLAB, LANES), U32), pltpu.SemaphoreType.DMA,
                        pltpu.SemaphoreType.DMA])
    return pl.pallas_call(
        _scatter_kernel, grid_spec=grid_spec,
        out_shape=jax.ShapeDtypeStruct((n_rows * SLAB, LANES), U32),
        compiler_params=_cparams(1), name="moe_scatter",
    )(pend, padded, pos, x1p)


def _expert_kernel(blk_e_ref, blk_on_ref, blk_new_ref, x_ref, w1_ref, w3_ref, w2_ref, y_ref,
                   w1_sc, w3_sc, w2_sc):
    del blk_e_ref
    i = pl.program_id(0)

    @pl.when(blk_new_ref[i] > 0)
    def _():
        w1_sc[...] = w1_ref[...].astype(BF16)
        w3_sc[...] = w3_ref[...].astype(BF16)
        w2_sc[...] = w2_ref[...].astype(BF16)

    @pl.when(blk_on_ref[i] > 0)
    def _():
        half = D_MODEL // 2
        lo, hi = _unpack_halves(_load_slabs(x_ref, (), 0, EXPERT_ROWS))
        xa, xb = lo.astype(BF16), hi.astype(BF16)

        def up(w_sc):
            return (jnp.dot(xa, w_sc[0:half, :], preferred_element_type=F32)
                    + jnp.dot(xb, w_sc[half:, :], preferred_element_type=F32))

        h1, h3 = up(w1_sc), up(w3_sc)
        act = (h1 * _sigmoid(h1) * h3).astype(BF16)
        _store_slabs(y_ref, (), _pack_halves(jnp.dot(act, w2_sc[...], preferred_element_type=F32)))


def _experts(xs, blk_e, blk_on, blk_new, w1, w3, w2):
    n_rows, w = xs.shape
    bm = EXPERT_ROWS * SLAB
    grid_spec = pltpu.PrefetchScalarGridSpec(
        num_scalar_prefetch=3, grid=(n_rows // bm,),
        in_specs=[pl.BlockSpec((bm, w), lambda i, e, on, nw: (i, 0)),
                  pl.BlockSpec((None, D_MODEL, EXPERT_DIM), lambda i, e, on, nw: (e[i], 0, 0)),
                  pl.BlockSpec((None, D_MODEL, EXPERT_DIM), lambda i, e, on, nw: (e[i], 0, 0)),
                  pl.BlockSpec((None, EXPERT_DIM, D_MODEL), lambda i, e, on, nw: (e[i], 0, 0))],
        out_specs=pl.BlockSpec((bm, w), lambda i, e, on, nw: (i, 0)),
        scratch_shapes=[pltpu.VMEM((D_MODEL, EXPERT_DIM), BF16), pltpu.VMEM((D_MODEL, EXPERT_DIM), BF16),
                        pltpu.VMEM((EXPERT_DIM, D_MODEL), BF16)])
    return pl.pallas_call(
        _expert_kernel, grid_spec=grid_spec,
        out_shape=jax.ShapeDtypeStruct((n_rows, w), U32),
        compiler_params=_cparams(1), name="moe_experts",
    )(blk_e, blk_on, blk_new, xs, w1, w3, w2)


def _shared_up_kernel(x_ref, w_ref, h_ref):
    h = jnp.dot(x_ref[...].astype(BF16), w_ref[...], preferred_element_type=F32)
    h1, h3 = h[:, :SHARED_DIM], h[:, SHARED_DIM:]
    h_ref[...] = (h1 * _sigmoid(h1) * h3).astype(BF16)


def _shared_up(x1, ws1, ws3):
    n, d = x1.shape
    tm = min(512, n)
    w13 = jnp.concatenate([ws1, ws3], axis=1).astype(BF16)
    return pl.pallas_call(
        _shared_up_kernel, grid=(n // tm,),
        in_specs=[pl.BlockSpec((tm, d), lambda i: (i, 0)),
                  pl.BlockSpec((d, 2 * SHARED_DIM), lambda i: (0, 0))],
        out_specs=pl.BlockSpec((tm, SHARED_DIM), lambda i: (i, 0)),
        out_shape=jax.ShapeDtypeStruct((n, SHARED_DIM), BF16),
        compiler_params=_cparams(1), name="shared_up",
    )(x1, w13)


def _combine_kernel(pos_ref, pos_next_ref, x1_ref, h_ref, ws2_ref, gate_ref, g_ref, b_ref, y_hbm, o_ref,
                    ybuf, sem, *, alpha):
    tm = x1_ref.shape[0]
    i = pl.program_id(0)
    slot = i % 2

    def issue(p_ref, s):
        def start(t, carry):
            for k in range(TOP_K):
                src = y_hbm.at[pl.ds(pl.multiple_of(p_ref[k, t] * SLAB, SLAB), SLAB), :]
                dst = ybuf.at[s, pl.ds(pl.multiple_of((k * tm + t) * SLAB, SLAB), SLAB), :]
                pltpu.make_async_copy(src, dst, sem.at[s]).start(priority=k % 2)
            return carry

        lax.fori_loop(0, tm, start, 0, unroll=ROW_COPY_UNROLL)

    @pl.when(i == 0)
    def _():
        issue(pos_ref, 0)

    @pl.when(i + 1 < pl.num_programs(0))
    def _():
        issue(pos_next_ref, 1 - slot)

    base = alpha * x1_ref[...] + jnp.dot(h_ref[...], ws2_ref[...], preferred_element_type=F32)
    pltpu.make_async_copy(y_hbm.at[pl.ds(0, tm * TOP_K * SLAB), :], ybuf.at[slot], sem.at[slot]).wait()

    gates = gate_ref[...]
    acc_lo = jnp.zeros((tm, D_MODEL // 2), F32)
    acc_hi = jnp.zeros((tm, D_MODEL // 2), F32)
    for k in range(TOP_K):
        lo, hi = _unpack_halves(_load_slabs(ybuf, (slot,), k * tm * SLAB, tm))
        gk = gates[:, k:k + 1]
        acc_lo = acc_lo + gk * lo
        acc_hi = acc_hi + gk * hi
    f = jnp.concatenate([acc_lo, acc_hi], axis=1)
    o_ref[...] = _layer_norm(base + f, g_ref[...], b_ref[...])


def _combine(x1, hsh, ws2, pos, gates_t, ys, ln_g, ln_b, alpha):
    n, d = x1.shape
    tm = min(COMBINE_TOKENS, n)
    row = pl.BlockSpec((1, d), lambda i: (0, 0))
    last = n // tm - 1
    return pl.pallas_call(
        functools.partial(_combine_kernel, alpha=alpha), grid=(n // tm,),
        in_specs=[pl.BlockSpec((TOP_K, tm), lambda i: (0, i), memory_space=pltpu.SMEM),
                  pl.BlockSpec((TOP_K, tm), lambda i: (0, jnp.minimum(i + 1, last)), memory_space=pltpu.SMEM),
                  pl.BlockSpec((tm, d), lambda i: (i, 0)),
                  pl.BlockSpec((tm, SHARED_DIM), lambda i: (i, 0)),
                  pl.BlockSpec((SHARED_DIM, d), lambda i: (0, 0)),
                  pl.BlockSpec((tm, TOP_K), lambda i: (i, 0)),
                  row, row,
                  pl.BlockSpec(memory_space=pl.ANY)],
        out_specs=pl.BlockSpec((tm, d), lambda i: (i, 0)),
        out_shape=jax.ShapeDtypeStruct((n, d), F32),
        scratch_shapes=[pltpu.VMEM((2, TOP_K * tm * SLAB, LANES), U32), pltpu.SemaphoreType.DMA((2,))],
        compiler_params=_cparams(1), name="moe_combine",
    )(pos, pos, x1, hsh, ws2.astype(BF16), gates_t, ln_g.astype(F32).reshape(1, d),
      ln_b.astype(F32).reshape(1, d), ys)


def _moe(x1, x1p, w_router, router_bias, w1, w3, w2, ws1, ws3, ws2, ln_g, ln_b, alpha):
    n = x1.shape[0]
    bm = EXPERT_ROWS
    eidx, gates, rank, cnt = _router(x1, w_router, router_bias)
    counts = cnt[:, 0].astype(I32)
    padded = (counts + bm - 1) // bm * bm
    pend = jnp.cumsum(padded)
    pstart = pend - padded
    pos = _positions(pstart, eidx, rank)
    n_rows = -(-(n * TOP_K) // bm) * bm + N_EXPERTS * bm
    blk_start = jnp.arange(n_rows // bm, dtype=I32) * bm
    blk_on = (blk_start < pend[-1]).astype(I32)
    last = jnp.maximum(pend[-1] - bm, 0)
    blk_e = jnp.sum((pend[None, :] <= jnp.minimum(blk_start, last)[:, None]).astype(I32), axis=1)
    blk_e = jnp.minimum(blk_e, N_EXPERTS - 1)
    changed = jnp.concatenate([jnp.ones((1,), I32), (blk_e[1:] != blk_e[:-1]).astype(I32)])
    blk_new = blk_on * changed
    xs = _scatter_rows(x1p, pos, pend, padded, n_rows)
    ys = _experts(xs, blk_e, blk_on, blk_new, w1, w3, w2)
    hsh = _shared_up(x1, ws1, ws3)
    return _combine(x1, hsh, ws2, pos, gates.T, ys, ln_g, ln_b, alpha)


def kernel(x, positions, w_in, conv_w, A_log, dt_bias, gdn_norm_w, idx_ln_g, idx_ln_b, w_o_attn, w_o_gdn, w_out, ln1_g, ln1_b, w_router, router_bias, w1, w3, w2, ws1, ws3, ws2, ln2_g, ln2_b):
    batch, seq, d = x.shape
    depth = w_in.shape[0]
    alpha = (2 * depth) ** 0.25
    n = batch * seq
    xf = x.reshape(n, d)
    tabs = _trig_tables(positions)
    for l in range(depth):
        proj = _project(xf, _pack_w_in(w_in, l))
        q_r, k_r, v_ext, iq_r, ik_lo, ik_hi, iw = _att_prep(proj, tabs, idx_ln_g[l], idx_ln_b[l])
        att = _dsa_attention(q_r, k_r, v_ext, iq_r, ik_lo, ik_hi, iw, batch, seq)
        gdn = _gated_deltanet(proj, conv_w[l], A_log[l], dt_bias[l], gdn_norm_w[l], batch, seq)
        merged = _merge(att, gdn, w_o_attn[l], w_o_gdn[l], proj)
        x1, x1p = _outproj_ln(merged, w_out[l], xf, ln1_g[l], ln1_b[l], alpha)
        xf = _moe(x1, x1p, w_router[l], router_bias[l], w1[l], w3[l], w2[l], ws1[l], ws3[l], ws2[l],
                  ln2_g[l], ln2_b[l], alpha)
    return xf.reshape(batch, seq, d)
```

```python
import functools
import math

import jax
import jax.numpy as jnp
import numpy as np
from jax import lax
from jax.experimental import pallas as pl
from jax.experimental.pallas import tpu as pltpu

F32 = jnp.float32
BF16 = jnp.bfloat16
I32 = jnp.int32
U32 = jnp.uint32
HI = lax.Precision.HIGHEST

D_MODEL = 2048
ATT_HEADS = 16
ATT_KV_HEADS = 2
HEAD_DIM = 128
IDX_HEADS = 8
IDX_DIM = 64
TOPK_MAX = 256
Q_BLOCK = 128
ROPE_THETA = 10000.0
GDN_HEADS = 16
GDN_DK = 128
GDN_DV = 128
CONV_WIDTH = 4
CHUNK = 64
N_EXPERTS = 64
TOP_K = 8
N_GROUPS = 8
TOPK_GROUPS = 4
EXPERT_DIM = 512
SHARED_DIM = 512
ROUTED_SCALE = 2.5
LN_EPS = 1e-5
RMS_EPS = 1e-6

ATT_Q = ATT_HEADS * HEAD_DIM
ATT_KV = ATT_KV_HEADS * HEAD_DIM
IDX_Q = IDX_HEADS * IDX_DIM
GDN_QK = GDN_HEADS * GDN_DK
GDN_V = GDN_HEADS * GDN_DV
SPLITS = (ATT_Q, ATT_KV, ATT_KV, IDX_Q, IDX_DIM, IDX_HEADS, GDN_QK, GDN_QK, GDN_V, GDN_V,
          GDN_HEADS, GDN_HEADS, D_MODEL, D_MODEL)
SPLIT_OFFSETS = tuple(int(o) for o in np.cumsum(SPLITS)[:-1])

LANES = 128
VMEM_LIMIT_BYTES = 56 * 1024 * 1024

OFF_AQ = 0
OFF_GQ = OFF_AQ + ATT_Q
OFF_GK = OFF_GQ + GDN_QK
OFF_GV = OFF_GK + GDN_QK
OFF_GZ = OFF_GV + GDN_V
OFF_GTA = OFF_GZ + GDN_V
OFF_GTB = OFF_GTA + D_MODEL
OFF_AK = OFF_GTB + D_MODEL
OFF_AV = OFF_AK + ATT_KV
OFF_IQ = OFF_AV + ATT_KV
OFF_SM = OFF_IQ + IDX_Q
PROJ_W = 16384
SM_IK = 0
SM_IW = SM_IK + IDX_DIM
SM_BETA = SM_IW + IDX_HEADS
SM_DT = SM_BETA + GDN_HEADS

KEY_CHUNK = 256
ATT_ROWS = 256
EXPERT_ROWS = 512
INT_MIN = -2147483648
MASK_BIAS = -1e30
M_INIT = -1e29


def _cparams(n_axes):
    return pltpu.CompilerParams(dimension_semantics=("arbitrary",) * n_axes,
                                vmem_limit_bytes=VMEM_LIMIT_BYTES)


def _nt(a, b, precision=None):
    return lax.dot_general(a, b, (((1,), (1,)), ((), ())), precision=precision,
                           preferred_element_type=F32)


def _tn(a, b, precision=None):
    return lax.dot_general(a, b, (((0,), (0,)), ((), ())), precision=precision,
                           preferred_element_type=F32)


def _sigmoid(x):
    return jax.nn.sigmoid(x)


def _layer_norm(x, g, b):
    mu = jnp.mean(x, axis=-1, keepdims=True)
    xc = x - mu
    var = jnp.mean(xc * xc, axis=-1, keepdims=True)
    return xc * lax.rsqrt(var + LN_EPS) * g + b


def _trig_kernel(pos_ref, inv_a_ref, inv_i_ref, sgn_a_ref, sgn_i_ref, cos_a, sin_a, cos_i, sin_i):
    p = pos_ref[...]
    ang_a = p * inv_a_ref[...]
    cos_a[...] = jnp.cos(ang_a)
    sin_a[...] = jnp.sin(ang_a) * sgn_a_ref[...]
    ang_i = p * inv_i_ref[...]
    cos_i[...] = jnp.cos(ang_i)
    sin_i[...] = jnp.sin(ang_i) * sgn_i_ref[...]


def _trig_tables(positions):
    n = positions.size
    tm = min(512, n)
    pos = positions.reshape(n, 1).astype(F32)
    inv_a = ROPE_THETA ** (-jnp.arange(0, HEAD_DIM, 2, dtype=F32) / HEAD_DIM)
    inv_i = ROPE_THETA ** (-jnp.arange(0, IDX_DIM, 2, dtype=F32) / IDX_DIM)
    half_a, half_i = HEAD_DIM // 2, IDX_DIM // 2
    inv_a_row = jnp.tile(inv_a, 2).reshape(1, LANES)
    inv_i_row = jnp.tile(inv_i, 4).reshape(1, LANES)
    lane = np.arange(LANES)
    sgn_a = jnp.asarray(np.where(lane < half_a, -1.0, 1.0), F32).reshape(1, LANES)
    sgn_i = jnp.asarray(np.where(lane % IDX_DIM < half_i, -1.0, 1.0), F32).reshape(1, LANES)
    row = pl.BlockSpec((1, LANES), lambda i: (0, 0))
    tab = pl.BlockSpec((tm, LANES), lambda i: (i, 0))
    return pl.pallas_call(
        _trig_kernel, grid=(n // tm,),
        in_specs=[pl.BlockSpec((tm, 1), lambda i: (i, 0)), row, row, row, row],
        out_specs=[tab, tab, tab, tab],
        out_shape=[jax.ShapeDtypeStruct((n, LANES), F32)] * 4,
        compiler_params=_cparams(1), name="trig_tables",
    )(pos, inv_a_row, inv_i_row, sgn_a, sgn_i)


def _pack_plan():
    src_off = dict(zip(("aq", "ak", "av", "iq", "ik", "iw", "gq", "gk", "gv", "gz", "gb", "ga", "gta", "gtb"),
                       (0,) + SPLIT_OFFSETS))
    order = (("aq", OFF_AQ, ATT_Q), ("gq", OFF_GQ, GDN_QK), ("gk", OFF_GK, GDN_QK), ("gv", OFF_GV, GDN_V),
             ("gz", OFF_GZ, GDN_V), ("gta", OFF_GTA, D_MODEL), ("gtb", OFF_GTB, D_MODEL), ("ak", OFF_AK, ATT_KV),
             ("av", OFF_AV, ATT_KV), ("iq", OFF_IQ, IDX_Q), ("ik", OFF_SM + SM_IK, IDX_DIM),
             ("iw", OFF_SM + SM_IW, IDX_HEADS), ("gb", OFF_SM + SM_BETA, GDN_HEADS), ("ga", OFF_SM + SM_DT, GDN_HEADS))
    src_col = np.full((PROJ_W,), -1, np.int64)
    for name, dst, width in order:
        src_col[dst:dst + width] = src_off[name] + np.arange(width)
    plan, shifts = [], set()
    for j in range(PROJ_W // LANES):
        cols = src_col[j * LANES:(j + 1) * LANES]
        if j == OFF_SM // LANES:
            a, b = src_off["ik"], src_off["gb"] - SM_BETA
            assert a % LANES == 0 and b % LANES == 0
            plan.append((3, a // LANES, b // LANES, 0))
        elif (cols < 0).all():
            plan.append((2, 0, 0, 0))
        else:
            assert (np.diff(cols) == 1).all()
            shift = int(cols[0] % LANES)
            if shift == 0:
                plan.append((0, cols[0] // LANES, cols[0] // LANES, 0))
            else:
                shifts.add(shift)
                plan.append((1, cols[0] // LANES, cols[0] // LANES + 1, shift))
    return np.asarray(plan, np.int32), tuple(sorted(shifts))


def _pack_kernel(mode_ref, ta_ref, tb_ref, shift_ref, a_ref, b_ref, o_ref, *, shifts):
    del ta_ref, tb_ref
    mode = mode_ref[pl.program_id(0)]
    lane = lax.broadcasted_iota(I32, a_ref.shape, 1)

    @pl.when(mode == 0)
    def _():
        o_ref[...] = a_ref[...].astype(BF16)

    for shift in shifts:
        @pl.when((mode == 1) & (shift_ref[pl.program_id(0)] == shift))
        def _(shift=shift):
            o_ref[...] = jnp.concatenate([a_ref[:, shift:], b_ref[:, :shift]], axis=1).astype(BF16)

    @pl.when(mode == 2)
    def _():
        o_ref[...] = jnp.zeros(o_ref.shape, BF16)

    @pl.when(mode == 3)
    def _():
        small = jnp.where(lane < SM_BETA, a_ref[...], jnp.where(lane < SM_DT + GDN_HEADS, b_ref[...], 0.0))
        o_ref[...] = small.astype(BF16)


def _pack_w_in(w_in, layer):
    _, d, width = w_in.shape
    plan, shifts = _pack_plan()
    last = (width - 1) // LANES
    grid_spec = pltpu.PrefetchScalarGridSpec(
        num_scalar_prefetch=4, grid=(PROJ_W // LANES,),
        in_specs=[pl.BlockSpec((None, d, LANES), lambda j, m, ta, tb, sh: (layer, 0, ta[j])),
                  pl.BlockSpec((None, d, LANES), lambda j, m, ta, tb, sh: (layer, 0, jnp.minimum(tb[j], last)))],
        out_specs=pl.BlockSpec((d, LANES), lambda j, m, ta, tb, sh: (0, j)))
    return pl.pallas_call(
        functools.partial(_pack_kernel, shifts=shifts), grid_spec=grid_spec,
        out_shape=jax.ShapeDtypeStruct((d, PROJ_W), BF16),
        compiler_params=_cparams(1), name="pack_w_in",
    )(*(jnp.asarray(plan[:, c]) for c in range(4)), w_in, w_in)


def _proj_kernel(x_ref, w_ref, o_ref, xb_ref):
    @pl.when(pl.program_id(1) == 0)
    def _():
        xb_ref[...] = x_ref[...].astype(BF16)

    o_ref[...] = jnp.dot(xb_ref[...], w_ref[...], preferred_element_type=F32)


def _project(x2d, w_packed):
    n, d = x2d.shape
    tm, tn = min(1024, n), 1024
    return pl.pallas_call(
        _proj_kernel, grid=(n // tm, PROJ_W // tn),
        in_specs=[pl.BlockSpec((tm, d), lambda i, j: (i, 0)),
                  pl.BlockSpec((d, tn), lambda i, j: (0, j))],
        out_specs=pl.BlockSpec((tm, tn), lambda i, j: (i, j)),
        out_shape=jax.ShapeDtypeStruct((n, PROJ_W), F32),
        scratch_shapes=[pltpu.VMEM((tm, d), BF16)],
        compiler_params=_cparams(2), name="in_proj",
    )(x2d, w_packed)


def _rope_head(x, cos, sin_signed):
    return x * cos + pltpu.roll(x, HEAD_DIM // 2, 1) * sin_signed


def _rope_idx(x, cos, sin_signed, first_half):
    half = IDX_DIM // 2
    partner = jnp.where(first_half, pltpu.roll(x, LANES - half, 1), pltpu.roll(x, half, 1))
    return x * cos + partner * sin_signed


def _att_prep_kernel(aq_ref, ak_ref, av_ref, iq_ref, sm_ref, cos_a_ref, sin_a_ref, cos_i_ref,
                     sin_i_ref, lng_ref, lnb_ref,
                     q_out, k_out, v_out, iq_out, iklo_out, ikhi_out, iw_out):
    cos_a, sin_a = cos_a_ref[...], sin_a_ref[...]
    cos_i, sin_i = cos_i_ref[...], sin_i_ref[...]
    rows = cos_a.shape[0]
    lane = lax.broadcasted_iota(I32, (rows, LANES), 1)
    first_half = (lane % IDX_DIM) < (IDX_DIM // 2)

    for h in range(ATT_HEADS):
        q_h = _rope_head(aq_ref[:, h * HEAD_DIM:(h + 1) * HEAD_DIM], cos_a, sin_a) * (
            HEAD_DIM ** -0.5 * math.log2(math.e))
        q_out[0, h] = q_h.astype(BF16)
    ones = jnp.ones((rows, HEAD_DIM), BF16)
    for g in range(ATT_KV_HEADS):
        sl = slice(g * HEAD_DIM, (g + 1) * HEAD_DIM)
        k_out[:, sl] = _rope_head(ak_ref[:, sl], cos_a, sin_a).astype(BF16)
        v_out[:, 2 * g * HEAD_DIM:(2 * g + 1) * HEAD_DIM] = av_ref[:, sl].astype(BF16)
        v_out[:, (2 * g + 1) * HEAD_DIM:(2 * g + 2) * HEAD_DIM] = ones
    for j in range(IDX_HEADS // 2):
        iq_out[0, j] = _rope_idx(iq_ref[:, j * LANES:(j + 1) * LANES], cos_i, sin_i, first_half).astype(BF16)

    sm = sm_ref[...]
    in_ik = lane < IDX_DIM
    mu = jnp.sum(jnp.where(in_ik, sm, 0.0), axis=1, keepdims=True) / IDX_DIM
    xc = sm - mu
    var = jnp.sum(jnp.where(in_ik, xc * xc, 0.0), axis=1, keepdims=True) / IDX_DIM
    ik = xc * lax.rsqrt(var + LN_EPS) * lng_ref[...] + lnb_ref[...]
    ik = jnp.where(in_ik, _rope_idx(ik, cos_i, sin_i, first_half), 0.0)
    iklo_out[...] = ik.astype(BF16)
    ikhi_out[...] = pltpu.roll(ik, IDX_DIM, 1).astype(BF16)
    iw_out[...] = (sm * IDX_HEADS ** -0.5) * IDX_DIM ** -0.5


def _att_prep(proj, tabs, idx_ln_g, idx_ln_b):
    n = proj.shape[0]
    tm = min(ATT_ROWS, n)
    cos_a, sin_a, cos_i, sin_i = tabs
    pad = jnp.zeros((LANES - IDX_DIM,), F32)
    lng = jnp.concatenate([idx_ln_g.astype(F32), pad]).reshape(1, LANES)
    lnb = jnp.concatenate([idx_ln_b.astype(F32), pad]).reshape(1, LANES)
    tab = pl.BlockSpec((tm, LANES), lambda i: (i, 0))
    row = pl.BlockSpec((1, LANES), lambda i: (0, 0))

    def col(width, off):
        return pl.BlockSpec((tm, width), lambda i: (i, off // width))

    return pl.pallas_call(
        _att_prep_kernel, grid=(n // tm,),
        in_specs=[col(ATT_Q, OFF_AQ), col(ATT_KV, OFF_AK), col(ATT_KV, OFF_AV), col(IDX_Q, OFF_IQ),
                  col(LANES, OFF_SM), tab, tab, tab, tab, row, row],
        out_specs=[pl.BlockSpec((1, ATT_HEADS, tm, HEAD_DIM), lambda i: (i, 0, 0, 0)),
                   pl.BlockSpec((tm, ATT_KV), lambda i: (i, 0)),
                   pl.BlockSpec((tm, 2 * ATT_KV), lambda i: (i, 0)),
                   pl.BlockSpec((1, IDX_HEADS // 2, tm, LANES), lambda i: (i, 0, 0, 0)),
                   tab, tab, tab],
        out_shape=[jax.ShapeDtypeStruct((n // tm, ATT_HEADS, tm, HEAD_DIM), BF16),
                   jax.ShapeDtypeStruct((n, ATT_KV), BF16),
                   jax.ShapeDtypeStruct((n, 2 * ATT_KV), BF16),
                   jax.ShapeDtypeStruct((n // tm, IDX_HEADS // 2, tm, LANES), BF16),
                   jax.ShapeDtypeStruct((n, LANES), BF16),
                   jax.ShapeDtypeStruct((n, LANES), BF16),
                   jax.ShapeDtypeStruct((n, LANES), F32)],
        compiler_params=_cparams(1), name="att_prep",
    )(proj, proj, proj, proj, proj, cos_a, sin_a, cos_i, sin_i, lng, lnb)


def _dsa_kernel(q_ref, k_ref, v_ref, iq_ref, iklo_ref, ikhi_ref, iw_ref, o_ref,
                keys_sc, acc_sc, m_sc, wrep_sc, sel_sc, *, topk, seq):
    qb, kc = q_ref.shape[2], KEY_CHUNK
    tiles = kc // LANES
    rep = ATT_HEADS // ATT_KV_HEADS
    i = pl.program_id(1)
    nch = (i * qb + qb + kc - 1) // kc
    row = lax.broadcasted_iota(I32, (qb, LANES), 0)
    lane = lax.broadcasted_iota(I32, (qb, LANES), 1)
    t_abs = i * qb + row

    iw = iw_ref[...]
    for h in range(IDX_HEADS):
        wrep_sc[h] = jnp.broadcast_to(iw[:, SM_IW + h:SM_IW + h + 1], (qb, LANES))
    qi = iq_ref[0].reshape((IDX_HEADS // 2) * qb, LANES)

    def score_chunk(c, carry):
        off = pl.multiple_of(c * kc, kc)
        s_lo = _nt(qi, iklo_ref[pl.ds(off, kc), :])
        s_hi = _nt(qi, ikhi_ref[pl.ds(off, kc), :])
        for ct in range(tiles):
            cs = slice(ct * LANES, (ct + 1) * LANES)
            acc = jnp.zeros((qb, LANES), F32)
            for j in range(IDX_HEADS // 2):
                rs = slice(j * qb, (j + 1) * qb)
                acc = acc + wrep_sc[2 * j] * jnp.maximum(s_lo[rs, cs], 0.0)
                acc = acc + wrep_sc[2 * j + 1] * jnp.maximum(s_hi[rs, cs], 0.0)
            bits = lax.bitcast_convert_type(acc, I32)
            key = bits ^ ((bits >> 31) & jnp.int32(0x7FFFFFFF))
            key = jnp.where(bits == jnp.int32(INT_MIN), 0, key)
            causal = (off + ct * LANES + lane) <= t_abs
            keys_sc[c, :, cs] = jnp.where(causal, key, jnp.int32(INT_MIN))
        return carry

    lax.fori_loop(0, nch, score_chunk, 0)

    kth = jnp.float32(topk)

    def count(pred):
        lane_g = lax.broadcasted_iota(I32, (Q_BLOCK, LANES), 1)
        parts = []
        for r0 in range(0, qb, Q_BLOCK):
            rs = slice(r0, r0 + Q_BLOCK)

            def body(c, cnt, rs=rs):
                off = c * kc
                for ct in range(tiles):
                    kt = keys_sc[c, rs, ct * LANES:(ct + 1) * LANES]
                    cnt = cnt + jnp.where(pred(kt, off + ct * LANES + lane_g, rs), 1.0, 0.0)
                return cnt

            parts.append(lax.fori_loop(0, nch, body, jnp.zeros((Q_BLOCK, LANES), F32)))
        return jnp.concatenate([jnp.broadcast_to(jnp.sum(cnt, axis=1, keepdims=True), (Q_BLOCK, LANES))
                                for cnt in parts], axis=0)

    zero = jnp.zeros((qb, LANES), I32)
    n_pos = count(lambda k, s, rs: k >= zero[rs])
    base = jnp.where(n_pos >= kth, zero, jnp.int32(INT_MIN))
    n_base = jnp.where(n_pos >= kth, n_pos, -1.0)
    short = t_abs < topk

    def unresolved(n_base):
        return jnp.max(jnp.where((n_base == kth) | short, 0.0, 1.0)) > 0.0

    def bit_cond(st):
        it, _, n_base = st
        return (it < 31) & unresolved(n_base)

    def bit_step(st):
        it, base, n_base = st
        cand = base | (jnp.int32(1) << (30 - it))
        tot = count(lambda k, s, rs: k >= cand[rs])
        take = tot >= kth
        return it + 1, jnp.where(take, cand, base), jnp.where(take, tot, n_base)

    _, thr, n_ge = lax.while_loop(bit_cond, bit_step, (jnp.int32(0), base, n_base))
    sel_sc[0] = thr
    sel_sc[1] = jnp.full((qb, LANES), seq, I32)

    @pl.when(jnp.max(n_ge) > kth)
    def _():
        nbits = max(1, (seq - 1).bit_length())
        need = kth - count(lambda k, s, rs: k > thr[rs])

        def idx_step(it, posv):
            cand = posv + (jnp.int32(1) << (nbits - 1 - it))
            below = count(lambda k, s, rs: (k == thr[rs]) & (s < cand[rs]))
            return jnp.where(below < need, cand, posv)

        sel_sc[1] = lax.fori_loop(0, nbits, idx_step, zero)

    thr = sel_sc[0]
    last_tie = sel_sc[1]

    m_sc[...] = jnp.full(m_sc.shape, M_INIT, F32)
    acc_sc[...] = jnp.zeros(acc_sc.shape, F32)

    def attend_chunk(c, carry):
        off = pl.multiple_of(c * kc, kc)
        kk = keys_sc[c]
        bias = []
        for ct in range(tiles):
            cs = slice(ct * LANES, (ct + 1) * LANES)
            sidx = off + ct * LANES + lane
            keep = (kk[:, cs] > thr) | ((kk[:, cs] == thr) & (sidx <= last_tie))
            keep = keep & (sidx <= t_abs)
            bias.append(jnp.where(keep, 0.0, MASK_BIAS))
        for g in range(ATT_KV_HEADS):
            kg = k_ref[pl.ds(off, kc), g * HEAD_DIM:(g + 1) * HEAD_DIM]
            vg = v_ref[pl.ds(off, kc), 2 * g * HEAD_DIM:(2 * g + 2) * HEAD_DIM]
            qg = q_ref[0, g * rep:(g + 1) * rep].reshape(rep * qb, HEAD_DIM)
            s = _nt(qg, kg)
            s_t = [(s[:, ct * LANES:(ct + 1) * LANES].reshape(rep, qb, LANES) + bias[ct][None]
                    ).reshape(rep * qb, LANES) for ct in range(tiles)]
            mx = s_t[0]
            for ct in range(1, tiles):
                mx = jnp.maximum(mx, s_t[ct])
            m_old = m_sc[g]
            m_new = jnp.maximum(m_old, jnp.max(mx, axis=1, keepdims=True))
            p = jnp.concatenate([jnp.exp2(st - m_new) for st in s_t], axis=1).astype(BF16)
            alpha = jnp.exp2(m_old - m_new)
            pv = jnp.dot(p, vg, preferred_element_type=F32)
            acc_sc[g] = acc_sc[g] * jnp.concatenate([alpha, alpha], axis=1) + pv
            m_sc[g] = m_new
        return carry

    lax.fori_loop(0, nch, attend_chunk, 0)

    for g in range(ATT_KV_HEADS):
        for r in range(rep):
            a = acc_sc[g, r * qb:(r + 1) * qb, :]
            h = g * rep + r
            o_ref[:, h * HEAD_DIM:(h + 1) * HEAD_DIM] = (a[:, :HEAD_DIM] / a[:, HEAD_DIM:]).astype(BF16)


def _dsa_attention(q_r, k_r, v_ext, iq_r, ik_lo, ik_hi, iw, batch, seq):
    n = batch * seq
    qb = q_r.shape[2]
    nqb = seq // qb
    topk = min(TOPK_MAX, seq // 4)
    rep = ATT_HEADS // ATT_KV_HEADS
    kern = functools.partial(_dsa_kernel, topk=topk, seq=seq)

    def per_batch(width):
        return pl.BlockSpec((None, seq, width), lambda b, i: (b, 0, 0))

    return pl.pallas_call(
        kern, grid=(batch, nqb),
        in_specs=[pl.BlockSpec((1, ATT_HEADS, qb, HEAD_DIM), lambda b, i: (b * nqb + i, 0, 0, 0)),
                  per_batch(ATT_KV), per_batch(2 * ATT_KV),
                  pl.BlockSpec((1, IDX_HEADS // 2, qb, LANES), lambda b, i: (b * nqb + i, 0, 0, 0)),
                  per_batch(LANES), per_batch(LANES),
                  pl.BlockSpec((qb, LANES), lambda b, i: (b * nqb + i, 0))],
        out_specs=pl.BlockSpec((qb, ATT_Q), lambda b, i: (b * nqb + i, 0)),
        out_shape=jax.ShapeDtypeStruct((n, ATT_Q), BF16),
        scratch_shapes=[pltpu.VMEM((seq // KEY_CHUNK, qb, KEY_CHUNK), I32),
                        pltpu.VMEM((ATT_KV_HEADS, rep * qb, 2 * HEAD_DIM), F32),
                        pltpu.VMEM((ATT_KV_HEADS, rep * qb, LANES), F32),
                        pltpu.VMEM((IDX_HEADS, qb, LANES), F32),
                        pltpu.VMEM((2, qb, LANES), I32)],
        compiler_params=_cparams(2), name="dsa_attention",
    )(q_r, k_r.reshape(batch, seq, ATT_KV), v_ext.reshape(batch, seq, 2 * ATT_KV), iq_r,
      ik_lo.reshape(batch, seq, LANES), ik_hi.reshape(batch, seq, LANES), iw)


GDN_PREP_HEADS = 8
GDN_SCAN_CHUNKS = 4
CONV_HALO = 8


def _split2(x):
    hi = x.astype(BF16)
    return hi, (x - hi.astype(F32)).astype(BF16)


def _split3(x):
    hi = x.astype(BF16)
    r = x - hi.astype(F32)
    mid = r.astype(BF16)
    return hi, mid, (r - mid.astype(F32)).astype(BF16)


def _lhs3(hi, lo):
    return jnp.concatenate([hi, lo, hi], axis=1)


def _rhs3(hi, lo):
    return jnp.concatenate([hi, hi, lo], axis=0)


def _gdn_prep_kernel(q_ref, k_ref, v_ref, qp_ref, kp_ref, vp_ref, sm_ref, gat_ref,
                     cwq_ref, cwk_ref, cwv_ref, alog_row_ref, dtb_row_ref, alog_col_ref, dtb_col_ref,
                     u_ref, w_ref, qd_ref, kd_ref, a_ref, e_ref, xx_sc, gct_sc, *, nc):
    hb = GDN_PREP_HEADS
    c = CHUNK
    hg = pl.program_id(1)
    chunk = pl.program_id(0) % nc
    heads = range(hb)

    def conv_silu(slot, cur_ref, prev_ref, cw_ref):
        prev = prev_ref[...]
        at_start = jnp.full(prev.shape, chunk, I32) == 0
        xx_sc[slot, 0:CONV_HALO, :] = jnp.where(at_start, 0.0, prev)
        xx_sc[slot, CONV_HALO:CONV_HALO + c, :] = cur_ref[...]
        cw = cw_ref[...]
        start = CONV_HALO - (CONV_WIDTH - 1)
        acc = xx_sc[slot, pl.ds(start, c), :] * cw[0:1, :]
        for j in range(1, CONV_WIDTH):
            acc = acc + xx_sc[slot, pl.ds(start + j, c), :] * cw[j:j + 1, :]
        return acc * _sigmoid(acc)

    qa = conv_silu(0, q_ref, qp_ref, cwq_ref)
    ka = conv_silu(1, k_ref, kp_ref, cwk_ref)
    va = conv_silu(2, v_ref, vp_ref, cwv_ref)

    rowi = lax.broadcasted_iota(I32, (c, c), 0)
    coli = lax.broadcasted_iota(I32, (c, c), 1)
    lower = rowi >= coli
    strict = rowi > coli
    eye = jnp.where(rowi == coli, 1.0, 0.0)

    sm = sm_ref[...]
    beta_slab = _sigmoid(sm)
    g_slab = -jnp.exp(alog_row_ref[...]) * jax.nn.softplus(sm + dtb_row_ref[...])
    ones_l = jnp.where(lower, 1.0, 0.0).astype(BF16)
    ones_u = jnp.where(rowi <= coli, 1.0, 0.0).astype(BF16)
    gc_slab = jnp.dot(jnp.concatenate([ones_l] * 3, axis=1), jnp.concatenate(_split3(g_slab), axis=0),
                      preferred_element_type=F32)
    g_t = -jnp.exp(alog_col_ref[:, :c]) * jax.nn.softplus(gat_ref[...] + dtb_col_ref[:, :c])
    gct_sc[...] = jnp.dot(jnp.concatenate(_split3(g_t), axis=1), jnp.concatenate([ones_u] * 3, axis=0),
                          preferred_element_type=F32)
    lane = lax.broadcasted_iota(I32, (c, LANES), 1)
    hsl = [slice(r * GDN_DK, (r + 1) * GDN_DK) for r in heads]

    gcc = [jnp.sum(jnp.where(lane == SM_DT + hg * hb + r, gc_slab, 0.0), axis=1, keepdims=True) for r in heads]
    bcol = [jnp.sum(jnp.where(lane == SM_BETA + hg * hb + r, beta_slab, 0.0), axis=1, keepdims=True)
            for r in heads]
    gcr = [gct_sc[pl.ds(hg * hb + r, 1), :] for r in heads]
    glast = [g[:, c - 1:c] for g in gcr]
    qn = [qa[:, s] * lax.rsqrt(jnp.sum(qa[:, s] * qa[:, s], axis=1, keepdims=True) + RMS_EPS) * GDN_DK ** -0.5
          for s in hsl]
    kn = [ka[:, s] * lax.rsqrt(jnp.sum(ka[:, s] * ka[:, s], axis=1, keepdims=True) + RMS_EPS) for s in hsl]
    eg = [jnp.exp(g) for g in gcc]
    decay = [jnp.where(lower, jnp.exp(gcc[r] - gcr[r]), 0.0) for r in heads]
    kb = [kn[r] * bcol[r] for r in heads]

    def nt3(a, b):
        (ah, al), (bh, bl) = _split2(a), _split2(b)
        return _nt(_lhs3(ah, al), jnp.concatenate([bh, bh, bl], axis=1))

    kq = [nt3(jnp.concatenate([kb[r], qn[r]], axis=0), kn[r]) for r in heads]
    pairs = range(hb // 2)
    r2 = lax.broadcasted_iota(I32, (2 * c, 2 * c), 0)
    c2 = lax.broadcasted_iota(I32, (2 * c, 2 * c), 1)
    same_head = (r2 < c) == (c2 < c)

    def diag2(m):
        return jnp.where(same_head, jnp.concatenate([m, m], axis=0), jnp.zeros((), m.dtype))

    def diag_rhs3(hi, lo):
        return _rhs3(diag2(hi), diag2(lo))

    y = [jnp.concatenate([jnp.where(strict, -(kq[2 * p + s][:c] * decay[2 * p + s]), 0.0) for s in range(2)],
                         axis=1) for p in pairs]
    eye2 = jnp.concatenate([eye, eye], axis=1)
    t_inv = [eye2 + y[p] for p in pairs]
    ys = [_split2(v) for v in y]
    y = [jnp.dot(_lhs3(*ys[p]), diag_rhs3(*ys[p]), preferred_element_type=F32) for p in pairs]
    n_fac = int(math.log2(c))
    for j in range(1, n_fac):
        ys = [_split2(v) for v in y]
        ts = [_split2(v) for v in t_inv]
        if j < n_fac - 1:
            prod = [jnp.dot(jnp.concatenate([_lhs3(*ys[p]), _lhs3(*ts[p])], axis=0), diag_rhs3(*ys[p]),
                            preferred_element_type=F32) for p in pairs]
            y = [q[:c] for q in prod]
            t_inv = [t_inv[p] + prod[p][c:] for p in pairs]
        else:
            t_inv = [t_inv[p] + jnp.dot(_lhs3(*ts[p]), diag_rhs3(*ys[p]), preferred_element_type=F32)
                     for p in pairs]
    ts = [_split2(v) for v in t_inv]
    rhs = [jnp.concatenate([jnp.concatenate([va[:, hsl[r]] * bcol[r], kb[r] * eg[r]], axis=1)
                            for r in (2 * p, 2 * p + 1)], axis=0) for p in pairs]
    uw2 = [jnp.dot(_lhs3(diag2(ts[p][0]), diag2(ts[p][1])), _rhs3(*_split2(rhs[p])),
                   preferred_element_type=F32) for p in pairs]
    uw = [uw2[r // 2][(r % 2) * c:(r % 2 + 1) * c] for r in heads]

    for r in heads:
        u_ref[:, hsl[r]] = uw[r][:, :GDN_DV]
        w_ref[:, hsl[r]] = uw[r][:, GDN_DV:].astype(BF16)
        qd_ref[:, hsl[r]] = (qn[r] * eg[r]).astype(BF16)
        kd_ref[:, hsl[r]] = (kn[r] * jnp.exp(glast[r] - gcc[r])).astype(BF16)
        a_ref[:, r * c:(r + 1) * c] = (kq[r][c:] * decay[r]).astype(BF16)
        e_ref[:, hsl[r]] = jnp.broadcast_to(jnp.exp(glast[r]), (CONV_HALO, GDN_DK))


def _gdn_scan_kernel(u_ref, w_ref, qd_ref, kd_ref, a_ref, e_ref, z_ref, nw_ref, o_ref, st_sc):
    c = CHUNK
    heads = range(GDN_HEADS)
    hsl = [slice(h * GDN_DK, (h + 1) * GDN_DK) for h in heads]

    @pl.when(pl.program_id(1) == 0)
    def _():
        st_sc[...] = jnp.zeros(st_sc.shape, F32)

    st = [st_sc[h] for h in heads]
    for sub in range(u_ref.shape[0] // c):
        rows = slice(sub * c, (sub + 1) * c)
        ws_qs = [jnp.dot(jnp.concatenate([w_ref[rows, hsl[h]], qd_ref[rows, hsl[h]]], axis=0), st[h].astype(BF16),
                         preferred_element_type=F32) for h in heads]
        v_new = [(u_ref[rows, hsl[h]] - ws_qs[h][:c]).astype(BF16) for h in heads]
        o = [ws_qs[h][c:] + jnp.dot(a_ref[rows, h * c:(h + 1) * c], v_new[h], preferred_element_type=F32)
             for h in heads]
        e_row = slice(sub * CONV_HALO, sub * CONV_HALO + 1)
        st = [st[h] * e_ref[e_row, hsl[h]] + _tn(kd_ref[rows, hsl[h]], v_new[h]) for h in heads]
        for h in heads:
            z = z_ref[rows, hsl[h]]
            on = o[h] * lax.rsqrt(jnp.mean(o[h] * o[h], axis=1, keepdims=True) + RMS_EPS)
            o_ref[rows, hsl[h]] = (on * nw_ref[...] * (z * _sigmoid(z))).astype(BF16)
    for h in heads:
        st_sc[h] = st[h]


def _gated_deltanet(proj, conv_w, a_log, dt_bias, norm_w, batch, seq):
    n = batch * seq
    nc = seq // CHUNK
    hb = GDN_PREP_HEADS
    w = hb * GDN_DK
    ga_t = proj[:, OFF_SM + SM_DT:OFF_SM + SM_DT + GDN_HEADS].reshape(batch * nc, CHUNK, GDN_HEADS)
    ga_t = ga_t.transpose(0, 2, 1)
    lane_pad = lambda vec, off: jnp.zeros((1, LANES), F32).at[0, off:off + GDN_HEADS].set(vec.astype(F32))
    alog_row, dtb_row = lane_pad(a_log, SM_DT), lane_pad(dt_bias, SM_DT)
    alog_col = jnp.broadcast_to(a_log.astype(F32)[:, None], (GDN_HEADS, LANES))
    dtb_col = jnp.broadcast_to(dt_bias.astype(F32)[:, None], (GDN_HEADS, LANES))
    nw = norm_w.astype(F32).reshape(1, GDN_DV)
    cw = conv_w.astype(F32)

    def cur(off):
        return pl.BlockSpec((CHUNK, w), lambda i, h: (i, off // w + h))

    def prev(off):
        per = CHUNK // CONV_HALO
        return pl.BlockSpec((CONV_HALO, w), lambda i, h: (jnp.maximum(i * per - 1, 0), off // w + h))

    def cwspec(off):
        return pl.BlockSpec((CONV_WIDTH, w), lambda i, h: (0, off // w + h))

    row = pl.BlockSpec((1, LANES), lambda i, h: (0, 0))
    colv = pl.BlockSpec((GDN_HEADS, LANES), lambda i, h: (0, 0))
    head_tile = pl.BlockSpec((CHUNK, w), lambda i, h: (i, h))
    u, wm, qd, kd, a_in, egl = pl.pallas_call(
        functools.partial(_gdn_prep_kernel, nc=nc), grid=(batch * nc, GDN_HEADS // hb),
        in_specs=[cur(OFF_GQ), cur(OFF_GK), cur(OFF_GV), prev(OFF_GQ), prev(OFF_GK), prev(OFF_GV),
                  pl.BlockSpec((CHUNK, LANES), lambda i, h: (i, OFF_SM // LANES)),
                  pl.BlockSpec((None, GDN_HEADS, CHUNK), lambda i, h: (i, 0, 0)),
                  cwspec(0), cwspec(GDN_QK), cwspec(2 * GDN_QK), row, row, colv, colv],
        out_specs=[head_tile, head_tile, head_tile, head_tile,
                   pl.BlockSpec((CHUNK, hb * CHUNK), lambda i, h: (i, h)),
                   pl.BlockSpec((CONV_HALO, w), lambda i, h: (i, h))],
        out_shape=[jax.ShapeDtypeStruct((n, GDN_V), F32), jax.ShapeDtypeStruct((n, GDN_V), BF16),
                   jax.ShapeDtypeStruct((n, GDN_QK), BF16), jax.ShapeDtypeStruct((n, GDN_QK), BF16),
                   jax.ShapeDtypeStruct((n, GDN_HEADS * CHUNK), BF16),
                   jax.ShapeDtypeStruct((batch * nc * CONV_HALO, GDN_V), F32)],
        scratch_shapes=[pltpu.VMEM((3, CONV_HALO + CHUNK, w), F32),
                        pltpu.VMEM((GDN_HEADS, CHUNK), F32)],
        compiler_params=_cparams(2), name="gdn_prep",
    )(proj, proj, proj, proj, proj, proj, proj, ga_t, cw, cw, cw, alog_row, dtb_row, alog_col, dtb_col)

    per = GDN_SCAN_CHUNKS if nc % GDN_SCAN_CHUNKS == 0 else 1
    steps = nc // per

    def full(width):
        return pl.BlockSpec((per * CHUNK, width), lambda b, t: (b * steps + t, 0))

    return pl.pallas_call(
        _gdn_scan_kernel, grid=(batch, steps),
        in_specs=[full(GDN_V), full(GDN_V), full(GDN_QK), full(GDN_QK), full(GDN_HEADS * CHUNK),
                  pl.BlockSpec((per * CONV_HALO, GDN_V), lambda b, t: (b * steps + t, 0)),
                  pl.BlockSpec((per * CHUNK, GDN_V), lambda b, t: (b * steps + t, OFF_GZ // GDN_V)),
                  pl.BlockSpec((1, GDN_DV), lambda b, t: (0, 0))],
        out_specs=full(GDN_V),
        out_shape=jax.ShapeDtypeStruct((n, GDN_V), BF16),
        scratch_shapes=[pltpu.VMEM((GDN_HEADS, GDN_DK, GDN_DV), F32)],
        compiler_params=_cparams(2), name="gdn_scan",
    )(u, wm, qd, kd, a_in, egl, proj, nw)


def _merge_kernel(att_ref, gdn_ref, wa_ref, wg_ref, ga_ref, gb_ref, o_ref):
    a = jnp.dot(att_ref[...], wa_ref[...], preferred_element_type=F32)
    g = jnp.dot(gdn_ref[...], wg_ref[...], preferred_element_type=F32)
    o_ref[...] = (_sigmoid(ga_ref[...]) * a + _sigmoid(gb_ref[...]) * g).astype(BF16)


def _merge(att, gdn, w_o_attn, w_o_gdn, proj):
    n = att.shape[0]
    tm, tn = min(512, n), 1024
    return pl.pallas_call(
        _merge_kernel, grid=(n // tm, D_MODEL // tn),
        in_specs=[pl.BlockSpec((tm, ATT_Q), lambda i, j: (i, 0)),
                  pl.BlockSpec((tm, GDN_V), lambda i, j: (i, 0)),
                  pl.BlockSpec((ATT_Q, tn), lambda i, j: (0, j)),
                  pl.BlockSpec((GDN_V, tn), lambda i, j: (0, j)),
                  pl.BlockSpec((tm, tn), lambda i, j: (i, OFF_GTA // tn + j)),
                  pl.BlockSpec((tm, tn), lambda i, j: (i, OFF_GTB // tn + j))],
        out_specs=pl.BlockSpec((tm, tn), lambda i, j: (i, j)),
        out_shape=jax.ShapeDtypeStruct((n, D_MODEL), BF16),
        compiler_params=_cparams(2), name="mixer_merge",
    )(att, gdn, w_o_attn.astype(BF16), w_o_gdn.astype(BF16), proj, proj)


def _pack_halves(x):
    w = x.shape[1] // 2
    bits = lax.bitcast_convert_type(x.astype(BF16).astype(F32), U32)
    return (bits[:, :w] >> 16) | (bits[:, w:] & jnp.uint32(0xFFFF0000))


def _unpack_halves(u):
    lo = lax.bitcast_convert_type(u << 16, F32)
    hi = lax.bitcast_convert_type(u & jnp.uint32(0xFFFF0000), F32)
    return lo, hi


SLAB = 8


def _store_slabs(ref, lead, u):
    m = u.shape[0]
    for s in range(SLAB):
        ref[lead + (pl.ds(s, m, stride=SLAB), slice(None))] = u[:, s * LANES:(s + 1) * LANES]


def _load_slabs(ref, lead, start, m):
    return jnp.concatenate([ref[lead + (pl.ds(start + s, m, stride=SLAB), slice(None))] for s in range(SLAB)],
                           axis=1)


def _outproj_ln_kernel(m_ref, w_ref, x_ref, g_ref, b_ref, x1_ref, x1p_ref, *, alpha):
    y = jnp.dot(m_ref[...], w_ref[...], preferred_element_type=F32)
    x1 = _layer_norm(alpha * x_ref[...] + y, g_ref[...], b_ref[...])
    x1_ref[...] = x1
    _store_slabs(x1p_ref, (), _pack_halves(x1))


def _outproj_ln(merged, w_out, x2d, ln_g, ln_b, alpha):
    n, d = x2d.shape
    tm = min(256, n)
    row = pl.BlockSpec((1, d), lambda i: (0, 0))
    tile = pl.BlockSpec((tm, d), lambda i: (i, 0))
    return pl.pallas_call(
        functools.partial(_outproj_ln_kernel, alpha=alpha), grid=(n // tm,),
        in_specs=[tile, pl.BlockSpec((d, d), lambda i: (0, 0)), tile, row, row],
        out_specs=[tile, pl.BlockSpec((tm * SLAB, LANES), lambda i: (i, 0))],
        out_shape=[jax.ShapeDtypeStruct((n, d), F32), jax.ShapeDtypeStruct((n * SLAB, LANES), U32)],
        compiler_params=_cparams(1), name="outproj_ln",
    )(merged, w_out.astype(BF16), x2d, ln_g.astype(F32).reshape(1, d), ln_b.astype(F32).reshape(1, d))


def _first_max(x, iota, size):
    m = jnp.max(x, axis=0, keepdims=True)
    f = jnp.min(jnp.where(x == m, iota, float(size)), axis=0, keepdims=True)
    return m, f


def _router_kernel(x_ref, w_ref, b_ref, eidx_ref, gate_ref, rank_ref, cnt_ref, carry_sc):
    tm = x_ref.shape[0]
    per = N_EXPERTS // N_GROUPS
    neg = -jnp.inf

    @pl.when(pl.program_id(0) == 0)
    def _():
        carry_sc[...] = jnp.zeros(carry_sc.shape, F32)

    logits = _nt(w_ref[...], x_ref[...], precision=HI)
    s = _sigmoid(logits)
    choice = s + b_ref[:, 0:1]
    io_g = lax.broadcasted_iota(I32, (per, tm), 0).astype(F32)
    io_e = lax.broadcasted_iota(I32, (N_EXPERTS, tm), 0).astype(F32)

    gs = []
    for g in range(N_GROUPS):
        xg = choice[g * per:(g + 1) * per]
        m1, f1 = _first_max(xg, io_g, per)
        m2 = jnp.max(jnp.where(io_g == f1, neg, xg), axis=0, keepdims=True)
        gs.append(m1 + m2)
    gw = jnp.concatenate(gs, axis=0)
    gsel = jnp.zeros((N_GROUPS, tm), F32)
    for _ in range(TOPK_GROUPS):
        _, f = _first_max(gw, io_g, N_GROUPS)
        hit = io_g == f
        gsel = jnp.where(hit, 1.0, gsel)
        gw = jnp.where(hit, neg, gw)
    x = jnp.concatenate([jnp.where(gsel[g:g + 1] > 0.0, choice[g * per:(g + 1) * per], neg)
                         for g in range(N_GROUPS)], axis=0)

    mem = jnp.zeros((N_EXPERTS, tm), F32)
    picks, wsel = [], []
    for _ in range(TOP_K):
        _, f = _first_max(x, io_e, N_EXPERTS)
        hit = io_e == f
        picks.append(f)
        wsel.append(jnp.sum(jnp.where(hit, s, 0.0), axis=0, keepdims=True))
        mem = jnp.where(hit, 1.0, mem)
        x = jnp.where(hit, neg, x)
    wsel = jnp.concatenate(wsel, axis=0)
    gate_ref[...] = wsel / jnp.sum(wsel, axis=0, keepdims=True) * ROUTED_SCALE
    eidx_ref[...] = jnp.concatenate(picks, axis=0).astype(I32)

    tr = lax.broadcasted_iota(I32, (tm, tm), 0)
    tc = lax.broadcasted_iota(I32, (tm, tm), 1)
    before = jnp.where(tr < tc, 1.0, 0.0).astype(BF16)
    prefix = jnp.dot(mem.astype(BF16), before, preferred_element_type=F32) + carry_sc[:, 0:1]
    ranks = [jnp.sum(jnp.where(io_e == f, prefix, 0.0), axis=0, keepdims=True) for f in picks]
    rank_ref[...] = jnp.concatenate(ranks, axis=0).astype(I32)
    carry_sc[...] = carry_sc[...] + jnp.sum(mem, axis=1, keepdims=True)
    cnt_ref[...] = carry_sc[...]


def _router(x1, w_router, router_bias):
    n, d = x1.shape
    tm = min(512, n)
    out = pl.BlockSpec((TOP_K, tm), lambda i: (0, i))
    bias = jnp.broadcast_to(router_bias.astype(F32)[:, None], (N_EXPERTS, LANES))
    return pl.pallas_call(
        _router_kernel, grid=(n // tm,),
        in_specs=[pl.BlockSpec((tm, d), lambda i: (i, 0)),
                  pl.BlockSpec((N_EXPERTS, d), lambda i: (0, 0)),
                  pl.BlockSpec((N_EXPERTS, LANES), lambda i: (0, 0))],
        out_specs=[out, out, out, pl.BlockSpec((N_EXPERTS, LANES), lambda i: (0, 0))],
        out_shape=[jax.ShapeDtypeStruct((TOP_K, n), I32), jax.ShapeDtypeStruct((TOP_K, n), F32),
                   jax.ShapeDtypeStruct((TOP_K, n), I32), jax.ShapeDtypeStruct((N_EXPERTS, LANES), F32)],
        scratch_shapes=[pltpu.VMEM((N_EXPERTS, LANES), F32)],
        compiler_params=_cparams(1), name="router",
    )(x1, w_router.astype(F32).T, bias)


SCATTER_TOKENS = 512
COMBINE_TOKENS = 256
ROW_COPY_UNROLL = 8


def _pos_kernel(pstart_ref, eidx_ref, rank_ref, pos_ref):
    e = eidx_ref[...]
    acc = rank_ref[...]
    for x in range(N_EXPERTS):
        acc = acc + jnp.where(e == x, pstart_ref[x], 0)
    pos_ref[...] = acc


def _positions(pstart, eidx, rank):
    n = eidx.shape[1]
    tm = min(2048, n)
    tile = pl.BlockSpec((TOP_K, tm), lambda i, ps: (0, i))
    return pl.pallas_call(
        _pos_kernel,
        grid_spec=pltpu.PrefetchScalarGridSpec(num_scalar_prefetch=1, grid=(n // tm,),
                                               in_specs=[tile, tile], out_specs=tile),
        out_shape=jax.ShapeDtypeStruct((TOP_K, n), I32),
        compiler_params=_cparams(1), name="moe_positions",
    )(pstart, eidx, rank)


def _scatter_kernel(pend_ref, padded_ref, pos_ref, x_ref, xs_ref, zero_sc, sem, zsem):
    tm = x_ref.shape[0] // SLAB
    bm = EXPERT_ROWS

    @pl.when(pl.program_id(0) == 0)
    def _():
        zero_sc[...] = jnp.zeros(zero_sc.shape, U32)

        def tail_copy(e):
            tail = pl.multiple_of((pend_ref[e] - bm) * SLAB, bm * SLAB)
            return pltpu.make_async_copy(zero_sc, xs_ref.at[pl.ds(tail, bm * SLAB), :], zsem)

        def zstart(e, carry):
            @pl.when(padded_ref[e] > 0)
            def _():
                tail_copy(e).start()
            return carry

        def zwait(e, carry):
            @pl.when(padded_ref[e] > 0)
            def _():
                tail_copy(e).wait()
            return carry

        lax.fori_loop(0, N_EXPERTS, zstart, 0)
        lax.fori_loop(0, N_EXPERTS, zwait, 0)

    def row_copy(t, k):
        src = x_ref.at[pl.ds(pl.multiple_of(t * SLAB, SLAB), SLAB), :]
        dst = xs_ref.at[pl.ds(pl.multiple_of(pos_ref[k, t] * SLAB, SLAB), SLAB), :]
        return pltpu.make_async_copy(src, dst, sem)

    def start(t, carry):
        for k in range(TOP_K):
            row_copy(t, k).start(priority=k % 2)
        return carry

    lax.fori_loop(0, tm, start, 0, unroll=ROW_COPY_UNROLL)
    rows = xs_ref.at[pl.ds(0, tm * TOP_K * SLAB), :]
    pltpu.make_async_copy(rows, rows, sem).wait()


def _scatter_rows(x1p, pos, pend, padded, n_rows):
    n = x1p.shape[0] // SLAB
    tm = min(SCATTER_TOKENS, n)
    grid_spec = pltpu.PrefetchScalarGridSpec(
        num_scalar_prefetch=2, grid=(n // tm,),
        in_specs=[pl.BlockSpec((TOP_K, tm), lambda i, pe, pd: (0, i), memory_space=pltpu.SMEM),
                  pl.BlockSpec((tm * SLAB, LANES), lambda i, pe, pd: (i, 0))],
        out_specs=pl.BlockSpec(memory_space=pl.ANY),
        scratch_shapes=[pltpu.VMEM((EXPERT_ROWS * SLAB, LANES), U32), pltpu.SemaphoreType.DMA,
                        pltpu.SemaphoreType.DMA])
    return pl.pallas_call(
        _scatter_kernel, grid_spec=grid_spec,
        out_shape=jax.ShapeDtypeStruct((n_rows * SLAB, LANES), U32),
        compiler_params=_cparams(1), name="moe_scatter",
    )(pend, padded, pos, x1p)


def _expert_kernel(blk_e_ref, blk_on_ref, blk_new_ref, x_ref, w1_ref, w3_ref, w2_ref, y_ref,
                   w1_sc, w3_sc, w2_sc):
    del blk_e_ref
    i = pl.program_id(0)

    @pl.when(blk_new_ref[i] > 0)
    def _():
        w1_sc[...] = w1_ref[...].astype(BF16)
        w3_sc[...] = w3_ref[...].astype(BF16)
        w2_sc[...] = w2_ref[...].astype(BF16)

    @pl.when(blk_on_ref[i] > 0)
    def _():
        half = D_MODEL // 2
        lo, hi = _unpack_halves(_load_slabs(x_ref, (), 0, EXPERT_ROWS))
        xa, xb = lo.astype(BF16), hi.astype(BF16)

        def up(w_sc):
            return (jnp.dot(xa, w_sc[0:half, :], preferred_element_type=F32)
                    + jnp.dot(xb, w_sc[half:, :], preferred_element_type=F32))

        h1, h3 = up(w1_sc), up(w3_sc)
        act = (h1 * _sigmoid(h1) * h3).astype(BF16)
        _store_slabs(y_ref, (), _pack_halves(jnp.dot(act, w2_sc[...], preferred_element_type=F32)))


def _experts(xs, blk_e, blk_on, blk_new, w1, w3, w2):
    n_rows, w = xs.shape
    bm = EXPERT_ROWS * SLAB
    grid_spec = pltpu.PrefetchScalarGridSpec(
        num_scalar_prefetch=3, grid=(n_rows // bm,),
        in_specs=[pl.BlockSpec((bm, w), lambda i, e, on, nw: (i, 0)),
                  pl.BlockSpec((None, D_MODEL, EXPERT_DIM), lambda i, e, on, nw: (e[i], 0, 0)),
                  pl.BlockSpec((None, D_MODEL, EXPERT_DIM), lambda i, e, on, nw: (e[i], 0, 0)),
                  pl.BlockSpec((None, EXPERT_DIM, D_MODEL), lambda i, e, on, nw: (e[i], 0, 0))],
        out_specs=pl.BlockSpec((bm, w), lambda i, e, on, nw: (i, 0)),
        scratch_shapes=[pltpu.VMEM((D_MODEL, EXPERT_DIM), BF16), pltpu.VMEM((D_MODEL, EXPERT_DIM), BF16),
                        pltpu.VMEM((EXPERT_DIM, D_MODEL), BF16)])
    return pl.pallas_call(
        _expert_kernel, grid_spec=grid_spec,
        out_shape=jax.ShapeDtypeStruct((n_rows, w), U32),
        compiler_params=_cparams(1), name="moe_experts",
    )(blk_e, blk_on, blk_new, xs, w1, w3, w2)


def _shared_up_kernel(x_ref, w_ref, h_ref):
    h = jnp.dot(x_ref[...].astype(BF16), w_ref[...], preferred_element_type=F32)
    h1, h3 = h[:, :SHARED_DIM], h[:, SHARED_DIM:]
    h_ref[...] = (h1 * _sigmoid(h1) * h3).astype(BF16)


def _shared_up(x1, ws1, ws3):
    n, d = x1.shape
    tm = min(512, n)
    w13 = jnp.concatenate([ws1, ws3], axis=1).astype(BF16)
    return pl.pallas_call(
        _shared_up_kernel, grid=(n // tm,),
        in_specs=[pl.BlockSpec((tm, d), lambda i: (i, 0)),
                  pl.BlockSpec((d, 2 * SHARED_DIM), lambda i: (0, 0))],
        out_specs=pl.BlockSpec((tm, SHARED_DIM), lambda i: (i, 0)),
        out_shape=jax.ShapeDtypeStruct((n, SHARED_DIM), BF16),
        compiler_params=_cparams(1), name="shared_up",
    )(x1, w13)


def _combine_kernel(pos_ref, pos_next_ref, x1_ref, h_ref, ws2_ref, gate_ref, g_ref, b_ref, y_hbm, o_ref,
                    ybuf, sem, *, alpha):
    tm = x1_ref.shape[0]
    i = pl.program_id(0)
    slot = i % 2

    def issue(p_ref, s):
        def start(t, carry):
            for k in range(TOP_K):
                src = y_hbm.at[pl.ds(pl.multiple_of(p_ref[k, t] * SLAB, SLAB), SLAB), :]
                dst = ybuf.at[s, pl.ds(pl.multiple_of((k * tm + t) * SLAB, SLAB), SLAB), :]
                pltpu.make_async_copy(src, dst, sem.at[s]).start(priority=k % 2)
            return carry

        lax.fori_loop(0, tm, start, 0, unroll=ROW_COPY_UNROLL)

    @pl.when(i == 0)
    def _():
        issue(pos_ref, 0)

    @pl.when(i + 1 < pl.num_programs(0))
    def _():
        issue(pos_next_ref, 1 - slot)

    base = alpha * x1_ref[...] + jnp.dot(h_ref[...], ws2_ref[...], preferred_element_type=F32)
    pltpu.make_async_copy(y_hbm.at[pl.ds(0, tm * TOP_K * SLAB), :], ybuf.at[slot], sem.at[slot]).wait()

    gates = gate_ref[...]
    acc_lo = jnp.zeros((tm, D_MODEL // 2), F32)
    acc_hi = jnp.zeros((tm, D_MODEL // 2), F32)
    for k in range(TOP_K):
        lo, hi = _unpack_halves(_load_slabs(ybuf, (slot,), k * tm * SLAB, tm))
        gk = gates[:, k:k + 1]
        acc_lo = acc_lo + gk * lo
        acc_hi = acc_hi + gk * hi
    f = jnp.concatenate([acc_lo, acc_hi], axis=1)
    o_ref[...] = _layer_norm(base + f, g_ref[...], b_ref[...])


def _combine(x1, hsh, ws2, pos, gates_t, ys, ln_g, ln_b, alpha):
    n, d = x1.shape
    tm = min(COMBINE_TOKENS, n)
    row = pl.BlockSpec((1, d), lambda i: (0, 0))
    last = n // tm - 1
    return pl.pallas_call(
        functools.partial(_combine_kernel, alpha=alpha), grid=(n // tm,),
        in_specs=[pl.BlockSpec((TOP_K, tm), lambda i: (0, i), memory_space=pltpu.SMEM),
                  pl.BlockSpec((TOP_K, tm), lambda i: (0, jnp.minimum(i + 1, last)), memory_space=pltpu.SMEM),
                  pl.BlockSpec((tm, d), lambda i: (i, 0)),
                  pl.BlockSpec((tm, SHARED_DIM), lambda i: (i, 0)),
                  pl.BlockSpec((SHARED_DIM, d), lambda i: (0, 0)),
                  pl.BlockSpec((tm, TOP_K), lambda i: (i, 0)),
                  row, row,
                  pl.BlockSpec(memory_space=pl.ANY)],
        out_specs=pl.BlockSpec((tm, d), lambda i: (i, 0)),
        out_shape=jax.ShapeDtypeStruct((n, d), F32),
        scratch_shapes=[pltpu.VMEM((2, TOP_K * tm * SLAB, LANES), U32), pltpu.SemaphoreType.DMA((2,))],
        compiler_params=_cparams(1), name="moe_combine",
    )(pos, pos, x1, hsh, ws2.astype(BF16), gates_t, ln_g.astype(F32).reshape(1, d),
      ln_b.astype(F32).reshape(1, d), ys)


def _moe(x1, x1p, w_router, router_bias, w1, w3, w2, ws1, ws3, ws2, ln_g, ln_b, alpha):
    n = x1.shape[0]
    bm = EXPERT_ROWS
    eidx, gates, rank, cnt = _router(x1, w_router, router_bias)
    counts = cnt[:, 0].astype(I32)
    padded = (counts + bm - 1) // bm * bm
    pend = jnp.cumsum(padded)
    pstart = pend - padded
    pos = _positions(pstart, eidx, rank)
    n_rows = -(-(n * TOP_K) // bm) * bm + N_EXPERTS * bm
    blk_start = jnp.arange(n_rows // bm, dtype=I32) * bm
    blk_on = (blk_start < pend[-1]).astype(I32)
    last = jnp.maximum(pend[-1] - bm, 0)
    blk_e = jnp.sum((pend[None, :] <= jnp.minimum(blk_start, last)[:, None]).astype(I32), axis=1)
    blk_e = jnp.minimum(blk_e, N_EXPERTS - 1)
    changed = jnp.concatenate([jnp.ones((1,), I32), (blk_e[1:] != blk_e[:-1]).astype(I32)])
    blk_new = blk_on * changed
    xs = _scatter_rows(x1p, pos, pend, padded, n_rows)
    ys = _experts(xs, blk_e, blk_on, blk_new, w1, w3, w2)
    hsh = _shared_up(x1, ws1, ws3)
    return _combine(x1, hsh, ws2, pos, gates.T, ys, ln_g, ln_b, alpha)


def kernel(x, positions, w_in, conv_w, A_log, dt_bias, gdn_norm_w, idx_ln_g, idx_ln_b, w_o_attn, w_o_gdn, w_out, ln1_g, ln1_b, w_router, router_bias, w1, w3, w2, ws1, ws3, ws2, ln2_g, ln2_b):
    batch, seq, d = x.shape
    depth = w_in.shape[0]
    alpha = (2 * depth) ** 0.25
    n = batch * seq
    xf = x.reshape(n, d)
    tabs = _trig_tables(positions)
    for l in range(depth):
        proj = _project(xf, _pack_w_in(w_in, l))
        q_r, k_r, v_ext, iq_r, ik_lo, ik_hi, iw = _att_prep(proj, tabs, idx_ln_g[l], idx_ln_b[l])
        att = _dsa_attention(q_r, k_r, v_ext, iq_r, ik_lo, ik_hi, iw, batch, seq)
        gdn = _gated_deltanet(proj, conv_w[l], A_log[l], dt_bias[l], gdn_norm_w[l], batch, seq)
        merged = _merge(att, gdn, w_o_attn[l], w_o_gdn[l], proj)
        x1, x1p = _outproj_ln(merged, w_out[l], xf, ln1_g[l], ln1_b[l], alpha)
        xf = _moe(x1, x1p, w_router[l], router_bias[l], w1[l], w3[l], w2[l], ws1[l], ws3[l], ws2[l],
                  ln2_g[l], ln2_b[l], alpha)
    return xf.reshape(batch, seq, d)
```

```python
import functools
import math

import jax
import jax.numpy as jnp
import numpy as np
from jax import lax
from jax.experimental import pallas as pl
from jax.experimental.pallas import tpu as pltpu

F32 = jnp.float32
BF16 = jnp.bfloat16
I32 = jnp.int32
U32 = jnp.uint32
HI = lax.Precision.HIGHEST

D_MODEL = 2048
ATT_HEADS = 16
ATT_KV_HEADS = 2
HEAD_DIM = 128
IDX_HEADS = 8
IDX_DIM = 64
TOPK_MAX = 256
Q_BLOCK = 128
ROPE_THETA = 10000.0
GDN_HEADS = 16
GDN_DK = 128
GDN_DV = 128
CONV_WIDTH = 4
CHUNK = 64
N_EXPERTS = 64
TOP_K = 8
N_GROUPS = 8
TOPK_GROUPS = 4
EXPERT_DIM = 512
SHARED_DIM = 512
ROUTED_SCALE = 2.5
LN_EPS = 1e-5
RMS_EPS = 1e-6

ATT_Q = ATT_HEADS * HEAD_DIM
ATT_KV = ATT_KV_HEADS * HEAD_DIM
IDX_Q = IDX_HEADS * IDX_DIM
GDN_QK = GDN_HEADS * GDN_DK
GDN_V = GDN_HEADS * GDN_DV
SPLITS = (ATT_Q, ATT_KV, ATT_KV, IDX_Q, IDX_DIM, IDX_HEADS, GDN_QK, GDN_QK, GDN_V, GDN_V,
          GDN_HEADS, GDN_HEADS, D_MODEL, D_MODEL)
SPLIT_OFFSETS = tuple(int(o) for o in np.cumsum(SPLITS)[:-1])

LANES = 128
VMEM_LIMIT_BYTES = 56 * 1024 * 1024

OFF_AQ = 0
OFF_GQ = OFF_AQ + ATT_Q
OFF_GK = OFF_GQ + GDN_QK
OFF_GV = OFF_GK + GDN_QK
OFF_GZ = OFF_GV + GDN_V
OFF_GTA = OFF_GZ + GDN_V
OFF_GTB = OFF_GTA + D_MODEL
OFF_AK = OFF_GTB + D_MODEL
OFF_AV = OFF_AK + ATT_KV
OFF_IQ = OFF_AV + ATT_KV
OFF_SM = OFF_IQ + IDX_Q
PROJ_W = 16384
SM_IK = 0
SM_IW = SM_IK + IDX_DIM
SM_BETA = SM_IW + IDX_HEADS
SM_DT = SM_BETA + GDN_HEADS

KEY_CHUNK = 256
ATT_ROWS = 256
EXPERT_ROWS = 512
INT_MIN = -2147483648
MASK_BIAS = -1e30
M_INIT = -1e29


def _cparams(n_axes):
    return pltpu.CompilerParams(dimension_semantics=("arbitrary",) * n_axes,
                                vmem_limit_bytes=VMEM_LIMIT_BYTES)


def _nt(a, b, precision=None):
    return lax.dot_general(a, b, (((1,), (1,)), ((), ())), precision=precision,
                           preferred_element_type=F32)


def _tn(a, b, precision=None):
    return lax.dot_general(a, b, (((0,), (0,)), ((), ())), precision=precision,
                           preferred_element_type=F32)


def _sigmoid(x):
    return jax.nn.sigmoid(x)


def _layer_norm(x, g, b):
    mu = jnp.mean(x, axis=-1, keepdims=True)
    xc = x - mu
    var = jnp.mean(xc * xc, axis=-1, keepdims=True)
    return xc * lax.rsqrt(var + LN_EPS) * g + b


def _trig_kernel(pos_ref, inv_a_ref, inv_i_ref, sgn_a_ref, sgn_i_ref, cos_a, sin_a, cos_i, sin_i):
    p = pos_ref[...]
    ang_a = p * inv_a_ref[...]
    cos_a[...] = jnp.cos(ang_a)
    sin_a[...] = jnp.sin(ang_a) * sgn_a_ref[...]
    ang_i = p * inv_i_ref[...]
    cos_i[...] = jnp.cos(ang_i)
    sin_i[...] = jnp.sin(ang_i) * sgn_i_ref[...]


def _trig_tables(positions):
    n = positions.size
    tm = min(512, n)
    pos = positions.reshape(n, 1).astype(F32)
    inv_a = ROPE_THETA ** (-jnp.arange(0, HEAD_DIM, 2, dtype=F32) / HEAD_DIM)
    inv_i = ROPE_THETA ** (-jnp.arange(0, IDX_DIM, 2, dtype=F32) / IDX_DIM)
    half_a, half_i = HEAD_DIM // 2, IDX_DIM // 2
    inv_a_row = jnp.tile(inv_a, 2).reshape(1, LANES)
    inv_i_row = jnp.tile(inv_i, 4).reshape(1, LANES)
    lane = np.arange(LANES)
    sgn_a = jnp.asarray(np.where(lane < half_a, -1.0, 1.0), F32).reshape(1, LANES)
    sgn_i = jnp.asarray(np.where(lane % IDX_DIM < half_i, -1.0, 1.0), F32).reshape(1, LANES)
    row = pl.BlockSpec((1, LANES), lambda i: (0, 0))
    tab = pl.BlockSpec((tm, LANES), lambda i: (i, 0))
    return pl.pallas_call(
        _trig_kernel, grid=(n // tm,),
        in_specs=[pl.BlockSpec((tm, 1), lambda i: (i, 0)), row, row, row, row],
        out_specs=[tab, tab, tab, tab],
        out_shape=[jax.ShapeDtypeStruct((n, LANES), F32)] * 4,
        compiler_params=_cparams(1), name="trig_tables",
    )(pos, inv_a_row, inv_i_row, sgn_a, sgn_i)


def _pack_plan():
    src_off = dict(zip(("aq", "ak", "av", "iq", "ik", "iw", "gq", "gk", "gv", "gz", "gb", "ga", "gta", "gtb"),
                       (0,) + SPLIT_OFFSETS))
    order = (("aq", OFF_AQ, ATT_Q), ("gq", OFF_GQ, GDN_QK), ("gk", OFF_GK, GDN_QK), ("gv", OFF_GV, GDN_V),
             ("gz", OFF_GZ, GDN_V), ("gta", OFF_GTA, D_MODEL), ("gtb", OFF_GTB, D_MODEL), ("ak", OFF_AK, ATT_KV),
             ("av", OFF_AV, ATT_KV), ("iq", OFF_IQ, IDX_Q), ("ik", OFF_SM + SM_IK, IDX_DIM),
             ("iw", OFF_SM + SM_IW, IDX_HEADS), ("gb", OFF_SM + SM_BETA, GDN_HEADS), ("ga", OFF_SM + SM_DT, GDN_HEADS))
    src_col = np.full((PROJ_W,), -1, np.int64)
    for name, dst, width in order:
        src_col[dst:dst + width] = src_off[name] + np.arange(width)
    plan, shifts = [], set()
    for j in range(PROJ_W // LANES):
        cols = src_col[j * LANES:(j + 1) * LANES]
        if j == OFF_SM // LANES:
            a, b = src_off["ik"], src_off["gb"] - SM_BETA
            assert a % LANES == 0 and b % LANES == 0
            plan.append((3, a // LANES, b // LANES, 0))
        elif (cols < 0).all():
            plan.append((2, 0, 0, 0))
        else:
            assert (np.diff(cols) == 1).all()
            shift = int(cols[0] % LANES)
            if shift == 0:
                plan.append((0, cols[0] // LANES, cols[0] // LANES, 0))
            else:
                shifts.add(shift)
                plan.append((1, cols[0] // LANES, cols[0] // LANES + 1, shift))
    return np.asarray(plan, np.int32), tuple(sorted(shifts))


def _pack_kernel(mode_ref, ta_ref, tb_ref, shift_ref, a_ref, b_ref, o_ref, *, shifts):
    del ta_ref, tb_ref
    mode = mode_ref[pl.program_id(0)]
    lane = lax.broadcasted_iota(I32, a_ref.shape, 1)

    @pl.when(mode == 0)
    def _():
        o_ref[...] = a_ref[...].astype(BF16)

    for shift in shifts:
        @pl.when((mode == 1) & (shift_ref[pl.program_id(0)] == shift))
        def _(shift=shift):
            o_ref[...] = jnp.concatenate([a_ref[:, shift:], b_ref[:, :shift]], axis=1).astype(BF16)

    @pl.when(mode == 2)
    def _():
        o_ref[...] = jnp.zeros(o_ref.shape, BF16)

    @pl.when(mode == 3)
    def _():
        small = jnp.where(lane < SM_BETA, a_ref[...], jnp.where(lane < SM_DT + GDN_HEADS, b_ref[...], 0.0))
        o_ref[...] = small.astype(BF16)


def _pack_w_in(w_in, layer):
    _, d, width = w_in.shape
    plan, shifts = _pack_plan()
    last = (width - 1) // LANES
    grid_spec = pltpu.PrefetchScalarGridSpec(
        num_scalar_prefetch=4, grid=(PROJ_W // LANES,),
        in_specs=[pl.BlockSpec((None, d, LANES), lambda j, m, ta, tb, sh: (layer, 0, ta[j])),
                  pl.BlockSpec((None, d, LANES), lambda j, m, ta, tb, sh: (layer, 0, jnp.minimum(tb[j], last)))],
        out_specs=pl.BlockSpec((d, LANES), lambda j, m, ta, tb, sh: (0, j)))
    return pl.pallas_call(
        functools.partial(_pack_kernel, shifts=shifts), grid_spec=grid_spec,
        out_shape=jax.ShapeDtypeStruct((d, PROJ_W), BF16),
        compiler_params=_cparams(1), name="pack_w_in",
    )(*(jnp.asarray(plan[:, c]) for c in range(4)), w_in, w_in)


def _proj_kernel(x_ref, w_ref, o_ref, xb_ref):
    @pl.when(pl.program_id(1) == 0)
    def _():
        xb_ref[...] = x_ref[...].astype(BF16)

    o_ref[...] = jnp.dot(xb_ref[...], w_ref[...], preferred_element_type=F32)


def _project(x2d, w_packed):
    n, d = x2d.shape
    tm, tn = min(1024, n), 1024
    return pl.pallas_call(
        _proj_kernel, grid=(n // tm, PROJ_W // tn),
        in_specs=[pl.BlockSpec((tm, d), lambda i, j: (i, 0)),
                  pl.BlockSpec((d, tn), lambda i, j: (0, j))],
        out_specs=pl.BlockSpec((tm, tn), lambda i, j: (i, j)),
        out_shape=jax.ShapeDtypeStruct((n, PROJ_W), F32),
        scratch_shapes=[pltpu.VMEM((tm, d), BF16)],
        compiler_params=_cparams(2), name="in_proj",
    )(x2d, w_packed)


def _rope_head(x, cos, sin_signed):
    return x * cos + pltpu.roll(x, HEAD_DIM // 2, 1) * sin_signed


def _rope_idx(x, cos, sin_signed, first_half):
    half = IDX_DIM // 2
    partner = jnp.where(first_half, pltpu.roll(x, LANES - half, 1), pltpu.roll(x, half, 1))
    return x * cos + partner * sin_signed


def _att_prep_kernel(aq_ref, ak_ref, av_ref, iq_ref, sm_ref, cos_a_ref, sin_a_ref, cos_i_ref,
                     sin_i_ref, lng_ref, lnb_ref,
                     q_out, k_out, v_out, iq_out, iklo_out, ikhi_out, iw_out):
    cos_a, sin_a = cos_a_ref[...], sin_a_ref[...]
    cos_i, sin_i = cos_i_ref[...], sin_i_ref[...]
    rows = cos_a.shape[0]
    lane = lax.broadcasted_iota(I32, (rows, LANES), 1)
    first_half = (lane % IDX_DIM) < (IDX_DIM // 2)

    for h in range(ATT_HEADS):
        q_h = _rope_head(aq_ref[:, h * HEAD_DIM:(h + 1) * HEAD_DIM], cos_a, sin_a) * (
            HEAD_DIM ** -0.5 * math.log2(math.e))
        q_out[0, h] = q_h.astype(BF16)
    ones = jnp.ones((rows, HEAD_DIM), BF16)
    for g in range(ATT_KV_HEADS):
        sl = slice(g * HEAD_DIM, (g + 1) * HEAD_DIM)
        k_out[:, sl] = _rope_head(ak_ref[:, sl], cos_a, sin_a).astype(BF16)
        v_out[:, 2 * g * HEAD_DIM:(2 * g + 1) * HEAD_DIM] = av_ref[:, sl].astype(BF16)
        v_out[:, (2 * g + 1) * HEAD_DIM:(2 * g + 2) * HEAD_DIM] = ones
    for j in range(IDX_HEADS // 2):
        iq_out[0, j] = _rope_idx(iq_ref[:, j * LANES:(j + 1) * LANES], cos_i, sin_i, first_half).astype(BF16)

    sm = sm_ref[...]
    in_ik = lane < IDX_DIM
    mu = jnp.sum(jnp.where(in_ik, sm, 0.0), axis=1, keepdims=True) / IDX_DIM
    xc = sm - mu
    var = jnp.sum(jnp.where(in_ik, xc * xc, 0.0), axis=1, keepdims=True) / IDX_DIM
    ik = xc * lax.rsqrt(var + LN_EPS) * lng_ref[...] + lnb_ref[...]
    ik = jnp.where(in_ik, _rope_idx(ik, cos_i, sin_i, first_half), 0.0)
    iklo_out[...] = ik.astype(BF16)
    ikhi_out[...] = pltpu.roll(ik, IDX_DIM, 1).astype(BF16)
    iw_out[...] = (sm * IDX_HEADS ** -0.5) * IDX_DIM ** -0.5


def _att_prep(proj, tabs, idx_ln_g, idx_ln_b):
    n = proj.shape[0]
    tm = min(ATT_ROWS, n)
    cos_a, sin_a, cos_i, sin_i = tabs
    pad = jnp.zeros((LANES - IDX_DIM,), F32)
    lng = jnp.concatenate([idx_ln_g.astype(F32), pad]).reshape(1, LANES)
    lnb = jnp.concatenate([idx_ln_b.astype(F32), pad]).reshape(1, LANES)
    tab = pl.BlockSpec((tm, LANES), lambda i: (i, 0))
    row = pl.BlockSpec((1, LANES), lambda i: (0, 0))

    def col(width, off):
        return pl.BlockSpec((tm, width), lambda i: (i, off // width))

    return pl.pallas_call(
        _att_prep_kernel, grid=(n // tm,),
        in_specs=[col(ATT_Q, OFF_AQ), col(ATT_KV, OFF_AK), col(ATT_KV, OFF_AV), col(IDX_Q, OFF_IQ),
                  col(LANES, OFF_SM), tab, tab, tab, tab, row, row],
        out_specs=[pl.BlockSpec((1, ATT_HEADS, tm, HEAD_DIM), lambda i: (i, 0, 0, 0)),
                   pl.BlockSpec((tm, ATT_KV), lambda i: (i, 0)),
                   pl.BlockSpec((tm, 2 * ATT_KV), lambda i: (i, 0)),
                   pl.BlockSpec((1, IDX_HEADS // 2, tm, LANES), lambda i: (i, 0, 0, 0)),
                   tab, tab, tab],
        out_shape=[jax.ShapeDtypeStruct((n // tm, ATT_HEADS, tm, HEAD_DIM), BF16),
                   jax.ShapeDtypeStruct((n, ATT_KV), BF16),
                   jax.ShapeDtypeStruct((n, 2 * ATT_KV), BF16),
                   jax.ShapeDtypeStruct((n // tm, IDX_HEADS // 2, tm, LANES), BF16),
                   jax.ShapeDtypeStruct((n, LANES), BF16),
                   jax.ShapeDtypeStruct((n, LANES), BF16),
                   jax.ShapeDtypeStruct((n, LANES), F32)],
        compiler_params=_cparams(1), name="att_prep",
    )(proj, proj, proj, proj, proj, cos_a, sin_a, cos_i, sin_i, lng, lnb)


def _dsa_kernel(q_ref, k_ref, v_ref, iq_ref, iklo_ref, ikhi_ref, iw_ref, o_ref,
                keys_sc, acc_sc, m_sc, wrep_sc, sel_sc, *, topk, seq):
    qb, kc = q_ref.shape[2], KEY_CHUNK
    tiles = kc // LANES
    rep = ATT_HEADS // ATT_KV_HEADS
    i = pl.program_id(1)
    nch = (i * qb + qb + kc - 1) // kc
    row = lax.broadcasted_iota(I32, (qb, LANES), 0)
    lane = lax.broadcasted_iota(I32, (qb, LANES), 1)
    t_abs = i * qb + row

    iw = iw_ref[...]
    for h in range(IDX_HEADS):
        wrep_sc[h] = jnp.broadcast_to(iw[:, SM_IW + h:SM_IW + h + 1], (qb, LANES))
    qi = iq_ref[0].reshape((IDX_HEADS // 2) * qb, LANES)

    def score_chunk(c, carry):
        off = pl.multiple_of(c * kc, kc)
        s_lo = _nt(qi, iklo_ref[pl.ds(off, kc), :])
        s_hi = _nt(qi, ikhi_ref[pl.ds(off, kc), :])
        for ct in range(tiles):
            cs = slice(ct * LANES, (ct + 1) * LANES)
            acc = jnp.zeros((qb, LANES), F32)
            for j in range(IDX_HEADS // 2):
                rs = slice(j * qb, (j + 1) * qb)
                acc = acc + wrep_sc[2 * j] * jnp.maximum(s_lo[rs, cs], 0.0)
                acc = acc + wrep_sc[2 * j + 1] * jnp.maximum(s_hi[rs, cs], 0.0)
            bits = lax.bitcast_convert_type(acc, I32)
            key = bits ^ ((bits >> 31) & jnp.int32(0x7FFFFFFF))
            key = jnp.where(bits == jnp.int32(INT_MIN), 0, key)
            causal = (off + ct * LANES + lane) <= t_abs
            keys_sc[c, :, cs] = jnp.where(causal, key, jnp.int32(INT_MIN))
        return carry

    lax.fori_loop(0, nch, score_chunk, 0)

    kth = jnp.float32(topk)

    def count(pred):
        lane_g = lax.broadcasted_iota(I32, (Q_BLOCK, LANES), 1)
        parts = []
        for r0 in range(0, qb, Q_BLOCK):
            rs = slice(r0, r0 + Q_BLOCK)

            def body(c, cnt, rs=rs):
                off = c * kc
                for ct in range(tiles):
                    kt = keys_sc[c, rs, ct * LANES:(ct + 1) * LANES]
                    cnt = cnt + jnp.where(pred(kt, off + ct * LANES + lane_g, rs), 1.0, 0.0)
                return cnt

            parts.append(lax.fori_loop(0, nch, body, jnp.zeros((Q_BLOCK, LANES), F32)))
        return jnp.concatenate([jnp.broadcast_to(jnp.sum(cnt, axis=1, keepdims=True), (Q_BLOCK, LANES))
                                for cnt in parts], axis=0)

    zero = jnp.zeros((qb, LANES), I32)
    n_pos = count(lambda k, s, rs: k >= zero[rs])
    base = jnp.where(n_pos >= kth, zero, jnp.int32(INT_MIN))
    n_base = jnp.where(n_pos >= kth, n_pos, -1.0)
    short = t_abs < topk

    def unresolved(n_base):
        return jnp.max(jnp.where((n_base == kth) | short, 0.0, 1.0)) > 0.0

    def bit_cond(st):
        it, _, n_base = st
        return (it < 31) & unresolved(n_base)

    def bit_step(st):
        it, base, n_base = st
        cand = base | (jnp.int32(1) << (30 - it))
        tot = count(lambda k, s, rs: k >= cand[rs])
        take = tot >= kth
        return it + 1, jnp.where(take, cand, base), jnp.where(take, tot, n_base)

    _, thr, n_ge = lax.while_loop(bit_cond, bit_step, (jnp.int32(0), base, n_base))
    sel_sc[0] = thr
    sel_sc[1] = jnp.full((qb, LANES), seq, I32)

    @pl.when(jnp.max(n_ge) > kth)
    def _():
        nbits = max(1, (seq - 1).bit_length())
        need = kth - count(lambda k, s, rs: k > thr[rs])

        def idx_step(it, posv):
            cand = posv + (jnp.int32(1) << (nbits - 1 - it))
            below = count(lambda k, s, rs: (k == thr[rs]) & (s < cand[rs]))
            return jnp.where(below < need, cand, posv)

        sel_sc[1] = lax.fori_loop(0, nbits, idx_step, zero)

    thr = sel_sc[0]
    last_tie = sel_sc[1]

    m_sc[...] = jnp.full(m_sc.shape, M_INIT, F32)
    acc_sc[...] = jnp.zeros(acc_sc.shape, F32)

    def attend_chunk(c, carry):
        off = pl.multiple_of(c * kc, kc)
        kk = keys_sc[c]
        bias = []
        for ct in range(tiles):
            cs = slice(ct * LANES, (ct + 1) * LANES)
            sidx = off + ct * LANES + lane
            keep = (kk[:, cs] > thr) | ((kk[:, cs] == thr) & (sidx <= last_tie))
            keep = keep & (sidx <= t_abs)
            bias.append(jnp.where(keep, 0.0, MASK_BIAS))
        for g in range(ATT_KV_HEADS):
            kg = k_ref[pl.ds(off, kc), g * HEAD_DIM:(g + 1) * HEAD_DIM]
            vg = v_ref[pl.ds(off, kc), 2 * g * HEAD_DIM:(2 * g + 2) * HEAD_DIM]
            qg = q_ref[0, g * rep:(g + 1) * rep].reshape(rep * qb, HEAD_DIM)
            s = _nt(qg, kg)
            s_t = [(s[:, ct * LANES:(ct + 1) * LANES].reshape(rep, qb, LANES) + bias[ct][None]
                    ).reshape(rep * qb, LANES) for ct in range(tiles)]
            mx = s_t[0]
            for ct in range(1, tiles):
                mx = jnp.maximum(mx, s_t[ct])
            m_old = m_sc[g]
            m_new = jnp.maximum(m_old, jnp.max(mx, axis=1, keepdims=True))
            p = jnp.concatenate([jnp.exp2(st - m_new) for st in s_t], axis=1).astype(BF16)
            alpha = jnp.exp2(m_old - m_new)
            pv = jnp.dot(p, vg, preferred_element_type=F32)
            acc_sc[g] = acc_sc[g] * jnp.concatenate([alpha, alpha], axis=1) + pv
            m_sc[g] = m_new
        return carry

    lax.fori_loop(0, nch, attend_chunk, 0)

    for g in range(ATT_KV_HEADS):
        for r in range(rep):
            a = acc_sc[g, r * qb:(r + 1) * qb, :]
            h = g * rep + r
            o_ref[:, h * HEAD_DIM:(h + 1) * HEAD_DIM] = (a[:, :HEAD_DIM] / a[:, HEAD_DIM:]).astype(BF16)


def _dsa_attention(q_r, k_r, v_ext, iq_r, ik_lo, ik_hi, iw, batch, seq):
    n = batch * seq
    qb = q_r.shape[2]
    nqb = seq // qb
    topk = min(TOPK_MAX, seq // 4)
    rep = ATT_HEADS // ATT_KV_HEADS
    kern = functools.partial(_dsa_kernel, topk=topk, seq=seq)

    def per_batch(width):
        return pl.BlockSpec((None, seq, width), lambda b, i: (b, 0, 0))

    return pl.pallas_call(
        kern, grid=(batch, nqb),
        in_specs=[pl.BlockSpec((1, ATT_HEADS, qb, HEAD_DIM), lambda b, i: (b * nqb + i, 0, 0, 0)),
                  per_batch(ATT_KV), per_batch(2 * ATT_KV),
                  pl.BlockSpec((1, IDX_HEADS // 2, qb, LANES), lambda b, i: (b * nqb + i, 0, 0, 0)),
                  per_batch(LANES), per_batch(LANES),
                  pl.BlockSpec((qb, LANES), lambda b, i: (b * nqb + i, 0))],
        out_specs=pl.BlockSpec((qb, ATT_Q), lambda b, i: (b * nqb + i, 0)),
        out_shape=jax.ShapeDtypeStruct((n, ATT_Q), BF16),
        scratch_shapes=[pltpu.VMEM((seq // KEY_CHUNK, qb, KEY_CHUNK), I32),
                        pltpu.VMEM((ATT_KV_HEADS, rep * qb, 2 * HEAD_DIM), F32),
                        pltpu.VMEM((ATT_KV_HEADS, rep * qb, LANES), F32),
                        pltpu.VMEM((IDX_HEADS, qb, LANES), F32),
                        pltpu.VMEM((2, qb, LANES), I32)],
        compiler_params=_cparams(2), name="dsa_attention",
    )(q_r, k_r.reshape(batch, seq, ATT_KV), v_ext.reshape(batch, seq, 2 * ATT_KV), iq_r,
      ik_lo.reshape(batch, seq, LANES), ik_hi.reshape(batch, seq, LANES), iw)


GDN_PREP_HEADS = 8
GDN_SCAN_CHUNKS = 8
CONV_HALO = 8


def _split2(x):
    hi = x.astype(BF16)
    return hi, (x - hi.astype(F32)).astype(BF16)


def _split3(x):
    hi = x.astype(BF16)
    r = x - hi.astype(F32)
    mid = r.astype(BF16)
    return hi, mid, (r - mid.astype(F32)).astype(BF16)


def _lhs3(hi, lo):
    return jnp.concatenate([hi, lo, hi], axis=1)


def _rhs3(hi, lo):
    return jnp.concatenate([hi, hi, lo], axis=0)


def _gdn_prep_kernel(q_ref, k_ref, v_ref, qp_ref, kp_ref, vp_ref, sm_ref, gat_ref,
                     cwq_ref, cwk_ref, cwv_ref, alog_row_ref, dtb_row_ref, alog_col_ref, dtb_col_ref,
                     u_ref, w_ref, qd_ref, kd_ref, a_ref, e_ref, xx_sc, gct_sc, *, nc):
    hb = GDN_PREP_HEADS
    c = CHUNK
    hg = pl.program_id(1)
    chunk = pl.program_id(0) % nc
    heads = range(hb)

    def conv_silu(slot, cur_ref, prev_ref, cw_ref):
        prev = prev_ref[...]
        at_start = jnp.full(prev.shape, chunk, I32) == 0
        xx_sc[slot, 0:CONV_HALO, :] = jnp.where(at_start, 0.0, prev)
        xx_sc[slot, CONV_HALO:CONV_HALO + c, :] = cur_ref[...]
        cw = cw_ref[...]
        start = CONV_HALO - (CONV_WIDTH - 1)
        acc = xx_sc[slot, pl.ds(start, c), :] * cw[0:1, :]
        for j in range(1, CONV_WIDTH):
            acc = acc + xx_sc[slot, pl.ds(start + j, c), :] * cw[j:j + 1, :]
        return acc * _sigmoid(acc)

    qa = conv_silu(0, q_ref, qp_ref, cwq_ref)
    ka = conv_silu(1, k_ref, kp_ref, cwk_ref)
    va = conv_silu(2, v_ref, vp_ref, cwv_ref)

    rowi = lax.broadcasted_iota(I32, (c, c), 0)
    coli = lax.broadcasted_iota(I32, (c, c), 1)
    lower = rowi >= coli
    strict = rowi > coli
    eye = jnp.where(rowi == coli, 1.0, 0.0)

    sm = sm_ref[...]
    beta_slab = _sigmoid(sm)
    g_slab = -jnp.exp(alog_row_ref[...]) * jax.nn.softplus(sm + dtb_row_ref[...])
    ones_l = jnp.where(lower, 1.0, 0.0).astype(BF16)
    ones_u = jnp.where(rowi <= coli, 1.0, 0.0).astype(BF16)
    gc_slab = jnp.dot(jnp.concatenate([ones_l] * 3, axis=1), jnp.concatenate(_split3(g_slab), axis=0),
                      preferred_element_type=F32)
    g_t = -jnp.exp(alog_col_ref[:, :c]) * jax.nn.softplus(gat_ref[...] + dtb_col_ref[:, :c])
    gct_sc[...] = jnp.dot(jnp.concatenate(_split3(g_t), axis=1), jnp.concatenate([ones_u] * 3, axis=0),
                          preferred_element_type=F32)
    lane = lax.broadcasted_iota(I32, (c, LANES), 1)
    hsl = [slice(r * GDN_DK, (r + 1) * GDN_DK) for r in heads]

    gcc = [jnp.sum(jnp.where(lane == SM_DT + hg * hb + r, gc_slab, 0.0), axis=1, keepdims=True) for r in heads]
    bcol = [jnp.sum(jnp.where(lane == SM_BETA + hg * hb + r, beta_slab, 0.0), axis=1, keepdims=True)
            for r in heads]
    gcr = [gct_sc[pl.ds(hg * hb + r, 1), :] for r in heads]
    glast = [g[:, c - 1:c] for g in gcr]
    qn = [qa[:, s] * lax.rsqrt(jnp.sum(qa[:, s] * qa[:, s], axis=1, keepdims=True) + RMS_EPS) * GDN_DK ** -0.5
          for s in hsl]
    kn = [ka[:, s] * lax.rsqrt(jnp.sum(ka[:, s] * ka[:, s], axis=1, keepdims=True) + RMS_EPS) for s in hsl]
    eg = [jnp.exp(g) for g in gcc]
    decay = [jnp.where(lower, jnp.exp(gcc[r] - gcr[r]), 0.0) for r in heads]
    kb = [kn[r] * bcol[r] for r in heads]

    def nt3(a, b):
        (ah, al), (bh, bl) = _split2(a), _split2(b)
        return _nt(_lhs3(ah, al), jnp.concatenate([bh, bh, bl], axis=1))

    kq = [nt3(jnp.concatenate([kb[r], qn[r]], axis=0), kn[r]) for r in heads]
    pairs = range(hb // 2)
    r2 = lax.broadcasted_iota(I32, (2 * c, 2 * c), 0)
    c2 = lax.broadcasted_iota(I32, (2 * c, 2 * c), 1)
    same_head = (r2 < c) == (c2 < c)

    def diag2(m):
        return jnp.where(same_head, jnp.concatenate([m, m], axis=0), jnp.zeros((), m.dtype))

    def diag_rhs3(hi, lo):
        return _rhs3(diag2(hi), diag2(lo))

    y = [jnp.concatenate([jnp.where(strict, -(kq[2 * p + s][:c] * decay[2 * p + s]), 0.0) for s in range(2)],
                         axis=1) for p in pairs]
    eye2 = jnp.concatenate([eye, eye], axis=1)
    t_inv = [eye2 + y[p] for p in pairs]
    ys = [_split2(v) for v in y]
    y = [jnp.dot(_lhs3(*ys[p]), diag_rhs3(*ys[p]), preferred_element_type=F32) for p in pairs]
    n_fac = int(math.log2(c))
    for j in range(1, n_fac):
        ys = [_split2(v) for v in y]
        ts = [_split2(v) for v in t_inv]
        if j < n_fac - 1:
            prod = [jnp.dot(jnp.concatenate([_lhs3(*ys[p]), _lhs3(*ts[p])], axis=0), diag_rhs3(*ys[p]),
                            preferred_element_type=F32) for p in pairs]
            y = [q[:c] for q in prod]
            t_inv = [t_inv[p] + prod[p][c:] for p in pairs]
        else:
            t_inv = [t_inv[p] + jnp.dot(_lhs3(*ts[p]), diag_rhs3(*ys[p]), preferred_element_type=F32)
                     for p in pairs]
    ts = [_split2(v) for v in t_inv]
    rhs = [jnp.concatenate([jnp.concatenate([va[:, hsl[r]] * bcol[r], kb[r] * eg[r]], axis=1)
                            for r in (2 * p, 2 * p + 1)], axis=0) for p in pairs]
    uw2 = [jnp.dot(_lhs3(diag2(ts[p][0]), diag2(ts[p][1])), _rhs3(*_split2(rhs[p])),
                   preferred_element_type=F32) for p in pairs]
    uw = [uw2[r // 2][(r % 2) * c:(r % 2 + 1) * c] for r in heads]

    for r in heads:
        u_ref[:, hsl[r]] = uw[r][:, :GDN_DV]
        w_ref[:, hsl[r]] = uw[r][:, GDN_DV:].astype(BF16)
        qd_ref[:, hsl[r]] = (qn[r] * eg[r]).astype(BF16)
        kd_ref[:, hsl[r]] = (kn[r] * jnp.exp(glast[r] - gcc[r])).astype(BF16)
        a_ref[:, r * c:(r + 1) * c] = (kq[r][c:] * decay[r]).astype(BF16)
        e_ref[:, hsl[r]] = jnp.broadcast_to(jnp.exp(glast[r]), (CONV_HALO, GDN_DK))


def _gdn_scan_kernel(u_ref, w_ref, qd_ref, kd_ref, a_ref, e_ref, z_ref, nw_ref, o_ref, st_sc):
    c = CHUNK
    heads = range(GDN_HEADS)
    hsl = [slice(h * GDN_DK, (h + 1) * GDN_DK) for h in heads]

    @pl.when(pl.program_id(1) == 0)
    def _():
        st_sc[...] = jnp.zeros(st_sc.shape, F32)

    st = [st_sc[h] for h in heads]
    for sub in range(u_ref.shape[0] // c):
        rows = slice(sub * c, (sub + 1) * c)
        ws_qs = [jnp.dot(jnp.concatenate([w_ref[rows, hsl[h]], qd_ref[rows, hsl[h]]], axis=0), st[h].astype(BF16),
                         preferred_element_type=F32) for h in heads]
        v_new = [(u_ref[rows, hsl[h]] - ws_qs[h][:c]).astype(BF16) for h in heads]
        o = [ws_qs[h][c:] + jnp.dot(a_ref[rows, h * c:(h + 1) * c], v_new[h], preferred_element_type=F32)
             for h in heads]
        e_row = slice(sub * CONV_HALO, sub * CONV_HALO + 1)
        st = [st[h] * e_ref[e_row, hsl[h]] + _tn(kd_ref[rows, hsl[h]], v_new[h]) for h in heads]
        for h in heads:
            z = z_ref[rows, hsl[h]]
            on = o[h] * lax.rsqrt(jnp.mean(o[h] * o[h], axis=1, keepdims=True) + RMS_EPS)
            o_ref[rows, hsl[h]] = (on * nw_ref[...] * (z * _sigmoid(z))).astype(BF16)
    for h in heads:
        st_sc[h] = st[h]


def _gated_deltanet(proj, conv_w, a_log, dt_bias, norm_w, batch, seq):
    n = batch * seq
    nc = seq // CHUNK
    hb = GDN_PREP_HEADS
    w = hb * GDN_DK
    ga_t = proj[:, OFF_SM + SM_DT:OFF_SM + SM_DT + GDN_HEADS].reshape(batch * nc, CHUNK, GDN_HEADS)
    ga_t = ga_t.transpose(0, 2, 1)
    lane_pad = lambda vec, off: jnp.zeros((1, LANES), F32).at[0, off:off + GDN_HEADS].set(vec.astype(F32))
    alog_row, dtb_row = lane_pad(a_log, SM_DT), lane_pad(dt_bias, SM_DT)
    alog_col = jnp.broadcast_to(a_log.astype(F32)[:, None], (GDN_HEADS, LANES))
    dtb_col = jnp.broadcast_to(dt_bias.astype(F32)[:, None], (GDN_HEADS, LANES))
    nw = norm_w.astype(F32).reshape(1, GDN_DV)
    cw = conv_w.astype(F32)

    def cur(off):
        return pl.BlockSpec((CHUNK, w), lambda i, h: (i, off // w + h))

    def prev(off):
        per = CHUNK // CONV_HALO
        return pl.BlockSpec((CONV_HALO, w), lambda i, h: (jnp.maximum(i * per - 1, 0), off // w + h))

    def cwspec(off):
        return pl.BlockSpec((CONV_WIDTH, w), lambda i, h: (0, off // w + h))

    row = pl.BlockSpec((1, LANES), lambda i, h: (0, 0))
    colv = pl.BlockSpec((GDN_HEADS, LANES), lambda i, h: (0, 0))
    head_tile = pl.BlockSpec((CHUNK, w), lambda i, h: (i, h))
    u, wm, qd, kd, a_in, egl = pl.pallas_call(
        functools.partial(_gdn_prep_kernel, nc=nc), grid=(batch * nc, GDN_HEADS // hb),
        in_specs=[cur(OFF_GQ), cur(OFF_GK), cur(OFF_GV), prev(OFF_GQ), prev(OFF_GK), prev(OFF_GV),
                  pl.BlockSpec((CHUNK, LANES), lambda i, h: (i, OFF_SM // LANES)),
                  pl.BlockSpec((None, GDN_HEADS, CHUNK), lambda i, h: (i, 0, 0)),
                  cwspec(0), cwspec(GDN_QK), cwspec(2 * GDN_QK), row, row, colv, colv],
        out_specs=[head_tile, head_tile, head_tile, head_tile,
                   pl.BlockSpec((CHUNK, hb * CHUNK), lambda i, h: (i, h)),
                   pl.BlockSpec((CONV_HALO, w), lambda i, h: (i, h))],
        out_shape=[jax.ShapeDtypeStruct((n, GDN_V), F32), jax.ShapeDtypeStruct((n, GDN_V), BF16),
                   jax.ShapeDtypeStruct((n, GDN_QK), BF16), jax.ShapeDtypeStruct((n, GDN_QK), BF16),
                   jax.ShapeDtypeStruct((n, GDN_HEADS * CHUNK), BF16),
                   jax.ShapeDtypeStruct((batch * nc * CONV_HALO, GDN_V), F32)],
        scratch_shapes=[pltpu.VMEM((3, CONV_HALO + CHUNK, w), F32),
                        pltpu.VMEM((GDN_HEADS, CHUNK), F32)],
        compiler_params=_cparams(2), name="gdn_prep",
    )(proj, proj, proj, proj, proj, proj, proj, ga_t, cw, cw, cw, alog_row, dtb_row, alog_col, dtb_col)

    per = GDN_SCAN_CHUNKS if nc % GDN_SCAN_CHUNKS == 0 else 1
    steps = nc // per

    def full(width):
        return pl.BlockSpec((per * CHUNK, width), lambda b, t: (b * steps + t, 0))

    return pl.pallas_call(
        _gdn_scan_kernel, grid=(batch, steps),
        in_specs=[full(GDN_V), full(GDN_V), full(GDN_QK), full(GDN_QK), full(GDN_HEADS * CHUNK),
                  pl.BlockSpec((per * CONV_HALO, GDN_V), lambda b, t: (b * steps + t, 0)),
                  pl.BlockSpec((per * CHUNK, GDN_V), lambda b, t: (b * steps + t, OFF_GZ // GDN_V)),
                  pl.BlockSpec((1, GDN_DV), lambda b, t: (0, 0))],
        out_specs=full(GDN_V),
        out_shape=jax.ShapeDtypeStruct((n, GDN_V), BF16),
        scratch_shapes=[pltpu.VMEM((GDN_HEADS, GDN_DK, GDN_DV), F32)],
        compiler_params=_cparams(2), name="gdn_scan",
    )(u, wm, qd, kd, a_in, egl, proj, nw)


def _merge_kernel(att_ref, gdn_ref, wa_ref, wg_ref, ga_ref, gb_ref, o_ref):
    a = jnp.dot(att_ref[...], wa_ref[...], preferred_element_type=F32)
    g = jnp.dot(gdn_ref[...], wg_ref[...], preferred_element_type=F32)
    o_ref[...] = (_sigmoid(ga_ref[...]) * a + _sigmoid(gb_ref[...]) * g).astype(BF16)


def _merge(att, gdn, w_o_attn, w_o_gdn, proj):
    n = att.shape[0]
    tm, tn = min(512, n), 1024
    return pl.pallas_call(
        _merge_kernel, grid=(n // tm, D_MODEL // tn),
        in_specs=[pl.BlockSpec((tm, ATT_Q), lambda i, j: (i, 0)),
                  pl.BlockSpec((tm, GDN_V), lambda i, j: (i, 0)),
                  pl.BlockSpec((ATT_Q, tn), lambda i, j: (0, j)),
                  pl.BlockSpec((GDN_V, tn), lambda i, j: (0, j)),
                  pl.BlockSpec((tm, tn), lambda i, j: (i, OFF_GTA // tn + j)),
                  pl.BlockSpec((tm, tn), lambda i, j: (i, OFF_GTB // tn + j))],
        out_specs=pl.BlockSpec((tm, tn), lambda i, j: (i, j)),
        out_shape=jax.ShapeDtypeStruct((n, D_MODEL), BF16),
        compiler_params=_cparams(2), name="mixer_merge",
    )(att, gdn, w_o_attn.astype(BF16), w_o_gdn.astype(BF16), proj, proj)


def _pack_halves(x):
    w = x.shape[1] // 2
    bits = lax.bitcast_convert_type(x.astype(BF16).astype(F32), U32)
    return (bits[:, :w] >> 16) | (bits[:, w:] & jnp.uint32(0xFFFF0000))


def _unpack_halves(u):
    lo = lax.bitcast_convert_type(u << 16, F32)
    hi = lax.bitcast_convert_type(u & jnp.uint32(0xFFFF0000), F32)
    return lo, hi


SLAB = 8


def _store_slabs(ref, lead, u):
    m = u.shape[0]
    for s in range(SLAB):
        ref[lead + (pl.ds(s, m, stride=SLAB), slice(None))] = u[:, s * LANES:(s + 1) * LANES]


def _load_slabs(ref, lead, start, m):
    return jnp.concatenate([ref[lead + (pl.ds(start + s, m, stride=SLAB), slice(None))] for s in range(SLAB)],
                           axis=1)


def _outproj_ln_kernel(m_ref, w_ref, x_ref, g_ref, b_ref, x1_ref, x1p_ref, *, alpha):
    y = jnp.dot(m_ref[...], w_ref[...], preferred_element_type=F32)
    x1 = _layer_norm(alpha * x_ref[...] + y, g_ref[...], b_ref[...])
    x1_ref[...] = x1
    _store_slabs(x1p_ref, (), _pack_halves(x1))


def _outproj_ln(merged, w_out, x2d, ln_g, ln_b, alpha):
    n, d = x2d.shape
    tm = min(256, n)
    row = pl.BlockSpec((1, d), lambda i: (0, 0))
    tile = pl.BlockSpec((tm, d), lambda i: (i, 0))
    return pl.pallas_call(
        functools.partial(_outproj_ln_kernel, alpha=alpha), grid=(n // tm,),
        in_specs=[tile, pl.BlockSpec((d, d), lambda i: (0, 0)), tile, row, row],
        out_specs=[tile, pl.BlockSpec((tm * SLAB, LANES), lambda i: (i, 0))],
        out_shape=[jax.ShapeDtypeStruct((n, d), F32), jax.ShapeDtypeStruct((n * SLAB, LANES), U32)],
        compiler_params=_cparams(1), name="outproj_ln",
    )(merged, w_out.astype(BF16), x2d, ln_g.astype(F32).reshape(1, d), ln_b.astype(F32).reshape(1, d))


def _first_max(x, iota, size):
    m = jnp.max(x, axis=0, keepdims=True)
    f = jnp.min(jnp.where(x == m, iota, float(size)), axis=0, keepdims=True)
    return m, f


def _router_kernel(x_ref, w_ref, b_ref, eidx_ref, gate_ref, rank_ref, cnt_ref, carry_sc):
    tm = x_ref.shape[0]
    per = N_EXPERTS // N_GROUPS
    neg = -jnp.inf

    @pl.when(pl.program_id(0) == 0)
    def _():
        carry_sc[...] = jnp.zeros(carry_sc.shape, F32)

    logits = _nt(w_ref[...], x_ref[...], precision=HI)
    s = _sigmoid(logits)
    choice = s + b_ref[:, 0:1]
    io_g = lax.broadcasted_iota(I32, (per, tm), 0).astype(F32)
    io_e = lax.broadcasted_iota(I32, (N_EXPERTS, tm), 0).astype(F32)

    gs = []
    for g in range(N_GROUPS):
        xg = choice[g * per:(g + 1) * per]
        m1, f1 = _first_max(xg, io_g, per)
        m2 = jnp.max(jnp.where(io_g == f1, neg, xg), axis=0, keepdims=True)
        gs.append(m1 + m2)
    gw = jnp.concatenate(gs, axis=0)
    gsel = jnp.zeros((N_GROUPS, tm), F32)
    for _ in range(TOPK_GROUPS):
        _, f = _first_max(gw, io_g, N_GROUPS)
        hit = io_g == f
        gsel = jnp.where(hit, 1.0, gsel)
        gw = jnp.where(hit, neg, gw)
    x = jnp.concatenate([jnp.where(gsel[g:g + 1] > 0.0, choice[g * per:(g + 1) * per], neg)
                         for g in range(N_GROUPS)], axis=0)

    mem = jnp.zeros((N_EXPERTS, tm), F32)
    picks, wsel = [], []
    for _ in range(TOP_K):
        _, f = _first_max(x, io_e, N_EXPERTS)
        hit = io_e == f
        picks.append(f)
        wsel.append(jnp.sum(jnp.where(hit, s, 0.0), axis=0, keepdims=True))
        mem = jnp.where(hit, 1.0, mem)
        x = jnp.where(hit, neg, x)
    wsel = jnp.concatenate(wsel, axis=0)
    gate_ref[...] = wsel / jnp.sum(wsel, axis=0, keepdims=True) * ROUTED_SCALE
    eidx_ref[...] = jnp.concatenate(picks, axis=0).astype(I32)

    tr = lax.broadcasted_iota(I32, (tm, tm), 0)
    tc = lax.broadcasted_iota(I32, (tm, tm), 1)
    before = jnp.where(tr < tc, 1.0, 0.0).astype(BF16)
    prefix = jnp.dot(mem.astype(BF16), before, preferred_element_type=F32) + carry_sc[:, 0:1]
    ranks = [jnp.sum(jnp.where(io_e == f, prefix, 0.0), axis=0, keepdims=True) for f in picks]
    rank_ref[...] = jnp.concatenate(ranks, axis=0).astype(I32)
    carry_sc[...] = carry_sc[...] + jnp.sum(mem, axis=1, keepdims=True)
    cnt_ref[...] = carry_sc[...]


def _router(x1, w_router, router_bias):
    n, d = x1.shape
    tm = min(512, n)
    out = pl.BlockSpec((TOP_K, tm), lambda i: (0, i))
    bias = jnp.broadcast_to(router_bias.astype(F32)[:, None], (N_EXPERTS, LANES))
    return pl.pallas_call(
        _router_kernel, grid=(n // tm,),
        in_specs=[pl.BlockSpec((tm, d), lambda i: (i, 0)),
                  pl.BlockSpec((N_EXPERTS, d), lambda i: (0, 0)),
                  pl.BlockSpec((N_EXPERTS, LANES), lambda i: (0, 0))],
        out_specs=[out, out, out, pl.BlockSpec((N_EXPERTS, LANES), lambda i: (0, 0))],
        out_shape=[jax.ShapeDtypeStruct((TOP_K, n), I32), jax.ShapeDtypeStruct((TOP_K, n), F32),
                   jax.ShapeDtypeStruct((TOP_K, n), I32), jax.ShapeDtypeStruct((N_EXPERTS, LANES), F32)],
        scratch_shapes=[pltpu.VMEM((N_EXPERTS, LANES), F32)],
        compiler_params=_cparams(1), name="router",
    )(x1, w_router.astype(F32).T, bias)


SCATTER_TOKENS = 512
COMBINE_TOKENS = 256
ROW_COPY_UNROLL = 8


def _pos_kernel(pstart_ref, eidx_ref, rank_ref, pos_ref):
    e = eidx_ref[...]
    acc = rank_ref[...]
    for x in range(N_EXPERTS):
        acc = acc + jnp.where(e == x, pstart_ref[x], 0)
    pos_ref[...] = acc


def _positions(pstart, eidx, rank):
    n = eidx.shape[1]
    tm = min(2048, n)
    tile = pl.BlockSpec((TOP_K, tm), lambda i, ps: (0, i))
    return pl.pallas_call(
        _pos_kernel,
        grid_spec=pltpu.PrefetchScalarGridSpec(num_scalar_prefetch=1, grid=(n // tm,),
                                               in_specs=[tile, tile], out_specs=tile),
        out_shape=jax.ShapeDtypeStruct((TOP_K, n), I32),
        compiler_params=_cparams(1), name="moe_positions",
    )(pstart, eidx, rank)


def _scatter_kernel(pend_ref, padded_ref, pos_ref, x_ref, xs_ref, zero_sc, sem, zsem):
    tm = x_ref.shape[0] // SLAB
    bm = EXPERT_ROWS

    @pl.when(pl.program_id(0) == 0)
    def _():
        zero_sc[...] = jnp.zeros(zero_sc.shape, U32)

        def tail_copy(e):
            tail = pl.multiple_of((pend_ref[e] - bm) * SLAB, bm * SLAB)
            return pltpu.make_async_copy(zero_sc, xs_ref.at[pl.ds(tail, bm * SLAB), :], zsem)

        def zstart(e, carry):
            @pl.when(padded_ref[e] > 0)
            def _():
                tail_copy(e).start()
            return carry

        def zwait(e, carry):
            @pl.when(padded_ref[e] > 0)
            def _():
                tail_copy(e).wait()
            return carry

        lax.fori_loop(0, N_EXPERTS, zstart, 0)
        lax.fori_loop(0, N_EXPERTS, zwait, 0)

    def row_copy(t, k):
        src = x_ref.at[pl.ds(pl.multiple_of(t * SLAB, SLAB), SLAB), :]
        dst = xs_ref.at[pl.ds(pl.multiple_of(pos_ref[k, t] * SLAB, SLAB), SLAB), :]
        return pltpu.make_async_copy(src, dst, sem)

    def start(t, carry):
        for k in range(TOP_K):
            row_copy(t, k).start(priority=k % 2)
        return carry

    lax.fori_loop(0, tm, start, 0, unroll=ROW_COPY_UNROLL)
    rows = xs_ref.at[pl.ds(0, tm * TOP_K * SLAB), :]
    pltpu.make_async_copy(rows, rows, sem).wait()


def _scatter_rows(x1p, pos, pend, padded, n_rows):
    n = x1p.shape[0] // SLAB
    tm = min(SCATTER_TOKENS, n)
    grid_spec = pltpu.PrefetchScalarGridSpec(
        num_scalar_prefetch=2, grid=(n // tm,),
        in_specs=[pl.BlockSpec((TOP_K, tm), lambda i, pe, pd: (0, i), memory_space=pltpu.SMEM),
                  pl.BlockSpec((tm * SLAB, LANES), lambda i, pe, pd: (i, 0))],
        out_specs=pl.BlockSpec(memory_space=pl.ANY),
        scratch_shapes=[pltpu.VMEM((EXPERT_ROWS * SLAB, LANES), U32), pltpu.SemaphoreType.DMA,
                        pltpu.SemaphoreType.DMA])
    return pl.pallas_call(
        _scatter_kernel, grid_spec=grid_spec,
        out_shape=jax.ShapeDtypeStruct((n_rows * SLAB, LANES), U32),
        compiler_params=_cparams(1), name="moe_scatter",
    )(pend, padded, pos, x1p)


def _expert_kernel(blk_e_ref, blk_on_ref, blk_new_ref, x_ref, w1_ref, w3_ref, w2_ref, y_ref,
                   w1_sc, w3_sc, w2_sc):
    del blk_e_ref
    i = pl.program_id(0)

    def swiglu_block(w1v, w3v, w2v):
        half = D_MODEL // 2
        lo, hi = _unpack_halves(_load_slabs(x_ref, (), 0, EXPERT_ROWS))
        xa, xb = lo.astype(BF16), hi.astype(BF16)

        def up(wv):
            return (jnp.dot(xa, wv[0:half, :], preferred_element_type=F32)
                    + jnp.dot(xb, wv[half:, :], preferred_element_type=F32))

        h1, h3 = up(w1v), up(w3v)
        act = (h1 * _sigmoid(h1) * h3).astype(BF16)
        _store_slabs(y_ref, (), _pack_halves(jnp.dot(act, w2v, preferred_element_type=F32)))

    @pl.when(blk_new_ref[i] > 0)
    def _():
        w1v, w3v, w2v = w1_ref[...].astype(BF16), w3_ref[...].astype(BF16), w2_ref[...].astype(BF16)
        w1_sc[...] = w1v
        w3_sc[...] = w3v
        w2_sc[...] = w2v
        swiglu_block(w1v, w3v, w2v)

    @pl.when((blk_on_ref[i] > 0) & (blk_new_ref[i] == 0))
    def _():
        swiglu_block(w1_sc[...], w3_sc[...], w2_sc[...])


def _experts(xs, blk_e, blk_on, blk_new, w1, w3, w2):
    n_rows, w = xs.shape
    bm = EXPERT_ROWS * SLAB
    grid_spec = pltpu.PrefetchScalarGridSpec(
        num_scalar_prefetch=3, grid=(n_rows // bm,),
        in_specs=[pl.BlockSpec((bm, w), lambda i, e, on, nw: (i, 0)),
                  pl.BlockSpec((None, D_MODEL, EXPERT_DIM), lambda i, e, on, nw: (e[i], 0, 0)),
                  pl.BlockSpec((None, D_MODEL, EXPERT_DIM), lambda i, e, on, nw: (e[i], 0, 0)),
                  pl.BlockSpec((None, EXPERT_DIM, D_MODEL), lambda i, e, on, nw: (e[i], 0, 0))],
        out_specs=pl.BlockSpec((bm, w), lambda i, e, on, nw: (i, 0)),
        scratch_shapes=[pltpu.VMEM((D_MODEL, EXPERT_DIM), BF16), pltpu.VMEM((D_MODEL, EXPERT_DIM), BF16),
                        pltpu.VMEM((EXPERT_DIM, D_MODEL), BF16)])
    return pl.pallas_call(
        _expert_kernel, grid_spec=grid_spec,
        out_shape=jax.ShapeDtypeStruct((n_rows, w), U32),
        compiler_params=_cparams(1), name="moe_experts",
    )(blk_e, blk_on, blk_new, xs, w1, w3, w2)


def _shared_up_kernel(x_ref, w_ref, h_ref):
    h = jnp.dot(x_ref[...].astype(BF16), w_ref[...], preferred_element_type=F32)
    h1, h3 = h[:, :SHARED_DIM], h[:, SHARED_DIM:]
    h_ref[...] = (h1 * _sigmoid(h1) * h3).astype(BF16)


def _shared_up(x1, ws1, ws3):
    n, d = x1.shape
    tm = min(512, n)
    w13 = jnp.concatenate([ws1, ws3], axis=1).astype(BF16)
    return pl.pallas_call(
        _shared_up_kernel, grid=(n // tm,),
        in_specs=[pl.BlockSpec((tm, d), lambda i: (i, 0)),
                  pl.BlockSpec((d, 2 * SHARED_DIM), lambda i: (0, 0))],
        out_specs=pl.BlockSpec((tm, SHARED_DIM), lambda i: (i, 0)),
        out_shape=jax.ShapeDtypeStruct((n, SHARED_DIM), BF16),
        compiler_params=_cparams(1), name="shared_up",
    )(x1, w13)


def _combine_kernel(pos_ref, pos_next_ref, x1_ref, h_ref, ws2_ref, gate_ref, g_ref, b_ref, y_hbm, o_ref,
                    ybuf, sem, *, alpha):
    tm = x1_ref.shape[0]
    i = pl.program_id(0)
    slot = i % 2

    def issue(p_ref, s):
        def start(t, carry):
            for k in range(TOP_K):
                src = y_hbm.at[pl.ds(pl.multiple_of(p_ref[k, t] * SLAB, SLAB), SLAB), :]
                dst = ybuf.at[s, pl.ds(pl.multiple_of((k * tm + t) * SLAB, SLAB), SLAB), :]
                pltpu.make_async_copy(src, dst, sem.at[s]).start(priority=k % 2)
            return carry

        lax.fori_loop(0, tm, start, 0, unroll=ROW_COPY_UNROLL)

    @pl.when(i == 0)
    def _():
        issue(pos_ref, 0)

    @pl.when(i + 1 < pl.num_programs(0))
    def _():
        issue(pos_next_ref, 1 - slot)

    base = alpha * x1_ref[...] + jnp.dot(h_ref[...], ws2_ref[...], preferred_element_type=F32)
    pltpu.make_async_copy(y_hbm.at[pl.ds(0, tm * TOP_K * SLAB), :], ybuf.at[slot], sem.at[slot]).wait()

    gates = gate_ref[...]
    acc_lo = jnp.zeros((tm, D_MODEL // 2), F32)
    acc_hi = jnp.zeros((tm, D_MODEL // 2), F32)
    for k in range(TOP_K):
        lo, hi = _unpack_halves(_load_slabs(ybuf, (slot,), k * tm * SLAB, tm))
        gk = gates[:, k:k + 1]
        acc_lo = acc_lo + gk * lo
        acc_hi = acc_hi + gk * hi
    f = jnp.concatenate([acc_lo, acc_hi], axis=1)
    o_ref[...] = _layer_norm(base + f, g_ref[...], b_ref[...])


def _combine(x1, hsh, ws2, pos, gates_t, ys, ln_g, ln_b, alpha):
    n, d = x1.shape
    tm = min(COMBINE_TOKENS, n)
    row = pl.BlockSpec((1, d), lambda i: (0, 0))
    last = n // tm - 1
    return pl.pallas_call(
        functools.partial(_combine_kernel, alpha=alpha), grid=(n // tm,),
        in_specs=[pl.BlockSpec((TOP_K, tm), lambda i: (0, i), memory_space=pltpu.SMEM),
                  pl.BlockSpec((TOP_K, tm), lambda i: (0, jnp.minimum(i + 1, last)), memory_space=pltpu.SMEM),
                  pl.BlockSpec((tm, d), lambda i: (i, 0)),
                  pl.BlockSpec((tm, SHARED_DIM), lambda i: (i, 0)),
                  pl.BlockSpec((SHARED_DIM, d), lambda i: (0, 0)),
                  pl.BlockSpec((tm, TOP_K), lambda i: (i, 0)),
                  row, row,
                  pl.BlockSpec(memory_space=pl.ANY)],
        out_specs=pl.BlockSpec((tm, d), lambda i: (i, 0)),
        out_shape=jax.ShapeDtypeStruct((n, d), F32),
        scratch_shapes=[pltpu.VMEM((2, TOP_K * tm * SLAB, LANES), U32), pltpu.SemaphoreType.DMA((2,))],
        compiler_params=_cparams(1), name="moe_combine",
    )(pos, pos, x1, hsh, ws2.astype(BF16), gates_t, ln_g.astype(F32).reshape(1, d),
      ln_b.astype(F32).reshape(1, d), ys)


def _moe(x1, x1p, w_router, router_bias, w1, w3, w2, ws1, ws3, ws2, ln_g, ln_b, alpha):
    n = x1.shape[0]
    bm = EXPERT_ROWS
    eidx, gates, rank, cnt = _router(x1, w_router, router_bias)
    counts = cnt[:, 0].astype(I32)
    padded = (counts + bm - 1) // bm * bm
    pend = jnp.cumsum(padded)
    pstart = pend - padded
    pos = _positions(pstart, eidx, rank)
    n_rows = -(-(n * TOP_K) // bm) * bm + N_EXPERTS * bm
    blk_start = jnp.arange(n_rows // bm, dtype=I32) * bm
    blk_on = (blk_start < pend[-1]).astype(I32)
    last = jnp.maximum(pend[-1] - bm, 0)
    blk_e = jnp.sum((pend[None, :] <= jnp.minimum(blk_start, last)[:, None]).astype(I32), axis=1)
    blk_e = jnp.minimum(blk_e, N_EXPERTS - 1)
    changed = jnp.concatenate([jnp.ones((1,), I32), (blk_e[1:] != blk_e[:-1]).astype(I32)])
    blk_new = blk_on * changed
    xs = _scatter_rows(x1p, pos, pend, padded, n_rows)
    ys = _experts(xs, blk_e, blk_on, blk_new, w1, w3, w2)
    hsh = _shared_up(x1, ws1, ws3)
    return _combine(x1, hsh, ws2, pos, gates.T, ys, ln_g, ln_b, alpha)


def kernel(x, positions, w_in, conv_w, A_log, dt_bias, gdn_norm_w, idx_ln_g, idx_ln_b, w_o_attn, w_o_gdn, w_out, ln1_g, ln1_b, w_router, router_bias, w1, w3, w2, ws1, ws3, ws2, ln2_g, ln2_b):
    batch, seq, d = x.shape
    depth = w_in.shape[0]
    alpha = (2 * depth) ** 0.25
    n = batch * seq
    xf = x.reshape(n, d)
    tabs = _trig_tables(positions)
    for l in range(depth):
        proj = _project(xf, _pack_w_in(w_in, l))
        q_r, k_r, v_ext, iq_r, ik_lo, ik_hi, iw = _att_prep(proj, tabs, idx_ln_g[l], idx_ln_b[l])
        att = _dsa_attention(q_r, k_r, v_ext, iq_r, ik_lo, ik_hi, iw, batch, seq)
        gdn = _gated_deltanet(proj, conv_w[l], A_log[l], dt_bias[l], gdn_norm_w[l], batch, seq)
        merged = _merge(att, gdn, w_o_attn[l], w_o_gdn[l], proj)
        x1, x1p = _outproj_ln(merged, w_out[l], xf, ln1_g[l], ln1_b[l], alpha)
        xf = _moe(x1, x1p, w_router[l], router_bias[l], w1[l], w3[l], w2[l], ws1[l], ws3[l], ws2[l],
                  ln2_g[l], ln2_b[l], alpha)
    return xf.reshape(batch, seq, d)
```

```python
import functools
import math

import jax
import jax.numpy as jnp
import numpy as np
from jax import lax
from jax.experimental import pallas as pl
from jax.experimental.pallas import tpu as pltpu

F32 = jnp.float32
BF16 = jnp.bfloat16
I32 = jnp.int32
U32 = jnp.uint32
HI = lax.Precision.HIGHEST

D_MODEL = 2048
ATT_HEADS = 16
ATT_KV_HEADS = 2
HEAD_DIM = 128
IDX_HEADS = 8
IDX_DIM = 64
TOPK_MAX = 256
Q_BLOCK = 128
ROPE_THETA = 10000.0
GDN_HEADS = 16
GDN_DK = 128
GDN_DV = 128
CONV_WIDTH = 4
CHUNK = 64
N_EXPERTS = 64
TOP_K = 8
N_GROUPS = 8
TOPK_GROUPS = 4
EXPERT_DIM = 512
SHARED_DIM = 512
ROUTED_SCALE = 2.5
LN_EPS = 1e-5
RMS_EPS = 1e-6

ATT_Q = ATT_HEADS * HEAD_DIM
ATT_KV = ATT_KV_HEADS * HEAD_DIM
IDX_Q = IDX_HEADS * IDX_DIM
GDN_QK = GDN_HEADS * GDN_DK
GDN_V = GDN_HEADS * GDN_DV
SPLITS = (ATT_Q, ATT_KV, ATT_KV, IDX_Q, IDX_DIM, IDX_HEADS, GDN_QK, GDN_QK, GDN_V, GDN_V,
          GDN_HEADS, GDN_HEADS, D_MODEL, D_MODEL)
SPLIT_OFFSETS = tuple(int(o) for o in np.cumsum(SPLITS)[:-1])

LANES = 128
VMEM_LIMIT_BYTES = 56 * 1024 * 1024

OFF_AQ = 0
OFF_GQ = OFF_AQ + ATT_Q
OFF_GK = OFF_GQ + GDN_QK
OFF_GV = OFF_GK + GDN_QK
OFF_GZ = OFF_GV + GDN_V
OFF_GTA = OFF_GZ + GDN_V
OFF_GTB = OFF_GTA + D_MODEL
OFF_AK = OFF_GTB + D_MODEL
OFF_AV = OFF_AK + ATT_KV
OFF_IQ = OFF_AV + ATT_KV
OFF_SM = OFF_IQ + IDX_Q
PROJ_W = 16384
SM_IK = 0
SM_IW = SM_IK + IDX_DIM
SM_BETA = SM_IW + IDX_HEADS
SM_DT = SM_BETA + GDN_HEADS

KEY_CHUNK = 256
ATT_ROWS = 256
EXPERT_ROWS = 512
INT_MIN = -2147483648
MASK_BIAS = -1e30
M_INIT = -1e29


def _cparams(n_axes):
    return pltpu.CompilerParams(dimension_semantics=("arbitrary",) * n_axes,
                                vmem_limit_bytes=VMEM_LIMIT_BYTES)


def _nt(a, b, precision=None):
    return lax.dot_general(a, b, (((1,), (1,)), ((), ())), precision=precision,
                           preferred_element_type=F32)


def _tn(a, b, precision=None):
    return lax.dot_general(a, b, (((0,), (0,)), ((), ())), precision=precision,
                           preferred_element_type=F32)


def _sigmoid(x):
    return jax.nn.sigmoid(x)


def _layer_norm(x, g, b):
    mu = jnp.mean(x, axis=-1, keepdims=True)
    xc = x - mu
    var = jnp.mean(xc * xc, axis=-1, keepdims=True)
    return xc * lax.rsqrt(var + LN_EPS) * g + b


def _trig_kernel(pos_ref, inv_a_ref, inv_i_ref, sgn_a_ref, sgn_i_ref, cos_a, sin_a, cos_i, sin_i):
    p = pos_ref[...]
    ang_a = p * inv_a_ref[...]
    cos_a[...] = jnp.cos(ang_a)
    sin_a[...] = jnp.sin(ang_a) * sgn_a_ref[...]
    ang_i = p * inv_i_ref[...]
    cos_i[...] = jnp.cos(ang_i)
    sin_i[...] = jnp.sin(ang_i) * sgn_i_ref[...]


def _trig_tables(positions):
    n = positions.size
    tm = min(512, n)
    pos = positions.reshape(n, 1).astype(F32)
    inv_a = ROPE_THETA ** (-jnp.arange(0, HEAD_DIM, 2, dtype=F32) / HEAD_DIM)
    inv_i = ROPE_THETA ** (-jnp.arange(0, IDX_DIM, 2, dtype=F32) / IDX_DIM)
    half_a, half_i = HEAD_DIM // 2, IDX_DIM // 2
    inv_a_row = jnp.tile(inv_a, 2).reshape(1, LANES)
    inv_i_row = jnp.tile(inv_i, 4).reshape(1, LANES)
    lane = np.arange(LANES)
    sgn_a = jnp.asarray(np.where(lane < half_a, -1.0, 1.0), F32).reshape(1, LANES)
    sgn_i = jnp.asarray(np.where(lane % IDX_DIM < half_i, -1.0, 1.0), F32).reshape(1, LANES)
    row = pl.BlockSpec((1, LANES), lambda i: (0, 0))
    tab = pl.BlockSpec((tm, LANES), lambda i: (i, 0))
    return pl.pallas_call(
        _trig_kernel, grid=(n // tm,),
        in_specs=[pl.BlockSpec((tm, 1), lambda i: (i, 0)), row, row, row, row],
        out_specs=[tab, tab, tab, tab],
        out_shape=[jax.ShapeDtypeStruct((n, LANES), F32)] * 4,
        compiler_params=_cparams(1), name="trig_tables",
    )(pos, inv_a_row, inv_i_row, sgn_a, sgn_i)


def _pack_plan():
    src_off = dict(zip(("aq", "ak", "av", "iq", "ik", "iw", "gq", "gk", "gv", "gz", "gb", "ga", "gta", "gtb"),
                       (0,) + SPLIT_OFFSETS))
    order = (("aq", OFF_AQ, ATT_Q), ("gq", OFF_GQ, GDN_QK), ("gk", OFF_GK, GDN_QK), ("gv", OFF_GV, GDN_V),
             ("gz", OFF_GZ, GDN_V), ("gta", OFF_GTA, D_MODEL), ("gtb", OFF_GTB, D_MODEL), ("ak", OFF_AK, ATT_KV),
             ("av", OFF_AV, ATT_KV), ("iq", OFF_IQ, IDX_Q), ("ik", OFF_SM + SM_IK, IDX_DIM),
             ("iw", OFF_SM + SM_IW, IDX_HEADS), ("gb", OFF_SM + SM_BETA, GDN_HEADS), ("ga", OFF_SM + SM_DT, GDN_HEADS))
    src_col = np.full((PROJ_W,), -1, np.int64)
    for name, dst, width in order:
        src_col[dst:dst + width] = src_off[name] + np.arange(width)
    plan, shifts = [], set()
    for j in range(PROJ_W // LANES):
        cols = src_col[j * LANES:(j + 1) * LANES]
        if j == OFF_SM // LANES:
            a, b = src_off["ik"], src_off["gb"] - SM_BETA
            assert a % LANES == 0 and b % LANES == 0
            plan.append((3, a // LANES, b // LANES, 0))
        elif (cols < 0).all():
            plan.append((2, 0, 0, 0))
        else:
            assert (np.diff(cols) == 1).all()
            shift = int(cols[0] % LANES)
            if shift == 0:
                plan.append((0, cols[0] // LANES, cols[0] // LANES, 0))
            else:
                shifts.add(shift)
                plan.append((1, cols[0] // LANES, cols[0] // LANES + 1, shift))
    return np.asarray(plan, np.int32), tuple(sorted(shifts))


def _pack_kernel(mode_ref, ta_ref, tb_ref, shift_ref, a_ref, b_ref, o_ref, *, shifts):
    del ta_ref, tb_ref
    mode = mode_ref[pl.program_id(0)]
    lane = lax.broadcasted_iota(I32, a_ref.shape, 1)

    @pl.when(mode == 0)
    def _():
        o_ref[...] = a_ref[...].astype(BF16)

    for shift in shifts:
        @pl.when((mode == 1) & (shift_ref[pl.program_id(0)] == shift))
        def _(shift=shift):
            o_ref[...] = jnp.concatenate([a_ref[:, shift:], b_ref[:, :shift]], axis=1).astype(BF16)

    @pl.when(mode == 2)
    def _():
        o_ref[...] = jnp.zeros(o_ref.shape, BF16)

    @pl.when(mode == 3)
    def _():
        small = jnp.where(lane < SM_BETA, a_ref[...], jnp.where(lane < SM_DT + GDN_HEADS, b_ref[...], 0.0))
        o_ref[...] = small.astype(BF16)


def _pack_w_in(w_in, layer):
    _, d, width = w_in.shape
    plan, shifts = _pack_plan()
    last = (width - 1) // LANES
    grid_spec = pltpu.PrefetchScalarGridSpec(
        num_scalar_prefetch=4, grid=(PROJ_W // LANES,),
        in_specs=[pl.BlockSpec((None, d, LANES), lambda j, m, ta, tb, sh: (layer, 0, ta[j])),
                  pl.BlockSpec((None, d, LANES), lambda j, m, ta, tb, sh: (layer, 0, jnp.minimum(tb[j], last)))],
        out_specs=pl.BlockSpec((d, LANES), lambda j, m, ta, tb, sh: (0, j)))
    return pl.pallas_call(
        functools.partial(_pack_kernel, shifts=shifts), grid_spec=grid_spec,
        out_shape=jax.ShapeDtypeStruct((d, PROJ_W), BF16),
        compiler_params=_cparams(1), name="pack_w_in",
    )(*(jnp.asarray(plan[:, c]) for c in range(4)), w_in, w_in)


def _proj_kernel(x_ref, w_ref, o_ref, xb_ref):
    @pl.when(pl.program_id(1) == 0)
    def _():
        xb_ref[...] = x_ref[...].astype(BF16)

    o_ref[...] = jnp.dot(xb_ref[...], w_ref[...], preferred_element_type=F32)


def _project(x2d, w_packed):
    n, d = x2d.shape
    tm, tn = min(1024, n), 1024
    return pl.pallas_call(
        _proj_kernel, grid=(n // tm, PROJ_W // tn),
        in_specs=[pl.BlockSpec((tm, d), lambda i, j: (i, 0)),
                  pl.BlockSpec((d, tn), lambda i, j: (0, j))],
        out_specs=pl.BlockSpec((tm, tn), lambda i, j: (i, j)),
        out_shape=jax.ShapeDtypeStruct((n, PROJ_W), F32),
        scratch_shapes=[pltpu.VMEM((tm, d), BF16)],
        compiler_params=_cparams(2), name="in_proj",
    )(x2d, w_packed)


def _rope_head(x, cos, sin_signed):
    return x * cos + pltpu.roll(x, HEAD_DIM // 2, 1) * sin_signed


def _rope_idx(x, cos, sin_signed, first_half):
    half = IDX_DIM // 2
    partner = jnp.where(first_half, pltpu.roll(x, LANES - half, 1), pltpu.roll(x, half, 1))
    return x * cos + partner * sin_signed


def _att_prep_kernel(aq_ref, ak_ref, av_ref, iq_ref, sm_ref, cos_a_ref, sin_a_ref, cos_i_ref,
                     sin_i_ref, lng_ref, lnb_ref,
                     q_out, k_out, v_out, iq_out, iklo_out, ikhi_out, iw_out):
    cos_a, sin_a = cos_a_ref[...], sin_a_ref[...]
    cos_i, sin_i = cos_i_ref[...], sin_i_ref[...]
    rows = cos_a.shape[0]
    lane = lax.broadcasted_iota(I32, (rows, LANES), 1)
    first_half = (lane % IDX_DIM) < (IDX_DIM // 2)

    for h in range(ATT_HEADS):
        q_h = _rope_head(aq_ref[:, h * HEAD_DIM:(h + 1) * HEAD_DIM], cos_a, sin_a) * (
            HEAD_DIM ** -0.5 * math.log2(math.e))
        q_out[0, h] = q_h.astype(BF16)
    ones = jnp.ones((rows, HEAD_DIM), BF16)
    for g in range(ATT_KV_HEADS):
        sl = slice(g * HEAD_DIM, (g + 1) * HEAD_DIM)
        k_out[:, sl] = _rope_head(ak_ref[:, sl], cos_a, sin_a).astype(BF16)
        v_out[:, 2 * g * HEAD_DIM:(2 * g + 1) * HEAD_DIM] = av_ref[:, sl].astype(BF16)
        v_out[:, (2 * g + 1) * HEAD_DIM:(2 * g + 2) * HEAD_DIM] = ones
    for j in range(IDX_HEADS // 2):
        iq_out[0, j] = _rope_idx(iq_ref[:, j * LANES:(j + 1) * LANES], cos_i, sin_i, first_half).astype(BF16)

    sm = sm_ref[...]
    in_ik = lane < IDX_DIM
    mu = jnp.sum(jnp.where(in_ik, sm, 0.0), axis=1, keepdims=True) / IDX_DIM
    xc = sm - mu
    var = jnp.sum(jnp.where(in_ik, xc * xc, 0.0), axis=1, keepdims=True) / IDX_DIM
    ik = xc * lax.rsqrt(var + LN_EPS) * lng_ref[...] + lnb_ref[...]
    ik = jnp.where(in_ik, _rope_idx(ik, cos_i, sin_i, first_half), 0.0)
    iklo_out[...] = ik.astype(BF16)
    ikhi_out[...] = pltpu.roll(ik, IDX_DIM, 1).astype(BF16)
    iw_out[...] = (sm * IDX_HEADS ** -0.5) * IDX_DIM ** -0.5


def _att_prep(proj, tabs, idx_ln_g, idx_ln_b):
    n = proj.shape[0]
    tm = min(ATT_ROWS, n)
    cos_a, sin_a, cos_i, sin_i = tabs
    pad = jnp.zeros((LANES - IDX_DIM,), F32)
    lng = jnp.concatenate([idx_ln_g.astype(F32), pad]).reshape(1, LANES)
    lnb = jnp.concatenate([idx_ln_b.astype(F32), pad]).reshape(1, LANES)
    tab = pl.BlockSpec((tm, LANES), lambda i: (i, 0))
    row = pl.BlockSpec((1, LANES), lambda i: (0, 0))

    def col(width, off):
        return pl.BlockSpec((tm, width), lambda i: (i, off // width))

    return pl.pallas_call(
        _att_prep_kernel, grid=(n // tm,),
        in_specs=[col(ATT_Q, OFF_AQ), col(ATT_KV, OFF_AK), col(ATT_KV, OFF_AV), col(IDX_Q, OFF_IQ),
                  col(LANES, OFF_SM), tab, tab, tab, tab, row, row],
        out_specs=[pl.BlockSpec((1, ATT_HEADS, tm, HEAD_DIM), lambda i: (i, 0, 0, 0)),
                   pl.BlockSpec((tm, ATT_KV), lambda i: (i, 0)),
                   pl.BlockSpec((tm, 2 * ATT_KV), lambda i: (i, 0)),
                   pl.BlockSpec((1, IDX_HEADS // 2, tm, LANES), lambda i: (i, 0, 0, 0)),
                   tab, tab, tab],
        out_shape=[jax.ShapeDtypeStruct((n // tm, ATT_HEADS, tm, HEAD_DIM), BF16),
                   jax.ShapeDtypeStruct((n, ATT_KV), BF16),
                   jax.ShapeDtypeStruct((n, 2 * ATT_KV), BF16),
                   jax.ShapeDtypeStruct((n // tm, IDX_HEADS // 2, tm, LANES), BF16),
                   jax.ShapeDtypeStruct((n, LANES), BF16),
                   jax.ShapeDtypeStruct((n, LANES), BF16),
                   jax.ShapeDtypeStruct((n, LANES), F32)],
        compiler_params=_cparams(1), name="att_prep",
    )(proj, proj, proj, proj, proj, cos_a, sin_a, cos_i, sin_i, lng, lnb)


def _dsa_kernel(q_ref, k_ref, v_ref, iq_ref, iklo_ref, ikhi_ref, iw_ref, o_ref,
                keys_sc, acc_sc, m_sc, wrep_sc, sel_sc, *, topk, seq):
    qb, kc = q_ref.shape[2], KEY_CHUNK
    tiles = kc // LANES
    rep = ATT_HEADS // ATT_KV_HEADS
    i = pl.program_id(1)
    nch = (i * qb + qb + kc - 1) // kc
    row = lax.broadcasted_iota(I32, (qb, LANES), 0)
    lane = lax.broadcasted_iota(I32, (qb, LANES), 1)
    t_abs = i * qb + row

    iw = iw_ref[...]
    for h in range(IDX_HEADS):
        wrep_sc[h] = jnp.broadcast_to(iw[:, SM_IW + h:SM_IW + h + 1], (qb, LANES))
    qi = iq_ref[0].reshape((IDX_HEADS // 2) * qb, LANES)

    def score_chunk(c, carry):
        off = pl.multiple_of(c * kc, kc)
        s_lo = _nt(qi, iklo_ref[pl.ds(off, kc), :])
        s_hi = _nt(qi, ikhi_ref[pl.ds(off, kc), :])
        for ct in range(tiles):
            cs = slice(ct * LANES, (ct + 1) * LANES)
            acc = jnp.zeros((qb, LANES), F32)
            for j in range(IDX_HEADS // 2):
                rs = slice(j * qb, (j + 1) * qb)
                acc = acc + wrep_sc[2 * j] * jnp.maximum(s_lo[rs, cs], 0.0)
                acc = acc + wrep_sc[2 * j + 1] * jnp.maximum(s_hi[rs, cs], 0.0)
            bits = lax.bitcast_convert_type(acc, I32)
            key = bits ^ ((bits >> 31) & jnp.int32(0x7FFFFFFF))
            key = jnp.where(bits == jnp.int32(INT_MIN), 0, key)
            causal = (off + ct * LANES + lane) <= t_abs
            keys_sc[c, :, cs] = jnp.where(causal, key, jnp.int32(INT_MIN))
        return carry

    lax.fori_loop(0, nch, score_chunk, 0)

    kth = jnp.float32(topk)

    def count(pred):
        lane_g = lax.broadcasted_iota(I32, (Q_BLOCK, LANES), 1)
        parts = []
        for r0 in range(0, qb, Q_BLOCK):
            rs = slice(r0, r0 + Q_BLOCK)

            def body(c, cnt, rs=rs):
                off = c * kc
                for ct in range(tiles):
                    kt = keys_sc[c, rs, ct * LANES:(ct + 1) * LANES]
                    cnt = cnt + jnp.where(pred(kt, off + ct * LANES + lane_g, rs), 1.0, 0.0)
                return cnt

            parts.append(lax.fori_loop(0, nch, body, jnp.zeros((Q_BLOCK, LANES), F32)))
        return jnp.concatenate([jnp.broadcast_to(jnp.sum(cnt, axis=1, keepdims=True), (Q_BLOCK, LANES))
                                for cnt in parts], axis=0)

    zero = jnp.zeros((qb, LANES), I32)
    n_pos = count(lambda k, s, rs: k >= zero[rs])
    base = jnp.where(n_pos >= kth, zero, jnp.int32(INT_MIN))
    n_base = jnp.where(n_pos >= kth, n_pos, -1.0)
    short = t_abs < topk

    def unresolved(n_base):
        return jnp.max(jnp.where((n_base == kth) | short, 0.0, 1.0)) > 0.0

    def bit_cond(st):
        it, _, n_base = st
        return (it < 31) & unresolved(n_base)

    def bit_step(st):
        it, base, n_base = st
        cand = base | (jnp.int32(1) << (30 - it))
        tot = count(lambda k, s, rs: k >= cand[rs])
        take = tot >= kth
        return it + 1, jnp.where(take, cand, base), jnp.where(take, tot, n_base)

    _, thr, n_ge = lax.while_loop(bit_cond, bit_step, (jnp.int32(0), base, n_base))
    sel_sc[0] = thr
    sel_sc[1] = jnp.full((qb, LANES), seq, I32)

    @pl.when(jnp.max(n_ge) > kth)
    def _():
        nbits = max(1, (seq - 1).bit_length())
        need = kth - count(lambda k, s, rs: k > thr[rs])

        def idx_step(it, posv):
            cand = posv + (jnp.int32(1) << (nbits - 1 - it))
            below = count(lambda k, s, rs: (k == thr[rs]) & (s < cand[rs]))
            return jnp.where(below < need, cand, posv)

        sel_sc[1] = lax.fori_loop(0, nbits, idx_step, zero)

    thr = sel_sc[0]
    last_tie = sel_sc[1]

    m_sc[...] = jnp.full(m_sc.shape, M_INIT, F32)
    acc_sc[...] = jnp.zeros(acc_sc.shape, F32)

    def attend_chunk(c, carry):
        off = pl.multiple_of(c * kc, kc)
        kk = keys_sc[c]
        bias = []
        for ct in range(tiles):
            cs = slice(ct * LANES, (ct + 1) * LANES)
            sidx = off + ct * LANES + lane
            keep = (kk[:, cs] > thr) | ((kk[:, cs] == thr) & (sidx <= last_tie))
            keep = keep & (sidx <= t_abs)
            bias.append(jnp.where(keep, 0.0, MASK_BIAS))
        for g in range(ATT_KV_HEADS):
            kg = k_ref[pl.ds(off, kc), g * HEAD_DIM:(g + 1) * HEAD_DIM]
            vg = v_ref[pl.ds(off, kc), 2 * g * HEAD_DIM:(2 * g + 2) * HEAD_DIM]
            qg = q_ref[0, g * rep:(g + 1) * rep].reshape(rep * qb, HEAD_DIM)
            s = _nt(qg, kg)
            s_t = [(s[:, ct * LANES:(ct + 1) * LANES].reshape(rep, qb, LANES) + bias[ct][None]
                    ).reshape(rep * qb, LANES) for ct in range(tiles)]
            mx = s_t[0]
            for ct in range(1, tiles):
                mx = jnp.maximum(mx, s_t[ct])
            m_old = m_sc[g]
            m_new = jnp.maximum(m_old, jnp.max(mx, axis=1, keepdims=True))
            p = jnp.concatenate([jnp.exp2(st - m_new) for st in s_t], axis=1).astype(BF16)
            alpha = jnp.exp2(m_old - m_new)
            pv = jnp.dot(p, vg, preferred_element_type=F32)
            acc_sc[g] = acc_sc[g] * jnp.concatenate([alpha, alpha], axis=1) + pv
            m_sc[g] = m_new
        return carry

    lax.fori_loop(0, nch, attend_chunk, 0)

    for g in range(ATT_KV_HEADS):
        for r in range(rep):
            a = acc_sc[g, r * qb:(r + 1) * qb, :]
            h = g * rep + r
            o_ref[:, h * HEAD_DIM:(h + 1) * HEAD_DIM] = (a[:, :HEAD_DIM] / a[:, HEAD_DIM:]).astype(BF16)


def _dsa_attention(q_r, k_r, v_ext, iq_r, ik_lo, ik_hi, iw, batch, seq):
    n = batch * seq
    qb = q_r.shape[2]
    nqb = seq // qb
    topk = min(TOPK_MAX, seq // 4)
    rep = ATT_HEADS // ATT_KV_HEADS
    kern = functools.partial(_dsa_kernel, topk=topk, seq=seq)

    def per_batch(width):
        return pl.BlockSpec((None, seq, width), lambda b, i: (b, 0, 0))

    return pl.pallas_call(
        kern, grid=(batch, nqb),
        in_specs=[pl.BlockSpec((1, ATT_HEADS, qb, HEAD_DIM), lambda b, i: (b * nqb + i, 0, 0, 0)),
                  per_batch(ATT_KV), per_batch(2 * ATT_KV),
                  pl.BlockSpec((1, IDX_HEADS // 2, qb, LANES), lambda b, i: (b * nqb + i, 0, 0, 0)),
                  per_batch(LANES), per_batch(LANES),
                  pl.BlockSpec((qb, LANES), lambda b, i: (b * nqb + i, 0))],
        out_specs=pl.BlockSpec((qb, ATT_Q), lambda b, i: (b * nqb + i, 0)),
        out_shape=jax.ShapeDtypeStruct((n, ATT_Q), BF16),
        scratch_shapes=[pltpu.VMEM((seq // KEY_CHUNK, qb, KEY_CHUNK), I32),
                        pltpu.VMEM((ATT_KV_HEADS, rep * qb, 2 * HEAD_DIM), F32),
                        pltpu.VMEM((ATT_KV_HEADS, rep * qb, LANES), F32),
                        pltpu.VMEM((IDX_HEADS, qb, LANES), F32),
                        pltpu.VMEM((2, qb, LANES), I32)],
        compiler_params=_cparams(2), name="dsa_attention",
    )(q_r, k_r.reshape(batch, seq, ATT_KV), v_ext.reshape(batch, seq, 2 * ATT_KV), iq_r,
      ik_lo.reshape(batch, seq, LANES), ik_hi.reshape(batch, seq, LANES), iw)


GDN_PREP_HEADS = 16
GDN_SCAN_CHUNKS = 8
CONV_HALO = 8


def _split2(x):
    hi = x.astype(BF16)
    return hi, (x - hi.astype(F32)).astype(BF16)


def _split3(x):
    hi = x.astype(BF16)
    r = x - hi.astype(F32)
    mid = r.astype(BF16)
    return hi, mid, (r - mid.astype(F32)).astype(BF16)


def _lhs3(hi, lo):
    return jnp.concatenate([hi, lo, hi], axis=1)


def _rhs3(hi, lo):
    return jnp.concatenate([hi, hi, lo], axis=0)


def _gdn_prep_kernel(q_ref, k_ref, v_ref, qp_ref, kp_ref, vp_ref, sm_ref, gat_ref,
                     cwq_ref, cwk_ref, cwv_ref, alog_row_ref, dtb_row_ref, alog_col_ref, dtb_col_ref,
                     u_ref, w_ref, qd_ref, kd_ref, a_ref, e_ref, xx_sc, gct_sc, *, nc):
    hb = GDN_PREP_HEADS
    c = CHUNK
    hg = pl.program_id(1)
    chunk = pl.program_id(0) % nc
    heads = range(hb)

    def conv_silu(slot, cur_ref, prev_ref, cw_ref):
        prev = prev_ref[...]
        at_start = jnp.full(prev.shape, chunk, I32) == 0
        xx_sc[slot, 0:CONV_HALO, :] = jnp.where(at_start, 0.0, prev)
        xx_sc[slot, CONV_HALO:CONV_HALO + c, :] = cur_ref[...]
        cw = cw_ref[...]
        start = CONV_HALO - (CONV_WIDTH - 1)
        acc = xx_sc[slot, pl.ds(start, c), :] * cw[0:1, :]
        for j in range(1, CONV_WIDTH):
            acc = acc + xx_sc[slot, pl.ds(start + j, c), :] * cw[j:j + 1, :]
        return acc * _sigmoid(acc)

    qa = conv_silu(0, q_ref, qp_ref, cwq_ref)
    ka = conv_silu(1, k_ref, kp_ref, cwk_ref)
    va = conv_silu(2, v_ref, vp_ref, cwv_ref)

    rowi = lax.broadcasted_iota(I32, (c, c), 0)
    coli = lax.broadcasted_iota(I32, (c, c), 1)
    lower = rowi >= coli
    strict = rowi > coli
    eye = jnp.where(rowi == coli, 1.0, 0.0)

    sm = sm_ref[...]
    beta_slab = _sigmoid(sm)
    g_slab = -jnp.exp(alog_row_ref[...]) * jax.nn.softplus(sm + dtb_row_ref[...])
    ones_l = jnp.where(lower, 1.0, 0.0).astype(BF16)
    ones_u = jnp.where(rowi <= coli, 1.0, 0.0).astype(BF16)
    gc_slab = jnp.dot(jnp.concatenate([ones_l] * 3, axis=1), jnp.concatenate(_split3(g_slab), axis=0),
                      preferred_element_type=F32)
    g_t = -jnp.exp(alog_col_ref[:, :c]) * jax.nn.softplus(gat_ref[...] + dtb_col_ref[:, :c])
    gct_sc[...] = jnp.dot(jnp.concatenate(_split3(g_t), axis=1), jnp.concatenate([ones_u] * 3, axis=0),
                          preferred_element_type=F32)
    lane = lax.broadcasted_iota(I32, (c, LANES), 1)
    hsl = [slice(r * GDN_DK, (r + 1) * GDN_DK) for r in heads]

    gcc = [jnp.sum(jnp.where(lane == SM_DT + hg * hb + r, gc_slab, 0.0), axis=1, keepdims=True) for r in heads]
    bcol = [jnp.sum(jnp.where(lane == SM_BETA + hg * hb + r, beta_slab, 0.0), axis=1, keepdims=True)
            for r in heads]
    gcr = [gct_sc[pl.ds(hg * hb + r, 1), :] for r in heads]
    glast = [g[:, c - 1:c] for g in gcr]
    qn = [qa[:, s] * lax.rsqrt(jnp.sum(qa[:, s] * qa[:, s], axis=1, keepdims=True) + RMS_EPS) * GDN_DK ** -0.5
          for s in hsl]
    kn = [ka[:, s] * lax.rsqrt(jnp.sum(ka[:, s] * ka[:, s], axis=1, keepdims=True) + RMS_EPS) for s in hsl]
    eg = [jnp.exp(g) for g in gcc]
    decay = [jnp.where(lower, jnp.exp(gcc[r] - gcr[r]), 0.0) for r in heads]
    kb = [kn[r] * bcol[r] for r in heads]

    def nt3(a, b):
        (ah, al), (bh, bl) = _split2(a), _split2(b)
        return _nt(_lhs3(ah, al), jnp.concatenate([bh, bh, bl], axis=1))

    kq = [nt3(jnp.concatenate([kb[r], qn[r]], axis=0), kn[r]) for r in heads]
    pairs = range(hb // 2)
    r2 = lax.broadcasted_iota(I32, (2 * c, 2 * c), 0)
    c2 = lax.broadcasted_iota(I32, (2 * c, 2 * c), 1)
    same_head = (r2 < c) == (c2 < c)

    def diag2(m):
        return jnp.where(same_head, jnp.concatenate([m, m], axis=0), jnp.zeros((), m.dtype))

    def diag_rhs3(hi, lo):
        return _rhs3(diag2(hi), diag2(lo))

    y = [jnp.concatenate([jnp.where(strict, -(kq[2 * p + s][:c] * decay[2 * p + s]), 0.0) for s in range(2)],
                         axis=1) for p in pairs]
    eye2 = jnp.concatenate([eye, eye], axis=1)
    t_inv = [eye2 + y[p] for p in pairs]
    ys = [_split2(v) for v in y]
    y = [jnp.dot(_lhs3(*ys[p]), diag_rhs3(*ys[p]), preferred_element_type=F32) for p in pairs]
    n_fac = int(math.log2(c))
    for j in range(1, n_fac):
        ys = [_split2(v) for v in y]
        ts = [_split2(v) for v in t_inv]
        if j < n_fac - 1:
            prod = [jnp.dot(jnp.concatenate([_lhs3(*ys[p]), _lhs3(*ts[p])], axis=0), diag_rhs3(*ys[p]),
                            preferred_element_type=F32) for p in pairs]
            y = [q[:c] for q in prod]
            t_inv = [t_inv[p] + prod[p][c:] for p in pairs]
        else:
            t_inv = [t_inv[p] + jnp.dot(_lhs3(*ts[p]), diag_rhs3(*ys[p]), preferred_element_type=F32)
                     for p in pairs]
    ts = [_split2(v) for v in t_inv]
    rhs = [jnp.concatenate([jnp.concatenate([va[:, hsl[r]] * bcol[r], kb[r] * eg[r]], axis=1)
                            for r in (2 * p, 2 * p + 1)], axis=0) for p in pairs]
    uw2 = [jnp.dot(_lhs3(diag2(ts[p][0]), diag2(ts[p][1])), _rhs3(*_split2(rhs[p])),
                   preferred_element_type=F32) for p in pairs]
    uw = [uw2[r // 2][(r % 2) * c:(r % 2 + 1) * c] for r in heads]

    for r in heads:
        u_ref[:, hsl[r]] = uw[r][:, :GDN_DV]
        w_ref[:, hsl[r]] = uw[r][:, GDN_DV:].astype(BF16)
        qd_ref[:, hsl[r]] = (qn[r] * eg[r]).astype(BF16)
        kd_ref[:, hsl[r]] = (kn[r] * jnp.exp(glast[r] - gcc[r])).astype(BF16)
        a_ref[:, r * c:(r + 1) * c] = (kq[r][c:] * decay[r]).astype(BF16)
        e_ref[:, hsl[r]] = jnp.broadcast_to(jnp.exp(glast[r]), (CONV_HALO, GDN_DK))


def _gdn_scan_kernel(u_ref, w_ref, qd_ref, kd_ref, a_ref, e_ref, z_ref, nw_ref, o_ref, st_sc):
    c = CHUNK
    heads = range(GDN_HEADS)
    hsl = [slice(h * GDN_DK, (h + 1) * GDN_DK) for h in heads]

    @pl.when(pl.program_id(1) == 0)
    def _():
        st_sc[...] = jnp.zeros(st_sc.shape, F32)

    st = [st_sc[h] for h in heads]
    for sub in range(u_ref.shape[0] // c):
        rows = slice(sub * c, (sub + 1) * c)
        ws_qs = [jnp.dot(jnp.concatenate([w_ref[rows, hsl[h]], qd_ref[rows, hsl[h]]], axis=0), st[h].astype(BF16),
                         preferred_element_type=F32) for h in heads]
        v_new = [(u_ref[rows, hsl[h]] - ws_qs[h][:c]).astype(BF16) for h in heads]
        o = [ws_qs[h][c:] + jnp.dot(a_ref[rows, h * c:(h + 1) * c], v_new[h], preferred_element_type=F32)
             for h in heads]
        e_row = slice(sub * CONV_HALO, sub * CONV_HALO + 1)
        st = [st[h] * e_ref[e_row, hsl[h]] + _tn(kd_ref[rows, hsl[h]], v_new[h]) for h in heads]
        for h in heads:
            z = z_ref[rows, hsl[h]]
            on = o[h] * lax.rsqrt(jnp.mean(o[h] * o[h], axis=1, keepdims=True) + RMS_EPS)
            o_ref[rows, hsl[h]] = (on * nw_ref[...] * (z * _sigmoid(z))).astype(BF16)
    for h in heads:
        st_sc[h] = st[h]


def _gated_deltanet(proj, conv_w, a_log, dt_bias, norm_w, batch, seq):
    n = batch * seq
    nc = seq // CHUNK
    hb = GDN_PREP_HEADS
    w = hb * GDN_DK
    ga_t = proj[:, OFF_SM + SM_DT:OFF_SM + SM_DT + GDN_HEADS].reshape(batch * nc, CHUNK, GDN_HEADS)
    ga_t = ga_t.transpose(0, 2, 1)
    lane_pad = lambda vec, off: jnp.zeros((1, LANES), F32).at[0, off:off + GDN_HEADS].set(vec.astype(F32))
    alog_row, dtb_row = lane_pad(a_log, SM_DT), lane_pad(dt_bias, SM_DT)
    alog_col = jnp.broadcast_to(a_log.astype(F32)[:, None], (GDN_HEADS, LANES))
    dtb_col = jnp.broadcast_to(dt_bias.astype(F32)[:, None], (GDN_HEADS, LANES))
    nw = norm_w.astype(F32).reshape(1, GDN_DV)
    cw = conv_w.astype(F32)

    def cur(off):
        return pl.BlockSpec((CHUNK, w), lambda i, h: (i, off // w + h))

    def prev(off):
        per = CHUNK // CONV_HALO
        return pl.BlockSpec((CONV_HALO, w), lambda i, h: (jnp.maximum(i * per - 1, 0), off // w + h))

    def cwspec(off):
        return pl.BlockSpec((CONV_WIDTH, w), lambda i, h: (0, off // w + h))

    row = pl.BlockSpec((1, LANES), lambda i, h: (0, 0))
    colv = pl.BlockSpec((GDN_HEADS, LANES), lambda i, h: (0, 0))
    head_tile = pl.BlockSpec((CHUNK, w), lambda i, h: (i, h))
    u, wm, qd, kd, a_in, egl = pl.pallas_call(
        functools.partial(_gdn_prep_kernel, nc=nc), grid=(batch * nc, GDN_HEADS // hb),
        in_specs=[cur(OFF_GQ), cur(OFF_GK), cur(OFF_GV), prev(OFF_GQ), prev(OFF_GK), prev(OFF_GV),
                  pl.BlockSpec((CHUNK, LANES), lambda i, h: (i, OFF_SM // LANES)),
                  pl.BlockSpec((None, GDN_HEADS, CHUNK), lambda i, h: (i, 0, 0)),
                  cwspec(0), cwspec(GDN_QK), cwspec(2 * GDN_QK), row, row, colv, colv],
        out_specs=[head_tile, head_tile, head_tile, head_tile,
                   pl.BlockSpec((CHUNK, hb * CHUNK), lambda i, h: (i, h)),
                   pl.BlockSpec((CONV_HALO, w), lambda i, h: (i, h))],
        out_shape=[jax.ShapeDtypeStruct((n, GDN_V), F32), jax.ShapeDtypeStruct((n, GDN_V), BF16),
                   jax.ShapeDtypeStruct((n, GDN_QK), BF16), jax.ShapeDtypeStruct((n, GDN_QK), BF16),
                   jax.ShapeDtypeStruct((n, GDN_HEADS * CHUNK), BF16),
                   jax.ShapeDtypeStruct((batch * nc * CONV_HALO, GDN_V), F32)],
        scratch_shapes=[pltpu.VMEM((3, CONV_HALO + CHUNK, w), F32),
                        pltpu.VMEM((GDN_HEADS, CHUNK), F32)],
        compiler_params=_cparams(2), name="gdn_prep",
    )(proj, proj, proj, proj, proj, proj, proj, ga_t, cw, cw, cw, alog_row, dtb_row, alog_col, dtb_col)

    per = GDN_SCAN_CHUNKS if nc % GDN_SCAN_CHUNKS == 0 else 1
    steps = nc // per

    def full(width):
        return pl.BlockSpec((per * CHUNK, width), lambda b, t: (b * steps + t, 0))

    return pl.pallas_call(
        _gdn_scan_kernel, grid=(batch, steps),
        in_specs=[full(GDN_V), full(GDN_V), full(GDN_QK), full(GDN_QK), full(GDN_HEADS * CHUNK),
                  pl.BlockSpec((per * CONV_HALO, GDN_V), lambda b, t: (b * steps + t, 0)),
                  pl.BlockSpec((per * CHUNK, GDN_V), lambda b, t: (b * steps + t, OFF_GZ // GDN_V)),
                  pl.BlockSpec((1, GDN_DV), lambda b, t: (0, 0))],
        out_specs=full(GDN_V),
        out_shape=jax.ShapeDtypeStruct((n, GDN_V), BF16),
        scratch_shapes=[pltpu.VMEM((GDN_HEADS, GDN_DK, GDN_DV), F32)],
        compiler_params=_cparams(2), name="gdn_scan",
    )(u, wm, qd, kd, a_in, egl, proj, nw)


def _merge_kernel(att_ref, gdn_ref, wa_ref, wg_ref, ga_ref, gb_ref, o_ref):
    a = jnp.dot(att_ref[...], wa_ref[...], preferred_element_type=F32)
    g = jnp.dot(gdn_ref[...], wg_ref[...], preferred_element_type=F32)
    o_ref[...] = (_sigmoid(ga_ref[...]) * a + _sigmoid(gb_ref[...]) * g).astype(BF16)


def _merge(att, gdn, w_o_attn, w_o_gdn, proj):
    n = att.shape[0]
    tm, tn = min(512, n), 1024
    return pl.pallas_call(
        _merge_kernel, grid=(n // tm, D_MODEL // tn),
        in_specs=[pl.BlockSpec((tm, ATT_Q), lambda i, j: (i, 0)),
                  pl.BlockSpec((tm, GDN_V), lambda i, j: (i, 0)),
                  pl.BlockSpec((ATT_Q, tn), lambda i, j: (0, j)),
                  pl.BlockSpec((GDN_V, tn), lambda i, j: (0, j)),
                  pl.BlockSpec((tm, tn), lambda i, j: (i, OFF_GTA // tn + j)),
                  pl.BlockSpec((tm, tn), lambda i, j: (i, OFF_GTB // tn + j))],
        out_specs=pl.BlockSpec((tm, tn), lambda i, j: (i, j)),
        out_shape=jax.ShapeDtypeStruct((n, D_MODEL), BF16),
        compiler_params=_cparams(2), name="mixer_merge",
    )(att, gdn, w_o_attn.astype(BF16), w_o_gdn.astype(BF16), proj, proj)


def _pack_halves(x):
    w = x.shape[1] // 2
    bits = lax.bitcast_convert_type(x.astype(BF16).astype(F32), U32)
    return (bits[:, :w] >> 16) | (bits[:, w:] & jnp.uint32(0xFFFF0000))


def _unpack_halves(u):
    lo = lax.bitcast_convert_type(u << 16, F32)
    hi = lax.bitcast_convert_type(u & jnp.uint32(0xFFFF0000), F32)
    return lo, hi


SLAB = 8


def _store_slabs(ref, lead, u):
    m = u.shape[0]
    for s in range(SLAB):
        ref[lead + (pl.ds(s, m, stride=SLAB), slice(None))] = u[:, s * LANES:(s + 1) * LANES]


def _load_slabs(ref, lead, start, m):
    return jnp.concatenate([ref[lead + (pl.ds(start + s, m, stride=SLAB), slice(None))] for s in range(SLAB)],
                           axis=1)


def _outproj_ln_kernel(m_ref, w_ref, x_ref, g_ref, b_ref, x1_ref, x1p_ref, *, alpha):
    y = jnp.dot(m_ref[...], w_ref[...], preferred_element_type=F32)
    x1 = _layer_norm(alpha * x_ref[...] + y, g_ref[...], b_ref[...])
    x1_ref[...] = x1
    _store_slabs(x1p_ref, (), _pack_halves(x1))


def _outproj_ln(merged, w_out, x2d, ln_g, ln_b, alpha):
    n, d = x2d.shape
    tm = min(256, n)
    row = pl.BlockSpec((1, d), lambda i: (0, 0))
    tile = pl.BlockSpec((tm, d), lambda i: (i, 0))
    return pl.pallas_call(
        functools.partial(_outproj_ln_kernel, alpha=alpha), grid=(n // tm,),
        in_specs=[tile, pl.BlockSpec((d, d), lambda i: (0, 0)), tile, row, row],
        out_specs=[tile, pl.BlockSpec((tm * SLAB, LANES), lambda i: (i, 0))],
        out_shape=[jax.ShapeDtypeStruct((n, d), F32), jax.ShapeDtypeStruct((n * SLAB, LANES), U32)],
        compiler_params=_cparams(1), name="outproj_ln",
    )(merged, w_out.astype(BF16), x2d, ln_g.astype(F32).reshape(1, d), ln_b.astype(F32).reshape(1, d))


def _first_max(x, iota, size):
    m = jnp.max(x, axis=0, keepdims=True)
    f = jnp.min(jnp.where(x == m, iota, float(size)), axis=0, keepdims=True)
    return m, f


def _router_kernel(x_ref, w_ref, b_ref, eidx_ref, gate_ref, rank_ref, cnt_ref, carry_sc):
    tm = x_ref.shape[0]
    per = N_EXPERTS // N_GROUPS
    neg = -jnp.inf

    @pl.when(pl.program_id(0) == 0)
    def _():
        carry_sc[...] = jnp.zeros(carry_sc.shape, F32)

    logits = _nt(w_ref[...], x_ref[...], precision=HI)
    s = _sigmoid(logits)
    choice = s + b_ref[:, 0:1]
    io_g = lax.broadcasted_iota(I32, (per, tm), 0).astype(F32)
    io_e = lax.broadcasted_iota(I32, (N_EXPERTS, tm), 0).astype(F32)

    gs = []
    for g in range(N_GROUPS):
        xg = choice[g * per:(g + 1) * per]
        m1, f1 = _first_max(xg, io_g, per)
        m2 = jnp.max(jnp.where(io_g == f1, neg, xg), axis=0, keepdims=True)
        gs.append(m1 + m2)
    gw = jnp.concatenate(gs, axis=0)
    gsel = jnp.zeros((N_GROUPS, tm), F32)
    for _ in range(TOPK_GROUPS):
        _, f = _first_max(gw, io_g, N_GROUPS)
        hit = io_g == f
        gsel = jnp.where(hit, 1.0, gsel)
        gw = jnp.where(hit, neg, gw)
    x = jnp.concatenate([jnp.where(gsel[g:g + 1] > 0.0, choice[g * per:(g + 1) * per], neg)
                         for g in range(N_GROUPS)], axis=0)

    mem = jnp.zeros((N_EXPERTS, tm), F32)
    picks, wsel = [], []
    for _ in range(TOP_K):
        _, f = _first_max(x, io_e, N_EXPERTS)
        hit = io_e == f
        picks.append(f)
        wsel.append(jnp.sum(jnp.where(hit, s, 0.0), axis=0, keepdims=True))
        mem = jnp.where(hit, 1.0, mem)
        x = jnp.where(hit, neg, x)
    wsel = jnp.concatenate(wsel, axis=0)
    gate_ref[...] = wsel / jnp.sum(wsel, axis=0, keepdims=True) * ROUTED_SCALE
    eidx_ref[...] = jnp.concatenate(picks, axis=0).astype(I32)

    tr = lax.broadcasted_iota(I32, (tm, tm), 0)
    tc = lax.broadcasted_iota(I32, (tm, tm), 1)
    before = jnp.where(tr < tc, 1.0, 0.0).astype(BF16)
    prefix = jnp.dot(mem.astype(BF16), before, preferred_element_type=F32) + carry_sc[:, 0:1]
    ranks = [jnp.sum(jnp.where(io_e == f, prefix, 0.0), axis=0, keepdims=True) for f in picks]
    rank_ref[...] = jnp.concatenate(ranks, axis=0).astype(I32)
    carry_sc[...] = carry_sc[...] + jnp.sum(mem, axis=1, keepdims=True)
    cnt_ref[...] = carry_sc[...]


def _router(x1, w_router, router_bias):
    n, d = x1.shape
    tm = min(512, n)
    out = pl.BlockSpec((TOP_K, tm), lambda i: (0, i))
    bias = jnp.broadcast_to(router_bias.astype(F32)[:, None], (N_EXPERTS, LANES))
    return pl.pallas_call(
        _router_kernel, grid=(n // tm,),
        in_specs=[pl.BlockSpec((tm, d), lambda i: (i, 0)),
                  pl.BlockSpec((N_EXPERTS, d), lambda i: (0, 0)),
                  pl.BlockSpec((N_EXPERTS, LANES), lambda i: (0, 0))],
        out_specs=[out, out, out, pl.BlockSpec((N_EXPERTS, LANES), lambda i: (0, 0))],
        out_shape=[jax.ShapeDtypeStruct((TOP_K, n), I32), jax.ShapeDtypeStruct((TOP_K, n), F32),
                   jax.ShapeDtypeStruct((TOP_K, n), I32), jax.ShapeDtypeStruct((N_EXPERTS, LANES), F32)],
        scratch_shapes=[pltpu.VMEM((N_EXPERTS, LANES), F32)],
        compiler_params=_cparams(1), name="router",
    )(x1, w_router.astype(F32).T, bias)


SCATTER_TOKENS = 1024
COMBINE_TOKENS = 256
ROW_COPY_UNROLL = 8


def _pos_kernel(pstart_ref, eidx_ref, rank_ref, pos_ref):
    e = eidx_ref[...]
    acc = rank_ref[...]
    for x in range(N_EXPERTS):
        acc = acc + jnp.where(e == x, pstart_ref[x], 0)
    pos_ref[...] = acc


def _positions(pstart, eidx, rank):
    n = eidx.shape[1]
    tm = min(2048, n)
    tile = pl.BlockSpec((TOP_K, tm), lambda i, ps: (0, i))
    return pl.pallas_call(
        _pos_kernel,
        grid_spec=pltpu.PrefetchScalarGridSpec(num_scalar_prefetch=1, grid=(n // tm,),
                                               in_specs=[tile, tile], out_specs=tile),
        out_shape=jax.ShapeDtypeStruct((TOP_K, n), I32),
        compiler_params=_cparams(1), name="moe_positions",
    )(pstart, eidx, rank)


def _scatter_kernel(pend_ref, padded_ref, pos_ref, x_ref, xs_ref, zero_sc, sem, zsem):
    tm = x_ref.shape[0] // SLAB
    bm = EXPERT_ROWS

    @pl.when(pl.program_id(0) == 0)
    def _():
        zero_sc[...] = jnp.zeros(zero_sc.shape, U32)

        def tail_copy(e):
            tail = pl.multiple_of((pend_ref[e] - bm) * SLAB, bm * SLAB)
            return pltpu.make_async_copy(zero_sc, xs_ref.at[pl.ds(tail, bm * SLAB), :], zsem)

        def zstart(e, carry):
            @pl.when(padded_ref[e] > 0)
            def _():
                tail_copy(e).start()
            return carry

        def zwait(e, carry):
            @pl.when(padded_ref[e] > 0)
            def _():
                tail_copy(e).wait()
            return carry

        lax.fori_loop(0, N_EXPERTS, zstart, 0)
        lax.fori_loop(0, N_EXPERTS, zwait, 0)

    def row_copy(t, k):
        src = x_ref.at[pl.ds(pl.multiple_of(t * SLAB, SLAB), SLAB), :]
        dst = xs_ref.at[pl.ds(pl.multiple_of(pos_ref[k, t] * SLAB, SLAB), SLAB), :]
        return pltpu.make_async_copy(src, dst, sem)

    def start(t, carry):
        for k in range(TOP_K):
            row_copy(t, k).start(priority=k % 2)
        return carry

    lax.fori_loop(0, tm, start, 0, unroll=ROW_COPY_UNROLL)
    rows = xs_ref.at[pl.ds(0, tm * TOP_K * SLAB), :]
    pltpu.make_async_copy(rows, rows, sem).wait()


def _scatter_rows(x1p, pos, pend, padded, n_rows):
    n = x1p.shape[0] // SLAB
    tm = min(SCATTER_TOKENS, n)
    grid_spec = pltpu.PrefetchScalarGridSpec(
        num_scalar_prefetch=2, grid=(n // tm,),
        in_specs=[pl.BlockSpec((TOP_K, tm), lambda i, pe, pd: (0, i), memory_space=pltpu.SMEM),
                  pl.BlockSpec((tm * SLAB, LANES), lambda i, pe, pd: (i, 0))],
        out_specs=pl.BlockSpec(memory_space=pl.ANY),
        scratch_shapes=[pltpu.VMEM((EXPERT_ROWS * SLAB, LANES), U32), pltpu.SemaphoreType.DMA,
                        pltpu.SemaphoreType.DMA])
    return pl.pallas_call(
        _scatter_kernel, grid_spec=grid_spec,
        out_shape=jax.ShapeDtypeStruct((n_rows * SLAB, LANES), U32),
        compiler_params=_cparams(1), name="moe_scatter",
    )(pend, padded, pos, x1p)


def _expert_kernel(blk_e_ref, blk_on_ref, blk_new_ref, x_ref, w1_ref, w3_ref, w2_ref, y_ref,
                   w1_sc, w3_sc, w2_sc):
    del blk_e_ref
    i = pl.program_id(0)

    def swiglu_block(w1v, w3v, w2v):
        half = D_MODEL // 2
        lo, hi = _unpack_halves(_load_slabs(x_ref, (), 0, EXPERT_ROWS))
        xa, xb = lo.astype(BF16), hi.astype(BF16)

        def up(wv):
            return (jnp.dot(xa, wv[0:half, :], preferred_element_type=F32)
                    + jnp.dot(xb, wv[half:, :], preferred_element_type=F32))

        h1, h3 = up(w1v), up(w3v)
        act = (h1 * _sigmoid(h1) * h3).astype(BF16)
        _store_slabs(y_ref, (), _pack_halves(jnp.dot(act, w2v, preferred_element_type=F32)))

    @pl.when(blk_new_ref[i] > 0)
    def _():
        w1v, w3v, w2v = w1_ref[...].astype(BF16), w3_ref[...].astype(BF16), w2_ref[...].astype(BF16)
        w1_sc[...] = w1v
        w3_sc[...] = w3v
        w2_sc[...] = w2v
        swiglu_block(w1v, w3v, w2v)

    @pl.when((blk_on_ref[i] > 0) & (blk_new_ref[i] == 0))
    def _():
        swiglu_block(w1_sc[...], w3_sc[...], w2_sc[...])


def _experts(xs, blk_e, blk_on, blk_new, w1, w3, w2):
    n_rows, w = xs.shape
    bm = EXPERT_ROWS * SLAB
    grid_spec = pltpu.PrefetchScalarGridSpec(
        num_scalar_prefetch=3, grid=(n_rows // bm,),
        in_specs=[pl.BlockSpec((bm, w), lambda i, e, on, nw: (i, 0)),
                  pl.BlockSpec((None, D_MODEL, EXPERT_DIM), lambda i, e, on, nw: (e[i], 0, 0)),
                  pl.BlockSpec((None, D_MODEL, EXPERT_DIM), lambda i, e, on, nw: (e[i], 0, 0)),
                  pl.BlockSpec((None, EXPERT_DIM, D_MODEL), lambda i, e, on, nw: (e[i], 0, 0))],
        out_specs=pl.BlockSpec((bm, w), lambda i, e, on, nw: (i, 0)),
        scratch_shapes=[pltpu.VMEM((D_MODEL, EXPERT_DIM), BF16), pltpu.VMEM((D_MODEL, EXPERT_DIM), BF16),
                        pltpu.VMEM((EXPERT_DIM, D_MODEL), BF16)])
    return pl.pallas_call(
        _expert_kernel, grid_spec=grid_spec,
        out_shape=jax.ShapeDtypeStruct((n_rows, w), U32),
        compiler_params=_cparams(1), name="moe_experts",
    )(blk_e, blk_on, blk_new, xs, w1, w3, w2)


def _shared_up_kernel(x_ref, w_ref, h_ref):
    h = jnp.dot(x_ref[...].astype(BF16), w_ref[...], preferred_element_type=F32)
    h1, h3 = h[:, :SHARED_DIM], h[:, SHARED_DIM:]
    h_ref[...] = (h1 * _sigmoid(h1) * h3).astype(BF16)


def _shared_up(x1, ws1, ws3):
    n, d = x1.shape
    tm = min(512, n)
    w13 = jnp.concatenate([ws1, ws3], axis=1).astype(BF16)
    return pl.pallas_call(
        _shared_up_kernel, grid=(n // tm,),
        in_specs=[pl.BlockSpec((tm, d), lambda i: (i, 0)),
                  pl.BlockSpec((d, 2 * SHARED_DIM), lambda i: (0, 0))],
        out_specs=pl.BlockSpec((tm, SHARED_DIM), lambda i: (i, 0)),
        out_shape=jax.ShapeDtypeStruct((n, SHARED_DIM), BF16),
        compiler_params=_cparams(1), name="shared_up",
    )(x1, w13)


def _combine_kernel(pos_ref, pos_next_ref, x1_ref, h_ref, ws2_ref, gate_ref, g_ref, b_ref, y_hbm, o_ref,
                    ybuf, sem, *, alpha):
    tm = x1_ref.shape[0]
    i = pl.program_id(0)
    slot = i % 2

    def issue(p_ref, s):
        def start(t, carry):
            for k in range(TOP_K):
                src = y_hbm.at[pl.ds(pl.multiple_of(p_ref[k, t] * SLAB, SLAB), SLAB), :]
                dst = ybuf.at[s, pl.ds(pl.multiple_of((k * tm + t) * SLAB, SLAB), SLAB), :]
                pltpu.make_async_copy(src, dst, sem.at[s]).start(priority=k % 2)
            return carry

        lax.fori_loop(0, tm, start, 0, unroll=ROW_COPY_UNROLL)

    @pl.when(i == 0)
    def _():
        issue(pos_ref, 0)

    @pl.when(i + 1 < pl.num_programs(0))
    def _():
        issue(pos_next_ref, 1 - slot)

    base = alpha * x1_ref[...] + jnp.dot(h_ref[...], ws2_ref[...], preferred_element_type=F32)
    pltpu.make_async_copy(y_hbm.at[pl.ds(0, tm * TOP_K * SLAB), :], ybuf.at[slot], sem.at[slot]).wait()

    gates = gate_ref[...]
    acc_lo = jnp.zeros((tm, D_MODEL // 2), F32)
    acc_hi = jnp.zeros((tm, D_MODEL // 2), F32)
    for k in range(TOP_K):
        lo, hi = _unpack_halves(_load_slabs(ybuf, (slot,), k * tm * SLAB, tm))
        gk = gates[:, k:k + 1]
        acc_lo = acc_lo + gk * lo
        acc_hi = acc_hi + gk * hi
    f = jnp.concatenate([acc_lo, acc_hi], axis=1)
    o_ref[...] = _layer_norm(base + f, g_ref[...], b_ref[...])


def _combine(x1, hsh, ws2, pos, gates_t, ys, ln_g, ln_b, alpha):
    n, d = x1.shape
    tm = min(COMBINE_TOKENS, n)
    row = pl.BlockSpec((1, d), lambda i: (0, 0))
    last = n // tm - 1
    return pl.pallas_call(
        functools.partial(_combine_kernel, alpha=alpha), grid=(n // tm,),
        in_specs=[pl.BlockSpec((TOP_K, tm), lambda i: (0, i), memory_space=pltpu.SMEM),
                  pl.BlockSpec((TOP_K, tm), lambda i: (0, jnp.minimum(i + 1, last)), memory_space=pltpu.SMEM),
                  pl.BlockSpec((tm, d), lambda i: (i, 0)),
                  pl.BlockSpec((tm, SHARED_DIM), lambda i: (i, 0)),
                  pl.BlockSpec((SHARED_DIM, d), lambda i: (0, 0)),
                  pl.BlockSpec((tm, TOP_K), lambda i: (i, 0)),
                  row, row,
                  pl.BlockSpec(memory_space=pl.ANY)],
        out_specs=pl.BlockSpec((tm, d), lambda i: (i, 0)),
        out_shape=jax.ShapeDtypeStruct((n, d), F32),
        scratch_shapes=[pltpu.VMEM((2, TOP_K * tm * SLAB, LANES), U32), pltpu.SemaphoreType.DMA((2,))],
        compiler_params=_cparams(1), name="moe_combine",
    )(pos, pos, x1, hsh, ws2.astype(BF16), gates_t, ln_g.astype(F32).reshape(1, d),
      ln_b.astype(F32).reshape(1, d), ys)


def _moe(x1, x1p, w_router, router_bias, w1, w3, w2, ws1, ws3, ws2, ln_g, ln_b, alpha):
    n = x1.shape[0]
    bm = EXPERT_ROWS
    eidx, gates, rank, cnt = _router(x1, w_router, router_bias)
    counts = cnt[:, 0].astype(I32)
    padded = (counts + bm - 1) // bm * bm
    pend = jnp.cumsum(padded)
    pstart = pend - padded
    pos = _positions(pstart, eidx, rank)
    n_rows = -(-(n * TOP_K) // bm) * bm + N_EXPERTS * bm
    blk_start = jnp.arange(n_rows // bm, dtype=I32) * bm
    blk_on = (blk_start < pend[-1]).astype(I32)
    last = jnp.maximum(pend[-1] - bm, 0)
    blk_e = jnp.sum((pend[None, :] <= jnp.minimum(blk_start, last)[:, None]).astype(I32), axis=1)
    blk_e = jnp.minimum(blk_e, N_EXPERTS - 1)
    changed = jnp.concatenate([jnp.ones((1,), I32), (blk_e[1:] != blk_e[:-1]).astype(I32)])
    blk_new = blk_on * changed
    xs = _scatter_rows(x1p, pos, pend, padded, n_rows)
    ys = _experts(xs, blk_e, blk_on, blk_new, w1, w3, w2)
    hsh = _shared_up(x1, ws1, ws3)
    return _combine(x1, hsh, ws2, pos, gates.T, ys, ln_g, ln_b, alpha)


def kernel(x, positions, w_in, conv_w, A_log, dt_bias, gdn_norm_w, idx_ln_g, idx_ln_b, w_o_attn, w_o_gdn, w_out, ln1_g, ln1_b, w_router, router_bias, w1, w3, w2, ws1, ws3, ws2, ln2_g, ln2_b):
    batch, seq, d = x.shape
    depth = w_in.shape[0]
    alpha = (2 * depth) ** 0.25
    n = batch * seq
    xf = x.reshape(n, d)
    tabs = _trig_tables(positions)
    for l in range(depth):
        proj = _project(xf, _pack_w_in(w_in, l))
        q_r, k_r, v_ext, iq_r, ik_lo, ik_hi, iw = _att_prep(proj, tabs, idx_ln_g[l], idx_ln_b[l])
        att = _dsa_attention(q_r, k_r, v_ext, iq_r, ik_lo, ik_hi, iw, batch, seq)
        gdn = _gated_deltanet(proj, conv_w[l], A_log[l], dt_bias[l], gdn_norm_w[l], batch, seq)
        merged = _merge(att, gdn, w_o_attn[l], w_o_gdn[l], proj)
        x1, x1p = _outproj_ln(merged, w_out[l], xf, ln1_g[l], ln1_b[l], alpha)
        xf = _moe(x1, x1p, w_router[l], router_bias[l], w1[l], w3[l], w2[l], ws1[l], ws3[l], ws2[l],
                  ln2_g[l], ln2_b[l], alpha)
    return xf.reshape(batch, seq, d)
```

```python
import functools
import math

import jax
import jax.numpy as jnp
import numpy as np
from jax import lax
from jax.experimental import pallas as pl
from jax.experimental.pallas import tpu as pltpu

F32 = jnp.float32
BF16 = jnp.bfloat16
I32 = jnp.int32
U32 = jnp.uint32
HI = lax.Precision.HIGHEST

D_MODEL = 2048
ATT_HEADS = 16
ATT_KV_HEADS = 2
HEAD_DIM = 128
IDX_HEADS = 8
IDX_DIM = 64
TOPK_MAX = 256
Q_BLOCK = 128
ROPE_THETA = 10000.0
GDN_HEADS = 16
GDN_DK = 128
GDN_DV = 128
CONV_WIDTH = 4
CHUNK = 64
N_EXPERTS = 64
TOP_K = 8
N_GROUPS = 8
TOPK_GROUPS = 4
EXPERT_DIM = 512
SHARED_DIM = 512
ROUTED_SCALE = 2.5
LN_EPS = 1e-5
RMS_EPS = 1e-6

ATT_Q = ATT_HEADS * HEAD_DIM
ATT_KV = ATT_KV_HEADS * HEAD_DIM
IDX_Q = IDX_HEADS * IDX_DIM
GDN_QK = GDN_HEADS * GDN_DK
GDN_V = GDN_HEADS * GDN_DV
SPLITS = (ATT_Q, ATT_KV, ATT_KV, IDX_Q, IDX_DIM, IDX_HEADS, GDN_QK, GDN_QK, GDN_V, GDN_V,
          GDN_HEADS, GDN_HEADS, D_MODEL, D_MODEL)
SPLIT_OFFSETS = tuple(int(o) for o in np.cumsum(SPLITS)[:-1])

LANES = 128
VMEM_LIMIT_BYTES = 56 * 1024 * 1024

OFF_AQ = 0
OFF_GQ = OFF_AQ + ATT_Q
OFF_GK = OFF_GQ + GDN_QK
OFF_GV = OFF_GK + GDN_QK
OFF_GZ = OFF_GV + GDN_V
OFF_GTA = OFF_GZ + GDN_V
OFF_GTB = OFF_GTA + D_MODEL
OFF_AK = OFF_GTB + D_MODEL
OFF_AV = OFF_AK + ATT_KV
OFF_IQ = OFF_AV + ATT_KV
OFF_SM = OFF_IQ + IDX_Q
PROJ_W = 16384
SM_IK = 0
SM_IW = SM_IK + IDX_DIM
SM_BETA = SM_IW + IDX_HEADS
SM_DT = SM_BETA + GDN_HEADS

KEY_CHUNK = 256
ATT_ROWS = 512
EXPERT_ROWS = 512
INT_MIN = -2147483648
MASK_BIAS = -1e30
M_INIT = -1e29


def _cparams(n_axes):
    return pltpu.CompilerParams(dimension_semantics=("arbitrary",) * n_axes,
                                vmem_limit_bytes=VMEM_LIMIT_BYTES)


def _nt(a, b, precision=None):
    return lax.dot_general(a, b, (((1,), (1,)), ((), ())), precision=precision,
                           preferred_element_type=F32)


def _tn(a, b, precision=None):
    return lax.dot_general(a, b, (((0,), (0,)), ((), ())), precision=precision,
                           preferred_element_type=F32)


def _sigmoid(x):
    return jax.nn.sigmoid(x)


def _layer_norm(x, g, b):
    mu = jnp.mean(x, axis=-1, keepdims=True)
    xc = x - mu
    var = jnp.mean(xc * xc, axis=-1, keepdims=True)
    return xc * lax.rsqrt(var + LN_EPS) * g + b


def _trig_kernel(pos_ref, inv_a_ref, inv_i_ref, sgn_a_ref, sgn_i_ref, cos_a, sin_a, cos_i, sin_i):
    p = pos_ref[...]
    ang_a = p * inv_a_ref[...]
    cos_a[...] = jnp.cos(ang_a)
    sin_a[...] = jnp.sin(ang_a) * sgn_a_ref[...]
    ang_i = p * inv_i_ref[...]
    cos_i[...] = jnp.cos(ang_i)
    sin_i[...] = jnp.sin(ang_i) * sgn_i_ref[...]


def _trig_tables(positions):
    n = positions.size
    tm = min(512, n)
    pos = positions.reshape(n, 1).astype(F32)
    inv_a = ROPE_THETA ** (-jnp.arange(0, HEAD_DIM, 2, dtype=F32) / HEAD_DIM)
    inv_i = ROPE_THETA ** (-jnp.arange(0, IDX_DIM, 2, dtype=F32) / IDX_DIM)
    half_a, half_i = HEAD_DIM // 2, IDX_DIM // 2
    inv_a_row = jnp.tile(inv_a, 2).reshape(1, LANES)
    inv_i_row = jnp.tile(inv_i, 4).reshape(1, LANES)
    lane = np.arange(LANES)
    sgn_a = jnp.asarray(np.where(lane < half_a, -1.0, 1.0), F32).reshape(1, LANES)
    sgn_i = jnp.asarray(np.where(lane % IDX_DIM < half_i, -1.0, 1.0), F32).reshape(1, LANES)
    row = pl.BlockSpec((1, LANES), lambda i: (0, 0))
    tab = pl.BlockSpec((tm, LANES), lambda i: (i, 0))
    return pl.pallas_call(
        _trig_kernel, grid=(n // tm,),
        in_specs=[pl.BlockSpec((tm, 1), lambda i: (i, 0)), row, row, row, row],
        out_specs=[tab, tab, tab, tab],
        out_shape=[jax.ShapeDtypeStruct((n, LANES), F32)] * 4,
        compiler_params=_cparams(1), name="trig_tables",
    )(pos, inv_a_row, inv_i_row, sgn_a, sgn_i)


def _pack_plan():
    src_off = dict(zip(("aq", "ak", "av", "iq", "ik", "iw", "gq", "gk", "gv", "gz", "gb", "ga", "gta", "gtb"),
                       (0,) + SPLIT_OFFSETS))
    order = (("aq", OFF_AQ, ATT_Q), ("gq", OFF_GQ, GDN_QK), ("gk", OFF_GK, GDN_QK), ("gv", OFF_GV, GDN_V),
             ("gz", OFF_GZ, GDN_V), ("gta", OFF_GTA, D_MODEL), ("gtb", OFF_GTB, D_MODEL), ("ak", OFF_AK, ATT_KV),
             ("av", OFF_AV, ATT_KV), ("iq", OFF_IQ, IDX_Q), ("ik", OFF_SM + SM_IK, IDX_DIM),
             ("iw", OFF_SM + SM_IW, IDX_HEADS), ("gb", OFF_SM + SM_BETA, GDN_HEADS), ("ga", OFF_SM + SM_DT, GDN_HEADS))
    src_col = np.full((PROJ_W,), -1, np.int64)
    for name, dst, width in order:
        src_col[dst:dst + width] = src_off[name] + np.arange(width)
    plan, shifts = [], set()
    for j in range(PROJ_W // LANES):
        cols = src_col[j * LANES:(j + 1) * LANES]
        if j == OFF_SM // LANES:
            a, b = src_off["ik"], src_off["gb"] - SM_BETA
            assert a % LANES == 0 and b % LANES == 0
            plan.append((3, a // LANES, b // LANES, 0))
        elif (cols < 0).all():
            plan.append((2, 0, 0, 0))
        else:
            assert (np.diff(cols) == 1).all()
            shift = int(cols[0] % LANES)
            if shift == 0:
                plan.append((0, cols[0] // LANES, cols[0] // LANES, 0))
            else:
                shifts.add(shift)
                plan.append((1, cols[0] // LANES, cols[0] // LANES + 1, shift))
    return np.asarray(plan, np.int32), tuple(sorted(shifts))


def _pack_kernel(mode_ref, ta_ref, tb_ref, shift_ref, a_ref, b_ref, o_ref, *, shifts):
    del ta_ref, tb_ref
    mode = mode_ref[pl.program_id(0)]
    lane = lax.broadcasted_iota(I32, a_ref.shape, 1)

    @pl.when(mode == 0)
    def _():
        o_ref[...] = a_ref[...].astype(BF16)

    for shift in shifts:
        @pl.when((mode == 1) & (shift_ref[pl.program_id(0)] == shift))
        def _(shift=shift):
            o_ref[...] = jnp.concatenate([a_ref[:, shift:], b_ref[:, :shift]], axis=1).astype(BF16)

    @pl.when(mode == 2)
    def _():
        o_ref[...] = jnp.zeros(o_ref.shape, BF16)

    @pl.when(mode == 3)
    def _():
        small = jnp.where(lane < SM_BETA, a_ref[...], jnp.where(lane < SM_DT + GDN_HEADS, b_ref[...], 0.0))
        o_ref[...] = small.astype(BF16)


def _pack_w_in(w_in, layer):
    _, d, width = w_in.shape
    plan, shifts = _pack_plan()
    last = (width - 1) // LANES
    grid_spec = pltpu.PrefetchScalarGridSpec(
        num_scalar_prefetch=4, grid=(PROJ_W // LANES,),
        in_specs=[pl.BlockSpec((None, d, LANES), lambda j, m, ta, tb, sh: (layer, 0, ta[j])),
                  pl.BlockSpec((None, d, LANES), lambda j, m, ta, tb, sh: (layer, 0, jnp.minimum(tb[j], last)))],
        out_specs=pl.BlockSpec((d, LANES), lambda j, m, ta, tb, sh: (0, j)))
    return pl.pallas_call(
        functools.partial(_pack_kernel, shifts=shifts), grid_spec=grid_spec,
        out_shape=jax.ShapeDtypeStruct((d, PROJ_W), BF16),
        compiler_params=_cparams(1), name="pack_w_in",
    )(*(jnp.asarray(plan[:, c]) for c in range(4)), w_in, w_in)


def _proj_kernel(x_ref, w_ref, o_ref, xb_ref):
    @pl.when(pl.program_id(1) == 0)
    def _():
        xb_ref[...] = x_ref[...].astype(BF16)

    o_ref[...] = jnp.dot(xb_ref[...], w_ref[...], preferred_element_type=F32)


def _project(x2d, w_packed):
    n, d = x2d.shape
    tm, tn = min(1024, n), 1024
    return pl.pallas_call(
        _proj_kernel, grid=(n // tm, PROJ_W // tn),
        in_specs=[pl.BlockSpec((tm, d), lambda i, j: (i, 0)),
                  pl.BlockSpec((d, tn), lambda i, j: (0, j))],
        out_specs=pl.BlockSpec((tm, tn), lambda i, j: (i, j)),
        out_shape=jax.ShapeDtypeStruct((n, PROJ_W), F32),
        scratch_shapes=[pltpu.VMEM((tm, d), BF16)],
        compiler_params=_cparams(2), name="in_proj",
    )(x2d, w_packed)


def _rope_head(x, cos, sin_signed):
    return x * cos + pltpu.roll(x, HEAD_DIM // 2, 1) * sin_signed


def _rope_idx(x, cos, sin_signed, first_half):
    half = IDX_DIM // 2
    partner = jnp.where(first_half, pltpu.roll(x, LANES - half, 1), pltpu.roll(x, half, 1))
    return x * cos + partner * sin_signed


def _att_prep_kernel(aq_ref, ak_ref, av_ref, iq_ref, sm_ref, cos_a_ref, sin_a_ref, cos_i_ref,
                     sin_i_ref, lng_ref, lnb_ref,
                     q_out, k_out, v_out, iq_out, iklo_out, ikhi_out, iw_out):
    cos_a, sin_a = cos_a_ref[...], sin_a_ref[...]
    cos_i, sin_i = cos_i_ref[...], sin_i_ref[...]
    rows = cos_a.shape[0]
    lane = lax.broadcasted_iota(I32, (rows, LANES), 1)
    first_half = (lane % IDX_DIM) < (IDX_DIM // 2)

    for h in range(ATT_HEADS):
        q_h = _rope_head(aq_ref[:, h * HEAD_DIM:(h + 1) * HEAD_DIM], cos_a, sin_a) * (
            HEAD_DIM ** -0.5 * math.log2(math.e))
        q_out[0, h] = q_h.astype(BF16)
    ones = jnp.ones((rows, HEAD_DIM), BF16)
    for g in range(ATT_KV_HEADS):
        sl = slice(g * HEAD_DIM, (g + 1) * HEAD_DIM)
        k_out[:, sl] = _rope_head(ak_ref[:, sl], cos_a, sin_a).astype(BF16)
        v_out[:, 2 * g * HEAD_DIM:(2 * g + 1) * HEAD_DIM] = av_ref[:, sl].astype(BF16)
        v_out[:, (2 * g + 1) * HEAD_DIM:(2 * g + 2) * HEAD_DIM] = ones
    for j in range(IDX_HEADS // 2):
        iq_out[0, j] = _rope_idx(iq_ref[:, j * LANES:(j + 1) * LANES], cos_i, sin_i, first_half).astype(BF16)

    sm = sm_ref[...]
    in_ik = lane < IDX_DIM
    mu = jnp.sum(jnp.where(in_ik, sm, 0.0), axis=1, keepdims=True) / IDX_DIM
    xc = sm - mu
    var = jnp.sum(jnp.where(in_ik, xc * xc, 0.0), axis=1, keepdims=True) / IDX_DIM
    ik = xc * lax.rsqrt(var + LN_EPS) * lng_ref[...] + lnb_ref[...]
    ik = jnp.where(in_ik, _rope_idx(ik, cos_i, sin_i, first_half), 0.0)
    iklo_out[...] = ik.astype(BF16)
    ikhi_out[...] = pltpu.roll(ik, IDX_DIM, 1).astype(BF16)
    iw_out[...] = (sm * IDX_HEADS ** -0.5) * IDX_DIM ** -0.5


def _att_prep(proj, tabs, idx_ln_g, idx_ln_b):
    n = proj.shape[0]
    tm = min(ATT_ROWS, n)
    cos_a, sin_a, cos_i, sin_i = tabs
    pad = jnp.zeros((LANES - IDX_DIM,), F32)
    lng = jnp.concatenate([idx_ln_g.astype(F32), pad]).reshape(1, LANES)
    lnb = jnp.concatenate([idx_ln_b.astype(F32), pad]).reshape(1, LANES)
    tab = pl.BlockSpec((tm, LANES), lambda i: (i, 0))
    row = pl.BlockSpec((1, LANES), lambda i: (0, 0))

    def col(width, off):
        return pl.BlockSpec((tm, width), lambda i: (i, off // width))

    return pl.pallas_call(
        _att_prep_kernel, grid=(n // tm,),
        in_specs=[col(ATT_Q, OFF_AQ), col(ATT_KV, OFF_AK), col(ATT_KV, OFF_AV), col(IDX_Q, OFF_IQ),
                  col(LANES, OFF_SM), tab, tab, tab, tab, row, row],
        out_specs=[pl.BlockSpec((1, ATT_HEADS, tm, HEAD_DIM), lambda i: (i, 0, 0, 0)),
                   pl.BlockSpec((tm, ATT_KV), lambda i: (i, 0)),
                   pl.BlockSpec((tm, 2 * ATT_KV), lambda i: (i, 0)),
                   pl.BlockSpec((1, IDX_HEADS // 2, tm, LANES), lambda i: (i, 0, 0, 0)),
                   tab, tab, tab],
        out_shape=[jax.ShapeDtypeStruct((n // tm, ATT_HEADS, tm, HEAD_DIM), BF16),
                   jax.ShapeDtypeStruct((n, ATT_KV), BF16),
                   jax.ShapeDtypeStruct((n, 2 * ATT_KV), BF16),
                   jax.ShapeDtypeStruct((n // tm, IDX_HEADS // 2, tm, LANES), BF16),
                   jax.ShapeDtypeStruct((n, LANES), BF16),
                   jax.ShapeDtypeStruct((n, LANES), BF16),
                   jax.ShapeDtypeStruct((n, LANES), F32)],
        compiler_params=_cparams(1), name="att_prep",
    )(proj, proj, proj, proj, proj, cos_a, sin_a, cos_i, sin_i, lng, lnb)


def _dsa_kernel(q_ref, k_ref, v_ref, iq_ref, iklo_ref, ikhi_ref, iw_ref, o_ref,
                keys_sc, acc_sc, m_sc, wrep_sc, sel_sc, *, topk, seq):
    qb, kc = q_ref.shape[2], KEY_CHUNK
    tiles = kc // LANES
    rep = ATT_HEADS // ATT_KV_HEADS
    i = pl.program_id(1)
    nch = (i * qb + qb + kc - 1) // kc
    row = lax.broadcasted_iota(I32, (qb, LANES), 0)
    lane = lax.broadcasted_iota(I32, (qb, LANES), 1)
    t_abs = i * qb + row

    iw = iw_ref[...]
    for h in range(IDX_HEADS):
        wrep_sc[h] = jnp.broadcast_to(iw[:, SM_IW + h:SM_IW + h + 1], (qb, LANES))
    qi = iq_ref[0].reshape((IDX_HEADS // 2) * qb, LANES)

    def score_chunk(c, carry):
        off = pl.multiple_of(c * kc, kc)
        s_lo = _nt(qi, iklo_ref[pl.ds(off, kc), :])
        s_hi = _nt(qi, ikhi_ref[pl.ds(off, kc), :])
        for ct in range(tiles):
            cs = slice(ct * LANES, (ct + 1) * LANES)
            acc = jnp.zeros((qb, LANES), F32)
            for j in range(IDX_HEADS // 2):
                rs = slice(j * qb, (j + 1) * qb)
                acc = acc + wrep_sc[2 * j] * jnp.maximum(s_lo[rs, cs], 0.0)
                acc = acc + wrep_sc[2 * j + 1] * jnp.maximum(s_hi[rs, cs], 0.0)
            bits = lax.bitcast_convert_type(acc, I32)
            key = bits ^ ((bits >> 31) & jnp.int32(0x7FFFFFFF))
            key = jnp.where(bits == jnp.int32(INT_MIN), 0, key)
            causal = (off + ct * LANES + lane) <= t_abs
            keys_sc[c, :, cs] = jnp.where(causal, key, jnp.int32(INT_MIN))
        return carry

    lax.fori_loop(0, nch, score_chunk, 0)

    kth = jnp.float32(topk)

    def count(pred):
        lane_g = lax.broadcasted_iota(I32, (Q_BLOCK, LANES), 1)
        parts = []
        for r0 in range(0, qb, Q_BLOCK):
            rs = slice(r0, r0 + Q_BLOCK)

            def body(c, cnt, rs=rs):
                off = c * kc
                for ct in range(tiles):
                    kt = keys_sc[c, rs, ct * LANES:(ct + 1) * LANES]
                    cnt = cnt + jnp.where(pred(kt, off + ct * LANES + lane_g, rs), 1.0, 0.0)
                return cnt

            parts.append(lax.fori_loop(0, nch, body, jnp.zeros((Q_BLOCK, LANES), F32)))
        return jnp.concatenate([jnp.broadcast_to(jnp.sum(cnt, axis=1, keepdims=True), (Q_BLOCK, LANES))
                                for cnt in parts], axis=0)

    zero = jnp.zeros((qb, LANES), I32)
    n_pos = count(lambda k, s, rs: k >= zero[rs])
    base = jnp.where(n_pos >= kth, zero, jnp.int32(INT_MIN))
    n_base = jnp.where(n_pos >= kth, n_pos, -1.0)
    short = t_abs < topk

    def unresolved(n_base):
        return jnp.max(jnp.where((n_base == kth) | short, 0.0, 1.0)) > 0.0

    def bit_cond(st):
        it, _, n_base = st
        return (it < 31) & unresolved(n_base)

    def bit_step(st):
        it, base, n_base = st
        cand = base | (jnp.int32(1) << (30 - it))
        tot = count(lambda k, s, rs: k >= cand[rs])
        take = tot >= kth
        return it + 1, jnp.where(take, cand, base), jnp.where(take, tot, n_base)

    _, thr, n_ge = lax.while_loop(bit_cond, bit_step, (jnp.int32(0), base, n_base))
    sel_sc[0] = thr
    sel_sc[1] = jnp.full((qb, LANES), seq, I32)

    @pl.when(jnp.max(n_ge) > kth)
    def _():
        nbits = max(1, (seq - 1).bit_length())
        need = kth - count(lambda k, s, rs: k > thr[rs])

        def idx_step(it, posv):
            cand = posv + (jnp.int32(1) << (nbits - 1 - it))
            below = count(lambda k, s, rs: (k == thr[rs]) & (s < cand[rs]))
            return jnp.where(below < need, cand, posv)

        sel_sc[1] = lax.fori_loop(0, nbits, idx_step, zero)

    thr = sel_sc[0]
    last_tie = sel_sc[1]

    m_sc[...] = jnp.full(m_sc.shape, M_INIT, F32)
    acc_sc[...] = jnp.zeros(acc_sc.shape, F32)

    def attend_chunk(c, carry):
        off = pl.multiple_of(c * kc, kc)
        kk = keys_sc[c]
        bias = []
        for ct in range(tiles):
            cs = slice(ct * LANES, (ct + 1) * LANES)
            sidx = off + ct * LANES + lane
            keep = (kk[:, cs] > thr) | ((kk[:, cs] == thr) & (sidx <= last_tie))
            keep = keep & (sidx <= t_abs)
            bias.append(jnp.where(keep, 0.0, MASK_BIAS))
        for g in range(ATT_KV_HEADS):
            kg = k_ref[pl.ds(off, kc), g * HEAD_DIM:(g + 1) * HEAD_DIM]
            vg = v_ref[pl.ds(off, kc), 2 * g * HEAD_DIM:(2 * g + 2) * HEAD_DIM]
            qg = q_ref[0, g * rep:(g + 1) * rep].reshape(rep * qb, HEAD_DIM)
            s = _nt(qg, kg)
            s_t = [(s[:, ct * LANES:(ct + 1) * LANES].reshape(rep, qb, LANES) + bias[ct][None]
                    ).reshape(rep * qb, LANES) for ct in range(tiles)]
            mx = s_t[0]
            for ct in range(1, tiles):
                mx = jnp.maximum(mx, s_t[ct])
            m_old = m_sc[g]
            m_new = jnp.maximum(m_old, jnp.max(mx, axis=1, keepdims=True))
            p = jnp.concatenate([jnp.exp2(st - m_new) for st in s_t], axis=1).astype(BF16)
            alpha = jnp.exp2(m_old - m_new)
            pv = jnp.dot(p, vg, preferred_element_type=F32)
            acc_sc[g] = acc_sc[g] * jnp.concatenate([alpha, alpha], axis=1) + pv
            m_sc[g] = m_new
        return carry

    lax.fori_loop(0, nch, attend_chunk, 0)

    for g in range(ATT_KV_HEADS):
        for r in range(rep):
            a = acc_sc[g, r * qb:(r + 1) * qb, :]
            h = g * rep + r
            o_ref[:, h * HEAD_DIM:(h + 1) * HEAD_DIM] = (a[:, :HEAD_DIM] / a[:, HEAD_DIM:]).astype(BF16)


def _dsa_attention(q_r, k_r, v_ext, iq_r, ik_lo, ik_hi, iw, batch, seq):
    n = batch * seq
    qb = q_r.shape[2]
    nqb = seq // qb
    topk = min(TOPK_MAX, seq // 4)
    rep = ATT_HEADS // ATT_KV_HEADS
    kern = functools.partial(_dsa_kernel, topk=topk, seq=seq)

    def per_batch(width):
        return pl.BlockSpec((None, seq, width), lambda b, i: (b, 0, 0))

    return pl.pallas_call(
        kern, grid=(batch, nqb),
        in_specs=[pl.BlockSpec((1, ATT_HEADS, qb, HEAD_DIM), lambda b, i: (b * nqb + i, 0, 0, 0)),
                  per_batch(ATT_KV), per_batch(2 * ATT_KV),
                  pl.BlockSpec((1, IDX_HEADS // 2, qb, LANES), lambda b, i: (b * nqb + i, 0, 0, 0)),
                  per_batch(LANES), per_batch(LANES),
                  pl.BlockSpec((qb, LANES), lambda b, i: (b * nqb + i, 0))],
        out_specs=pl.BlockSpec((qb, ATT_Q), lambda b, i: (b * nqb + i, 0)),
        out_shape=jax.ShapeDtypeStruct((n, ATT_Q), BF16),
        scratch_shapes=[pltpu.VMEM((seq // KEY_CHUNK, qb, KEY_CHUNK), I32),
                        pltpu.VMEM((ATT_KV_HEADS, rep * qb, 2 * HEAD_DIM), F32),
                        pltpu.VMEM((ATT_KV_HEADS, rep * qb, LANES), F32),
                        pltpu.VMEM((IDX_HEADS, qb, LANES), F32),
                        pltpu.VMEM((2, qb, LANES), I32)],
        compiler_params=_cparams(2), name="dsa_attention",
    )(q_r, k_r.reshape(batch, seq, ATT_KV), v_ext.reshape(batch, seq, 2 * ATT_KV), iq_r,
      ik_lo.reshape(batch, seq, LANES), ik_hi.reshape(batch, seq, LANES), iw)


GDN_PREP_HEADS = 16
GDN_SCAN_CHUNKS = 8
CONV_HALO = 8


def _split2(x):
    hi = x.astype(BF16)
    return hi, (x - hi.astype(F32)).astype(BF16)


def _split3(x):
    hi = x.astype(BF16)
    r = x - hi.astype(F32)
    mid = r.astype(BF16)
    return hi, mid, (r - mid.astype(F32)).astype(BF16)


def _lhs3(hi, lo):
    return jnp.concatenate([hi, lo, hi], axis=1)


def _rhs3(hi, lo):
    return jnp.concatenate([hi, hi, lo], axis=0)


def _gdn_prep_kernel(q_ref, k_ref, v_ref, qp_ref, kp_ref, vp_ref, sm_ref, gat_ref,
                     cwq_ref, cwk_ref, cwv_ref, alog_row_ref, dtb_row_ref, alog_col_ref, dtb_col_ref,
                     u_ref, w_ref, qd_ref, kd_ref, a_ref, e_ref, xx_sc, gct_sc, *, nc):
    hb = GDN_PREP_HEADS
    c = CHUNK
    hg = pl.program_id(1)
    chunk = pl.program_id(0) % nc
    heads = range(hb)

    def conv_silu(slot, cur_ref, prev_ref, cw_ref):
        prev = prev_ref[...]
        at_start = jnp.full(prev.shape, chunk, I32) == 0
        xx_sc[slot, 0:CONV_HALO, :] = jnp.where(at_start, 0.0, prev)
        xx_sc[slot, CONV_HALO:CONV_HALO + c, :] = cur_ref[...]
        cw = cw_ref[...]
        start = CONV_HALO - (CONV_WIDTH - 1)
        acc = xx_sc[slot, pl.ds(start, c), :] * cw[0:1, :]
        for j in range(1, CONV_WIDTH):
            acc = acc + xx_sc[slot, pl.ds(start + j, c), :] * cw[j:j + 1, :]
        return acc * _sigmoid(acc)

    qa = conv_silu(0, q_ref, qp_ref, cwq_ref)
    ka = conv_silu(1, k_ref, kp_ref, cwk_ref)
    va = conv_silu(2, v_ref, vp_ref, cwv_ref)

    rowi = lax.broadcasted_iota(I32, (c, c), 0)
    coli = lax.broadcasted_iota(I32, (c, c), 1)
    lower = rowi >= coli
    strict = rowi > coli
    eye = jnp.where(rowi == coli, 1.0, 0.0)

    sm = sm_ref[...]
    beta_slab = _sigmoid(sm)
    g_slab = -jnp.exp(alog_row_ref[...]) * jax.nn.softplus(sm + dtb_row_ref[...])
    ones_l = jnp.where(lower, 1.0, 0.0).astype(BF16)
    ones_u = jnp.where(rowi <= coli, 1.0, 0.0).astype(BF16)
    gc_slab = jnp.dot(jnp.concatenate([ones_l] * 3, axis=1), jnp.concatenate(_split3(g_slab), axis=0),
                      preferred_element_type=F32)
    g_t = -jnp.exp(alog_col_ref[:, :c]) * jax.nn.softplus(gat_ref[...] + dtb_col_ref[:, :c])
    gct_sc[...] = jnp.dot(jnp.concatenate(_split3(g_t), axis=1), jnp.concatenate([ones_u] * 3, axis=0),
                          preferred_element_type=F32)
    lane = lax.broadcasted_iota(I32, (c, LANES), 1)
    hsl = [slice(r * GDN_DK, (r + 1) * GDN_DK) for r in heads]

    gcc = [jnp.sum(jnp.where(lane == SM_DT + hg * hb + r, gc_slab, 0.0), axis=1, keepdims=True) for r in heads]
    bcol = [jnp.sum(jnp.where(lane == SM_BETA + hg * hb + r, beta_slab, 0.0), axis=1, keepdims=True)
            for r in heads]
    gcr = [gct_sc[pl.ds(hg * hb + r, 1), :] for r in heads]
    glast = [g[:, c - 1:c] for g in gcr]
    qn = [qa[:, s] * lax.rsqrt(jnp.sum(qa[:, s] * qa[:, s], axis=1, keepdims=True) + RMS_EPS) * GDN_DK ** -0.5
          for s in hsl]
    kn = [ka[:, s] * lax.rsqrt(jnp.sum(ka[:, s] * ka[:, s], axis=1, keepdims=True) + RMS_EPS) for s in hsl]
    eg = [jnp.exp(g) for g in gcc]
    decay = [jnp.where(lower, jnp.exp(gcc[r] - gcr[r]), 0.0) for r in heads]
    kb = [kn[r] * bcol[r] for r in heads]

    def nt3(a, b):
        (ah, al), (bh, bl) = _split2(a), _split2(b)
        return _nt(_lhs3(ah, al), jnp.concatenate([bh, bh, bl], axis=1))

    kq = [nt3(jnp.concatenate([kb[r], qn[r]], axis=0), kn[r]) for r in heads]
    pairs = range(hb // 2)
    r2 = lax.broadcasted_iota(I32, (2 * c, 2 * c), 0)
    c2 = lax.broadcasted_iota(I32, (2 * c, 2 * c), 1)
    same_head = (r2 < c) == (c2 < c)

    def diag2(m):
        return jnp.where(same_head, jnp.concatenate([m, m], axis=0), jnp.zeros((), m.dtype))

    def diag_rhs3(hi, lo):
        return _rhs3(diag2(hi), diag2(lo))

    y = [jnp.concatenate([jnp.where(strict, -(kq[2 * p + s][:c] * decay[2 * p + s]), 0.0) for s in range(2)],
                         axis=1) for p in pairs]
    eye2 = jnp.concatenate([eye, eye], axis=1)
    t_inv = [eye2 + y[p] for p in pairs]
    ys = [_split2(v) for v in y]
    y = [jnp.dot(_lhs3(*ys[p]), diag_rhs3(*ys[p]), preferred_element_type=F32) for p in pairs]
    n_fac = int(math.log2(c))
    for j in range(1, n_fac):
        ys = [_split2(v) for v in y]
        ts = [_split2(v) for v in t_inv]
        if j < n_fac - 1:
            prod = [jnp.dot(jnp.concatenate([_lhs3(*ys[p]), _lhs3(*ts[p])], axis=0), diag_rhs3(*ys[p]),
                            preferred_element_type=F32) for p in pairs]
            y = [q[:c] for q in prod]
            t_inv = [t_inv[p] + prod[p][c:] for p in pairs]
        else:
            t_inv = [t_inv[p] + jnp.dot(_lhs3(*ts[p]), diag_rhs3(*ys[p]), preferred_element_type=F32)
                     for p in pairs]
    ts = [_split2(v) for v in t_inv]
    rhs = [jnp.concatenate([jnp.concatenate([va[:, hsl[r]] * bcol[r], kb[r] * eg[r]], axis=1)
                            for r in (2 * p, 2 * p + 1)], axis=0) for p in pairs]
    uw2 = [jnp.dot(_lhs3(diag2(ts[p][0]), diag2(ts[p][1])), _rhs3(*_split2(rhs[p])),
                   preferred_element_type=F32) for p in pairs]
    uw = [uw2[r // 2][(r % 2) * c:(r % 2 + 1) * c] for r in heads]

    for r in heads:
        u_ref[:, hsl[r]] = uw[r][:, :GDN_DV]
        w_ref[:, hsl[r]] = uw[r][:, GDN_DV:].astype(BF16)
        qd_ref[:, hsl[r]] = (qn[r] * eg[r]).astype(BF16)
        kd_ref[:, hsl[r]] = (kn[r] * jnp.exp(glast[r] - gcc[r])).astype(BF16)
        a_ref[:, r * c:(r + 1) * c] = (kq[r][c:] * decay[r]).astype(BF16)
        e_ref[:, hsl[r]] = jnp.broadcast_to(jnp.exp(glast[r]), (CONV_HALO, GDN_DK))


def _gdn_scan_kernel(u_ref, w_ref, qd_ref, kd_ref, a_ref, e_ref, z_ref, nw_ref, o_ref, st_sc):
    c = CHUNK
    heads = range(GDN_HEADS)
    hsl = [slice(h * GDN_DK, (h + 1) * GDN_DK) for h in heads]

    @pl.when(pl.program_id(1) == 0)
    def _():
        st_sc[...] = jnp.zeros(st_sc.shape, F32)

    st = [st_sc[h] for h in heads]
    for sub in range(u_ref.shape[0] // c):
        rows = slice(sub * c, (sub + 1) * c)
        ws_qs = [jnp.dot(jnp.concatenate([w_ref[rows, hsl[h]], qd_ref[rows, hsl[h]]], axis=0), st[h].astype(BF16),
                         preferred_element_type=F32) for h in heads]
        v_new = [(u_ref[rows, hsl[h]] - ws_qs[h][:c]).astype(BF16) for h in heads]
        o = [ws_qs[h][c:] + jnp.dot(a_ref[rows, h * c:(h + 1) * c], v_new[h], preferred_element_type=F32)
             for h in heads]
        e_row = slice(sub * CONV_HALO, sub * CONV_HALO + 1)
        st = [st[h] * e_ref[e_row, hsl[h]] + _tn(kd_ref[rows, hsl[h]], v_new[h]) for h in heads]
        for h in heads:
            z = z_ref[rows, hsl[h]]
            on = o[h] * lax.rsqrt(jnp.mean(o[h] * o[h], axis=1, keepdims=True) + RMS_EPS)
            o_ref[rows, hsl[h]] = (on * nw_ref[...] * (z * _sigmoid(z))).astype(BF16)
    for h in heads:
        st_sc[h] = st[h]


def _gated_deltanet(proj, conv_w, a_log, dt_bias, norm_w, batch, seq):
    n = batch * seq
    nc = seq // CHUNK
    hb = GDN_PREP_HEADS
    w = hb * GDN_DK
    ga_t = proj[:, OFF_SM + SM_DT:OFF_SM + SM_DT + GDN_HEADS].reshape(batch * nc, CHUNK, GDN_HEADS)
    ga_t = ga_t.transpose(0, 2, 1)
    lane_pad = lambda vec, off: jnp.zeros((1, LANES), F32).at[0, off:off + GDN_HEADS].set(vec.astype(F32))
    alog_row, dtb_row = lane_pad(a_log, SM_DT), lane_pad(dt_bias, SM_DT)
    alog_col = jnp.broadcast_to(a_log.astype(F32)[:, None], (GDN_HEADS, LANES))
    dtb_col = jnp.broadcast_to(dt_bias.astype(F32)[:, None], (GDN_HEADS, LANES))
    nw = norm_w.astype(F32).reshape(1, GDN_DV)
    cw = conv_w.astype(F32)

    def cur(off):
        return pl.BlockSpec((CHUNK, w), lambda i, h: (i, off // w + h))

    def prev(off):
        per = CHUNK // CONV_HALO
        return pl.BlockSpec((CONV_HALO, w), lambda i, h: (jnp.maximum(i * per - 1, 0), off // w + h))

    def cwspec(off):
        return pl.BlockSpec((CONV_WIDTH, w), lambda i, h: (0, off // w + h))

    row = pl.BlockSpec((1, LANES), lambda i, h: (0, 0))
    colv = pl.BlockSpec((GDN_HEADS, LANES), lambda i, h: (0, 0))
    head_tile = pl.BlockSpec((CHUNK, w), lambda i, h: (i, h))
    u, wm, qd, kd, a_in, egl = pl.pallas_call(
        functools.partial(_gdn_prep_kernel, nc=nc), grid=(batch * nc, GDN_HEADS // hb),
        in_specs=[cur(OFF_GQ), cur(OFF_GK), cur(OFF_GV), prev(OFF_GQ), prev(OFF_GK), prev(OFF_GV),
                  pl.BlockSpec((CHUNK, LANES), lambda i, h: (i, OFF_SM // LANES)),
                  pl.BlockSpec((None, GDN_HEADS, CHUNK), lambda i, h: (i, 0, 0)),
                  cwspec(0), cwspec(GDN_QK), cwspec(2 * GDN_QK), row, row, colv, colv],
        out_specs=[head_tile, head_tile, head_tile, head_tile,
                   pl.BlockSpec((CHUNK, hb * CHUNK), lambda i, h: (i, h)),
                   pl.BlockSpec((CONV_HALO, w), lambda i, h: (i, h))],
        out_shape=[jax.ShapeDtypeStruct((n, GDN_V), F32), jax.ShapeDtypeStruct((n, GDN_V), BF16),
                   jax.ShapeDtypeStruct((n, GDN_QK), BF16), jax.ShapeDtypeStruct((n, GDN_QK), BF16),
                   jax.ShapeDtypeStruct((n, GDN_HEADS * CHUNK), BF16),
                   jax.ShapeDtypeStruct((batch * nc * CONV_HALO, GDN_V), F32)],
        scratch_shapes=[pltpu.VMEM((3, CONV_HALO + CHUNK, w), F32),
                        pltpu.VMEM((GDN_HEADS, CHUNK), F32)],
        compiler_params=_cparams(2), name="gdn_prep",
    )(proj, proj, proj, proj, proj, proj, proj, ga_t, cw, cw, cw, alog_row, dtb_row, alog_col, dtb_col)

    per = GDN_SCAN_CHUNKS if nc % GDN_SCAN_CHUNKS == 0 else 1
    steps = nc // per

    def full(width):
        return pl.BlockSpec((per * CHUNK, width), lambda b, t: (b * steps + t, 0))

    return pl.pallas_call(
        _gdn_scan_kernel, grid=(batch, steps),
        in_specs=[full(GDN_V), full(GDN_V), full(GDN_QK), full(GDN_QK), full(GDN_HEADS * CHUNK),
                  pl.BlockSpec((per * CONV_HALO, GDN_V), lambda b, t: (b * steps + t, 0)),
                  pl.BlockSpec((per * CHUNK, GDN_V), lambda b, t: (b * steps + t, OFF_GZ // GDN_V)),
                  pl.BlockSpec((1, GDN_DV), lambda b, t: (0, 0))],
        out_specs=full(GDN_V),
        out_shape=jax.ShapeDtypeStruct((n, GDN_V), BF16),
        scratch_shapes=[pltpu.VMEM((GDN_HEADS, GDN_DK, GDN_DV), F32)],
        compiler_params=_cparams(2), name="gdn_scan",
    )(u, wm, qd, kd, a_in, egl, proj, nw)


def _merge_kernel(att_ref, gdn_ref, wa_ref, wg_ref, ga_ref, gb_ref, o_ref):
    a = jnp.dot(att_ref[...], wa_ref[...], preferred_element_type=F32)
    g = jnp.dot(gdn_ref[...], wg_ref[...], preferred_element_type=F32)
    o_ref[...] = (_sigmoid(ga_ref[...]) * a + _sigmoid(gb_ref[...]) * g).astype(BF16)


def _merge(att, gdn, w_o_attn, w_o_gdn, proj):
    n = att.shape[0]
    tm, tn = min(512, n), 1024
    return pl.pallas_call(
        _merge_kernel, grid=(n // tm, D_MODEL // tn),
        in_specs=[pl.BlockSpec((tm, ATT_Q), lambda i, j: (i, 0)),
                  pl.BlockSpec((tm, GDN_V), lambda i, j: (i, 0)),
                  pl.BlockSpec((ATT_Q, tn), lambda i, j: (0, j)),
                  pl.BlockSpec((GDN_V, tn), lambda i, j: (0, j)),
                  pl.BlockSpec((tm, tn), lambda i, j: (i, OFF_GTA // tn + j)),
                  pl.BlockSpec((tm, tn), lambda i, j: (i, OFF_GTB // tn + j))],
        out_specs=pl.BlockSpec((tm, tn), lambda i, j: (i, j)),
        out_shape=jax.ShapeDtypeStruct((n, D_MODEL), BF16),
        compiler_params=_cparams(2), name="mixer_merge",
    )(att, gdn, w_o_attn.astype(BF16), w_o_gdn.astype(BF16), proj, proj)


def _pack_halves(x):
    w = x.shape[1] // 2
    bits = lax.bitcast_convert_type(x.astype(BF16).astype(F32), U32)
    return (bits[:, :w] >> 16) | (bits[:, w:] & jnp.uint32(0xFFFF0000))


def _unpack_halves(u):
    lo = lax.bitcast_convert_type(u << 16, F32)
    hi = lax.bitcast_convert_type(u & jnp.uint32(0xFFFF0000), F32)
    return lo, hi


SLAB = 8


def _store_slabs(ref, lead, u):
    m = u.shape[0]
    for s in range(SLAB):
        ref[lead + (pl.ds(s, m, stride=SLAB), slice(None))] = u[:, s * LANES:(s + 1) * LANES]


def _load_slabs(ref, lead, start, m):
    return jnp.concatenate([ref[lead + (pl.ds(start + s, m, stride=SLAB), slice(None))] for s in range(SLAB)],
                           axis=1)


def _outproj_ln_kernel(m_ref, w_ref, x_ref, g_ref, b_ref, x1_ref, x1p_ref, *, alpha):
    y = jnp.dot(m_ref[...], w_ref[...], preferred_element_type=F32)
    x1 = _layer_norm(alpha * x_ref[...] + y, g_ref[...], b_ref[...])
    x1_ref[...] = x1
    _store_slabs(x1p_ref, (), _pack_halves(x1))


def _outproj_ln(merged, w_out, x2d, ln_g, ln_b, alpha):
    n, d = x2d.shape
    tm = min(256, n)
    row = pl.BlockSpec((1, d), lambda i: (0, 0))
    tile = pl.BlockSpec((tm, d), lambda i: (i, 0))
    return pl.pallas_call(
        functools.partial(_outproj_ln_kernel, alpha=alpha), grid=(n // tm,),
        in_specs=[tile, pl.BlockSpec((d, d), lambda i: (0, 0)), tile, row, row],
        out_specs=[tile, pl.BlockSpec((tm * SLAB, LANES), lambda i: (i, 0))],
        out_shape=[jax.ShapeDtypeStruct((n, d), F32), jax.ShapeDtypeStruct((n * SLAB, LANES), U32)],
        compiler_params=_cparams(1), name="outproj_ln",
    )(merged, w_out.astype(BF16), x2d, ln_g.astype(F32).reshape(1, d), ln_b.astype(F32).reshape(1, d))


def _first_max(x, iota, size):
    m = jnp.max(x, axis=0, keepdims=True)
    f = jnp.min(jnp.where(x == m, iota, float(size)), axis=0, keepdims=True)
    return m, f


def _router_kernel(x_ref, w_ref, b_ref, eidx_ref, gate_ref, rank_ref, cnt_ref, carry_sc):
    tm = x_ref.shape[0]
    per = N_EXPERTS // N_GROUPS
    neg = -jnp.inf

    @pl.when(pl.program_id(0) == 0)
    def _():
        carry_sc[...] = jnp.zeros(carry_sc.shape, F32)

    logits = _nt(w_ref[...], x_ref[...], precision=HI)
    s = _sigmoid(logits)
    choice = s + b_ref[:, 0:1]
    io_g = lax.broadcasted_iota(I32, (per, tm), 0).astype(F32)
    io_e = lax.broadcasted_iota(I32, (N_EXPERTS, tm), 0).astype(F32)

    gs = []
    for g in range(N_GROUPS):
        xg = choice[g * per:(g + 1) * per]
        m1, f1 = _first_max(xg, io_g, per)
        m2 = jnp.max(jnp.where(io_g == f1, neg, xg), axis=0, keepdims=True)
        gs.append(m1 + m2)
    gw = jnp.concatenate(gs, axis=0)
    gsel = jnp.zeros((N_GROUPS, tm), F32)
    for _ in range(TOPK_GROUPS):
        _, f = _first_max(gw, io_g, N_GROUPS)
        hit = io_g == f
        gsel = jnp.where(hit, 1.0, gsel)
        gw = jnp.where(hit, neg, gw)
    x = jnp.concatenate([jnp.where(gsel[g:g + 1] > 0.0, choice[g * per:(g + 1) * per], neg)
                         for g in range(N_GROUPS)], axis=0)

    mem = jnp.zeros((N_EXPERTS, tm), F32)
    picks, wsel = [], []
    for _ in range(TOP_K):
        _, f = _first_max(x, io_e, N_EXPERTS)
        hit = io_e == f
        picks.append(f)
        wsel.append(jnp.sum(jnp.where(hit, s, 0.0), axis=0, keepdims=True))
        mem = jnp.where(hit, 1.0, mem)
        x = jnp.where(hit, neg, x)
    wsel = jnp.concatenate(wsel, axis=0)
    gate_ref[...] = wsel / jnp.sum(wsel, axis=0, keepdims=True) * ROUTED_SCALE
    eidx_ref[...] = jnp.concatenate(picks, axis=0).astype(I32)

    tr = lax.broadcasted_iota(I32, (tm, tm), 0)
    tc = lax.broadcasted_iota(I32, (tm, tm), 1)
    before = jnp.where(tr < tc, 1.0, 0.0).astype(BF16)
    prefix = jnp.dot(mem.astype(BF16), before, preferred_element_type=F32) + carry_sc[:, 0:1]
    ranks = [jnp.sum(jnp.where(io_e == f, prefix, 0.0), axis=0, keepdims=True) for f in picks]
    rank_ref[...] = jnp.concatenate(ranks, axis=0).astype(I32)
    carry_sc[...] = carry_sc[...] + jnp.sum(mem, axis=1, keepdims=True)
    cnt_ref[...] = carry_sc[...]


def _router(x1, w_router, router_bias):
    n, d = x1.shape
    tm = min(512, n)
    out = pl.BlockSpec((TOP_K, tm), lambda i: (0, i))
    bias = jnp.broadcast_to(router_bias.astype(F32)[:, None], (N_EXPERTS, LANES))
    return pl.pallas_call(
        _router_kernel, grid=(n // tm,),
        in_specs=[pl.BlockSpec((tm, d), lambda i: (i, 0)),
                  pl.BlockSpec((N_EXPERTS, d), lambda i: (0, 0)),
                  pl.BlockSpec((N_EXPERTS, LANES), lambda i: (0, 0))],
        out_specs=[out, out, out, pl.BlockSpec((N_EXPERTS, LANES), lambda i: (0, 0))],
        out_shape=[jax.ShapeDtypeStruct((TOP_K, n), I32), jax.ShapeDtypeStruct((TOP_K, n), F32),
                   jax.ShapeDtypeStruct((TOP_K, n), I32), jax.ShapeDtypeStruct((N_EXPERTS, LANES), F32)],
        scratch_shapes=[pltpu.VMEM((N_EXPERTS, LANES), F32)],
        compiler_params=_cparams(1), name="router",
    )(x1, w_router.astype(F32).T, bias)


SCATTER_TOKENS = 1024
COMBINE_TOKENS = 256
ROW_COPY_UNROLL = 8


def _pos_kernel(pstart_ref, eidx_ref, rank_ref, pos_ref):
    e = eidx_ref[...]
    acc = rank_ref[...]
    for x in range(N_EXPERTS):
        acc = acc + jnp.where(e == x, pstart_ref[x], 0)
    pos_ref[...] = acc


def _positions(pstart, eidx, rank):
    n = eidx.shape[1]
    tm = min(2048, n)
    tile = pl.BlockSpec((TOP_K, tm), lambda i, ps: (0, i))
    return pl.pallas_call(
        _pos_kernel,
        grid_spec=pltpu.PrefetchScalarGridSpec(num_scalar_prefetch=1, grid=(n // tm,),
                                               in_specs=[tile, tile], out_specs=tile),
        out_shape=jax.ShapeDtypeStruct((TOP_K, n), I32),
        compiler_params=_cparams(1), name="moe_positions",
    )(pstart, eidx, rank)


def _scatter_kernel(pend_ref, padded_ref, pos_ref, x_ref, xs_ref, zero_sc, sem, zsem):
    tm = x_ref.shape[0] // SLAB
    bm = EXPERT_ROWS

    @pl.when(pl.program_id(0) == 0)
    def _():
        zero_sc[...] = jnp.zeros(zero_sc.shape, U32)

        def tail_copy(e):
            tail = pl.multiple_of((pend_ref[e] - bm) * SLAB, bm * SLAB)
            return pltpu.make_async_copy(zero_sc, xs_ref.at[pl.ds(tail, bm * SLAB), :], zsem)

        def zstart(e, carry):
            @pl.when(padded_ref[e] > 0)
            def _():
                tail_copy(e).start()
            return carry

        def zwait(e, carry):
            @pl.when(padded_ref[e] > 0)
            def _():
                tail_copy(e).wait()
            return carry

        lax.fori_loop(0, N_EXPERTS, zstart, 0)
        lax.fori_loop(0, N_EXPERTS, zwait, 0)

    def row_copy(t, k):
        src = x_ref.at[pl.ds(pl.multiple_of(t * SLAB, SLAB), SLAB), :]
        dst = xs_ref.at[pl.ds(pl.multiple_of(pos_ref[k, t] * SLAB, SLAB), SLAB), :]
        return pltpu.make_async_copy(src, dst, sem)

    def start(t, carry):
        for k in range(TOP_K):
            row_copy(t, k).start(priority=k % 2)
        return carry

    lax.fori_loop(0, tm, start, 0, unroll=ROW_COPY_UNROLL)
    rows = xs_ref.at[pl.ds(0, tm * TOP_K * SLAB), :]
    pltpu.make_async_copy(rows, rows, sem).wait()


def _scatter_rows(x1p, pos, pend, padded, n_rows):
    n = x1p.shape[0] // SLAB
    tm = min(SCATTER_TOKENS, n)
    grid_spec = pltpu.PrefetchScalarGridSpec(
        num_scalar_prefetch=2, grid=(n // tm,),
        in_specs=[pl.BlockSpec((TOP_K, tm), lambda i, pe, pd: (0, i), memory_space=pltpu.SMEM),
                  pl.BlockSpec((tm * SLAB, LANES), lambda i, pe, pd: (i, 0))],
        out_specs=pl.BlockSpec(memory_space=pl.ANY),
        scratch_shapes=[pltpu.VMEM((EXPERT_ROWS * SLAB, LANES), U32), pltpu.SemaphoreType.DMA,
                        pltpu.SemaphoreType.DMA])
    return pl.pallas_call(
        _scatter_kernel, grid_spec=grid_spec,
        out_shape=jax.ShapeDtypeStruct((n_rows * SLAB, LANES), U32),
        compiler_params=_cparams(1), name="moe_scatter",
    )(pend, padded, pos, x1p)


def _expert_kernel(blk_e_ref, blk_on_ref, blk_new_ref, x_ref, w1_ref, w3_ref, w2_ref, y_ref,
                   w1_sc, w3_sc, w2_sc):
    del blk_e_ref
    i = pl.program_id(0)

    def swiglu_block(w1v, w3v, w2v):
        half = D_MODEL // 2
        lo, hi = _unpack_halves(_load_slabs(x_ref, (), 0, EXPERT_ROWS))
        xa, xb = lo.astype(BF16), hi.astype(BF16)

        def up(wv):
            return (jnp.dot(xa, wv[0:half, :], preferred_element_type=F32)
                    + jnp.dot(xb, wv[half:, :], preferred_element_type=F32))

        h1, h3 = up(w1v), up(w3v)
        act = (h1 * _sigmoid(h1) * h3).astype(BF16)
        _store_slabs(y_ref, (), _pack_halves(jnp.dot(act, w2v, preferred_element_type=F32)))

    @pl.when(blk_new_ref[i] > 0)
    def _():
        w1v, w3v, w2v = w1_ref[...].astype(BF16), w3_ref[...].astype(BF16), w2_ref[...].astype(BF16)
        w1_sc[...] = w1v
        w3_sc[...] = w3v
        w2_sc[...] = w2v
        swiglu_block(w1v, w3v, w2v)

    @pl.when((blk_on_ref[i] > 0) & (blk_new_ref[i] == 0))
    def _():
        swiglu_block(w1_sc[...], w3_sc[...], w2_sc[...])


def _experts(xs, blk_e, blk_on, blk_new, w1, w3, w2):
    n_rows, w = xs.shape
    bm = EXPERT_ROWS * SLAB
    grid_spec = pltpu.PrefetchScalarGridSpec(
        num_scalar_prefetch=3, grid=(n_rows // bm,),
        in_specs=[pl.BlockSpec((bm, w), lambda i, e, on, nw: (i, 0)),
                  pl.BlockSpec((None, D_MODEL, EXPERT_DIM), lambda i, e, on, nw: (e[i], 0, 0)),
                  pl.BlockSpec((None, D_MODEL, EXPERT_DIM), lambda i, e, on, nw: (e[i], 0, 0)),
                  pl.BlockSpec((None, EXPERT_DIM, D_MODEL), lambda i, e, on, nw: (e[i], 0, 0))],
        out_specs=pl.BlockSpec((bm, w), lambda i, e, on, nw: (i, 0)),
        scratch_shapes=[pltpu.VMEM((D_MODEL, EXPERT_DIM), BF16), pltpu.VMEM((D_MODEL, EXPERT_DIM), BF16),
                        pltpu.VMEM((EXPERT_DIM, D_MODEL), BF16)])
    return pl.pallas_call(
        _expert_kernel, grid_spec=grid_spec,
        out_shape=jax.ShapeDtypeStruct((n_rows, w), U32),
        compiler_params=_cparams(1), name="moe_experts",
    )(blk_e, blk_on, blk_new, xs, w1, w3, w2)


def _shared_up_kernel(x_ref, w_ref, h_ref):
    h = jnp.dot(x_ref[...].astype(BF16), w_ref[...], preferred_element_type=F32)
    h1, h3 = h[:, :SHARED_DIM], h[:, SHARED_DIM:]
    h_ref[...] = (h1 * _sigmoid(h1) * h3).astype(BF16)


def _shared_up(x1, ws1, ws3):
    n, d = x1.shape
    tm = min(512, n)
    w13 = jnp.concatenate([ws1, ws3], axis=1).astype(BF16)
    return pl.pallas_call(
        _shared_up_kernel, grid=(n // tm,),
        in_specs=[pl.BlockSpec((tm, d), lambda i: (i, 0)),
                  pl.BlockSpec((d, 2 * SHARED_DIM), lambda i: (0, 0))],
        out_specs=pl.BlockSpec((tm, SHARED_DIM), lambda i: (i, 0)),
        out_shape=jax.ShapeDtypeStruct((n, SHARED_DIM), BF16),
        compiler_params=_cparams(1), name="shared_up",
    )(x1, w13)


def _combine_kernel(pos_ref, pos_next_ref, x1_ref, h_ref, ws2_ref, gate_ref, g_ref, b_ref, y_hbm, o_ref,
                    ybuf, sem, *, alpha):
    tm = x1_ref.shape[0]
    i = pl.program_id(0)
    slot = i % 2

    def issue(p_ref, s):
        def start(t, carry):
            for k in range(TOP_K):
                src = y_hbm.at[pl.ds(pl.multiple_of(p_ref[k, t] * SLAB, SLAB), SLAB), :]
                dst = ybuf.at[s, pl.ds(pl.multiple_of((k * tm + t) * SLAB, SLAB), SLAB), :]
                pltpu.make_async_copy(src, dst, sem.at[s]).start(priority=k % 2)
            return carry

        lax.fori_loop(0, tm, start, 0, unroll=ROW_COPY_UNROLL)

    @pl.when(i == 0)
    def _():
        issue(pos_ref, 0)

    @pl.when(i + 1 < pl.num_programs(0))
    def _():
        issue(pos_next_ref, 1 - slot)

    base = alpha * x1_ref[...] + jnp.dot(h_ref[...], ws2_ref[...], preferred_element_type=F32)
    pltpu.make_async_copy(y_hbm.at[pl.ds(0, tm * TOP_K * SLAB), :], ybuf.at[slot], sem.at[slot]).wait()

    gates = gate_ref[...]
    acc_lo = jnp.zeros((tm, D_MODEL // 2), F32)
    acc_hi = jnp.zeros((tm, D_MODEL // 2), F32)
    for k in range(TOP_K):
        lo, hi = _unpack_halves(_load_slabs(ybuf, (slot,), k * tm * SLAB, tm))
        gk = gates[:, k:k + 1]
        acc_lo = acc_lo + gk * lo
        acc_hi = acc_hi + gk * hi
    f = jnp.concatenate([acc_lo, acc_hi], axis=1)
    o_ref[...] = _layer_norm(base + f, g_ref[...], b_ref[...])


def _combine(x1, hsh, ws2, pos, gates_t, ys, ln_g, ln_b, alpha):
    n, d = x1.shape
    tm = min(COMBINE_TOKENS, n)
    row = pl.BlockSpec((1, d), lambda i: (0, 0))
    last = n // tm - 1
    return pl.pallas_call(
        functools.partial(_combine_kernel, alpha=alpha), grid=(n // tm,),
        in_specs=[pl.BlockSpec((TOP_K, tm), lambda i: (0, i), memory_space=pltpu.SMEM),
                  pl.BlockSpec((TOP_K, tm), lambda i: (0, jnp.minimum(i + 1, last)), memory_space=pltpu.SMEM),
                  pl.BlockSpec((tm, d), lambda i: (i, 0)),
                  pl.BlockSpec((tm, SHARED_DIM), lambda i: (i, 0)),
                  pl.BlockSpec((SHARED_DIM, d), lambda i: (0, 0)),
                  pl.BlockSpec((tm, TOP_K), lambda i: (i, 0)),
                  row, row,
                  pl.BlockSpec(memory_space=pl.ANY)],
        out_specs=pl.BlockSpec((tm, d), lambda i: (i, 0)),
        out_shape=jax.ShapeDtypeStruct((n, d), F32),
        scratch_shapes=[pltpu.VMEM((2, TOP_K * tm * SLAB, LANES), U32), pltpu.SemaphoreType.DMA((2,))],
        compiler_params=_cparams(1), name="moe_combine",
    )(pos, pos, x1, hsh, ws2.astype(BF16), gates_t, ln_g.astype(F32).reshape(1, d),
      ln_b.astype(F32).reshape(1, d), ys)


def _moe(x1, x1p, w_router, router_bias, w1, w3, w2, ws1, ws3, ws2, ln_g, ln_b, alpha):
    n = x1.shape[0]
    bm = EXPERT_ROWS
    eidx, gates, rank, cnt = _router(x1, w_router, router_bias)
    counts = cnt[:, 0].astype(I32)
    padded = (counts + bm - 1) // bm * bm
    pend = jnp.cumsum(padded)
    pstart = pend - padded
    pos = _positions(pstart, eidx, rank)
    n_rows = -(-(n * TOP_K) // bm) * bm + N_EXPERTS * bm
    blk_start = jnp.arange(n_rows // bm, dtype=I32) * bm
    blk_on = (blk_start < pend[-1]).astype(I32)
    last = jnp.maximum(pend[-1] - bm, 0)
    blk_e = jnp.sum((pend[None, :] <= jnp.minimum(blk_start, last)[:, None]).astype(I32), axis=1)
    blk_e = jnp.minimum(blk_e, N_EXPERTS - 1)
    changed = jnp.concatenate([jnp.ones((1,), I32), (blk_e[1:] != blk_e[:-1]).astype(I32)])
    blk_new = blk_on * changed
    xs = _scatter_rows(x1p, pos, pend, padded, n_rows)
    ys = _experts(xs, blk_e, blk_on, blk_new, w1, w3, w2)
    hsh = _shared_up(x1, ws1, ws3)
    return _combine(x1, hsh, ws2, pos, gates.T, ys, ln_g, ln_b, alpha)


def kernel(x, positions, w_in, conv_w, A_log, dt_bias, gdn_norm_w, idx_ln_g, idx_ln_b, w_o_attn, w_o_gdn, w_out, ln1_g, ln1_b, w_router, router_bias, w1, w3, w2, ws1, ws3, ws2, ln2_g, ln2_b):
    batch, seq, d = x.shape
    depth = w_in.shape[0]
    alpha = (2 * depth) ** 0.25
    n = batch * seq
    xf = x.reshape(n, d)
    tabs = _trig_tables(positions)
    for l in range(depth):
        proj = _project(xf, _pack_w_in(w_in, l))
        q_r, k_r, v_ext, iq_r, ik_lo, ik_hi, iw = _att_prep(proj, tabs, idx_ln_g[l], idx_ln_b[l])
        att = _dsa_attention(q_r, k_r, v_ext, iq_r, ik_lo, ik_hi, iw, batch, seq)
        gdn = _gated_deltanet(proj, conv_w[l], A_log[l], dt_bias[l], gdn_norm_w[l], batch, seq)
        merged = _merge(att, gdn, w_o_attn[l], w_o_gdn[l], proj)
        x1, x1p = _outproj_ln(merged, w_out[l], xf, ln1_g[l], ln1_b[l], alpha)
        xf = _moe(x1, x1p, w_router[l], router_bias[l], w1[l], w3[l], w2[l], ws1[l], ws3[l], ws2[l],
                  ln2_g[l], ln2_b[l], alpha)
    return xf.reshape(batch, seq, d)
```
